```python
import functools
import jax
import jax.numpy as jnp
from jax import lax
import numpy as np

D_MODEL = 1024
BATCH = 4
SEQ = 4096
DEPTH = 2
DEC_BATCH = 32
DEC_SEQ = 4
PAST_LEN = 16384
PAGE_SIZE = 128

HEAD_DIM = 64
N_ATTN_HEADS = 8
N_MLSTM_HEADS = 4
N_HGRN_HEADS = 4
ATTN_WIDTH = N_ATTN_HEADS * HEAD_DIM
MLSTM_WIDTH = N_MLSTM_HEADS * HEAD_DIM
HGRN_WIDTH = N_HGRN_HEADS * HEAD_DIM
MIX_WIDTH = ATTN_WIDTH + MLSTM_WIDTH + HGRN_WIDTH
PROJ_SPLITS = (ATTN_WIDTH, ATTN_WIDTH, ATTN_WIDTH,
               MLSTM_WIDTH, MLSTM_WIDTH, MLSTM_WIDTH, MLSTM_WIDTH, N_MLSTM_HEADS, N_MLSTM_HEADS,
               HGRN_WIDTH, HGRN_WIDTH, HGRN_WIDTH, HGRN_WIDTH)
PROJ_WIDTH = 3 * ATTN_WIDTH + 4 * MLSTM_WIDTH + 2 * N_MLSTM_HEADS + 4 * HGRN_WIDTH
DILATED_PATTERNS = ((128, 1), (512, 4), (2048, 16))
MAX_WINDOW = 2048
QBLOCK = 128
N_REL_BUCKETS = 32
REL_MAX_DISTANCE = 2048
CONV_WIDTH = 4
MLSTM_CHUNK = 64
HGRN_CHUNK = 64
D_FF = 4 * D_MODEL
EPS = 1e-6
NEG_BIG = -1e30

kernel_name = 'hymba_dilated_mlstm_hgrn2_step'


def rmsnorm(x, g):
    xf = x.astype(jnp.float32)
    y = xf * lax.rsqrt(jnp.mean(xf * xf, axis=-1, keepdims=True) + EPS)
    return (y * g.astype(jnp.float32)).astype(x.dtype)


def head_rmsnorm(h, g):
    y = h * lax.rsqrt(jnp.mean(h * h, axis=-1, keepdims=True) + EPS)
    return y * g.astype(jnp.float32).reshape(h.shape[-2:])


def t5_causal_bucket(dist):
    n = np.asarray(dist).astype(np.int32)
    max_exact = N_REL_BUCKETS // 2
    scaled = np.log(np.maximum(n, 1) / max_exact) / np.log(REL_MAX_DISTANCE / max_exact)
    large = np.minimum(max_exact + (scaled * (N_REL_BUCKETS - max_exact)).astype(np.int32), N_REL_BUCKETS - 1)
    return np.where(n < max_exact, n, large).astype(np.int32)


def dilated_branch_prompt(q, k, v, rel_bias, window, dilation):
    B, S, H, E = q.shape
    ls = S // dilation
    nsub = window // dilation
    nb = -(-ls // QBLOCK)
    lp = nb * QBLOCK

    def to_blocks(t):
        t = jnp.swapaxes(t.reshape(B, ls, dilation, H, E), 1, 2)
        t = jnp.pad(t, ((0, 0), (0, 0), (0, lp - ls), (0, 0), (0, 0)))
        return t.reshape(B, dilation, nb, QBLOCK, H, E)

    def with_prev(t):
        prev = jnp.pad(t, ((0, 0), (0, 0), (1, 0), (0, 0), (0, 0), (0, 0)))[:, :, :-1]
        return jnp.concatenate([prev, t], axis=3)

    def from_blocks(t):
        t = t.reshape((B, dilation, lp) + t.shape[4:])[:, :, :ls]
        t = jnp.swapaxes(t, 1, 2)
        return t.reshape((B, S) + t.shape[3:])

    qb = to_blocks(q)
    kk = with_prev(to_blocks(k))
    vv = with_prev(to_blocks(v))
    s = jnp.einsum('brnqhe,brnkhe->brnhqk', qb, kk, preferred_element_type=jnp.float32) * E ** -0.5
    a = np.arange(QBLOCK)[:, None]
    b = np.arange(2 * QBLOCK)[None, :]
    rel = QBLOCK + a - b
    band = (rel >= 0) & (rel <= nsub)
    bias = jnp.transpose(rel_bias.astype(jnp.float32)[t5_causal_bucket(np.clip(rel, 0, None) * dilation)], (2, 0, 1))
    first = (np.arange(nb) == 0)[:, None, None]
    valid = band[None] & ~(first & (b < QBLOCK)[None])
    s = jnp.where(valid[None, None, :, None], s + bias, NEG_BIG)
    m = jnp.max(s, axis=-1, keepdims=True)
    p = jnp.exp(s - m)
    l = jnp.sum(p, axis=-1, keepdims=True)
    o = jnp.einsum('brnhqk,brnkhe->brnqhe', p, vv.astype(jnp.float32)) / jnp.transpose(l, (0, 1, 2, 4, 3, 5))
    lse = jnp.transpose((m + jnp.log(l))[..., 0], (0, 1, 2, 4, 3))
    return from_blocks(o), from_blocks(lse)


def dilated_branch_sample(q, k_all, v_all, rel_bias, window, dilation, n_past):
    T, E = q.shape[1], q.shape[3]
    j = np.arange(window // dilation + 1)
    idx = n_past + np.arange(T)[:, None] - dilation * j[None, :]
    valid = idx >= 0
    idx = np.maximum(idx, 0)
    kg = k_all[:, idx]
    vg = v_all[:, idx]
    s = jnp.einsum('bthe,btjhe->bhtj', q, kg, preferred_element_type=jnp.float32) * E ** -0.5
    bias = rel_bias.astype(jnp.float32)[t5_causal_bucket(dilation * j)].T
    s = jnp.where(valid[None, None], s + bias[:, None, :], NEG_BIG)
    m = jnp.max(s, axis=-1, keepdims=True)
    p = jnp.exp(s - m)
    l = jnp.sum(p, axis=-1, keepdims=True)
    o = jnp.einsum('bhtj,btjhe->bthe', p, vg.astype(jnp.float32)) / jnp.swapaxes(l, 1, 2)
    lse = jnp.swapaxes((m + jnp.log(l))[..., 0], 1, 2)
    return o, lse


def dilated_mixture(branches):
    outs = jnp.stack([o for o, _ in branches])
    w = jax.nn.softmax(jnp.stack([lse for _, lse in branches]), axis=0)
    return jnp.sum(w[..., None] * outs, axis=0)


def attn_prompt(q, k, v, rel_bias):
    return dilated_mixture([dilated_branch_prompt(q, k, v, rel_bias, w, d) for w, d in DILATED_PATTERNS])


def attn_sample(q, k, v, rel_bias, k_buf, v_buf):
    n_past = k_buf.shape[1]
    k_all = jnp.concatenate([k_buf.astype(k.dtype), k], axis=1)
    v_all = jnp.concatenate([v_buf.astype(v.dtype), v], axis=1)
    return dilated_mixture([dilated_branch_sample(q, k_all, v_all, rel_bias, w, d, n_past) for w, d in DILATED_PATTERNS])


def causal_conv(x, buf, w):
    S = x.shape[1]
    xp = jnp.concatenate([buf.astype(x.dtype), x], axis=1)
    y = xp[:, :S] * w[0]
    for j in range(1, CONV_WIDTH):
        y = y + xp[:, j:j + S] * w[j]
    return y, xp[:, S:]


def _chunk_len(S, C):
    return C if S % C == 0 else S


def _to_chunks(t, L):
    B, S = t.shape[:2]
    t = t.reshape((B, S // L, L) + t.shape[2:])
    return jnp.moveaxis(jnp.moveaxis(t, 1, 0), 3, 2)


def _from_chunks(t):
    nc, B, H, L, E = t.shape
    return jnp.transpose(t, (1, 0, 3, 2, 4)).reshape(B, nc * L, H, E)


def mlstm_chunked(q, k, v, ig, lf, C0, n0, m0):
    S = q.shape[1]
    L = _chunk_len(S, MLSTM_CHUNK)
    causal = np.tril(np.ones((L, L), dtype=bool))

    def step(carry, xs):
        C, n, m = carry
        qc, kc, vc, ic, fc = xs
        b = jnp.cumsum(fc, axis=-1)
        D = jnp.where(causal, b[..., :, None] - b[..., None, :] + ic[..., None, :], NEG_BIG)
        inter = b + m[..., None]
        mt = jnp.maximum(jnp.max(D, axis=-1), inter)
        wts = jnp.exp(D - mt[..., None]) * jnp.einsum('bhte,bhse->bhts', qc, kc)
        g = jnp.exp(inter - mt)
        num = jnp.einsum('bhts,bhse->bhte', wts, vc) + g[..., None] * jnp.einsum('bhte,bhef->bhtf', qc, C)
        den = jnp.sum(wts, axis=-1) + g * jnp.einsum('bhte,bhe->bht', qc, n)
        h = num / jnp.maximum(jnp.abs(den), jnp.exp(-mt))[..., None]
        mL = mt[..., -1]
        decay = jnp.exp(b[..., -1:] - b + ic - mL[..., None])
        gL = g[..., -1]
        C = gL[..., None, None] * C + jnp.einsum('bhs,bhse,bhsf->bhef', decay, kc, vc)
        n = gL[..., None] * n + jnp.einsum('bhs,bhse->bhe', decay, kc)
        return (C, n, mL), h

    xs = (_to_chunks(q, L), _to_chunks(k, L), _to_chunks(v, L), _to_chunks(ig, L), _to_chunks(lf, L))
    (C, n, m), h = lax.scan(step, (C0, n0, m0), xs)
    return _from_chunks(h), C, n, m


def hgrn2_chunked(q, k, log_f, i, S0):
    S = q.shape[1]
    L = _chunk_len(S, HGRN_CHUNK)
    causal = np.tril(np.ones((L, L), dtype=bool))

    def step(St, xs):
        qc, kc, fc, ic = xs
        b = jnp.cumsum(fc, axis=2)
        diff = b[:, :, :, None, :] - b[:, :, None, :, :]
        dec = jnp.exp(jnp.where(causal[:, :, None], diff, NEG_BIG))
        A = jnp.einsum('bhtk,bhsk,bhtsk->bhts', qc, kc, dec)
        o = jnp.einsum('bhts,bhsv->bhtv', A, ic) + jnp.einsum('bhtk,bhkv->bhtv', qc * jnp.exp(b), St)
        bL = b[:, :, -1:, :]
        St = jnp.exp(bL[:, :, 0])[..., None] * St + jnp.einsum('bhsk,bhsv->bhkv', kc * jnp.exp(bL - b), ic)
        return St, o

    xs = (_to_chunks(q, L), _to_chunks(k, L), _to_chunks(log_f, L), _to_chunks(i, L))
    St, o = lax.scan(step, S0, xs)
    return _from_chunks(o), St


def trunk_layer(x, attn_fn, conv_buf, C0, n0, m0, S0,
                g_mix, w_in, gate_b, conv_w, mlstm_g, lb, hgrn_g, w_out, g_ffn, w_up, w_down):
    f32 = jnp.float32
    B, S, _ = x.shape

    def heads(t):
        return t.reshape(B, S, -1, HEAD_DIM)

    z = rmsnorm(x, g_mix) @ w_in
    cuts = [int(c) for c in np.cumsum(PROJ_SPLITS)[:-1]]
    (aq, ak, av, mq, mk, mv, mo, mi, mf, hq, hf, hi, hg) = jnp.split(z, cuts, axis=-1)
    ak, av = heads(ak), heads(av)
    attn_out = attn_fn(heads(aq), ak, av)
    qk, conv_new = causal_conv(jnp.concatenate([mq, mk], axis=-1), conv_buf, conv_w)
    qk = jax.nn.silu(qk.astype(f32))
    ig = mi.astype(f32) + gate_b[0].astype(f32)
    lf = jax.nn.log_sigmoid(mf.astype(f32) + gate_b[1].astype(f32))
    h_m, C, n, m = mlstm_chunked(heads(qk[..., :MLSTM_WIDTH]), heads(qk[..., MLSTM_WIDTH:]) * HEAD_DIM ** -0.5,
                                 heads(mv).astype(f32), ig, lf, C0.astype(f32), n0.astype(f32), m0.astype(f32))
    mlstm_out = jax.nn.sigmoid(heads(mo).astype(f32)) * head_rmsnorm(h_m, mlstm_g)
    lbh = lb.reshape(N_HGRN_HEADS, HEAD_DIM)
    fpre = heads(hf).astype(f32)
    f_gate = lbh + (1.0 - lbh) * jax.nn.sigmoid(fpre)
    log_f = jnp.log(f_gate)
    k_h = 1.0 - f_gate
    h_h, S_new = hgrn2_chunked(jax.nn.silu(heads(hq).astype(f32)), k_h, log_f, heads(hi).astype(f32), S0.astype(f32))
    hgrn_out = jax.nn.sigmoid(heads(hg).astype(f32)) * head_rmsnorm(h_h, hgrn_g)
    mix = jnp.concatenate([attn_out.reshape(B, S, ATTN_WIDTH), mlstm_out.reshape(B, S, MLSTM_WIDTH),
                           hgrn_out.reshape(B, S, HGRN_WIDTH)], axis=-1)
    x = x + mix.astype(x.dtype) @ w_out
    u = rmsnorm(x, g_ffn) @ w_up
    x = x + jnp.square(jax.nn.relu(u)) @ w_down
    return x, (ak, av, conv_new, C, n, m, S_new)


def setup_inputs(seed: int = 0) -> dict:
    key = jax.random.key(seed)
    ks = jax.random.split(key, 24)
    f32 = jnp.float32

    def nrm(k, shape, scale):
        return scale * jax.random.normal(k, shape, f32)

    n_buf = min(MAX_WINDOW, PAST_LEN)
    f_bias = jnp.linspace(3.0, 6.0, N_MLSTM_HEADS, dtype=f32)
    mlstm_gate_b = jnp.stack([nrm(ks[10], (DEPTH, N_MLSTM_HEADS), 0.1),
                              f_bias + nrm(ks[11], (DEPTH, N_MLSTM_HEADS), 0.1)], axis=1)
    return {
        'x_prompt': nrm(ks[0], (BATCH, SEQ, D_MODEL), 1.0),
        'x_sample': nrm(ks[1], (DEC_BATCH, DEC_SEQ, D_MODEL), 1.0),
        'cache_attn_k': nrm(ks[2], (DEPTH, DEC_BATCH, n_buf, N_ATTN_HEADS, HEAD_DIM), 1.0),
        'cache_attn_v': nrm(ks[3], (DEPTH, DEC_BATCH, n_buf, N_ATTN_HEADS, HEAD_DIM), 1.0),
        'state_mlstm_conv': nrm(ks[4], (DEPTH, DEC_BATCH, CONV_WIDTH - 1, 2 * MLSTM_WIDTH), 1.0),
        'state_mlstm_C': nrm(ks[5], (DEPTH, DEC_BATCH, N_MLSTM_HEADS, HEAD_DIM, HEAD_DIM), 0.3),
        'state_mlstm_n': nrm(ks[6], (DEPTH, DEC_BATCH, N_MLSTM_HEADS, HEAD_DIM), 0.3),
        'state_mlstm_m': nrm(ks[7], (DEPTH, DEC_BATCH, N_MLSTM_HEADS), 1.0),
        'state_hgrn_S': nrm(ks[8], (DEPTH, DEC_BATCH, N_HGRN_HEADS, HEAD_DIM, HEAD_DIM), 1.0),
        'rel_bias': nrm(ks[9], (N_REL_BUCKETS, N_ATTN_HEADS), 0.5),
        'g_mix': 1.0 + nrm(ks[12], (DEPTH, D_MODEL), 0.05),
        'w_in': nrm(ks[13], (DEPTH, D_MODEL, PROJ_WIDTH), D_MODEL ** -0.5),
        'mlstm_gate_b': mlstm_gate_b,
        'mlstm_conv_w': nrm(ks[14], (DEPTH, CONV_WIDTH, 2 * MLSTM_WIDTH), CONV_WIDTH ** -0.5),
        'mlstm_norm_g': 1.0 + nrm(ks[15], (DEPTH, MLSTM_WIDTH), 0.05),
        'hgrn_lb_raw': nrm(ks[16], (DEPTH, HGRN_WIDTH), 0.5),
        'hgrn_norm_g': 1.0 + nrm(ks[17], (DEPTH, HGRN_WIDTH), 0.05),
        'w_out': nrm(ks[18], (DEPTH, MIX_WIDTH, D_MODEL), MIX_WIDTH ** -0.5),
        'g_ffn': 1.0 + nrm(ks[19], (DEPTH, D_MODEL), 0.05),
        'w_up': nrm(ks[20], (DEPTH, D_MODEL, D_FF), D_MODEL ** -0.5),
        'w_down': nrm(ks[21], (DEPTH, D_FF, D_MODEL), D_FF ** -0.5),
        'g_final': 1.0 + nrm(ks[22], (D_MODEL,), 0.05),
    }


def reference(x_prompt, x_sample, cache_attn_k, cache_attn_v, state_mlstm_conv, state_mlstm_C,
              state_mlstm_n, state_mlstm_m, state_hgrn_S, rel_bias, g_mix, w_in, mlstm_gate_b,
              mlstm_conv_w, mlstm_norm_g, hgrn_lb_raw, hgrn_norm_g, w_out, g_ffn, w_up, w_down, g_final):
    f32 = jnp.float32
    bp, sp = x_prompt.shape[:2]
    n_keep = min(MAX_WINDOW, sp)
    lb_p = jax.nn.softmax(hgrn_lb_raw.astype(f32), axis=0)
    hgrn_lb = jnp.cumsum(lb_p, axis=0) - lb_p[0]
    zeros_conv = jnp.zeros((bp, CONV_WIDTH - 1, 2 * MLSTM_WIDTH), x_prompt.dtype)
    zeros_C = jnp.zeros((bp, N_MLSTM_HEADS, HEAD_DIM, HEAD_DIM), f32)
    zeros_n = jnp.zeros((bp, N_MLSTM_HEADS, HEAD_DIM), f32)
    zeros_m = jnp.zeros((bp, N_MLSTM_HEADS), f32)
    zeros_S = jnp.zeros((bp, N_HGRN_HEADS, HEAD_DIM, HEAD_DIM), f32)
    prompt_attn = functools.partial(attn_prompt, rel_bias=rel_bias)
    hp, hs = x_prompt, x_sample
    p_states, s_states = [], []
    for l in range(DEPTH):
        weights = (g_mix[l], w_in[l], mlstm_gate_b[l], mlstm_conv_w[l], mlstm_norm_g[l], hgrn_lb[l],
                   hgrn_norm_g[l], w_out[l], g_ffn[l], w_up[l], w_down[l])
        hp, (pk, pv, pconv, pC, pn, pm, pS) = trunk_layer(hp, prompt_attn, zeros_conv, zeros_C, zeros_n, zeros_m,
                                                         zeros_S, *weights)
        p_states.append((pk[:, sp - n_keep:], pv[:, sp - n_keep:], pconv, pC, pn, pm, pS))
        sample_attn = functools.partial(attn_sample, rel_bias=rel_bias, k_buf=cache_attn_k[l], v_buf=cache_attn_v[l])
        hs, s_st = trunk_layer(hs, sample_attn, state_mlstm_conv[l], state_mlstm_C[l], state_mlstm_n[l],
                               state_mlstm_m[l], state_hgrn_S[l], *weights)
        s_states.append(s_st)
    y_prompt = rmsnorm(hp, g_final)
    y_sample = rmsnorm(hs, g_final)
    p_attn_k, p_attn_v, p_mlstm_conv, p_mlstm_C, p_mlstm_n, p_mlstm_m, p_hgrn_S = [jnp.stack(a) for a in zip(*p_states)]
    s_attn_k, s_attn_v, s_mlstm_conv, s_mlstm_C, s_mlstm_n, s_mlstm_m, s_hgrn_S = [jnp.stack(a) for a in zip(*s_states)]
    return (y_prompt, y_sample,
            p_attn_k, p_attn_v, p_mlstm_conv, p_mlstm_C, p_mlstm_n, p_mlstm_m, p_hgrn_S,
            s_attn_k, s_attn_v, s_mlstm_conv, s_mlstm_C, s_mlstm_n, s_mlstm_m, s_hgrn_S)
```

```python
import functools
from typing import NamedTuple

import jax
import jax.numpy as jnp
import numpy as np
from jax import lax
from jax.experimental import pallas as pl
from jax.experimental.pallas import tpu as pltpu

F32 = jnp.float32
BF16 = jnp.bfloat16
HIGHEST = lax.Precision.HIGHEST

D_MODEL = 1024
HEAD_DIM = 64
N_ATTN_HEADS = 8
N_MLSTM_HEADS = 4
N_HGRN_HEADS = 4
ATTN_WIDTH = N_ATTN_HEADS * HEAD_DIM
MLSTM_WIDTH = N_MLSTM_HEADS * HEAD_DIM
HGRN_WIDTH = N_HGRN_HEADS * HEAD_DIM
DILATED_PATTERNS = ((128, 1), (512, 4), (2048, 16))
QBLOCK = 128
N_REL_BUCKETS = 32
REL_MAX_DISTANCE = 2048
CONV_WIDTH = 4
MLSTM_CHUNK = 64
D_FF = 4 * D_MODEL
EPS = 1e-6
NEG_BIG = -1e30

LANES = 128
SUBLANES = 8

COL_AQ = 0
COL_AK = COL_AQ + ATTN_WIDTH
COL_AV = COL_AK + ATTN_WIDTH
COL_MQK = COL_AV + ATTN_WIDTH
COL_MV = COL_MQK + 2 * MLSTM_WIDTH
COL_MO = COL_MV + MLSTM_WIDTH
COL_HQ = COL_MO + MLSTM_WIDTH
COL_HF = COL_HQ + HGRN_WIDTH
COL_HI = COL_HF + HGRN_WIDTH
COL_HG = COL_HI + HGRN_WIDTH
COL_GATE = COL_HG + HGRN_WIDTH
ZW = COL_GATE + LANES
GATE_F_LANE = 64

VMEM_LIMIT = 56 * 1024 * 1024


def _cparams(sem, vmem=VMEM_LIMIT):
    return pltpu.CompilerParams(dimension_semantics=sem, vmem_limit_bytes=vmem)


def _const_spec(shape):
    nd = len(shape)
    return pl.BlockSpec(shape, lambda *_: (0,) * nd, pipeline_mode=pl.Buffered(1))


def _inproj_kernel(x_ref, g_ref, w_ref, z_ref):
    x = x_ref[...]
    ms = jnp.mean(x * x, axis=-1, keepdims=True)
    xn = (x * lax.rsqrt(ms + EPS)) * g_ref[...]
    z_ref[...] = jnp.dot(xn.astype(BF16), w_ref[...], preferred_element_type=F32)


def _inproj(x2d, g, w):
    n = x2d.shape[0]
    tm = min(512, n)
    assert n % tm == 0
    return pl.pallas_call(
        _inproj_kernel,
        out_shape=jax.ShapeDtypeStruct((n, ZW), F32),
        grid=(n // tm,),
        in_specs=[pl.BlockSpec((tm, D_MODEL), lambda i: (i, 0)),
                  _const_spec((1, D_MODEL)),
                  _const_spec((D_MODEL, ZW))],
        out_specs=pl.BlockSpec((tm, ZW), lambda i: (i, 0)),
        compiler_params=_cparams(("parallel",)),
        name="inproj",
    )(x2d, g, w)


FF_CHUNK = 1024


def _outffn_kernel(x_ref, a_ref, m_ref, h_ref, wo_ref, gf_ref, wu_ref, wd_ref, gl_ref, y_ref, xn_sc, *, final):
    x1 = x_ref[...]
    x1 = x1 + jnp.dot(a_ref[...], wo_ref[0:ATTN_WIDTH, :], preferred_element_type=F32)
    x1 = x1 + jnp.dot(m_ref[...], wo_ref[ATTN_WIDTH:ATTN_WIDTH + MLSTM_WIDTH, :], preferred_element_type=F32)
    x1 = x1 + jnp.dot(h_ref[...], wo_ref[ATTN_WIDTH + MLSTM_WIDTH:, :], preferred_element_type=F32)
    ms = jnp.mean(x1 * x1, axis=-1, keepdims=True)
    xn_sc[...] = ((x1 * lax.rsqrt(ms + EPS)) * gf_ref[...]).astype(BF16)
    y_ref[...] = x1
    for c in range(D_FF // FF_CHUNK):
        u = jnp.dot(xn_sc[...], wu_ref[:, c * FF_CHUNK:(c + 1) * FF_CHUNK], preferred_element_type=F32)
        hh = jnp.square(jnp.maximum(u, 0.0)).astype(BF16)
        y_ref[...] += jnp.dot(hh, wd_ref[c * FF_CHUNK:(c + 1) * FF_CHUNK, :], preferred_element_type=F32)
    if final:
        x2 = y_ref[...]
        ms2 = jnp.mean(x2 * x2, axis=-1, keepdims=True)
        y_ref[...] = (x2 * lax.rsqrt(ms2 + EPS)) * gl_ref[...]


def _outffn(x2d, attn, ml, hg, w_out, g_ffn, w_up, w_down, g_final, final):
    n = x2d.shape[0]
    tm = min(512, n)
    assert n % tm == 0
    row = lambda w: pl.BlockSpec((tm, w), lambda i: (i, 0))
    return pl.pallas_call(
        functools.partial(_outffn_kernel, final=final),
        out_shape=jax.ShapeDtypeStruct((n, D_MODEL), F32),
        grid=(n // tm,),
        in_specs=[row(D_MODEL), row(ATTN_WIDTH), row(MLSTM_WIDTH), row(HGRN_WIDTH),
                  _const_spec((D_MODEL, D_MODEL)), _const_spec((1, D_MODEL)),
                  _const_spec((D_MODEL, D_FF)), _const_spec((D_FF, D_MODEL)), _const_spec((1, D_MODEL))],
        out_specs=row(D_MODEL),
        scratch_shapes=[pltpu.VMEM((tm, D_MODEL), BF16)],
        compiler_params=_cparams(("parallel",)),
        name="outffn",
    )(x2d, attn, ml, hg, w_out, g_ffn, w_up, w_down, g_final)


def _t5_causal_bucket(dist):
    n = np.asarray(dist).astype(np.int32)
    max_exact = N_REL_BUCKETS // 2
    scaled = np.log(np.maximum(n, 1) / max_exact) / np.log(REL_MAX_DISTANCE / max_exact)
    large = np.minimum(max_exact + (scaled * (N_REL_BUCKETS - max_exact)).astype(np.int32), N_REL_BUCKETS - 1)
    return np.where(n < max_exact, n, large).astype(np.int32)


def _prompt_tables(rel_bias):
    a = np.arange(QBLOCK)[:, None]
    b = np.arange(2 * QBLOCK)[None, :]
    rel = QBLOCK + a - b
    band = (rel >= 0) & (rel <= QBLOCK)
    rb = rel_bias.astype(F32)
    biases = []
    for _, dil in DILATED_PATTERNS:
        bk = _t5_causal_bucket(np.clip(rel, 0, None) * dil)
        bias = jnp.transpose(rb[bk], (2, 0, 1))
        biases.append(bias.reshape(N_ATTN_HEADS // 2, 2 * QBLOCK, 2 * QBLOCK))
    mask = np.stack([np.tile(band, (2, 1)), np.tile(band & (b >= QBLOCK), (2, 1))]).astype(np.float32)
    return jnp.stack(biases), jnp.asarray(mask)


def _attn_prompt_kernel(q_ref, k_ref, v_ref, bias_ref, mask_ref, out_ref,
                        x4, qs, ks, vs, o1, o4, o16, l1, l4, l16, *, seq):
    S = seq
    nblk = S // QBLOCK
    scale = HEAD_DIM ** -0.5
    zpad = jnp.zeros((QBLOCK, LANES), BF16)
    for di in range(3):
        ks[di, 0:QBLOCK, :] = zpad
        vs[di, 0:QBLOCK, :] = zpad

    R = 512
    for src, dst, off, mul in ((q_ref, qs, 0, scale), (k_ref, ks, QBLOCK, 1.0), (v_ref, vs, QBLOCK, 1.0)):
        for i in range(S // R):
            blk = src[i * R:(i + 1) * R, :]
            dst[0, off + i * R:off + (i + 1) * R, :] = (blk * mul).astype(BF16)
        for r in range(4):
            for i in range(S // 4 // R):
                blk = src[pl.ds(r + 4 * i * R, R, stride=4), :]
                row0 = r * (S // 4) + i * R
                x4[row0:row0 + R, :] = blk
                dst[1, off + row0:off + row0 + R, :] = (blk * mul).astype(BF16)
        n16 = S // 16
        for c4 in range(4):
            for a in range(4):
                blk = x4[pl.ds(c4 * (S // 4) + a, n16, stride=4), :]
                row0 = (c4 * 4 + a) * n16
                dst[2, off + row0:off + row0 + n16, :] = (blk * mul).astype(BF16)

    lane = lax.broadcasted_iota(jnp.int32, (QBLOCK, LANES), 1)
    head_a = lane < HEAD_DIM

    def branch(di, dil, o_ref, l_ref):
        per_class = (S // dil) // QBLOCK

        def body(i, carry):
            base = pl.multiple_of(i * QBLOCK, QBLOCK)
            q = qs[di, pl.ds(base, QBLOCK), :]
            kk = ks[di, pl.ds(base, 2 * QBLOCK), :]
            vv = vs[di, pl.ds(base, 2 * QBLOCK), :]
            zero = jnp.zeros_like(q)
            q2 = jnp.concatenate([jnp.where(head_a, q, zero), jnp.where(head_a, zero, q)], axis=0)
            s = lax.dot_general(q2, kk, (((1,), (1,)), ((), ())), preferred_element_type=F32)
            n = i % per_class
            first = (n == 0).astype(jnp.int32)
            s = jnp.where(mask_ref[first] > 0.5, s + bias_ref[di], NEG_BIG)
            m = jnp.max(s, axis=-1, keepdims=True)
            p = jnp.exp(s - m)
            l = jnp.sum(p, axis=-1, keepdims=True)
            o2 = jnp.dot(p.astype(BF16), vv, preferred_element_type=F32) / l
            lse = m + jnp.log(l)
            o = jnp.where(head_a, o2[0:QBLOCK], o2[QBLOCK:])
            ls = jnp.where(head_a, jnp.broadcast_to(lse[0:QBLOCK], (QBLOCK, LANES)),
                           jnp.broadcast_to(lse[QBLOCK:], (QBLOCK, LANES)))
            if dil == 1:
                o_ref[pl.ds(base, QBLOCK), :] = o
                l_ref[pl.ds(base, QBLOCK), :] = ls
            else:
                c = i // per_class
                res = c if dil == 4 else 4 * (c % 4) + c // 4
                start = dil * QBLOCK * n + res
                o_ref[pl.ds(start, QBLOCK, stride=dil), :] = o
                l_ref[pl.ds(start, QBLOCK, stride=dil), :] = ls
            return carry

        lax.fori_loop(0, nblk, body, 0)

    branch(0, 1, o1, l1)
    branch(1, 4, o4, l4)
    branch(2, 16, o16, l16)

    T = 256

    def mix(i, carry):
        r0 = pl.multiple_of(i * T, T)
        la, lb, lc = l1[pl.ds(r0, T), :], l4[pl.ds(r0, T), :], l16[pl.ds(r0, T), :]
        mx = jnp.maximum(jnp.maximum(la, lb), lc)
        ea, eb, ec = jnp.exp(la - mx), jnp.exp(lb - mx), jnp.exp(lc - mx)
        num = ea * o1[pl.ds(r0, T), :] + eb * o4[pl.ds(r0, T), :] + ec * o16[pl.ds(r0, T), :]
        out_ref[pl.ds(r0, T), :] = (num / (ea + eb + ec)).astype(out_ref.dtype)
        return carry

    lax.fori_loop(0, S // T, mix, 0)


def _attn_prompt(z3, bias, mask):
    B, S, _ = z3.shape
    assert S % (16 * QBLOCK) == 0
    npair = N_ATTN_HEADS // 2
    col = lambda c0: pl.BlockSpec((None, S, LANES), lambda b, p: (b, 0, c0 // LANES + p))
    f32s = lambda: pltpu.VMEM((S, LANES), F32)
    return pl.pallas_call(
        functools.partial(_attn_prompt_kernel, seq=S),
        out_shape=jax.ShapeDtypeStruct((B, S, ATTN_WIDTH), BF16),
        grid=(B, npair),
        in_specs=[col(COL_AQ), col(COL_AK), col(COL_AV),
                  pl.BlockSpec((3, None, 2 * QBLOCK, 2 * QBLOCK), lambda b, p: (0, p, 0, 0)),
                  _const_spec((2, 2 * QBLOCK, 2 * QBLOCK))],
        out_specs=pl.BlockSpec((None, S, LANES), lambda b, p: (b, 0, p)),
        scratch_shapes=[f32s(),
                        pltpu.VMEM((3, S, LANES), BF16),
                        pltpu.VMEM((3, S + QBLOCK, LANES), BF16),
                        pltpu.VMEM((3, S + QBLOCK, LANES), BF16),
                        f32s(), f32s(), f32s(), f32s(), f32s(), f32s()],
        compiler_params=_cparams(("parallel", "parallel")),
        name="attn_prompt",
    )(z3, z3, z3, bias, mask)


TPAD = SUBLANES
TAIL = 512
STRIDE16 = 16
NGROUP = 128


def _sample_tables(rel_bias, n_past, n_tok):
    assert n_past >= max(w for w, _ in DILATED_PATTERNS) and n_past % STRIDE16 == 0 and n_tok <= 4
    assert n_past // STRIDE16 == NGROUP
    H = N_ATTN_HEADS
    t = np.arange(TPAD)[:, None]
    rb = rel_bias.astype(F32)

    def table(dj, ok):
        ok = ok & (t < n_tok)
        bk = _t5_causal_bucket(np.where(ok, dj, 0))
        bias = jnp.transpose(rb[bk], (2, 0, 1)).reshape(H * TPAD, -1)
        return bias, np.tile(ok, (H, 1)).astype(np.float32)

    r = np.arange(TAIL)[None, :]
    tn = np.arange(TPAD)[None, :]
    main_b, main_m, new_b, new_m = [], [], [], []
    for window, dil in DILATED_PATTERNS:
        nsub = window // dil
        if dil < STRIDE16:
            dj = TAIL + t - r
            ok = (dj % dil == 0) & (dj // dil >= 1) & (dj // dil <= nsub)
        else:
            res = np.arange(4 * NGROUP)[None, :] // NGROUP
            grp = np.arange(4 * NGROUP)[None, :] % NGROUP
            dj = n_past + t - (STRIDE16 * grp + res)
            ok = (dj % dil == 0) & (dj // dil >= 1) & (dj // dil <= nsub)
        bm, mm = table(dj, ok)
        djn = t - tn
        okn = (djn >= 0) & (djn % dil == 0) & (djn // dil <= nsub) & (tn < n_tok)
        bn, mn = table(djn, okn)
        main_b.append(bm), main_m.append(mm), new_b.append(bn), new_m.append(mn)
    return (jnp.stack(main_b), jnp.asarray(np.stack(main_m)), jnp.stack(new_b), jnp.asarray(np.stack(new_m)))


def _attn_sample_kernel(q_ref, kn_ref, vn_ref, kt_ref, vt_ref, ks_ref, vs_ref, mb_ref, mm_ref, nb_ref, nm_ref, out_ref):
    H = N_ATTN_HEADS
    rows = H * TPAD
    same_head = (_iota2((rows, ATTN_WIDTH), 0) >> 3) == (_iota2((rows, ATTN_WIDTH), 1) >> HEAD_SHIFT)
    q = q_ref[...] * (HEAD_DIM ** -0.5)
    qm = jnp.where(same_head, jnp.concatenate([q] * H, axis=0), 0.0).astype(BF16)
    nt = (((1,), (1,)), ((), ()))
    kn = kn_ref[...].astype(BF16)
    vn = vn_ref[...].astype(BF16)
    s_new = lax.dot_general(qm, kn, nt, preferred_element_type=F32)
    s_tail = lax.dot_general(qm, kt_ref[...].astype(BF16), nt, preferred_element_type=F32)
    s_str = jnp.concatenate(
        [lax.dot_general(qm, ks_ref[:, r * ATTN_WIDTH:(r + 1) * ATTN_WIDTH].astype(BF16), nt,
                         preferred_element_type=F32) for r in range(4)], axis=1)
    vt = vt_ref[...].astype(BF16)

    outs, lses = [], []
    for di, (_, dil) in enumerate(DILATED_PATTERNS):
        s_main = s_tail if dil < STRIDE16 else s_str
        sm = jnp.where(mm_ref[di] > 0.5, s_main + mb_ref[di], NEG_BIG)
        sn = jnp.where(nm_ref[di] > 0.5, s_new + nb_ref[di], NEG_BIG)
        m = jnp.maximum(jnp.max(sm, axis=-1, keepdims=True), jnp.max(sn, axis=-1, keepdims=True))
        pm = jnp.exp(sm - m)
        pn = jnp.exp(sn - m)
        l = jnp.sum(pm, axis=-1, keepdims=True) + jnp.sum(pn, axis=-1, keepdims=True)
        o = jnp.dot(pn.astype(BF16), vn, preferred_element_type=F32)
        pmb = pm.astype(BF16)
        if dil < STRIDE16:
            o = o + jnp.dot(pmb, vt, preferred_element_type=F32)
        else:
            for r in range(4):
                o = o + jnp.dot(pmb[:, r * NGROUP:(r + 1) * NGROUP],
                                vs_ref[:, r * ATTN_WIDTH:(r + 1) * ATTN_WIDTH].astype(BF16),
                                preferred_element_type=F32)
        outs.append(o / l)
        lses.append(m + jnp.log(l))
    mx = jnp.maximum(jnp.maximum(lses[0], lses[1]), lses[2])
    es = [jnp.exp(ls - mx) for ls in lses]
    mixed = (es[0] * outs[0] + es[1] * outs[1] + es[2] * outs[2]) / (es[0] + es[1] + es[2])
    mixed = jnp.where(same_head, mixed, 0.0)
    acc = mixed[0:TPAD]
    for h in range(1, H):
        acc = acc + mixed[h * TPAD:(h + 1) * TPAD]
    out_ref[...] = acc.astype(out_ref.dtype)


def _attn_sample(z3, cache_k, cache_v, tables):
    B, T, _ = z3.shape
    assert T == TPAD
    n_past = cache_k.shape[1]
    ck = cache_k.reshape(B, n_past, ATTN_WIDTH)
    cv = cache_v.reshape(B, n_past, ATTN_WIDTH)
    cks = cache_k.reshape(B, NGROUP, STRIDE16 * ATTN_WIDTH)
    cvs = cache_v.reshape(B, NGROUP, STRIDE16 * ATTN_WIDTH)
    mb, mm, nb, nm = tables
    new = lambda c0: pl.BlockSpec((None, TPAD, ATTN_WIDTH), lambda b: (b, 0, c0 // ATTN_WIDTH))
    tail = pl.BlockSpec((None, TAIL, ATTN_WIDTH), lambda b: (b, n_past // TAIL - 1, 0))
    strided = pl.BlockSpec((None, NGROUP, 4 * ATTN_WIDTH), lambda b: (b, 0, 0))
    return pl.pallas_call(
        _attn_sample_kernel,
        out_shape=jax.ShapeDtypeStruct((B, TPAD, ATTN_WIDTH), BF16),
        grid=(B,),
        in_specs=[new(COL_AQ), new(COL_AK), new(COL_AV), tail, tail, strided, strided,
                  _const_spec(mb.shape), _const_spec(mm.shape), _const_spec(nb.shape), _const_spec(nm.shape)],
        out_specs=pl.BlockSpec((None, TPAD, ATTN_WIDTH), lambda b: (b, 0, 0)),
        compiler_params=_cparams(("parallel",)),
        name="attn_sample",
    )(z3, z3, z3, ck, cv, cks, cvs, mb, mm, nb, nm)


QUAD = N_MLSTM_HEADS * HEAD_DIM
HEAD_SHIFT = 6


def _iota2(shape, axis):
    return lax.broadcasted_iota(jnp.int32, shape, axis)


def _log2(n):
    k = int(n).bit_length() - 1
    assert 1 << k == n
    return k


def _seg_mask(rows, row_shift, cols, col_shift):
    return (_iota2((rows, cols), 0) >> row_shift) == (_iota2((rows, cols), 1) >> col_shift)


def _cumulate_rows(x, length, op, fill):
    row = _iota2(x.shape, 0) & (length - 1)
    sh = 1
    while sh < length:
        x = op(x, jnp.where(row >= sh, pltpu.roll(x, sh, axis=0), fill))
        sh *= 2
    return x


def _hdot(a, b):
    return jnp.dot(a, b, precision=HIGHEST, preferred_element_type=F32)


def _head_rmsnorm_gate(h, seg64, g_row, gate_pre):
    ms = _hdot(h * h, seg64) * (1.0 / HEAD_DIM)
    return jax.nn.sigmoid(gate_pre) * (h * lax.rsqrt(ms + EPS) * g_row)


def _mlstm_kernel(qk_ref, v_ref, o_ref, gate_ref, conv0_ref, c0_ref, n0_ref, m0_ref, gb_ref, cw_ref, ng_ref,
                  out_ref, c_out, n_out, m_out, xp, cs, ns, ms, *, tile, chunk, n_valid):
    TS, L = tile, chunk
    t = pl.program_id(1)
    PAD = SUBLANES

    @pl.when(t == 0)
    def _():
        cs[...] = c0_ref[...]
        ns[...] = n0_ref[...]
        ms[...] = m0_ref[...]
        xp[0:PAD, :] = conv0_ref[...]

    @pl.when(t > 0)
    def _():
        xp[0:PAD, :] = xp[TS:TS + PAD, :]

    xp[PAD:PAD + TS, :] = qk_ref[...]

    lshift = _log2(L)
    seg64 = _seg_mask(QUAD, HEAD_SHIFT, QUAD, HEAD_SHIFT)
    seg64f = seg64.astype(F32)
    segl = _seg_mask(4 * L, lshift, QUAD, HEAD_SHIFT)
    seglf = segl.astype(F32)
    e_l = ((_iota2((LANES, 4 * L), 0) == (_iota2((LANES, 4 * L), 1) >> lshift))).astype(F32)
    e_64 = ((_iota2((LANES, QUAD), 0) == (_iota2((LANES, QUAD), 1) >> HEAD_SHIFT))).astype(F32)
    dmask = (_iota2((L, 4 * L), 1) & (L - 1)) == _iota2((L, 4 * L), 0)
    causal = (_iota2((L, 4 * L), 1) & (L - 1)) <= _iota2((L, 4 * L), 0)
    ones_ll = jnp.ones((L, L), F32)
    lane = _iota2((L, LANES), 1)
    row = _iota2((L, LANES), 0)
    head_lane = lane < N_MLSTM_HEADS
    cw = cw_ref[...]
    gb = gb_ref[...]
    ng = ng_ref[...]

    def chunk_body(c, carry):
        r0 = pl.multiple_of(c * L, L)
        win = xp[pl.ds(r0, L + PAD), :]
        acc = win[PAD:PAD + L] * cw[CONV_WIDTH - 1:CONV_WIDTH, :]
        for j in range(1, CONV_WIDTH):
            acc = acc + win[PAD - j:PAD - j + L] * cw[CONV_WIDTH - 1 - j:CONV_WIDTH - j, :]
        qk = acc * jax.nn.sigmoid(acc)
        q = qk[:, 0:QUAD]
        k = qk[:, QUAD:] * (HEAD_DIM ** -0.5)
        v = v_ref[pl.ds(r0, L), :]

        g = gate_ref[pl.ds(r0, L), :] + gb
        ig = g
        lf = jnp.minimum(g, 0.0) - jnp.log(1.0 + jnp.exp(-jnp.abs(g)))
        if n_valid < TS:
            ok = (row + r0) < n_valid
            ig = jnp.where(ok, ig, NEG_BIG)
            lf = jnp.where(ok, lf, 0.0)
        b = pltpu.roll(_cumulate_rows(lf, L, jnp.add, 0.0), GATE_F_LANE, axis=1)
        a = ig - b
        cm = _cumulate_rows(a, L, jnp.maximum, -jnp.inf)
        mprev = ms[...]
        M = jnp.maximum(cm, mprev)
        mt = b + M
        gg = jnp.exp(mprev - M)
        emt = jnp.exp(-mt)
        ML = M[L - 1:L, :]
        decay = jnp.exp(a - ML)
        z0 = jnp.zeros_like(a)
        st1 = jnp.concatenate([jnp.where(head_lane, a, z0), jnp.where(head_lane, M, z0)], axis=0)
        st2 = jnp.concatenate([jnp.where(head_lane, gg, z0), jnp.where(head_lane, emt, z0),
                               jnp.where(head_lane, decay, z0)], axis=0)
        ex1 = _hdot(st1, e_l)
        ex2 = _hdot(st2, e_64)
        aexp, Mexp = ex1[0:L], ex1[L:2 * L]
        gexp, emtexp, decexp = ex2[0:L], ex2[L:2 * L], ex2[2 * L:3 * L]
        arow = _hdot(ones_ll, jnp.where(dmask, aexp, 0.0))

        qb = q.astype(BF16)
        kb = k.astype(BF16)
        vb = v.astype(BF16)
        zb = jnp.zeros((4 * L, QUAD), BF16)
        kbd = jnp.where(segl, jnp.concatenate([kb] * N_MLSTM_HEADS, axis=0), zb)
        vbd = jnp.where(segl, jnp.concatenate([vb] * N_MLSTM_HEADS, axis=0), zb)
        qkt = lax.dot_general(qb, kbd, (((1,), (1,)), ((), ())), preferred_element_type=F32)
        wts = jnp.where(causal, jnp.exp(arow - Mexp), 0.0) * qkt
        cmat = cs[...]
        num = jnp.dot(wts.astype(BF16), vbd, preferred_element_type=F32)
        num = num + gexp * jnp.dot(qb, cmat.astype(BF16), preferred_element_type=F32)
        den = _hdot(wts, seglf) + gexp * _hdot(q * ns[...], seg64f)
        h = num / jnp.maximum(jnp.abs(den), emtexp)
        out_ref[pl.ds(r0, L), :] = _head_rmsnorm_gate(h, seg64f, ng, o_ref[pl.ds(r0, L), :]).astype(out_ref.dtype)

        kd = k * decexp
        u = lax.dot_general(kd.astype(BF16), vb, (((0,), (0,)), ((), ())), preferred_element_type=F32)
        glexp = gexp[L - 1:L, :]
        cs[...] = glexp * cmat + jnp.where(seg64, u, 0.0)
        ns[...] = glexp * ns[...] + jnp.sum(kd, axis=0, keepdims=True)
        ms[...] = mt[L - 1:L, :]
        return carry

    lax.fori_loop(0, TS // L, chunk_body, 0)
    c_out[...] = cs[...]
    n_out[...] = ns[...]
    m_out[...] = ms[...]


def _mlstm(z3, conv0, c0bd, n0, m0, gate_b, conv_w, norm_g, n_valid, chunk):
    B, S, _ = z3.shape
    ts = min(512, S)
    assert S % ts == 0 and ts % chunk == 0
    blk = lambda w, c0: pl.BlockSpec((None, ts, w), lambda b, t: (b, t, c0 // w))
    per_b = lambda r, w: pl.BlockSpec((None, r, w), lambda b, t: (b, 0, 0))
    return pl.pallas_call(
        functools.partial(_mlstm_kernel, tile=ts, chunk=chunk, n_valid=n_valid),
        out_shape=[jax.ShapeDtypeStruct((B, S, QUAD), BF16),
                   jax.ShapeDtypeStruct((B, QUAD, QUAD), F32),
                   jax.ShapeDtypeStruct((B, 1, QUAD), F32),
                   jax.ShapeDtypeStruct((B, 1, LANES), F32)],
        grid=(B, S // ts),
        in_specs=[blk(2 * QUAD, COL_MQK), blk(QUAD, COL_MV), blk(QUAD, COL_MO), blk(LANES, COL_GATE),
                  per_b(SUBLANES, 2 * QUAD), per_b(QUAD, QUAD), per_b(1, QUAD), per_b(1, LANES),
                  _const_spec((1, LANES)), _const_spec((CONV_WIDTH, 2 * QUAD)), _const_spec((1, QUAD))],
        out_specs=[pl.BlockSpec((None, ts, QUAD), lambda b, t: (b, t, 0)),
                   per_b(QUAD, QUAD), per_b(1, QUAD), per_b(1, LANES)],
        scratch_shapes=[pltpu.VMEM((ts + 2 * SUBLANES, 2 * QUAD), F32),
                        pltpu.VMEM((QUAD, QUAD), F32), pltpu.VMEM((1, QUAD), F32), pltpu.VMEM((1, LANES), F32)],
        compiler_params=_cparams(("parallel", "arbitrary")),
        name="mlstm",
    )(z3, z3, z3, z3, conv0, c0bd, n0, m0, gate_b, conv_w, norm_g)


def _hgrn_kernel(q_ref, f_ref, i_ref, g_ref, s0_ref, lb_ref, ng_ref, out_ref, s_out, st, *, tile, chunk, n_valid):
    TS, L = tile, chunk
    t = pl.program_id(1)

    @pl.when(t == 0)
    def _():
        st[...] = s0_ref[...]

    seg64 = _seg_mask(QUAD, HEAD_SHIFT, QUAD, HEAD_SHIFT)
    seg64f = seg64.astype(F32)
    seg64b = seg64.astype(BF16)
    row = _iota2((L, QUAD), 0)
    lb = lb_ref[...]
    ng = ng_ref[...]

    def chunk_body(c, carry):
        r0 = pl.multiple_of(c * L, L)
        hq = q_ref[pl.ds(r0, L), :]
        q = hq * jax.nn.sigmoid(hq)
        f = lb + (1.0 - lb) * jax.nn.sigmoid(f_ref[pl.ds(r0, L), :])
        logf = jnp.log(f)
        kk = 1.0 - f
        iv = i_ref[pl.ds(r0, L), :]
        if n_valid < TS:
            ok = (row + r0) < n_valid
            logf = jnp.where(ok, logf, 0.0)
            kk = jnp.where(ok, kk, 0.0)
        b = _cumulate_rows(logf, L, jnp.add, 0.0)
        parts = []
        for j in range(L):
            dec = jnp.exp(jnp.where(row >= j, b - b[j:j + 1, :], NEG_BIG))
            parts.append(dec * q * kk[j:j + 1, :])
        tstack = jnp.concatenate(parts, axis=0).astype(BF16)
        y = jnp.dot(tstack, seg64b, preferred_element_type=F32)
        o = y[0:L] * iv[0:1, :]
        for j in range(1, L):
            o = o + y[j * L:(j + 1) * L] * iv[j:j + 1, :]
        smat = st[...]
        o = o + lax.dot_general((q * jnp.exp(b)).astype(BF16), smat.astype(BF16), (((1,), (1,)), ((), ())),
                                preferred_element_type=F32)
        out_ref[pl.ds(r0, L), :] = _head_rmsnorm_gate(o, seg64f, ng, g_ref[pl.ds(r0, L), :]).astype(out_ref.dtype)
        bl = b[L - 1:L, :]
        ktil = kk * jnp.exp(bl - b)
        u = lax.dot_general(iv.astype(BF16), ktil.astype(BF16), (((0,), (0,)), ((), ())),
                            preferred_element_type=F32)
        st[...] = smat * jnp.exp(bl) + jnp.where(seg64, u, 0.0)
        return carry

    lax.fori_loop(0, TS // L, chunk_body, 0)
    s_out[...] = st[...]


def _hgrn(z3, s0t, lb, norm_g, n_valid, chunk):
    B, S, _ = z3.shape
    ts = min(512, S)
    assert S % ts == 0 and ts % chunk == 0
    blk = lambda c0: pl.BlockSpec((None, ts, QUAD), lambda b, t: (b, t, c0 // QUAD))
    per_b = pl.BlockSpec((None, QUAD, QUAD), lambda b, t: (b, 0, 0))
    return pl.pallas_call(
        functools.partial(_hgrn_kernel, tile=ts, chunk=chunk, n_valid=n_valid),
        out_shape=[jax.ShapeDtypeStruct((B, S, QUAD), BF16), jax.ShapeDtypeStruct((B, QUAD, QUAD), F32)],
        grid=(B, S // ts),
        in_specs=[blk(COL_HQ), blk(COL_HF), blk(COL_HI), blk(COL_HG), per_b,
                  _const_spec((1, QUAD)), _const_spec((1, QUAD))],
        out_specs=[pl.BlockSpec((None, ts, QUAD), lambda b, t: (b, t, 0)), per_b],
        scratch_shapes=[pltpu.VMEM((QUAD, QUAD), F32)],
        compiler_params=_cparams(("parallel", "arbitrary")),
        name="hgrn",
    )(z3, z3, z3, z3, s0t, lb, norm_g)


PROJ_SPLITS = (ATTN_WIDTH, ATTN_WIDTH, ATTN_WIDTH,
               MLSTM_WIDTH, MLSTM_WIDTH, MLSTM_WIDTH, MLSTM_WIDTH, N_MLSTM_HEADS, N_MLSTM_HEADS,
               HGRN_WIDTH, HGRN_WIDTH, HGRN_WIDTH, HGRN_WIDTH)


def _prep_w_in(w):
    cuts = [int(c) for c in np.cumsum(PROJ_SPLITS)[:-1]]
    aq, ak, av, mq, mk, mv, mo, mi, mf, hq, hf, hi, hg = jnp.split(w, cuts, axis=1)
    zpad = lambda n: jnp.zeros((w.shape[0], n), w.dtype)
    gate = jnp.concatenate([mi, zpad(GATE_F_LANE - N_MLSTM_HEADS), mf, zpad(LANES - GATE_F_LANE - N_MLSTM_HEADS)], axis=1)
    return jnp.concatenate([aq, ak, av, mq, mk, mv, mo, hq, hf, hi, hg, gate], axis=1).astype(BF16)


HGRN_SUBCHUNK = 16


class _LayerWeights(NamedTuple):
    g_mix: jax.Array
    w_in: jax.Array
    gate_b: jax.Array
    conv_w: jax.Array
    mlstm_g: jax.Array
    lb: jax.Array
    hgrn_g: jax.Array
    w_out: jax.Array
    g_ffn: jax.Array
    w_up: jax.Array
    w_down: jax.Array
    g_final: jax.Array


def _gate_row(gate_b):
    row = jnp.zeros((1, LANES), F32)
    row = row.at[0, 0:N_MLSTM_HEADS].set(gate_b[0].astype(F32))
    return row.at[0, GATE_F_LANE:GATE_F_LANE + N_MLSTM_HEADS].set(gate_b[1].astype(F32))


def _embed_block_diag(s):
    b, h, e, f = s.shape
    out = jnp.zeros((b, h, e, h, f), F32)
    for i in range(h):
        out = out.at[:, i, :, i, :].set(s[:, i].astype(F32))
    return out.reshape(b, h * e, h * f)


def _extract_block_diag(s, h):
    b, r, c = s.shape
    s5 = s.reshape(b, h, r // h, h, c // h)
    return jnp.stack([s5[:, i, :, i, :] for i in range(h)], axis=1)


def _trunk_layer(x2d, B, S, n_valid, attn_fn, states, w, final, chunk_m, chunk_h):
    conv_buf, C0, n0, m0, S0 = states
    z = _inproj(x2d, w.g_mix, w.w_in)
    z3 = z.reshape(B, S, ZW)
    attn = attn_fn(z3)
    conv0 = jnp.zeros((B, SUBLANES, 2 * MLSTM_WIDTH), F32).at[:, SUBLANES - (CONV_WIDTH - 1):].set(conv_buf.astype(F32))
    m0r = jnp.zeros((B, 1, LANES), F32).at[:, 0, 0:N_MLSTM_HEADS].set(m0.astype(F32))
    ml, cbd, n, m = _mlstm(z3, conv0, _embed_block_diag(C0), n0.astype(F32).reshape(B, 1, MLSTM_WIDTH), m0r,
                           w.gate_b, w.conv_w, w.mlstm_g, n_valid, chunk_m)
    s0t = _embed_block_diag(jnp.swapaxes(S0, -1, -2))
    hg, st = _hgrn(z3, s0t, w.lb, w.hgrn_g, n_valid, chunk_h)
    n2 = B * S
    y = _outffn(x2d, attn.reshape(n2, ATTN_WIDTH), ml.reshape(n2, MLSTM_WIDTH), hg.reshape(n2, HGRN_WIDTH),
                w.w_out, w.g_ffn, w.w_up, w.w_down, w.g_final, final)
    k_rows = z3[:, :, COL_AK:COL_AK + ATTN_WIDTH].reshape(B, S, N_ATTN_HEADS, HEAD_DIM)
    v_rows = z3[:, :, COL_AV:COL_AV + ATTN_WIDTH].reshape(B, S, N_ATTN_HEADS, HEAD_DIM)
    conv_new = z3[:, n_valid - (CONV_WIDTH - 1):n_valid, COL_MQK:COL_MQK + 2 * MLSTM_WIDTH]
    c_new = _extract_block_diag(cbd, N_MLSTM_HEADS)
    s_new = jnp.swapaxes(_extract_block_diag(st, N_HGRN_HEADS), -1, -2)
    return y, (k_rows, v_rows, conv_new, c_new, n.reshape(B, N_MLSTM_HEADS, HEAD_DIM), m[:, 0, 0:N_MLSTM_HEADS], s_new)


def kernel(x_prompt, x_sample, cache_attn_k, cache_attn_v, state_mlstm_conv, state_mlstm_C, state_mlstm_n, state_mlstm_m, state_hgrn_S, rel_bias, g_mix, w_in, mlstm_gate_b, mlstm_conv_w, mlstm_norm_g, hgrn_lb_raw, hgrn_norm_g, w_out, g_ffn, w_up, w_down, g_final):
    depth = w_in.shape[0]
    bp, sp, _ = x_prompt.shape
    bs, ts, _ = x_sample.shape
    n_keep = min(max(w for w, _ in DILATED_PATTERNS), sp)
    n_past = cache_attn_k.shape[2]
    lb_p = jax.nn.softmax(hgrn_lb_raw.astype(F32), axis=0)
    hgrn_lb = jnp.cumsum(lb_p, axis=0) - lb_p[0]
    p_tables = _prompt_tables(rel_bias)
    s_tables = _sample_tables(rel_bias, n_past, ts)

    hp = x_prompt.reshape(bp * sp, D_MODEL)
    hs = jnp.zeros((bs, TPAD, D_MODEL), F32).at[:, :ts].set(x_sample).reshape(bs * TPAD, D_MODEL)
    zero_states = (jnp.zeros((bp, CONV_WIDTH - 1, 2 * MLSTM_WIDTH), F32),
                   jnp.zeros((bp, N_MLSTM_HEADS, HEAD_DIM, HEAD_DIM), F32),
                   jnp.zeros((bp, N_MLSTM_HEADS, HEAD_DIM), F32),
                   jnp.zeros((bp, N_MLSTM_HEADS), F32),
                   jnp.zeros((bp, N_HGRN_HEADS, HEAD_DIM, HEAD_DIM), F32))
    p_states, s_states = [], []
    for l in range(depth):
        final = l == depth - 1
        weights = _LayerWeights(
            g_mix=g_mix[l][None], w_in=_prep_w_in(w_in[l]), gate_b=_gate_row(mlstm_gate_b[l]),
            conv_w=mlstm_conv_w[l], mlstm_g=mlstm_norm_g[l][None], lb=hgrn_lb[l][None], hgrn_g=hgrn_norm_g[l][None],
            w_out=w_out[l].astype(BF16), g_ffn=g_ffn[l][None], w_up=w_up[l].astype(BF16),
            w_down=w_down[l].astype(BF16), g_final=g_final[None])
        hp, st = _trunk_layer(hp, bp, sp, sp, functools.partial(_attn_prompt, bias=p_tables[0], mask=p_tables[1]),
                              zero_states, weights, final, MLSTM_CHUNK, HGRN_SUBCHUNK)
        k, v, conv, C, n, m, Sn = st
        p_states.append((k[:, sp - n_keep:], v[:, sp - n_keep:], conv, C, n, m, Sn))
        sample_attn = functools.partial(_attn_sample, cache_k=cache_attn_k[l], cache_v=cache_attn_v[l], tables=s_tables)
        states = (state_mlstm_conv[l], state_mlstm_C[l], state_mlstm_n[l], state_mlstm_m[l], state_hgrn_S[l])
        hs, st = _trunk_layer(hs, bs, TPAD, ts, sample_attn, states, weights, final, TPAD, TPAD)
        k, v, conv, C, n, m, Sn = st
        s_states.append((k[:, :ts], v[:, :ts], conv, C, n, m, Sn))
    y_prompt = hp.reshape(bp, sp, D_MODEL)
    y_sample = hs.reshape(bs, TPAD, D_MODEL)[:, :ts]
    p_out = [jnp.stack(a) for a in zip(*p_states)]
    s_out = [jnp.stack(a) for a in zip(*s_states)]
    return (y_prompt, y_sample, *p_out, *s_out)
```

```python
import functools
from typing import NamedTuple

import jax
import jax.numpy as jnp
import numpy as np
from jax import lax
from jax.experimental import pallas as pl
from jax.experimental.pallas import tpu as pltpu

F32 = jnp.float32
BF16 = jnp.bfloat16

D_MODEL = 1024
HEAD_DIM = 64
N_ATTN_HEADS = 8
N_MLSTM_HEADS = 4
N_HGRN_HEADS = 4
ATTN_WIDTH = N_ATTN_HEADS * HEAD_DIM
MLSTM_WIDTH = N_MLSTM_HEADS * HEAD_DIM
HGRN_WIDTH = N_HGRN_HEADS * HEAD_DIM
DILATED_PATTERNS = ((128, 1), (512, 4), (2048, 16))
QBLOCK = 128
N_REL_BUCKETS = 32
REL_MAX_DISTANCE = 2048
CONV_WIDTH = 4
MLSTM_CHUNK = 64
D_FF = 4 * D_MODEL
EPS = 1e-6
NEG_BIG = -1e30

LANES = 128
SUBLANES = 8

COL_AQ = 0
COL_AK = COL_AQ + ATTN_WIDTH
COL_AV = COL_AK + ATTN_WIDTH
COL_MQK = COL_AV + ATTN_WIDTH
COL_MV = COL_MQK + 2 * MLSTM_WIDTH
COL_MO = COL_MV + MLSTM_WIDTH
COL_HQ = COL_MO + MLSTM_WIDTH
COL_HF = COL_HQ + HGRN_WIDTH
COL_HI = COL_HF + HGRN_WIDTH
COL_HG = COL_HI + HGRN_WIDTH
COL_GI = COL_HG + HGRN_WIDTH
COL_GF = COL_GI + MLSTM_WIDTH
ZW = COL_GF + MLSTM_WIDTH

VMEM_LIMIT = 56 * 1024 * 1024


def _cparams(sem, vmem=VMEM_LIMIT):
    return pltpu.CompilerParams(dimension_semantics=sem, vmem_limit_bytes=vmem)


def _const_spec(shape):
    nd = len(shape)
    return pl.BlockSpec(shape, lambda *_: (0,) * nd, pipeline_mode=pl.Buffered(1))


def _inproj_kernel(x_ref, g_ref, w_ref, *rest, tiles_per_seq, first_keep):
    x = x_ref[...]
    ms = jnp.mean(x * x, axis=-1, keepdims=True)
    xn = ((x * lax.rsqrt(ms + EPS)) * g_ref[...]).astype(BF16)
    if first_keep is None:
        (z_ref,) = rest
        z_ref[...] = jnp.dot(xn, w_ref[...], preferred_element_type=F32)
        return
    wkv_ref, z_ref, kt_ref, vt_ref = rest
    z_ref[...] = jnp.dot(xn, w_ref[...], preferred_element_type=F32)

    @pl.when(pl.program_id(0) % tiles_per_seq >= first_keep)
    def _():
        kv = lax.dot_general(wkv_ref[...], xn, (((1,), (1,)), ((), ())), preferred_element_type=F32)
        kt_ref[...] = kv[0:ATTN_WIDTH]
        vt_ref[...] = kv[ATTN_WIDTH:]


def _inproj(x2d, g, w, wkv_t=None, seq=None, n_keep=None):
    n = x2d.shape[0]
    tm = min(512, n)
    assert n % tm == 0
    x_spec = pl.BlockSpec((tm, D_MODEL), lambda i: (i, 0))
    z_spec = pl.BlockSpec((tm, ZW), lambda i: (i, 0))
    z_shape = jax.ShapeDtypeStruct((n, ZW), F32)
    if wkv_t is None:
        return pl.pallas_call(
            functools.partial(_inproj_kernel, tiles_per_seq=None, first_keep=None),
            out_shape=z_shape, grid=(n // tm,),
            in_specs=[x_spec, _const_spec((1, D_MODEL)), _const_spec((D_MODEL, ZW))],
            out_specs=z_spec, compiler_params=_cparams(("parallel",)), name="inproj",
        )(x2d, g, w)
    assert seq % tm == 0 and n_keep % tm == 0
    tps, first_keep = seq // tm, (seq - n_keep) // tm
    t_spec = pl.BlockSpec((None, ATTN_WIDTH, tm), lambda i: (i // tps, 0, jnp.maximum(i % tps - first_keep, 0)))
    t_shape = jax.ShapeDtypeStruct((n // seq, ATTN_WIDTH, n_keep), F32)
    return pl.pallas_call(
        functools.partial(_inproj_kernel, tiles_per_seq=tps, first_keep=first_keep),
        out_shape=[z_shape, t_shape, t_shape], grid=(n // tm,),
        in_specs=[x_spec, _const_spec((1, D_MODEL)), _const_spec((D_MODEL, ZW)), _const_spec((2 * ATTN_WIDTH, D_MODEL))],
        out_specs=[z_spec, t_spec, t_spec], compiler_params=_cparams(("arbitrary",)), name="inproj_kt",
    )(x2d, g, w, wkv_t)


FF_CHUNK = 1024


def _outffn_kernel(x_ref, a_ref, m_ref, h_ref, wo_ref, gf_ref, wu_ref, wd_ref, gl_ref, y_ref, xn_sc, *, final):
    x1 = x_ref[...]
    x1 = x1 + jnp.dot(a_ref[...], wo_ref[0:ATTN_WIDTH, :], preferred_element_type=F32)
    x1 = x1 + jnp.dot(m_ref[...], wo_ref[ATTN_WIDTH:ATTN_WIDTH + MLSTM_WIDTH, :], preferred_element_type=F32)
    x1 = x1 + jnp.dot(h_ref[...], wo_ref[ATTN_WIDTH + MLSTM_WIDTH:, :], preferred_element_type=F32)
    ms = jnp.mean(x1 * x1, axis=-1, keepdims=True)
    xn_sc[...] = ((x1 * lax.rsqrt(ms + EPS)) * gf_ref[...]).astype(BF16)
    y_ref[...] = x1
    for c in range(D_FF // FF_CHUNK):
        u = jnp.dot(xn_sc[...], wu_ref[:, c * FF_CHUNK:(c + 1) * FF_CHUNK], preferred_element_type=F32)
        hh = jnp.square(jnp.maximum(u, 0.0)).astype(BF16)
        y_ref[...] += jnp.dot(hh, wd_ref[c * FF_CHUNK:(c + 1) * FF_CHUNK, :], preferred_element_type=F32)
    if final:
        x2 = y_ref[...]
        ms2 = jnp.mean(x2 * x2, axis=-1, keepdims=True)
        y_ref[...] = (x2 * lax.rsqrt(ms2 + EPS)) * gl_ref[...]


def _outffn(x2d, attn, ml, hg, w_out, g_ffn, w_up, w_down, g_final, final):
    n = x2d.shape[0]
    tm = min(512, n)
    assert n % tm == 0
    row = lambda w: pl.BlockSpec((tm, w), lambda i: (i, 0))
    return pl.pallas_call(
        functools.partial(_outffn_kernel, final=final),
        out_shape=jax.ShapeDtypeStruct((n, D_MODEL), F32),
        grid=(n // tm,),
        in_specs=[row(D_MODEL), row(ATTN_WIDTH), row(MLSTM_WIDTH), row(HGRN_WIDTH),
                  _const_spec((D_MODEL, D_MODEL)), _const_spec((1, D_MODEL)),
                  _const_spec((D_MODEL, D_FF)), _const_spec((D_FF, D_MODEL)), _const_spec((1, D_MODEL))],
        out_specs=row(D_MODEL),
        scratch_shapes=[pltpu.VMEM((tm, D_MODEL), BF16)],
        compiler_params=_cparams(("parallel",)),
        name="outffn",
    )(x2d, attn, ml, hg, w_out, g_ffn, w_up, w_down, g_final)


def _t5_causal_bucket(dist):
    n = np.asarray(dist).astype(np.int32)
    max_exact = N_REL_BUCKETS // 2
    scaled = np.log(np.maximum(n, 1) / max_exact) / np.log(REL_MAX_DISTANCE / max_exact)
    large = np.minimum(max_exact + (scaled * (N_REL_BUCKETS - max_exact)).astype(np.int32), N_REL_BUCKETS - 1)
    return np.where(n < max_exact, n, large).astype(np.int32)


BIAS_DIST = 2304


def _bias_by_distance(rel_bias):
    assert BIAS_DIST > max(w for w, _ in DILATED_PATTERNS) + SUBLANES
    return rel_bias.astype(F32)[_t5_causal_bucket(np.arange(BIAS_DIST))].T


def _prompt_tables(bias_dist):
    H = N_ATTN_HEADS
    a = np.arange(QBLOCK)[:, None]
    b = np.arange(2 * QBLOCK)[None, :]
    rel = QBLOCK + a - b
    band = (rel >= 0) & (rel <= QBLOCK)
    period = 3 * QBLOCK
    biases = []
    for window, dil in DILATED_PATTERNS:
        nsub = window // dil
        v = bias_dist[:, 0:nsub * dil + 1:dil]
        v = jnp.concatenate([v, jnp.broadcast_to(v[:, -1:], (H, 2 * QBLOCK - nsub - 1))], axis=1)
        r = jnp.concatenate([v[:, ::-1], jnp.broadcast_to(v[:, 0:1], (H, period - 2 * QBLOCK))], axis=1)
        skew = jnp.tile(r, (1, QBLOCK + 2))[:, :QBLOCK * (period + 1)].reshape(H, QBLOCK, period + 1)
        bias = skew[:, ::-1, :2 * QBLOCK]
        biases.append(bias.reshape(H // 2, 2 * QBLOCK, 2 * QBLOCK))
    mask = np.stack([np.tile(band, (2, 1)), np.tile(band & (b >= QBLOCK), (2, 1))]).astype(np.float32)
    return jnp.stack(biases), jnp.asarray(mask)


ATTN_UNROLL = 8


def _attn_prompt_kernel(q_ref, k_ref, v_ref, bias_ref, mask_ref, out_ref,
                        x4, qs, ks, vs, o1, o4, o16, l1, l4, l16, *, seq):
    S = seq
    nblk = S // QBLOCK
    scale = HEAD_DIM ** -0.5
    zpad = jnp.zeros((QBLOCK, LANES), BF16)
    for di in range(3):
        ks[di, 0:QBLOCK, :] = zpad
        vs[di, 0:QBLOCK, :] = zpad

    R = 512
    for src, dst, off, mul in ((q_ref, qs, 0, scale), (k_ref, ks, QBLOCK, 1.0), (v_ref, vs, QBLOCK, 1.0)):
        for i in range(S // R):
            blk = src[i * R:(i + 1) * R, :]
            dst[0, off + i * R:off + (i + 1) * R, :] = (blk * mul).astype(BF16)
        for r in range(4):
            for i in range(S // 4 // R):
                blk = src[pl.ds(r + 4 * i * R, R, stride=4), :]
                row0 = r * (S // 4) + i * R
                x4[row0:row0 + R, :] = blk
                dst[1, off + row0:off + row0 + R, :] = (blk * mul).astype(BF16)
        n16 = S // 16
        for c4 in range(4):
            for a in range(4):
                blk = x4[pl.ds(c4 * (S // 4) + a, n16, stride=4), :]
                row0 = (c4 * 4 + a) * n16
                dst[2, off + row0:off + row0 + n16, :] = (blk * mul).astype(BF16)

    lane = lax.broadcasted_iota(jnp.int32, (QBLOCK, LANES), 1)
    head_a = lane < HEAD_DIM

    def branch(di, dil, o_ref, l_ref):
        per_class = (S // dil) // QBLOCK

        def body(i, carry):
            base = pl.multiple_of(i * QBLOCK, QBLOCK)
            q = qs[di, pl.ds(base, QBLOCK), :]
            kk = ks[di, pl.ds(base, 2 * QBLOCK), :]
            vv = vs[di, pl.ds(base, 2 * QBLOCK), :]
            zero = jnp.zeros_like(q)
            q2 = jnp.concatenate([jnp.where(head_a, q, zero), jnp.where(head_a, zero, q)], axis=0)
            s = lax.dot_general(q2, kk, (((1,), (1,)), ((), ())), preferred_element_type=F32)
            n = i % per_class
            first = (n == 0).astype(jnp.int32)
            s = jnp.where(mask_ref[first] > 0.5, s + bias_ref[di], NEG_BIG)
            m = jnp.max(s, axis=-1, keepdims=True)
            p = jnp.exp(s - m)
            l = jnp.sum(p, axis=-1, keepdims=True)
            o2 = jnp.dot(p.astype(BF16), vv, preferred_element_type=F32) / l
            lse = m + jnp.log(l)
            o = jnp.where(head_a, o2[0:QBLOCK], o2[QBLOCK:])
            ls = jnp.where(head_a, jnp.broadcast_to(lse[0:QBLOCK], (QBLOCK, LANES)),
                           jnp.broadcast_to(lse[QBLOCK:], (QBLOCK, LANES)))
            if dil == 1:
                o_ref[pl.ds(base, QBLOCK), :] = o
                l_ref[pl.ds(base, QBLOCK), :] = ls
            else:
                c = i // per_class
                res = c if dil == 4 else 4 * (c % 4) + c // 4
                start = dil * QBLOCK * n + res
                o_ref[pl.ds(start, QBLOCK, stride=dil), :] = o
                l_ref[pl.ds(start, QBLOCK, stride=dil), :] = ls
            return carry

        lax.fori_loop(0, nblk, body, 0, unroll=ATTN_UNROLL)

    branch(0, 1, o1, l1)
    branch(1, 4, o4, l4)
    branch(2, 16, o16, l16)

    T = 256

    def mix(i, carry):
        r0 = pl.multiple_of(i * T, T)
        la, lb, lc = l1[pl.ds(r0, T), :], l4[pl.ds(r0, T), :], l16[pl.ds(r0, T), :]
        mx = jnp.maximum(jnp.maximum(la, lb), lc)
        ea, eb, ec = jnp.exp(la - mx), jnp.exp(lb - mx), jnp.exp(lc - mx)
        num = ea * o1[pl.ds(r0, T), :] + eb * o4[pl.ds(r0, T), :] + ec * o16[pl.ds(r0, T), :]
        out_ref[pl.ds(r0, T), :] = (num / (ea + eb + ec)).astype(out_ref.dtype)
        return carry

    lax.fori_loop(0, S // T, mix, 0)


def _attn_prompt(z3, bias, mask):
    B, S, _ = z3.shape
    assert S % (16 * QBLOCK) == 0
    npair = N_ATTN_HEADS // 2
    col = lambda c0: pl.BlockSpec((None, S, LANES), lambda b, p: (b, 0, c0 // LANES + p))
    f32s = lambda: pltpu.VMEM((S, LANES), F32)
    return pl.pallas_call(
        functools.partial(_attn_prompt_kernel, seq=S),
        out_shape=jax.ShapeDtypeStruct((B, S, ATTN_WIDTH), BF16),
        grid=(B, npair),
        in_specs=[col(COL_AQ), col(COL_AK), col(COL_AV),
                  pl.BlockSpec((3, None, 2 * QBLOCK, 2 * QBLOCK), lambda b, p: (0, p, 0, 0)),
                  _const_spec((2, 2 * QBLOCK, 2 * QBLOCK))],
        out_specs=pl.BlockSpec((None, S, LANES), lambda b, p: (b, 0, p)),
        scratch_shapes=[f32s(),
                        pltpu.VMEM((3, S, LANES), BF16),
                        pltpu.VMEM((3, S + QBLOCK, LANES), BF16),
                        pltpu.VMEM((3, S + QBLOCK, LANES), BF16),
                        f32s(), f32s(), f32s(), f32s(), f32s(), f32s()],
        compiler_params=_cparams(("parallel", "parallel")),
        name="attn_prompt",
    )(z3, z3, z3, bias, mask)


TPAD = SUBLANES
TAIL = 512


def _sample_tables(bias_dist, n_past, n_tok):
    assert n_past >= max(w for w, _ in DILATED_PATTERNS) and n_tok <= TPAD and TAIL >= DILATED_PATTERNS[1][0]
    H = N_ATTN_HEADS
    t = np.arange(TPAD)[:, None]

    def by_row(width):
        rows = [bias_dist[:, tt + 1:tt + 1 + width][:, ::-1] for tt in range(TPAD)]
        return jnp.stack(rows, axis=1).reshape(H * TPAD, width)

    bias_tail, bias_full = by_row(TAIL), by_row(n_past)
    new_rows = [jnp.concatenate([bias_dist[:, 0:tt + 1][:, ::-1],
                                 jnp.broadcast_to(bias_dist[:, 0:1], (H, TPAD - tt - 1))], axis=1) for tt in range(TPAD)]
    bias_new = jnp.stack(new_rows, axis=1).reshape(H * TPAD, TPAD)

    def valid(dj, dil, nsub, lo):
        ok = (dj % dil == 0) & (dj // dil >= lo) & (dj // dil <= nsub) & (t < n_tok)
        return np.tile(ok, (H, 1)).astype(np.float32)

    tn = np.arange(TPAD)[None, :]
    m_tail, m_new = [], []
    for window, dil in DILATED_PATTERNS:
        nsub = window // dil
        if window <= TAIL:
            m_tail.append(valid(TAIL + t - np.arange(TAIL)[None, :], dil, nsub, 1))
        else:
            m_full = valid(n_past + t - np.arange(n_past)[None, :], dil, nsub, 1)
        m_new.append(valid(t - tn, dil, nsub, 0) * (tn < n_tok))
    return (bias_tail, bias_full, bias_new, jnp.asarray(np.stack(m_tail)), jnp.asarray(m_full),
            jnp.asarray(np.stack(m_new).astype(np.float32)))


def _attn_sample_kernel(q_ref, kn_ref, vn_ref, kt_ref, vt_ref, bt_ref, bf_ref, bn_ref, mt_ref, mf_ref, mn_ref,
                        out_ref, *, n_past):
    H = N_ATTN_HEADS
    rows = H * TPAD
    same_head = (_iota2((rows, ATTN_WIDTH), 0) >> 3) == (_iota2((rows, ATTN_WIDTH), 1) >> HEAD_SHIFT)
    q = q_ref[...] * (HEAD_DIM ** -0.5)
    qm = jnp.where(same_head, jnp.concatenate([q] * H, axis=0), 0.0).astype(BF16)
    nt = (((1,), (1,)), ((), ()))
    kn = kn_ref[...].astype(BF16)
    vn = vn_ref[...].astype(BF16)
    kt = kt_ref[...].astype(BF16)
    vt = vt_ref[...].astype(BF16)
    s_new = lax.dot_general(qm, kn, nt, preferred_element_type=F32)
    s_full = jnp.dot(qm, kt, preferred_element_type=F32)
    s_tail = s_full[:, n_past - TAIL:]

    outs, lses = [], []
    for di, (window, _) in enumerate(DILATED_PATTERNS):
        if window <= TAIL:
            sm = jnp.where(mt_ref[di] > 0.5, s_tail + bt_ref[...], NEG_BIG)
            vmain = vt[:, n_past - TAIL:]
        else:
            sm = jnp.where(mf_ref[...] > 0.5, s_full + bf_ref[...], NEG_BIG)
            vmain = vt
        sn = jnp.where(mn_ref[di] > 0.5, s_new + bn_ref[...], NEG_BIG)
        m = jnp.maximum(jnp.max(sm, axis=-1, keepdims=True), jnp.max(sn, axis=-1, keepdims=True))
        pm = jnp.exp(sm - m)
        pn = jnp.exp(sn - m)
        l = jnp.sum(pm, axis=-1, keepdims=True) + jnp.sum(pn, axis=-1, keepdims=True)
        o = jnp.dot(pn.astype(BF16), vn, preferred_element_type=F32)
        o = o + lax.dot_general(pm.astype(BF16), vmain, nt, preferred_element_type=F32)
        outs.append(o / l)
        lses.append(m + jnp.log(l))
    mx = jnp.maximum(jnp.maximum(lses[0], lses[1]), lses[2])
    es = [jnp.exp(ls - mx) for ls in lses]
    mixed = (es[0] * outs[0] + es[1] * outs[1] + es[2] * outs[2]) / (es[0] + es[1] + es[2])
    mixed = jnp.where(same_head, mixed, 0.0)
    acc = mixed[0:TPAD]
    for h in range(1, H):
        acc = acc + mixed[h * TPAD:(h + 1) * TPAD]
    out_ref[...] = acc.astype(out_ref.dtype)


def _attn_sample(z3, cache_kt, cache_vt, layer, tables):
    B, T, _ = z3.shape
    assert T == TPAD
    n_past = cache_kt.shape[-1]
    new = lambda c0: pl.BlockSpec((None, TPAD, ATTN_WIDTH), lambda b: (b, 0, c0 // ATTN_WIDTH))
    buf = pl.BlockSpec((None, None, ATTN_WIDTH, n_past), lambda b: (layer, b, 0, 0))
    return pl.pallas_call(
        functools.partial(_attn_sample_kernel, n_past=n_past),
        out_shape=jax.ShapeDtypeStruct((B, TPAD, ATTN_WIDTH), BF16),
        grid=(B,),
        in_specs=[new(COL_AQ), new(COL_AK), new(COL_AV), buf, buf] + [_const_spec(t.shape) for t in tables],
        out_specs=pl.BlockSpec((None, TPAD, ATTN_WIDTH), lambda b: (b, 0, 0)),
        compiler_params=_cparams(("parallel",)),
        name="attn_sample",
    )(z3, z3, z3, cache_kt, cache_vt, *tables)


QUAD = N_MLSTM_HEADS * HEAD_DIM
CHUNK_UNROLL = 2
HEAD_SHIFT = 6


def _iota2(shape, axis):
    return lax.broadcasted_iota(jnp.int32, shape, axis)


def _log2(n):
    k = int(n).bit_length() - 1
    assert 1 << k == n
    return k


def _seg_mask(rows, row_shift, cols, col_shift):
    return (_iota2((rows, cols), 0) >> row_shift) == (_iota2((rows, cols), 1) >> col_shift)


def _cumulate_rows(x, length, op, fill):
    row = _iota2(x.shape, 0) & (length - 1)
    sh = 1
    while sh < length:
        x = op(x, jnp.where(row >= sh, pltpu.roll(x, sh, axis=0), fill))
        sh *= 2
    return x


def _split2(x):
    hi = x.astype(BF16)
    lo = (x - hi.astype(F32)).astype(BF16)
    return hi, lo


def _dot2(x, w):
    hi, lo = _split2(x)
    return jnp.dot(hi, w, preferred_element_type=F32) + jnp.dot(lo, w, preferred_element_type=F32)


def _dot2r(w, x):
    hi, lo = _split2(x)
    return jnp.dot(w, hi, preferred_element_type=F32) + jnp.dot(w, lo, preferred_element_type=F32)


def _head_rmsnorm_gate(h, seg64b, g_row, gate_pre):
    ms = _dot2(h * h, seg64b) * (1.0 / HEAD_DIM)
    return jax.nn.sigmoid(gate_pre) * (h * lax.rsqrt(ms + EPS) * g_row)


def _mlstm_kernel(qk_ref, v_ref, o_ref, gi_ref, gf_ref, conv0_ref, c0_ref, n0_ref, m0_ref, gb_ref, cw_ref, ng_ref,
                  out_ref, c_out, n_out, m_out, xp, cs, ns, ms, *, tile, chunk, n_valid):
    TS, L = tile, chunk
    t = pl.program_id(1)
    PAD = SUBLANES

    @pl.when(t == 0)
    def _():
        cs[...] = c0_ref[...]
        ns[...] = n0_ref[...]
        ms[...] = m0_ref[...]
        xp[0:PAD, :] = conv0_ref[...]

    @pl.when(t > 0)
    def _():
        xp[0:PAD, :] = xp[TS:TS + PAD, :]

    xp[PAD:PAD + TS, :] = qk_ref[...]

    LK = MLSTM_CHUNK
    seg64 = _seg_mask(QUAD, HEAD_SHIFT, QUAD, HEAD_SHIFT)
    seg64b = seg64.astype(BF16)
    dmask = (_iota2((LK, QUAD), 1) & (LK - 1)) == _iota2((LK, QUAD), 0)
    causal = (_iota2((L, QUAD), 1) & (LK - 1)) <= _iota2((L, QUAD), 0)
    tril = (_iota2((L, L), 1) <= _iota2((L, L), 0)).astype(BF16)
    ones_lk = jnp.ones((L, LK), BF16)
    row = _iota2((L, QUAD), 0)
    cw = cw_ref[...]
    gb = gb_ref[...]
    ng = ng_ref[...]

    def key_rows(x, fill):
        if L == LK:
            return x
        return jnp.concatenate([x, jnp.full((LK - L, QUAD), fill, x.dtype)], axis=0)

    def chunk_body(c, carry):
        r0 = pl.multiple_of(c * L, L)
        win = xp[pl.ds(r0, L + PAD), :]
        acc = win[PAD:PAD + L] * cw[CONV_WIDTH - 1:CONV_WIDTH, :]
        for j in range(1, CONV_WIDTH):
            acc = acc + win[PAD - j:PAD - j + L] * cw[CONV_WIDTH - 1 - j:CONV_WIDTH - j, :]
        qk = acc * jax.nn.sigmoid(acc)
        q = qk[:, 0:QUAD]
        k = qk[:, QUAD:] * (HEAD_DIM ** -0.5)
        v = v_ref[pl.ds(r0, L), :]

        ig = gi_ref[pl.ds(r0, L), :] + gb[0:1, :]
        g2 = gf_ref[pl.ds(r0, L), :] + gb[1:2, :]
        lf = jnp.minimum(g2, 0.0) - jnp.log(1.0 + jnp.exp(-jnp.abs(g2)))
        if n_valid < TS:
            ok = (row + r0) < n_valid
            ig = jnp.where(ok, ig, NEG_BIG)
            lf = jnp.where(ok, lf, 0.0)
        b = _dot2r(tril, lf)
        a = ig - b
        cm = _cumulate_rows(a, L, jnp.maximum, -jnp.inf)
        mprev = ms[...]
        M = jnp.maximum(cm, mprev)
        mt = b + M
        gg = jnp.exp(mprev - M)
        emt = jnp.exp(-mt)
        decay = jnp.exp(a - M[L - 1:L, :])
        arow = _dot2r(ones_lk, jnp.where(dmask, key_rows(a, NEG_BIG), 0.0))

        qb = q.astype(BF16)
        vb = v.astype(BF16)
        zb = jnp.zeros((QUAD, QUAD), BF16)
        kbd = jnp.where(seg64, jnp.concatenate([key_rows(k, 0.0).astype(BF16)] * N_MLSTM_HEADS, axis=0), zb)
        vbd = jnp.where(seg64, jnp.concatenate([key_rows(v, 0.0).astype(BF16)] * N_MLSTM_HEADS, axis=0), zb)
        qkt = lax.dot_general(qb, kbd, (((1,), (1,)), ((), ())), preferred_element_type=F32)
        wts = jnp.where(causal, jnp.exp(arow - M), 0.0) * qkt
        cmat = cs[...]
        num = jnp.dot(wts.astype(BF16), vbd, preferred_element_type=F32)
        num = num + gg * jnp.dot(qb, cmat.astype(BF16), preferred_element_type=F32)
        den = _dot2(wts + gg * (q * ns[...]), seg64b)
        h = num / jnp.maximum(jnp.abs(den), emt)
        out_ref[pl.ds(r0, L), :] = _head_rmsnorm_gate(h, seg64b, ng, o_ref[pl.ds(r0, L), :]).astype(out_ref.dtype)

        kd = k * decay
        u = lax.dot_general(kd.astype(BF16), vb, (((0,), (0,)), ((), ())), preferred_element_type=F32)
        gl = gg[L - 1:L, :]
        cs[...] = gl * cmat + jnp.where(seg64, u, 0.0)
        ns[...] = gl * ns[...] + jnp.sum(kd, axis=0, keepdims=True)
        ms[...] = mt[L - 1:L, :]
        return carry

    lax.fori_loop(0, TS // L, chunk_body, 0, unroll=min(CHUNK_UNROLL, TS // L))
    c_out[...] = cs[...]
    n_out[...] = ns[...]
    m_out[...] = ms[...]


def _mlstm(z3, conv0, c0bd, n0, m0, gate_b, conv_w, norm_g, n_valid, chunk):
    B, S, _ = z3.shape
    ts = min(512, S)
    assert S % ts == 0 and ts % chunk == 0
    blk = lambda w, c0: pl.BlockSpec((None, ts, w), lambda b, t: (b, t, c0 // w))
    per_b = lambda r, w: pl.BlockSpec((None, r, w), lambda b, t: (b, 0, 0))
    return pl.pallas_call(
        functools.partial(_mlstm_kernel, tile=ts, chunk=chunk, n_valid=n_valid),
        out_shape=[jax.ShapeDtypeStruct((B, S, QUAD), BF16),
                   jax.ShapeDtypeStruct((B, QUAD, QUAD), F32),
                   jax.ShapeDtypeStruct((B, 1, QUAD), F32),
                   jax.ShapeDtypeStruct((B, 1, QUAD), F32)],
        grid=(B, S // ts),
        in_specs=[blk(2 * QUAD, COL_MQK), blk(QUAD, COL_MV), blk(QUAD, COL_MO), blk(QUAD, COL_GI), blk(QUAD, COL_GF),
                  per_b(SUBLANES, 2 * QUAD), per_b(QUAD, QUAD), per_b(1, QUAD), per_b(1, QUAD),
                  _const_spec((2, QUAD)), _const_spec((CONV_WIDTH, 2 * QUAD)), _const_spec((1, QUAD))],
        out_specs=[pl.BlockSpec((None, ts, QUAD), lambda b, t: (b, t, 0)),
                   per_b(QUAD, QUAD), per_b(1, QUAD), per_b(1, QUAD)],
        scratch_shapes=[pltpu.VMEM((ts + 2 * SUBLANES, 2 * QUAD), F32),
                        pltpu.VMEM((QUAD, QUAD), F32), pltpu.VMEM((1, QUAD), F32), pltpu.VMEM((1, QUAD), F32)],
        compiler_params=_cparams(("parallel", "arbitrary")),
        name="mlstm",
    )(z3, z3, z3, z3, z3, conv0, c0bd, n0, m0, gate_b, conv_w, norm_g)


HGRN_FAST_CHUNK = 64
HGRN_SAFE_DECAY = 80.0


def _hgrn_kernel(q_ref, f_ref, i_ref, g_ref, s0_ref, lb_ref, ng_ref, out_ref, s_out, st, qs, fs, ks, *,
                 tile, sub, n_valid, fast):
    TS = tile
    t = pl.program_id(1)

    @pl.when(t == 0)
    def _():
        st[...] = s0_ref[...]

    seg64 = _seg_mask(QUAD, HEAD_SHIFT, QUAD, HEAD_SHIFT)
    seg64b = seg64.astype(BF16)
    lb = lb_ref[...]
    ng = ng_ref[...]
    nt = (((1,), (1,)), ((), ()))
    tn = (((0,), (0,)), ((), ()))

    P = HGRN_FAST_CHUNK if fast else sub
    prow = _iota2((P, QUAD), 0)

    def prep(c, worst):
        r0 = pl.multiple_of(c * P, P)
        hq = q_ref[pl.ds(r0, P), :]
        f = lb + (1.0 - lb) * jax.nn.sigmoid(f_ref[pl.ds(r0, P), :])
        logf = jnp.log(f)
        kk = 1.0 - f
        if n_valid < TS:
            ok = (prow + r0) < n_valid
            logf = jnp.where(ok, logf, 0.0)
            kk = jnp.where(ok, kk, 0.0)
        qs[pl.ds(r0, P), :] = hq * jax.nn.sigmoid(hq)
        fs[pl.ds(r0, P), :] = logf
        ks[pl.ds(r0, P), :] = kk
        return jnp.minimum(worst, jnp.sum(logf, axis=0, keepdims=True))

    worst = lax.fori_loop(0, TS // P, prep, jnp.zeros((1, QUAD), F32))

    def finish(o, r0, rows, smat, b, kk, iv, qt):
        o = o + lax.dot_general(qt, smat.astype(BF16), nt, preferred_element_type=F32)
        out_ref[pl.ds(r0, rows), :] = _head_rmsnorm_gate(o, seg64b, ng, g_ref[pl.ds(r0, rows), :]).astype(out_ref.dtype)
        bl = b[rows - 1:rows, :]
        ktil = kk * jnp.exp(bl - b)
        u = lax.dot_general(iv.astype(BF16), ktil.astype(BF16), tn, preferred_element_type=F32)
        st[...] = smat * jnp.exp(bl) + jnp.where(seg64, u, 0.0)

    def fast_loop():
        L = HGRN_FAST_CHUNK
        tril = (_iota2((L, L), 1) <= _iota2((L, L), 0)).astype(BF16)
        causal = (_iota2((L, QUAD), 1) & (L - 1)) <= _iota2((L, QUAD), 0)
        zb = jnp.zeros((QUAD, QUAD), BF16)

        def body(c, carry):
            r0 = pl.multiple_of(c * L, L)
            kk = ks[pl.ds(r0, L), :]
            iv = i_ref[pl.ds(r0, L), :]
            b = _dot2r(tril, fs[pl.ds(r0, L), :])
            qt = (qs[pl.ds(r0, L), :] * jnp.exp(b)).astype(BF16)
            kt = (kk * jnp.exp(-b)).astype(BF16)
            kbd = jnp.where(seg64, jnp.concatenate([kt] * N_HGRN_HEADS, axis=0), zb)
            ibd = jnp.where(seg64, jnp.concatenate([iv.astype(BF16)] * N_HGRN_HEADS, axis=0), zb)
            amat = lax.dot_general(qt, kbd, nt, preferred_element_type=F32)
            amat = jnp.where(causal, amat, 0.0).astype(BF16)
            o = jnp.dot(amat, ibd, preferred_element_type=F32)
            finish(o, r0, L, st[...], b, kk, iv, qt)
            return carry

        lax.fori_loop(0, TS // L, body, 0, unroll=min(CHUNK_UNROLL, TS // L))

    def exact_loop():
        L = sub
        row = _iota2((L, QUAD), 0)

        def body(c, carry):
            r0 = pl.multiple_of(c * L, L)
            q = qs[pl.ds(r0, L), :]
            kk = ks[pl.ds(r0, L), :]
            iv = i_ref[pl.ds(r0, L), :]
            b = _cumulate_rows(fs[pl.ds(r0, L), :], L, jnp.add, 0.0)
            parts = []
            for j in range(L):
                dec = jnp.exp(jnp.where(row >= j, b - b[j:j + 1, :], NEG_BIG))
                parts.append(dec * q * kk[j:j + 1, :])
            tstack = jnp.concatenate(parts, axis=0).astype(BF16)
            y = jnp.dot(tstack, seg64b, preferred_element_type=F32)
            o = y[0:L] * iv[0:1, :]
            for j in range(1, L):
                o = o + y[j * L:(j + 1) * L] * iv[j:j + 1, :]
            finish(o, r0, L, st[...], b, kk, iv, (q * jnp.exp(b)).astype(BF16))
            return carry

        lax.fori_loop(0, TS // L, body, 0, unroll=min(CHUNK_UNROLL, TS // L))

    if fast:
        safe = jnp.min(worst) > -HGRN_SAFE_DECAY
        pl.when(safe)(fast_loop)
        pl.when(jnp.logical_not(safe))(exact_loop)
    else:
        exact_loop()
    s_out[...] = st[...]


def _hgrn(z3, s0t, lb, norm_g, n_valid, sub):
    B, S, _ = z3.shape
    ts = min(512, S)
    fast = ts % HGRN_FAST_CHUNK == 0
    assert S % ts == 0 and ts % sub == 0
    blk = lambda c0: pl.BlockSpec((None, ts, QUAD), lambda b, t: (b, t, c0 // QUAD))
    per_b = pl.BlockSpec((None, QUAD, QUAD), lambda b, t: (b, 0, 0))
    stage = lambda: pltpu.VMEM((ts, QUAD), F32)
    return pl.pallas_call(
        functools.partial(_hgrn_kernel, tile=ts, sub=sub, n_valid=n_valid, fast=fast),
        out_shape=[jax.ShapeDtypeStruct((B, S, QUAD), BF16), jax.ShapeDtypeStruct((B, QUAD, QUAD), F32)],
        grid=(B, S // ts),
        in_specs=[blk(COL_HQ), blk(COL_HF), blk(COL_HI), blk(COL_HG), per_b,
                  _const_spec((1, QUAD)), _const_spec((1, QUAD))],
        out_specs=[pl.BlockSpec((None, ts, QUAD), lambda b, t: (b, t, 0)), per_b],
        scratch_shapes=[pltpu.VMEM((QUAD, QUAD), F32), stage(), stage(), stage()],
        compiler_params=_cparams(("parallel", "arbitrary")),
        name="hgrn",
    )(z3, z3, z3, z3, s0t, lb, norm_g)


PROJ_SPLITS = (ATTN_WIDTH, ATTN_WIDTH, ATTN_WIDTH,
               MLSTM_WIDTH, MLSTM_WIDTH, MLSTM_WIDTH, MLSTM_WIDTH, N_MLSTM_HEADS, N_MLSTM_HEADS,
               HGRN_WIDTH, HGRN_WIDTH, HGRN_WIDTH, HGRN_WIDTH)


def _prep_w_in(w):
    cuts = [int(c) for c in np.cumsum(PROJ_SPLITS)[:-1]]
    aq, ak, av, mq, mk, mv, mo, mi, mf, hq, hf, hi, hg = jnp.split(w, cuts, axis=1)
    gi, gf = jnp.repeat(mi, HEAD_DIM, axis=1), jnp.repeat(mf, HEAD_DIM, axis=1)
    return jnp.concatenate([aq, ak, av, mq, mk, mv, mo, hq, hf, hi, hg, gi, gf], axis=1).astype(BF16)


HGRN_SUBCHUNK = 16


class _LayerWeights(NamedTuple):
    g_mix: jax.Array
    w_in: jax.Array
    wkv_t: jax.Array
    gate_b: jax.Array
    conv_w: jax.Array
    mlstm_g: jax.Array
    lb: jax.Array
    hgrn_g: jax.Array
    w_out: jax.Array
    g_ffn: jax.Array
    w_up: jax.Array
    w_down: jax.Array
    g_final: jax.Array


def _gate_row(gate_b):
    return jnp.repeat(gate_b.astype(F32), HEAD_DIM, axis=1)


def _embed_block_diag(s):
    b, h, e, f = s.shape
    out = jnp.zeros((b, h, e, h, f), F32)
    for i in range(h):
        out = out.at[:, i, :, i, :].set(s[:, i].astype(F32))
    return out.reshape(b, h * e, h * f)


def _extract_block_diag(s, h):
    b, r, c = s.shape
    s5 = s.reshape(b, h, r // h, h, c // h)
    return jnp.stack([s5[:, i, :, i, :] for i in range(h)], axis=1)


def _trunk_layer(x2d, B, S, n_valid, attn_fn, states, w, final, chunk_m, chunk_h, n_keep=None):
    conv_buf, C0, n0, m0, S0 = states
    if n_keep is None:
        z = _inproj(x2d, w.g_mix, w.w_in)
    else:
        z, kt, vt = _inproj(x2d, w.g_mix, w.w_in, w.wkv_t, S, n_keep)
    z3 = z.reshape(B, S, ZW)
    attn = attn_fn(z3)
    conv0 = jnp.zeros((B, SUBLANES, 2 * MLSTM_WIDTH), F32).at[:, SUBLANES - (CONV_WIDTH - 1):].set(conv_buf.astype(F32))
    m0r = jnp.repeat(m0.astype(F32), HEAD_DIM, axis=1).reshape(B, 1, MLSTM_WIDTH)
    ml, cbd, n, m = _mlstm(z3, conv0, _embed_block_diag(C0), n0.astype(F32).reshape(B, 1, MLSTM_WIDTH), m0r,
                           w.gate_b, w.conv_w, w.mlstm_g, n_valid, chunk_m)
    s0t = _embed_block_diag(jnp.swapaxes(S0, -1, -2))
    hg, st = _hgrn(z3, s0t, w.lb, w.hgrn_g, n_valid, chunk_h)
    n2 = B * S
    y = _outffn(x2d, attn.reshape(n2, ATTN_WIDTH), ml.reshape(n2, MLSTM_WIDTH), hg.reshape(n2, HGRN_WIDTH),
                w.w_out, w.g_ffn, w.w_up, w.w_down, w.g_final, final)
    if n_keep is None:
        k_rows = z3[:, :n_valid, COL_AK:COL_AK + ATTN_WIDTH].reshape(B, n_valid, N_ATTN_HEADS, HEAD_DIM)
        v_rows = z3[:, :n_valid, COL_AV:COL_AV + ATTN_WIDTH].reshape(B, n_valid, N_ATTN_HEADS, HEAD_DIM)
    else:
        k_rows = jnp.transpose(kt.reshape(B, N_ATTN_HEADS, HEAD_DIM, n_keep), (0, 3, 1, 2))
        v_rows = jnp.transpose(vt.reshape(B, N_ATTN_HEADS, HEAD_DIM, n_keep), (0, 3, 1, 2))
    conv_new = z3[:, n_valid - (CONV_WIDTH - 1):n_valid, COL_MQK:COL_MQK + 2 * MLSTM_WIDTH]
    c_new = _extract_block_diag(cbd, N_MLSTM_HEADS)
    s_new = jnp.swapaxes(_extract_block_diag(st, N_HGRN_HEADS), -1, -2)
    return y, (k_rows, v_rows, conv_new, c_new, n.reshape(B, N_MLSTM_HEADS, HEAD_DIM), m[:, 0, ::HEAD_DIM], s_new)


def kernel(x_prompt, x_sample, cache_attn_k, cache_attn_v, state_mlstm_conv, state_mlstm_C, state_mlstm_n, state_mlstm_m, state_hgrn_S, rel_bias, g_mix, w_in, mlstm_gate_b, mlstm_conv_w, mlstm_norm_g, hgrn_lb_raw, hgrn_norm_g, w_out, g_ffn, w_up, w_down, g_final):
    depth = w_in.shape[0]
    bp, sp, _ = x_prompt.shape
    bs, ts, _ = x_sample.shape
    n_keep = min(max(w for w, _ in DILATED_PATTERNS), sp)
    n_past = cache_attn_k.shape[2]
    lb_p = jax.nn.softmax(hgrn_lb_raw.astype(F32), axis=0)
    hgrn_lb = jnp.cumsum(lb_p, axis=0) - lb_p[0]
    bias_dist = _bias_by_distance(rel_bias)
    p_tables = _prompt_tables(bias_dist)
    s_tables = _sample_tables(bias_dist, n_past, ts)
    cache_kt = jnp.transpose(cache_attn_k, (0, 1, 3, 4, 2)).reshape(depth, bs, ATTN_WIDTH, n_past)
    cache_vt = jnp.transpose(cache_attn_v, (0, 1, 3, 4, 2)).reshape(depth, bs, ATTN_WIDTH, n_past)

    hp = x_prompt.reshape(bp * sp, D_MODEL)
    hs = jnp.zeros((bs, TPAD, D_MODEL), F32).at[:, :ts].set(x_sample).reshape(bs * TPAD, D_MODEL)
    zero_states = (jnp.zeros((bp, CONV_WIDTH - 1, 2 * MLSTM_WIDTH), F32),
                   jnp.zeros((bp, N_MLSTM_HEADS, HEAD_DIM, HEAD_DIM), F32),
                   jnp.zeros((bp, N_MLSTM_HEADS, HEAD_DIM), F32),
                   jnp.zeros((bp, N_MLSTM_HEADS), F32),
                   jnp.zeros((bp, N_HGRN_HEADS, HEAD_DIM, HEAD_DIM), F32))
    p_states, s_states = [], []
    for l in range(depth):
        final = l == depth - 1
        w_in_l = _prep_w_in(w_in[l])
        weights = _LayerWeights(
            g_mix=g_mix[l][None], w_in=w_in_l, wkv_t=w_in_l[:, COL_AK:COL_AV + ATTN_WIDTH].T,
            gate_b=_gate_row(mlstm_gate_b[l]),
            conv_w=mlstm_conv_w[l], mlstm_g=mlstm_norm_g[l][None], lb=hgrn_lb[l][None], hgrn_g=hgrn_norm_g[l][None],
            w_out=w_out[l].astype(BF16), g_ffn=g_ffn[l][None], w_up=w_up[l].astype(BF16),
            w_down=w_down[l].astype(BF16), g_final=g_final[None])
        hp, st = _trunk_layer(hp, bp, sp, sp, functools.partial(_attn_prompt, bias=p_tables[0], mask=p_tables[1]),
                              zero_states, weights, final, MLSTM_CHUNK, HGRN_SUBCHUNK, n_keep)
        p_states.append(st)
        sample_attn = functools.partial(_attn_sample, cache_kt=cache_kt, cache_vt=cache_vt, layer=l, tables=s_tables)
        states = (state_mlstm_conv[l], state_mlstm_C[l], state_mlstm_n[l], state_mlstm_m[l], state_hgrn_S[l])
        hs, st = _trunk_layer(hs, bs, TPAD, ts, sample_attn, states, weights, final, TPAD, TPAD)
        s_states.append(st)
    y_prompt = hp.reshape(bp, sp, D_MODEL)
    y_sample = hs.reshape(bs, TPAD, D_MODEL)[:, :ts]
    p_out = [jnp.stack(a) for a in zip(*p_states)]
    s_out = [jnp.stack(a) for a in zip(*s_states)]
    return (y_prompt, y_sample, *p_out, *s_out)
```

```python
import functools
from typing import NamedTuple

import jax
import jax.numpy as jnp
import numpy as np
from jax import lax
from jax.experimental import pallas as pl
from jax.experimental.pallas import tpu as pltpu

F32 = jnp.float32
BF16 = jnp.bfloat16

D_MODEL = 1024
HEAD_DIM = 64
N_ATTN_HEADS = 8
N_MLSTM_HEADS = 4
N_HGRN_HEADS = 4
ATTN_WIDTH = N_ATTN_HEADS * HEAD_DIM
MLSTM_WIDTH = N_MLSTM_HEADS * HEAD_DIM
HGRN_WIDTH = N_HGRN_HEADS * HEAD_DIM
DILATED_PATTERNS = ((128, 1), (512, 4), (2048, 16))
QBLOCK = 128
N_REL_BUCKETS = 32
REL_MAX_DISTANCE = 2048
CONV_WIDTH = 4
MLSTM_CHUNK = 64
D_FF = 4 * D_MODEL
EPS = 1e-6
NEG_BIG = -1e30

LANES = 128
SUBLANES = 8

COL_AQ = 0
COL_AK = COL_AQ + ATTN_WIDTH
COL_AV = COL_AK + ATTN_WIDTH
COL_MQK = COL_AV + ATTN_WIDTH
COL_MV = COL_MQK + 2 * MLSTM_WIDTH
COL_MO = COL_MV + MLSTM_WIDTH
COL_HQ = COL_MO + MLSTM_WIDTH
COL_HF = COL_HQ + HGRN_WIDTH
COL_HI = COL_HF + HGRN_WIDTH
COL_HG = COL_HI + HGRN_WIDTH
COL_GI = COL_HG + HGRN_WIDTH
COL_GF = COL_GI + MLSTM_WIDTH
ZW = COL_GF + MLSTM_WIDTH

VMEM_LIMIT = 56 * 1024 * 1024


def _cparams(sem, vmem=VMEM_LIMIT):
    return pltpu.CompilerParams(dimension_semantics=sem, vmem_limit_bytes=vmem)


def _const_spec(shape):
    nd = len(shape)
    return pl.BlockSpec(shape, lambda *_: (0,) * nd, pipeline_mode=pl.Buffered(1))


def _inproj_kernel(x_ref, g_ref, w_ref, *rest, tiles_per_seq, first_keep):
    x = x_ref[...]
    ms = jnp.mean(x * x, axis=-1, keepdims=True)
    xn = ((x * lax.rsqrt(ms + EPS)) * g_ref[...]).astype(BF16)
    if first_keep is None:
        (z_ref,) = rest
        z_ref[...] = jnp.dot(xn, w_ref[...], preferred_element_type=F32)
        return
    wkv_ref, z_ref, kt_ref, vt_ref = rest
    z_ref[...] = jnp.dot(xn, w_ref[...], preferred_element_type=F32)

    @pl.when(pl.program_id(0) % tiles_per_seq >= first_keep)
    def _():
        kv = lax.dot_general(wkv_ref[...], xn, (((1,), (1,)), ((), ())), preferred_element_type=F32)
        kt_ref[...] = kv[0:ATTN_WIDTH]
        vt_ref[...] = kv[ATTN_WIDTH:]


def _inproj(x2d, g, w, wkv_t=None, seq=None, n_keep=None):
    n = x2d.shape[0]
    tm = min(512, n)
    assert n % tm == 0
    x_spec = pl.BlockSpec((tm, D_MODEL), lambda i: (i, 0))
    z_spec = pl.BlockSpec((tm, ZW), lambda i: (i, 0))
    z_shape = jax.ShapeDtypeStruct((n, ZW), F32)
    if wkv_t is None:
        return pl.pallas_call(
            functools.partial(_inproj_kernel, tiles_per_seq=None, first_keep=None),
            out_shape=z_shape, grid=(n // tm,),
            in_specs=[x_spec, _const_spec((1, D_MODEL)), _const_spec((D_MODEL, ZW))],
            out_specs=z_spec, compiler_params=_cparams(("parallel",)), name="inproj",
        )(x2d, g, w)
    assert seq % tm == 0 and n_keep % tm == 0
    tps, first_keep = seq // tm, (seq - n_keep) // tm
    t_spec = pl.BlockSpec((None, ATTN_WIDTH, tm), lambda i: (i // tps, 0, jnp.maximum(i % tps - first_keep, 0)))
    t_shape = jax.ShapeDtypeStruct((n // seq, ATTN_WIDTH, n_keep), F32)
    return pl.pallas_call(
        functools.partial(_inproj_kernel, tiles_per_seq=tps, first_keep=first_keep),
        out_shape=[z_shape, t_shape, t_shape], grid=(n // tm,),
        in_specs=[x_spec, _const_spec((1, D_MODEL)), _const_spec((D_MODEL, ZW)), _const_spec((2 * ATTN_WIDTH, D_MODEL))],
        out_specs=[z_spec, t_spec, t_spec], compiler_params=_cparams(("arbitrary",)), name="inproj_kt",
    )(x2d, g, w, wkv_t)


FF_CHUNK = 1024


def _outffn_kernel(x_ref, a_ref, m_ref, h_ref, wo_ref, gf_ref, wu_ref, wd_ref, gl_ref, y_ref, xn_sc, *, final):
    x1 = x_ref[...]
    x1 = x1 + jnp.dot(a_ref[...], wo_ref[0:ATTN_WIDTH, :], preferred_element_type=F32)
    x1 = x1 + jnp.dot(m_ref[...], wo_ref[ATTN_WIDTH:ATTN_WIDTH + MLSTM_WIDTH, :], preferred_element_type=F32)
    x1 = x1 + jnp.dot(h_ref[...], wo_ref[ATTN_WIDTH + MLSTM_WIDTH:, :], preferred_element_type=F32)
    ms = jnp.mean(x1 * x1, axis=-1, keepdims=True)
    xn_sc[...] = ((x1 * lax.rsqrt(ms + EPS)) * gf_ref[...]).astype(BF16)
    y_ref[...] = x1
    for c in range(D_FF // FF_CHUNK):
        u = jnp.dot(xn_sc[...], wu_ref[:, c * FF_CHUNK:(c + 1) * FF_CHUNK], preferred_element_type=F32)
        hh = jnp.square(jnp.maximum(u, 0.0)).astype(BF16)
        y_ref[...] += jnp.dot(hh, wd_ref[c * FF_CHUNK:(c + 1) * FF_CHUNK, :], preferred_element_type=F32)
    if final:
        x2 = y_ref[...]
        ms2 = jnp.mean(x2 * x2, axis=-1, keepdims=True)
        y_ref[...] = (x2 * lax.rsqrt(ms2 + EPS)) * gl_ref[...]


def _outffn(x2d, attn, ml, hg, w_out, g_ffn, w_up, w_down, g_final, final):
    n = x2d.shape[0]
    tm = min(512, n)
    assert n % tm == 0
    row = lambda w: pl.BlockSpec((tm, w), lambda i: (i, 0))
    return pl.pallas_call(
        functools.partial(_outffn_kernel, final=final),
        out_shape=jax.ShapeDtypeStruct((n, D_MODEL), F32),
        grid=(n // tm,),
        in_specs=[row(D_MODEL), row(ATTN_WIDTH), row(MLSTM_WIDTH), row(HGRN_WIDTH),
                  _const_spec((D_MODEL, D_MODEL)), _const_spec((1, D_MODEL)),
                  _const_spec((D_MODEL, D_FF)), _const_spec((D_FF, D_MODEL)), _const_spec((1, D_MODEL))],
        out_specs=row(D_MODEL),
        scratch_shapes=[pltpu.VMEM((tm, D_MODEL), BF16)],
        compiler_params=_cparams(("parallel",)),
        name="outffn",
    )(x2d, attn, ml, hg, w_out, g_ffn, w_up, w_down, g_final)


def _t5_causal_bucket(dist):
    n = np.asarray(dist).astype(np.int32)
    max_exact = N_REL_BUCKETS // 2
    scaled = np.log(np.maximum(n, 1) / max_exact) / np.log(REL_MAX_DISTANCE / max_exact)
    large = np.minimum(max_exact + (scaled * (N_REL_BUCKETS - max_exact)).astype(np.int32), N_REL_BUCKETS - 1)
    return np.where(n < max_exact, n, large).astype(np.int32)


BIAS_DIST = 2304


def _bias_by_distance(rel_bias):
    assert BIAS_DIST > max(w for w, _ in DILATED_PATTERNS) + SUBLANES
    return rel_bias.astype(F32)[_t5_causal_bucket(np.arange(BIAS_DIST)[::-1])].T


def _dist_slice(bias_desc, lo, hi, step=1):
    last = BIAS_DIST - 1
    return bias_desc[:, last - hi:last - lo + 1:step]


def _prompt_tables(bias_dist):
    H = N_ATTN_HEADS
    a = np.arange(QBLOCK)[:, None]
    b = np.arange(2 * QBLOCK)[None, :]
    rel = QBLOCK + a - b
    band = (rel >= 0) & (rel <= QBLOCK)
    period = 3 * QBLOCK
    biases = []
    for window, dil in DILATED_PATTERNS:
        nsub = window // dil
        assert nsub == QBLOCK
        vd = _dist_slice(bias_dist, 0, nsub * dil, dil)
        rp = jnp.concatenate([vd, jnp.broadcast_to(vd[:, -1:], (H, QBLOCK)),
                              jnp.broadcast_to(vd[:, 0:1], (H, QBLOCK - 1))], axis=1)
        skew = jnp.tile(rp, (1, QBLOCK))[:, :QBLOCK * (period - 1)].reshape(H, QBLOCK, period - 1)
        biases.append(skew[:, :, :2 * QBLOCK].reshape(H // 2, 2 * QBLOCK, 2 * QBLOCK))
    mask = np.stack([np.tile(band, (2, 1)), np.tile(band & (b >= QBLOCK), (2, 1))])
    return jnp.where(mask[None, None], jnp.stack(biases)[:, :, None], -jnp.inf)


ATTN_UNROLL = 8


def _attn_prompt_kernel(q_ref, k_ref, v_ref, bias_ref, out_ref,
                        x4, qs, ks, vs, o1, o4, o16, l1, l4, l16, *, seq):
    S = seq
    nblk = S // QBLOCK
    scale = HEAD_DIM ** -0.5
    zpad = jnp.zeros((QBLOCK, LANES), BF16)
    for di in range(3):
        ks[di, 0:QBLOCK, :] = zpad
        vs[di, 0:QBLOCK, :] = zpad

    R = 512
    for src, dst, off, mul in ((q_ref, qs, 0, scale), (k_ref, ks, QBLOCK, 1.0), (v_ref, vs, QBLOCK, 1.0)):
        for i in range(S // R):
            blk = src[i * R:(i + 1) * R, :]
            dst[0, off + i * R:off + (i + 1) * R, :] = (blk * mul).astype(BF16)
        for r in range(4):
            for i in range(S // 4 // R):
                blk = src[pl.ds(r + 4 * i * R, R, stride=4), :]
                row0 = r * (S // 4) + i * R
                x4[row0:row0 + R, :] = blk
                dst[1, off + row0:off + row0 + R, :] = (blk * mul).astype(BF16)
        n16 = S // 16
        for c4 in range(4):
            for a in range(4):
                blk = x4[pl.ds(c4 * (S // 4) + a, n16, stride=4), :]
                row0 = (c4 * 4 + a) * n16
                dst[2, off + row0:off + row0 + n16, :] = (blk * mul).astype(BF16)

    lane = lax.broadcasted_iota(jnp.int32, (QBLOCK, LANES), 1)
    head_a = lane < HEAD_DIM

    def branch(di, dil, o_ref, l_ref):
        per_class = (S // dil) // QBLOCK

        def body(i, carry):
            base = pl.multiple_of(i * QBLOCK, QBLOCK)
            q = qs[di, pl.ds(base, QBLOCK), :]
            kk = ks[di, pl.ds(base, 2 * QBLOCK), :]
            vv = vs[di, pl.ds(base, 2 * QBLOCK), :]
            zero = jnp.zeros_like(q)
            q2 = jnp.concatenate([jnp.where(head_a, q, zero), jnp.where(head_a, zero, q)], axis=0)
            s = lax.dot_general(q2, kk, (((1,), (1,)), ((), ())), preferred_element_type=F32)
            n = i % per_class
            first = (n == 0).astype(jnp.int32)
            s = s + bias_ref[di, first]
            m = jnp.max(s, axis=-1, keepdims=True)
            p = jnp.exp(s - m)
            l = jnp.sum(p, axis=-1, keepdims=True)
            o2 = jnp.dot(p.astype(BF16), vv, preferred_element_type=F32) / l
            lse = m + jnp.log(l)
            o = jnp.where(head_a, o2[0:QBLOCK], o2[QBLOCK:])
            ls = jnp.where(head_a, jnp.broadcast_to(lse[0:QBLOCK], (QBLOCK, LANES)),
                           jnp.broadcast_to(lse[QBLOCK:], (QBLOCK, LANES)))
            if dil == 1:
                o_ref[pl.ds(base, QBLOCK), :] = o
                l_ref[pl.ds(base, QBLOCK), :] = ls
            else:
                c = i // per_class
                res = c if dil == 4 else 4 * (c % 4) + c // 4
                start = dil * QBLOCK * n + res
                o_ref[pl.ds(start, QBLOCK, stride=dil), :] = o
                l_ref[pl.ds(start, QBLOCK, stride=dil), :] = ls
            return carry

        lax.fori_loop(0, nblk, body, 0, unroll=ATTN_UNROLL)

    branch(0, 1, o1, l1)
    branch(1, 4, o4, l4)
    branch(2, 16, o16, l16)

    T = 256

    def mix(i, carry):
        r0 = pl.multiple_of(i * T, T)
        la, lb, lc = l1[pl.ds(r0, T), :], l4[pl.ds(r0, T), :], l16[pl.ds(r0, T), :]
        mx = jnp.maximum(jnp.maximum(la, lb), lc)
        ea, eb, ec = jnp.exp(la - mx), jnp.exp(lb - mx), jnp.exp(lc - mx)
        num = ea * o1[pl.ds(r0, T), :] + eb * o4[pl.ds(r0, T), :] + ec * o16[pl.ds(r0, T), :]
        out_ref[pl.ds(r0, T), :] = (num / (ea + eb + ec)).astype(out_ref.dtype)
        return carry

    lax.fori_loop(0, S // T, mix, 0)


def _attn_prompt(z3, bias):
    B, S, _ = z3.shape
    assert S % (16 * QBLOCK) == 0
    npair = N_ATTN_HEADS // 2
    col = lambda c0: pl.BlockSpec((None, S, LANES), lambda b, p: (b, 0, c0 // LANES + p))
    f32s = lambda: pltpu.VMEM((S, LANES), F32)
    return pl.pallas_call(
        functools.partial(_attn_prompt_kernel, seq=S),
        out_shape=jax.ShapeDtypeStruct((B, S, ATTN_WIDTH), BF16),
        grid=(B, npair),
        in_specs=[col(COL_AQ), col(COL_AK), col(COL_AV),
                  pl.BlockSpec((3, None, 2, 2 * QBLOCK, 2 * QBLOCK), lambda b, p: (0, p, 0, 0, 0))],
        out_specs=pl.BlockSpec((None, S, LANES), lambda b, p: (b, 0, p)),
        scratch_shapes=[f32s(),
                        pltpu.VMEM((3, S, LANES), BF16),
                        pltpu.VMEM((3, S + QBLOCK, LANES), BF16),
                        pltpu.VMEM((3, S + QBLOCK, LANES), BF16),
                        f32s(), f32s(), f32s(), f32s(), f32s(), f32s()],
        compiler_params=_cparams(("parallel", "parallel")),
        name="attn_prompt",
    )(z3, z3, z3, bias)


TPAD = SUBLANES
TAIL = 512


def _sample_tables(bias_dist, n_past, n_tok):
    assert n_past >= max(w for w, _ in DILATED_PATTERNS) and n_tok <= TPAD and TAIL >= DILATED_PATTERNS[1][0]
    H = N_ATTN_HEADS
    t = np.arange(TPAD)[:, None]

    def by_row(width):
        rows = [_dist_slice(bias_dist, tt + 1, tt + width) for tt in range(TPAD)]
        return jnp.stack(rows, axis=1).reshape(H * TPAD, width)

    bias_tail, bias_full = by_row(TAIL), by_row(n_past)
    zero_dist = _dist_slice(bias_dist, 0, 0)
    new_rows = [jnp.concatenate([_dist_slice(bias_dist, 0, tt),
                                 jnp.broadcast_to(zero_dist, (H, TPAD - tt - 1))], axis=1) for tt in range(TPAD)]
    bias_new = jnp.stack(new_rows, axis=1).reshape(H * TPAD, TPAD)

    def valid(dj, dil, nsub, lo):
        ok = (dj % dil == 0) & (dj // dil >= lo) & (dj // dil <= nsub) & (t < n_tok)
        return np.tile(ok, (H, 1)).astype(np.float32)

    tn = np.arange(TPAD)[None, :]
    m_tail, m_new = [], []
    for window, dil in DILATED_PATTERNS:
        nsub = window // dil
        if window <= TAIL:
            m_tail.append(valid(TAIL + t - np.arange(TAIL)[None, :], dil, nsub, 1))
        else:
            m_full = valid(n_past + t - np.arange(n_past)[None, :], dil, nsub, 1)
        m_new.append(valid(t - tn, dil, nsub, 0) * (tn < n_tok))
    return (bias_tail, bias_full, bias_new, jnp.asarray(np.stack(m_tail)), jnp.asarray(m_full),
            jnp.asarray(np.stack(m_new).astype(np.float32)))


def _attn_sample_kernel(q_ref, kn_ref, vn_ref, kt_ref, vt_ref, bt_ref, bf_ref, bn_ref, mt_ref, mf_ref, mn_ref,
                        out_ref, *, n_past):
    H = N_ATTN_HEADS
    rows = H * TPAD
    same_head = (_iota2((rows, ATTN_WIDTH), 0) >> 3) == (_iota2((rows, ATTN_WIDTH), 1) >> HEAD_SHIFT)
    q = q_ref[...] * (HEAD_DIM ** -0.5)
    qm = jnp.where(same_head, jnp.concatenate([q] * H, axis=0), 0.0).astype(BF16)
    nt = (((1,), (1,)), ((), ()))
    kn = kn_ref[...].astype(BF16)
    vn = vn_ref[...].astype(BF16)
    kt = kt_ref[...].astype(BF16)
    vt = vt_ref[...].astype(BF16)
    s_new = lax.dot_general(qm, kn, nt, preferred_element_type=F32)
    s_full = jnp.dot(qm, kt, preferred_element_type=F32)
    s_tail = s_full[:, n_past - TAIL:]

    outs, lses = [], []
    for di, (window, _) in enumerate(DILATED_PATTERNS):
        if window <= TAIL:
            sm = jnp.where(mt_ref[di] > 0.5, s_tail + bt_ref[...], NEG_BIG)
            vmain = vt[:, n_past - TAIL:]
        else:
            sm = jnp.where(mf_ref[...] > 0.5, s_full + bf_ref[...], NEG_BIG)
            vmain = vt
        sn = jnp.where(mn_ref[di] > 0.5, s_new + bn_ref[...], NEG_BIG)
        m = jnp.maximum(jnp.max(sm, axis=-1, keepdims=True), jnp.max(sn, axis=-1, keepdims=True))
        pm = jnp.exp(sm - m)
        pn = jnp.exp(sn - m)
        l = jnp.sum(pm, axis=-1, keepdims=True) + jnp.sum(pn, axis=-1, keepdims=True)
        o = jnp.dot(pn.astype(BF16), vn, preferred_element_type=F32)
        o = o + lax.dot_general(pm.astype(BF16), vmain, nt, preferred_element_type=F32)
        outs.append(o / l)
        lses.append(m + jnp.log(l))
    mx = jnp.maximum(jnp.maximum(lses[0], lses[1]), lses[2])
    es = [jnp.exp(ls - mx) for ls in lses]
    mixed = (es[0] * outs[0] + es[1] * outs[1] + es[2] * outs[2]) / (es[0] + es[1] + es[2])
    mixed = jnp.where(same_head, mixed, 0.0)
    acc = mixed[0:TPAD]
    for h in range(1, H):
        acc = acc + mixed[h * TPAD:(h + 1) * TPAD]
    out_ref[...] = acc.astype(out_ref.dtype)


def _attn_sample(z3, cache_kt, cache_vt, layer, tables):
    B, T, _ = z3.shape
    assert T == TPAD
    n_past = cache_kt.shape[-1]
    new = lambda c0: pl.BlockSpec((None, TPAD, ATTN_WIDTH), lambda b: (b, 0, c0 // ATTN_WIDTH))
    buf = pl.BlockSpec((None, None, ATTN_WIDTH, n_past), lambda b: (layer, b, 0, 0))
    return pl.pallas_call(
        functools.partial(_attn_sample_kernel, n_past=n_past),
        out_shape=jax.ShapeDtypeStruct((B, TPAD, ATTN_WIDTH), BF16),
        grid=(B,),
        in_specs=[new(COL_AQ), new(COL_AK), new(COL_AV), buf, buf] + [_const_spec(t.shape) for t in tables],
        out_specs=pl.BlockSpec((None, TPAD, ATTN_WIDTH), lambda b: (b, 0, 0)),
        compiler_params=_cparams(("parallel",)),
        name="attn_sample",
    )(z3, z3, z3, cache_kt, cache_vt, *tables)


QUAD = N_MLSTM_HEADS * HEAD_DIM
CHUNK_UNROLL = 2
CHUNK_GROUP = 4
HEAD_SHIFT = 6


def _iota2(shape, axis):
    return lax.broadcasted_iota(jnp.int32, shape, axis)


def _log2(n):
    k = int(n).bit_length() - 1
    assert 1 << k == n
    return k


def _seg_mask(rows, row_shift, cols, col_shift):
    return (_iota2((rows, cols), 0) >> row_shift) == (_iota2((rows, cols), 1) >> col_shift)


def _cumulate_rows(x, length, op, fill):
    row = _iota2(x.shape, 0) & (length - 1)
    sh = 1
    while sh < length:
        x = op(x, jnp.where(row >= sh, pltpu.roll(x, sh, axis=0), fill))
        sh *= 2
    return x


def _split2(x):
    hi = x.astype(BF16)
    lo = (x - hi.astype(F32)).astype(BF16)
    return hi, lo


def _dot2(x, w):
    hi, lo = _split2(x)
    return jnp.dot(hi, w, preferred_element_type=F32) + jnp.dot(lo, w, preferred_element_type=F32)


def _dot2r(w, x):
    hi, lo = _split2(x)
    return jnp.dot(w, hi, preferred_element_type=F32) + jnp.dot(w, lo, preferred_element_type=F32)


def _head_rmsnorm_gate(h, seg64b, g_row, gate_pre):
    ms = _dot2(h * h, seg64b) * (1.0 / HEAD_DIM)
    return jax.nn.sigmoid(gate_pre) * (h * lax.rsqrt(ms + EPS) * g_row)


def _mlstm_kernel(qk_ref, v_ref, o_ref, gi_ref, gf_ref, conv0_ref, c0_ref, n0_ref, m0_ref, gb_ref, cw_ref, ng_ref,
                  out_ref, c_out, n_out, m_out, xp, cs, ns, ms, *, tile, chunk, n_valid):
    TS, L = tile, chunk
    t = pl.program_id(1)
    PAD = SUBLANES

    @pl.when(t == 0)
    def _():
        cs[...] = c0_ref[...]
        ns[...] = n0_ref[...]
        ms[...] = m0_ref[...]
        xp[0:PAD, :] = conv0_ref[...]

    @pl.when(t > 0)
    def _():
        xp[0:PAD, :] = xp[TS:TS + PAD, :]

    xp[PAD:PAD + TS, :] = qk_ref[...]

    LK = MLSTM_CHUNK
    seg64 = _seg_mask(QUAD, HEAD_SHIFT, QUAD, HEAD_SHIFT)
    seg64b = seg64.astype(BF16)
    dmask = (_iota2((LK, QUAD), 1) & (LK - 1)) == _iota2((LK, QUAD), 0)
    causal = (_iota2((L, QUAD), 1) & (LK - 1)) <= _iota2((L, QUAD), 0)
    tril = (_iota2((L, L), 1) <= _iota2((L, L), 0)).astype(BF16)
    ones_lk = jnp.ones((L, LK), BF16)
    row = _iota2((L, QUAD), 0)
    cw = cw_ref[...]
    gb = gb_ref[...]
    ng = ng_ref[...]

    def key_rows(x, fill):
        if L == LK:
            return x
        return jnp.concatenate([x, jnp.full((LK - L, QUAD), fill, x.dtype)], axis=0)

    G = min(CHUNK_GROUP, TS // L)
    assert TS % (L * G) == 0
    nt = (((1,), (1,)), ((), ()))
    tn = (((0,), (0,)), ((), ()))
    zb = jnp.zeros((QUAD, QUAD), BF16)

    def group_body(gi, carry):
        R = range(G)
        r0 = [pl.multiple_of((gi * G + j) * L, L) for j in R]
        q, k, v, ig, lf = [], [], [], [], []
        for j in R:
            win = xp[pl.ds(r0[j], L + PAD), :]
            acc = win[PAD:PAD + L] * cw[CONV_WIDTH - 1:CONV_WIDTH, :]
            for s in range(1, CONV_WIDTH):
                acc = acc + win[PAD - s:PAD - s + L] * cw[CONV_WIDTH - 1 - s:CONV_WIDTH - s, :]
            qk = acc * jax.nn.sigmoid(acc)
            q.append(qk[:, 0:QUAD])
            k.append(qk[:, QUAD:] * (HEAD_DIM ** -0.5))
            v.append(v_ref[pl.ds(r0[j], L), :])
            ig_j = gi_ref[pl.ds(r0[j], L), :] + gb[0:1, :]
            g2 = gf_ref[pl.ds(r0[j], L), :] + gb[1:2, :]
            lf_j = jnp.minimum(g2, 0.0) - jnp.log(1.0 + jnp.exp(-jnp.abs(g2)))
            if n_valid < TS:
                ok = (row + r0[j]) < n_valid
                ig_j = jnp.where(ok, ig_j, NEG_BIG)
                lf_j = jnp.where(ok, lf_j, 0.0)
            ig.append(ig_j)
            lf.append(lf_j)
        b = [_dot2r(tril, lf[j]) for j in R]
        a = [ig[j] - b[j] for j in R]
        cm = [_cumulate_rows(a[j], L, jnp.maximum, -jnp.inf) for j in R]
        arow = [_dot2r(ones_lk, jnp.where(dmask, key_rows(a[j], NEG_BIG), 0.0)) for j in R]
        qb = [q[j].astype(BF16) for j in R]
        vb = [v[j].astype(BF16) for j in R]
        kbd = [jnp.where(seg64, jnp.concatenate([key_rows(k[j], 0.0).astype(BF16)] * N_MLSTM_HEADS, axis=0), zb) for j in R]
        vbd = [jnp.where(seg64, jnp.concatenate([key_rows(v[j], 0.0).astype(BF16)] * N_MLSTM_HEADS, axis=0), zb) for j in R]
        qkt = [lax.dot_general(qb[j], kbd[j], nt, preferred_element_type=F32) for j in R]

        mprev = ms[...]
        M, gg, emt = [], [], []
        for j in R:
            M.append(jnp.maximum(cm[j], mprev))
            mt = b[j] + M[j]
            gg.append(jnp.exp(mprev - M[j]))
            emt.append(jnp.exp(-mt))
            mprev = mt[L - 1:L, :]
        ms[...] = mprev

        wts = [jnp.where(causal, jnp.exp(arow[j] - M[j]), 0.0) * qkt[j] for j in R]
        kd = [k[j] * jnp.exp(a[j] - M[j][L - 1:L, :]) for j in R]
        num = [jnp.dot(wts[j].astype(BF16), vbd[j], preferred_element_type=F32) for j in R]
        u = [lax.dot_general(kd[j].astype(BF16), vb[j], tn, preferred_element_type=F32) for j in R]

        cmat, nvec = [cs[...]], [ns[...]]
        for j in R:
            gl = gg[j][L - 1:L, :]
            cmat.append(gl * cmat[j] + jnp.where(seg64, u[j], 0.0))
            nvec.append(gl * nvec[j] + jnp.sum(kd[j], axis=0, keepdims=True))
        cs[...] = cmat[G]
        ns[...] = nvec[G]

        inter = [jnp.dot(qb[j], cmat[j].astype(BF16), preferred_element_type=F32) for j in R]
        den = [_dot2(wts[j] + gg[j] * (q[j] * nvec[j]), seg64b) for j in R]
        h = [(num[j] + gg[j] * inter[j]) / jnp.maximum(jnp.abs(den[j]), emt[j]) for j in R]
        msq = [_dot2(h[j] * h[j], seg64b) * (1.0 / HEAD_DIM) for j in R]
        for j in R:
            y = h[j] * lax.rsqrt(msq[j] + EPS) * ng
            out_ref[pl.ds(r0[j], L), :] = (jax.nn.sigmoid(o_ref[pl.ds(r0[j], L), :]) * y).astype(out_ref.dtype)
        return carry

    lax.fori_loop(0, TS // (L * G), group_body, 0)
    c_out[...] = cs[...]
    n_out[...] = ns[...]
    m_out[...] = ms[...]


def _mlstm(z3, conv0, c0bd, n0, m0, gate_b, conv_w, norm_g, n_valid, chunk):
    B, S, _ = z3.shape
    ts = min(512, S)
    assert S % ts == 0 and ts % chunk == 0
    blk = lambda w, c0: pl.BlockSpec((None, ts, w), lambda b, t: (b, t, c0 // w))
    per_b = lambda r, w: pl.BlockSpec((None, r, w), lambda b, t: (b, 0, 0))
    return pl.pallas_call(
        functools.partial(_mlstm_kernel, tile=ts, chunk=chunk, n_valid=n_valid),
        out_shape=[jax.ShapeDtypeStruct((B, S, QUAD), BF16),
                   jax.ShapeDtypeStruct((B, QUAD, QUAD), F32),
                   jax.ShapeDtypeStruct((B, 1, QUAD), F32),
                   jax.ShapeDtypeStruct((B, 1, QUAD), F32)],
        grid=(B, S // ts),
        in_specs=[blk(2 * QUAD, COL_MQK), blk(QUAD, COL_MV), blk(QUAD, COL_MO), blk(QUAD, COL_GI), blk(QUAD, COL_GF),
                  per_b(SUBLANES, 2 * QUAD), per_b(QUAD, QUAD), per_b(1, QUAD), per_b(1, QUAD),
                  _const_spec((2, QUAD)), _const_spec((CONV_WIDTH, 2 * QUAD)), _const_spec((1, QUAD))],
        out_specs=[pl.BlockSpec((None, ts, QUAD), lambda b, t: (b, t, 0)),
                   per_b(QUAD, QUAD), per_b(1, QUAD), per_b(1, QUAD)],
        scratch_shapes=[pltpu.VMEM((ts + 2 * SUBLANES, 2 * QUAD), F32),
                        pltpu.VMEM((QUAD, QUAD), F32), pltpu.VMEM((1, QUAD), F32), pltpu.VMEM((1, QUAD), F32)],
        compiler_params=_cparams(("parallel", "arbitrary")),
        name="mlstm",
    )(z3, z3, z3, z3, z3, conv0, c0bd, n0, m0, gate_b, conv_w, norm_g)


HGRN_FAST_CHUNK = 64
HGRN_SAFE_DECAY = 80.0


def _hgrn_kernel(q_ref, f_ref, i_ref, g_ref, s0_ref, lb_ref, ng_ref, out_ref, s_out, st, qs, fs, ks, *,
                 tile, sub, n_valid, fast):
    TS = tile
    t = pl.program_id(1)

    @pl.when(t == 0)
    def _():
        st[...] = s0_ref[...]

    seg64 = _seg_mask(QUAD, HEAD_SHIFT, QUAD, HEAD_SHIFT)
    seg64b = seg64.astype(BF16)
    lb = lb_ref[...]
    ng = ng_ref[...]
    nt = (((1,), (1,)), ((), ()))
    tn = (((0,), (0,)), ((), ()))

    P = HGRN_FAST_CHUNK if fast else sub
    prow = _iota2((P, QUAD), 0)

    def prep(c, worst):
        r0 = pl.multiple_of(c * P, P)
        hq = q_ref[pl.ds(r0, P), :]
        f = lb + (1.0 - lb) * jax.nn.sigmoid(f_ref[pl.ds(r0, P), :])
        logf = jnp.log(f)
        kk = 1.0 - f
        if n_valid < TS:
            ok = (prow + r0) < n_valid
            logf = jnp.where(ok, logf, 0.0)
            kk = jnp.where(ok, kk, 0.0)
        qs[pl.ds(r0, P), :] = hq * jax.nn.sigmoid(hq)
        fs[pl.ds(r0, P), :] = logf
        ks[pl.ds(r0, P), :] = kk
        return jnp.minimum(worst, jnp.sum(logf, axis=0, keepdims=True))

    worst = lax.fori_loop(0, TS // P, prep, jnp.zeros((1, QUAD), F32))

    def finish(o, r0, rows, smat, b, kk, iv, qt):
        o = o + lax.dot_general(qt, smat.astype(BF16), nt, preferred_element_type=F32)
        out_ref[pl.ds(r0, rows), :] = _head_rmsnorm_gate(o, seg64b, ng, g_ref[pl.ds(r0, rows), :]).astype(out_ref.dtype)
        bl = b[rows - 1:rows, :]
        ktil = kk * jnp.exp(bl - b)
        u = lax.dot_general(iv.astype(BF16), ktil.astype(BF16), tn, preferred_element_type=F32)
        st[...] = smat * jnp.exp(bl) + jnp.where(seg64, u, 0.0)

    def fast_loop():
        L = HGRN_FAST_CHUNK
        tril = (_iota2((L, L), 1) <= _iota2((L, L), 0)).astype(BF16)
        causal = (_iota2((L, QUAD), 1) & (L - 1)) <= _iota2((L, QUAD), 0)
        zb = jnp.zeros((QUAD, QUAD), BF16)

        G = min(CHUNK_GROUP, TS // L)
        assert TS % (L * G) == 0

        def body(gi, carry):
            R = range(G)
            r0 = [pl.multiple_of((gi * G + j) * L, L) for j in R]
            kk = [ks[pl.ds(r0[j], L), :] for j in R]
            ivb = [i_ref[pl.ds(r0[j], L), :].astype(BF16) for j in R]
            b = [_dot2r(tril, fs[pl.ds(r0[j], L), :]) for j in R]
            qt = [(qs[pl.ds(r0[j], L), :] * jnp.exp(b[j])).astype(BF16) for j in R]
            kbd = [jnp.where(seg64, jnp.concatenate([(kk[j] * jnp.exp(-b[j])).astype(BF16)] * N_HGRN_HEADS, axis=0), zb)
                   for j in R]
            ibd = [jnp.where(seg64, jnp.concatenate([ivb[j]] * N_HGRN_HEADS, axis=0), zb) for j in R]
            amat = [lax.dot_general(qt[j], kbd[j], nt, preferred_element_type=F32) for j in R]
            ktil = [(kk[j] * jnp.exp(b[j][L - 1:L, :] - b[j])).astype(BF16) for j in R]
            u = [lax.dot_general(ivb[j], ktil[j], tn, preferred_element_type=F32) for j in R]
            o = [jnp.dot(jnp.where(causal, amat[j], 0.0).astype(BF16), ibd[j], preferred_element_type=F32) for j in R]
            smat = [st[...]]
            for j in R:
                smat.append(smat[j] * jnp.exp(b[j][L - 1:L, :]) + jnp.where(seg64, u[j], 0.0))
            st[...] = smat[G]
            inter = [lax.dot_general(qt[j], smat[j].astype(BF16), nt, preferred_element_type=F32) for j in R]
            o = [o[j] + inter[j] for j in R]
            msq = [_dot2(o[j] * o[j], seg64b) * (1.0 / HEAD_DIM) for j in R]
            for j in R:
                y = o[j] * lax.rsqrt(msq[j] + EPS) * ng
                out_ref[pl.ds(r0[j], L), :] = (jax.nn.sigmoid(g_ref[pl.ds(r0[j], L), :]) * y).astype(out_ref.dtype)
            return carry

        lax.fori_loop(0, TS // (L * G), body, 0)

    def exact_loop():
        L = sub
        row = _iota2((L, QUAD), 0)

        def body(c, carry):
            r0 = pl.multiple_of(c * L, L)
            q = qs[pl.ds(r0, L), :]
            kk = ks[pl.ds(r0, L), :]
            iv = i_ref[pl.ds(r0, L), :]
            b = _cumulate_rows(fs[pl.ds(r0, L), :], L, jnp.add, 0.0)
            parts = []
            for j in range(L):
                dec = jnp.exp(jnp.where(row >= j, b - b[j:j + 1, :], NEG_BIG))
                parts.append(dec * q * kk[j:j + 1, :])
            tstack = jnp.concatenate(parts, axis=0).astype(BF16)
            y = jnp.dot(tstack, seg64b, preferred_element_type=F32)
            o = y[0:L] * iv[0:1, :]
            for j in range(1, L):
                o = o + y[j * L:(j + 1) * L] * iv[j:j + 1, :]
            finish(o, r0, L, st[...], b, kk, iv, (q * jnp.exp(b)).astype(BF16))
            return carry

        lax.fori_loop(0, TS // L, body, 0, unroll=min(CHUNK_UNROLL, TS // L))

    if fast:
        safe = jnp.min(worst) > -HGRN_SAFE_DECAY
        pl.when(safe)(fast_loop)
        pl.when(jnp.logical_not(safe))(exact_loop)
    else:
        exact_loop()
    s_out[...] = st[...]


def _hgrn(z3, s0t, lb, norm_g, n_valid, sub):
    B, S, _ = z3.shape
    ts = min(512, S)
    fast = ts % HGRN_FAST_CHUNK == 0
    assert S % ts == 0 and ts % sub == 0
    blk = lambda c0: pl.BlockSpec((None, ts, QUAD), lambda b, t: (b, t, c0 // QUAD))
    per_b = pl.BlockSpec((None, QUAD, QUAD), lambda b, t: (b, 0, 0))
    stage = lambda: pltpu.VMEM((ts, QUAD), F32)
    return pl.pallas_call(
        functools.partial(_hgrn_kernel, tile=ts, sub=sub, n_valid=n_valid, fast=fast),
        out_shape=[jax.ShapeDtypeStruct((B, S, QUAD), BF16), jax.ShapeDtypeStruct((B, QUAD, QUAD), F32)],
        grid=(B, S // ts),
        in_specs=[blk(COL_HQ), blk(COL_HF), blk(COL_HI), blk(COL_HG), per_b,
                  _const_spec((1, QUAD)), _const_spec((1, QUAD))],
        out_specs=[pl.BlockSpec((None, ts, QUAD), lambda b, t: (b, t, 0)), per_b],
        scratch_shapes=[pltpu.VMEM((QUAD, QUAD), F32), stage(), stage(), stage()],
        compiler_params=_cparams(("parallel", "arbitrary")),
        name="hgrn",
    )(z3, z3, z3, z3, s0t, lb, norm_g)


PROJ_SPLITS = (ATTN_WIDTH, ATTN_WIDTH, ATTN_WIDTH,
               MLSTM_WIDTH, MLSTM_WIDTH, MLSTM_WIDTH, MLSTM_WIDTH, N_MLSTM_HEADS, N_MLSTM_HEADS,
               HGRN_WIDTH, HGRN_WIDTH, HGRN_WIDTH, HGRN_WIDTH)


def _prep_w_in(w):
    cuts = [int(c) for c in np.cumsum(PROJ_SPLITS)[:-1]]
    aq, ak, av, mq, mk, mv, mo, mi, mf, hq, hf, hi, hg = jnp.split(w, cuts, axis=1)
    gi, gf = jnp.repeat(mi, HEAD_DIM, axis=1), jnp.repeat(mf, HEAD_DIM, axis=1)
    return jnp.concatenate([aq, ak, av, mq, mk, mv, mo, hq, hf, hi, hg, gi, gf], axis=1).astype(BF16)


HGRN_SUBCHUNK = 16


class _LayerWeights(NamedTuple):
    g_mix: jax.Array
    w_in: jax.Array
    wkv_t: jax.Array
    gate_b: jax.Array
    conv_w: jax.Array
    mlstm_g: jax.Array
    lb: jax.Array
    hgrn_g: jax.Array
    w_out: jax.Array
    g_ffn: jax.Array
    w_up: jax.Array
    w_down: jax.Array
    g_final: jax.Array


def _gate_row(gate_b):
    return jnp.repeat(gate_b.astype(F32), HEAD_DIM, axis=1)


def _embed_block_diag(s):
    b, h, e, f = s.shape
    out = jnp.zeros((b, h, e, h, f), F32)
    for i in range(h):
        out = out.at[:, i, :, i, :].set(s[:, i].astype(F32))
    return out.reshape(b, h * e, h * f)


def _extract_block_diag(s, h):
    b, r, c = s.shape
    s5 = s.reshape(b, h, r // h, h, c // h)
    return jnp.stack([s5[:, i, :, i, :] for i in range(h)], axis=1)


def _trunk_layer(x2d, B, S, n_valid, attn_fn, states, w, final, chunk_m, chunk_h, n_keep=None):
    conv_buf, C0, n0, m0, S0 = states
    if n_keep is None:
        z = _inproj(x2d, w.g_mix, w.w_in)
    else:
        z, kt, vt = _inproj(x2d, w.g_mix, w.w_in, w.wkv_t, S, n_keep)
    z3 = z.reshape(B, S, ZW)
    attn = attn_fn(z3)
    conv0 = jnp.zeros((B, SUBLANES, 2 * MLSTM_WIDTH), F32).at[:, SUBLANES - (CONV_WIDTH - 1):].set(conv_buf.astype(F32))
    m0r = jnp.repeat(m0.astype(F32), HEAD_DIM, axis=1).reshape(B, 1, MLSTM_WIDTH)
    ml, cbd, n, m = _mlstm(z3, conv0, _embed_block_diag(C0), n0.astype(F32).reshape(B, 1, MLSTM_WIDTH), m0r,
                           w.gate_b, w.conv_w, w.mlstm_g, n_valid, chunk_m)
    s0t = _embed_block_diag(jnp.swapaxes(S0, -1, -2))
    hg, st = _hgrn(z3, s0t, w.lb, w.hgrn_g, n_valid, chunk_h)
    n2 = B * S
    y = _outffn(x2d, attn.reshape(n2, ATTN_WIDTH), ml.reshape(n2, MLSTM_WIDTH), hg.reshape(n2, HGRN_WIDTH),
                w.w_out, w.g_ffn, w.w_up, w.w_down, w.g_final, final)
    if n_keep is None:
        k_rows = z3[:, :n_valid, COL_AK:COL_AK + ATTN_WIDTH].reshape(B, n_valid, N_ATTN_HEADS, HEAD_DIM)
        v_rows = z3[:, :n_valid, COL_AV:COL_AV + ATTN_WIDTH].reshape(B, n_valid, N_ATTN_HEADS, HEAD_DIM)
    else:
        k_rows = jnp.transpose(kt.reshape(B, N_ATTN_HEADS, HEAD_DIM, n_keep), (0, 3, 1, 2))
        v_rows = jnp.transpose(vt.reshape(B, N_ATTN_HEADS, HEAD_DIM, n_keep), (0, 3, 1, 2))
    conv_new = z3[:, n_valid - (CONV_WIDTH - 1):n_valid, COL_MQK:COL_MQK + 2 * MLSTM_WIDTH]
    c_new = _extract_block_diag(cbd, N_MLSTM_HEADS)
    s_new = jnp.swapaxes(_extract_block_diag(st, N_HGRN_HEADS), -1, -2)
    return y, (k_rows, v_rows, conv_new, c_new, n.reshape(B, N_MLSTM_HEADS, HEAD_DIM), m[:, 0, ::HEAD_DIM], s_new)


def kernel(x_prompt, x_sample, cache_attn_k, cache_attn_v, state_mlstm_conv, state_mlstm_C, state_mlstm_n, state_mlstm_m, state_hgrn_S, rel_bias, g_mix, w_in, mlstm_gate_b, mlstm_conv_w, mlstm_norm_g, hgrn_lb_raw, hgrn_norm_g, w_out, g_ffn, w_up, w_down, g_final):
    depth = w_in.shape[0]
    bp, sp, _ = x_prompt.shape
    bs, ts, _ = x_sample.shape
    n_keep = min(max(w for w, _ in DILATED_PATTERNS), sp)
    n_past = cache_attn_k.shape[2]
    lb_p = jax.nn.softmax(hgrn_lb_raw.astype(F32), axis=0)
    hgrn_lb = jnp.cumsum(lb_p, axis=0) - lb_p[0]
    bias_dist = _bias_by_distance(rel_bias)
    p_tables = _prompt_tables(bias_dist)
    s_tables = _sample_tables(bias_dist, n_past, ts)
    cache_kt = jnp.transpose(cache_attn_k, (0, 1, 3, 4, 2)).reshape(depth, bs, ATTN_WIDTH, n_past)
    cache_vt = jnp.transpose(cache_attn_v, (0, 1, 3, 4, 2)).reshape(depth, bs, ATTN_WIDTH, n_past)

    hp = x_prompt.reshape(bp * sp, D_MODEL)
    hs = jnp.zeros((bs, TPAD, D_MODEL), F32).at[:, :ts].set(x_sample).reshape(bs * TPAD, D_MODEL)
    zero_states = (jnp.zeros((bp, CONV_WIDTH - 1, 2 * MLSTM_WIDTH), F32),
                   jnp.zeros((bp, N_MLSTM_HEADS, HEAD_DIM, HEAD_DIM), F32),
                   jnp.zeros((bp, N_MLSTM_HEADS, HEAD_DIM), F32),
                   jnp.zeros((bp, N_MLSTM_HEADS), F32),
                   jnp.zeros((bp, N_HGRN_HEADS, HEAD_DIM, HEAD_DIM), F32))
    p_states, s_states = [], []
    for l in range(depth):
        final = l == depth - 1
        w_in_l = _prep_w_in(w_in[l])
        weights = _LayerWeights(
            g_mix=g_mix[l][None], w_in=w_in_l, wkv_t=w_in_l[:, COL_AK:COL_AV + ATTN_WIDTH].T,
            gate_b=_gate_row(mlstm_gate_b[l]),
            conv_w=mlstm_conv_w[l], mlstm_g=mlstm_norm_g[l][None], lb=hgrn_lb[l][None], hgrn_g=hgrn_norm_g[l][None],
            w_out=w_out[l].astype(BF16), g_ffn=g_ffn[l][None], w_up=w_up[l].astype(BF16),
            w_down=w_down[l].astype(BF16), g_final=g_final[None])
        hp, st = _trunk_layer(hp, bp, sp, sp, functools.partial(_attn_prompt, bias=p_tables),
                              zero_states, weights, final, MLSTM_CHUNK, HGRN_SUBCHUNK, n_keep)
        p_states.append(st)
        sample_attn = functools.partial(_attn_sample, cache_kt=cache_kt, cache_vt=cache_vt, layer=l, tables=s_tables)
        states = (state_mlstm_conv[l], state_mlstm_C[l], state_mlstm_n[l], state_mlstm_m[l], state_hgrn_S[l])
        hs, st = _trunk_layer(hs, bs, TPAD, ts, sample_attn, states, weights, final, TPAD, TPAD)
        s_states.append(st)
    y_prompt = hp.reshape(bp, sp, D_MODEL)
    y_sample = hs.reshape(bs, TPAD, D_MODEL)[:, :ts]
    p_out = [jnp.stack(a) for a in zip(*p_states)]
    s_out = [jnp.stack(a) for a in zip(*s_states)]
    return (y_prompt, y_sample, *p_out, *s_out)
```

```python
import functools
from typing import NamedTuple

import jax
import jax.numpy as jnp
import numpy as np
from jax import lax
from jax.experimental import pallas as pl
from jax.experimental.pallas import tpu as pltpu

F32 = jnp.float32
BF16 = jnp.bfloat16

D_MODEL = 1024
HEAD_DIM = 64
N_ATTN_HEADS = 8
N_MLSTM_HEADS = 4
N_HGRN_HEADS = 4
ATTN_WIDTH = N_ATTN_HEADS * HEAD_DIM
MLSTM_WIDTH = N_MLSTM_HEADS * HEAD_DIM
HGRN_WIDTH = N_HGRN_HEADS * HEAD_DIM
DILATED_PATTERNS = ((128, 1), (512, 4), (2048, 16))
QBLOCK = 128
N_REL_BUCKETS = 32
REL_MAX_DISTANCE = 2048
CONV_WIDTH = 4
MLSTM_CHUNK = 64
D_FF = 4 * D_MODEL
EPS = 1e-6
NEG_BIG = -1e30

LANES = 128
SUBLANES = 8

COL_AQ = 0
COL_AK = COL_AQ + ATTN_WIDTH
COL_AV = COL_AK + ATTN_WIDTH
COL_MQK = COL_AV + ATTN_WIDTH
COL_MV = COL_MQK + 2 * MLSTM_WIDTH
COL_MO = COL_MV + MLSTM_WIDTH
COL_HQ = COL_MO + MLSTM_WIDTH
COL_HF = COL_HQ + HGRN_WIDTH
COL_HI = COL_HF + HGRN_WIDTH
COL_HG = COL_HI + HGRN_WIDTH
COL_GI = COL_HG + HGRN_WIDTH
COL_GF = COL_GI + MLSTM_WIDTH
ZW = COL_GF + MLSTM_WIDTH

VMEM_LIMIT = 56 * 1024 * 1024


def _cparams(sem, vmem=VMEM_LIMIT):
    return pltpu.CompilerParams(dimension_semantics=sem, vmem_limit_bytes=vmem)


def _const_spec(shape):
    nd = len(shape)
    return pl.BlockSpec(shape, lambda *_: (0,) * nd, pipeline_mode=pl.Buffered(1))


def _inproj_kernel(x_ref, g_ref, w_ref, *rest, tiles_per_seq, first_keep, n_prev=0):
    x = x_ref[...]
    ms = jnp.mean(x * x, axis=-1, keepdims=True)
    xn = ((x * lax.rsqrt(ms + EPS)) * g_ref[...]).astype(BF16)
    if first_keep is None:
        (z_ref,) = rest
        z_ref[...] = jnp.dot(xn, w_ref[...], preferred_element_type=F32)
        return
    wkv_ref = rest[0]
    z_ref, kt_ref, vt_ref = rest[1 + n_prev:]
    z_ref[...] = jnp.dot(xn, w_ref[...], preferred_element_type=F32)

    @pl.when(pl.program_id(0) % tiles_per_seq >= first_keep)
    def _():
        kv = lax.dot_general(wkv_ref[...], xn, (((1,), (1,)), ((), ())), preferred_element_type=F32)
        kt_ref[...] = kv[0:ATTN_WIDTH]
        vt_ref[...] = kv[ATTN_WIDTH:]


def _inproj(x2d, g, w, wkv_t=None, seq=None, n_keep=None, stacked=None):
    n = x2d.shape[0]
    tm = min(512, n)
    assert n % tm == 0
    x_spec = pl.BlockSpec((tm, D_MODEL), lambda i: (i, 0))
    z_spec = pl.BlockSpec((tm, ZW), lambda i: (i, 0))
    z_shape = jax.ShapeDtypeStruct((n, ZW), F32)
    if wkv_t is None:
        return pl.pallas_call(
            functools.partial(_inproj_kernel, tiles_per_seq=None, first_keep=None),
            out_shape=z_shape, grid=(n // tm,),
            in_specs=[x_spec, _const_spec((1, D_MODEL)), _const_spec((D_MODEL, ZW))],
            out_specs=z_spec, compiler_params=_cparams(("parallel",)), name="inproj",
        )(x2d, g, w)
    assert seq % tm == 0 and n_keep % tm == 0
    layer, depth, prev = stacked
    tps, first_keep = seq // tm, (seq - n_keep) // tm
    t_spec = pl.BlockSpec((None, None, ATTN_WIDTH, tm),
                          lambda i: (layer, i // tps, 0, jnp.maximum(i % tps - first_keep, 0)))
    t_shape = jax.ShapeDtypeStruct((depth, n // seq, ATTN_WIDTH, n_keep), F32)
    in_specs = [x_spec, _const_spec((1, D_MODEL)), _const_spec((D_MODEL, ZW)), _const_spec((2 * ATTN_WIDTH, D_MODEL))]
    args = [x2d, g, w, wkv_t]
    aliases = {}
    if prev is not None:
        in_specs += [pl.BlockSpec(memory_space=pl.ANY)] * 2
        aliases = {len(args): 1, len(args) + 1: 2}
        args += list(prev)
    return pl.pallas_call(
        functools.partial(_inproj_kernel, tiles_per_seq=tps, first_keep=first_keep, n_prev=len(args) - 4),
        out_shape=[z_shape, t_shape, t_shape], grid=(n // tm,),
        in_specs=in_specs, out_specs=[z_spec, t_spec, t_spec], input_output_aliases=aliases,
        compiler_params=_cparams(("arbitrary",)), name="inproj_kt",
    )(*args)


FF_CHUNK = 1024


def _outffn_kernel(x_ref, a_ref, m_ref, h_ref, wo_ref, gf_ref, wu_ref, wd_ref, gl_ref, y_ref, xn_sc, *, final):
    x1 = x_ref[...]
    x1 = x1 + jnp.dot(a_ref[...], wo_ref[0:ATTN_WIDTH, :], preferred_element_type=F32)
    x1 = x1 + jnp.dot(m_ref[...], wo_ref[ATTN_WIDTH:ATTN_WIDTH + MLSTM_WIDTH, :], preferred_element_type=F32)
    x1 = x1 + jnp.dot(h_ref[...], wo_ref[ATTN_WIDTH + MLSTM_WIDTH:, :], preferred_element_type=F32)
    ms = jnp.mean(x1 * x1, axis=-1, keepdims=True)
    xn_sc[...] = ((x1 * lax.rsqrt(ms + EPS)) * gf_ref[...]).astype(BF16)
    y_ref[...] = x1
    for c in range(D_FF // FF_CHUNK):
        u = jnp.dot(xn_sc[...], wu_ref[:, c * FF_CHUNK:(c + 1) * FF_CHUNK], preferred_element_type=F32)
        hh = jnp.square(jnp.maximum(u, 0.0)).astype(BF16)
        y_ref[...] += jnp.dot(hh, wd_ref[c * FF_CHUNK:(c + 1) * FF_CHUNK, :], preferred_element_type=F32)
    if final:
        x2 = y_ref[...]
        ms2 = jnp.mean(x2 * x2, axis=-1, keepdims=True)
        y_ref[...] = (x2 * lax.rsqrt(ms2 + EPS)) * gl_ref[...]


def _outffn(x2d, attn, ml, hg, w_out, g_ffn, w_up, w_down, g_final, final):
    n = x2d.shape[0]
    tm = min(512, n)
    assert n % tm == 0
    row = lambda w: pl.BlockSpec((tm, w), lambda i: (i, 0))
    return pl.pallas_call(
        functools.partial(_outffn_kernel, final=final),
        out_shape=jax.ShapeDtypeStruct((n, D_MODEL), F32),
        grid=(n // tm,),
        in_specs=[row(D_MODEL), row(ATTN_WIDTH), row(MLSTM_WIDTH), row(HGRN_WIDTH),
                  _const_spec((D_MODEL, D_MODEL)), _const_spec((1, D_MODEL)),
                  _const_spec((D_MODEL, D_FF)), _const_spec((D_FF, D_MODEL)), _const_spec((1, D_MODEL))],
        out_specs=row(D_MODEL),
        scratch_shapes=[pltpu.VMEM((tm, D_MODEL), BF16)],
        compiler_params=_cparams(("parallel",)),
        name="outffn",
    )(x2d, attn, ml, hg, w_out, g_ffn, w_up, w_down, g_final)


def _t5_causal_bucket(dist):
    n = np.asarray(dist).astype(np.int32)
    max_exact = N_REL_BUCKETS // 2
    scaled = np.log(np.maximum(n, 1) / max_exact) / np.log(REL_MAX_DISTANCE / max_exact)
    large = np.minimum(max_exact + (scaled * (N_REL_BUCKETS - max_exact)).astype(np.int32), N_REL_BUCKETS - 1)
    return np.where(n < max_exact, n, large).astype(np.int32)


BIAS_DIST = 2304


def _bias_by_distance(rel_bias):
    assert BIAS_DIST > max(w for w, _ in DILATED_PATTERNS) + SUBLANES
    return rel_bias.astype(F32)[_t5_causal_bucket(np.arange(BIAS_DIST)[::-1])].T


def _dist_slice(bias_desc, lo, hi, step=1):
    last = BIAS_DIST - 1
    return bias_desc[:, last - hi:last - lo + 1:step]


def _prompt_tables(bias_dist):
    H = N_ATTN_HEADS
    a = np.arange(QBLOCK)[:, None]
    b = np.arange(2 * QBLOCK)[None, :]
    rel = QBLOCK + a - b
    band = (rel >= 0) & (rel <= QBLOCK)
    period = 3 * QBLOCK
    biases = []
    for window, dil in DILATED_PATTERNS:
        nsub = window // dil
        assert nsub == QBLOCK
        vd = _dist_slice(bias_dist, 0, nsub * dil, dil)
        rp = jnp.concatenate([vd, jnp.broadcast_to(vd[:, -1:], (H, QBLOCK)),
                              jnp.broadcast_to(vd[:, 0:1], (H, QBLOCK - 1))], axis=1)
        skew = jnp.tile(rp, (1, QBLOCK))[:, :QBLOCK * (period - 1)].reshape(H, QBLOCK, period - 1)
        biases.append(skew[:, :, :2 * QBLOCK].reshape(H // 2, 2 * QBLOCK, 2 * QBLOCK))
    mask = np.stack([np.tile(band, (2, 1)), np.tile(band & (b >= QBLOCK), (2, 1))])
    table = jnp.where(mask[None, None], jnp.stack(biases)[:, :, None], -jnp.inf)
    return jnp.swapaxes(table, -1, -2)


ATTN_GROUP = 8


def _attn_prompt_kernel(q_ref, k_ref, v_ref, bias_ref, out_ref,
                        x4, qs, ks, vts, o1, o4, o16, l1, l4, l16, *, seq):
    S = seq
    nblk = S // QBLOCK
    scale = HEAD_DIM ** -0.5
    zpad = jnp.zeros((QBLOCK, LANES), BF16)
    for di in range(3):
        ks[di, 0:QBLOCK, :] = zpad
        vts[di, 0] = zpad

    R = 512

    def put(src, di, row0, blk):
        if src is q_ref:
            qs[di, row0:row0 + blk.shape[0], :] = (blk * scale).astype(BF16)
        elif src is k_ref:
            ks[di, QBLOCK + row0:QBLOCK + row0 + blk.shape[0], :] = blk.astype(BF16)
        else:
            blk_t = blk.T.astype(BF16)
            for t in range(blk.shape[0] // QBLOCK):
                vts[di, 1 + row0 // QBLOCK + t] = blk_t[:, t * QBLOCK:(t + 1) * QBLOCK]

    for src in (q_ref, k_ref, v_ref):
        for i in range(S // R):
            put(src, 0, i * R, src[i * R:(i + 1) * R, :])
        for r in range(4):
            for i in range(S // 4 // R):
                blk = src[pl.ds(r + 4 * i * R, R, stride=4), :]
                row0 = r * (S // 4) + i * R
                x4[row0:row0 + R, :] = blk
                put(src, 1, row0, blk)
        n16 = S // 16
        for c4 in range(4):
            for a in range(4):
                put(src, 2, (c4 * 4 + a) * n16, x4[pl.ds(c4 * (S // 4) + a, n16, stride=4), :])

    lane = lax.broadcasted_iota(jnp.int32, (QBLOCK, LANES), 1)
    head_a = lane < HEAD_DIM
    row_a = lax.broadcasted_iota(jnp.int32, (QBLOCK, LANES), 0) < HEAD_DIM
    nt = (((1,), (1,)), ((), ()))

    def branch(di, dil, o_ref, l_ref):
        per_class = (S // dil) // QBLOCK

        def body(gi, carry):
            R = range(ATTN_GROUP)
            idx = [gi * ATTN_GROUP + j for j in R]
            base = [pl.multiple_of(i * QBLOCK, QBLOCK) for i in idx]
            n = [i % per_class for i in idx]
            zero = jnp.zeros((QBLOCK, LANES), BF16)
            st = []
            for j in R:
                q = qs[di, pl.ds(base[j], QBLOCK), :]
                kk = ks[di, pl.ds(base[j], 2 * QBLOCK), :]
                q2 = jnp.concatenate([jnp.where(head_a, q, zero), jnp.where(head_a, zero, q)], axis=0)
                st.append(lax.dot_general(kk, q2, nt, preferred_element_type=F32))
            st = [st[j] + bias_ref[di, jnp.where(n[j] == 0, 1, 0)] for j in R]
            m = [jnp.max(st[j], axis=0, keepdims=True) for j in R]
            p = [jnp.exp(st[j] - m[j]) for j in R]
            l = [jnp.sum(p[j], axis=0, keepdims=True) for j in R]
            ot = [jnp.dot(jnp.concatenate([vts[di, idx[j]], vts[di, idx[j] + 1]], axis=1), p[j].astype(BF16),
                          preferred_element_type=F32) for j in R]
            for j in R:
                otn = ot[j] / l[j]
                lse = m[j] + jnp.log(l[j])
                o = jnp.where(row_a, otn[:, 0:QBLOCK], otn[:, QBLOCK:]).T
                ls = jnp.where(row_a, jnp.broadcast_to(lse[:, 0:QBLOCK], (QBLOCK, LANES)),
                               jnp.broadcast_to(lse[:, QBLOCK:], (QBLOCK, LANES))).T
                if dil == 1:
                    o_ref[pl.ds(base[j], QBLOCK), :] = o
                    l_ref[pl.ds(base[j], QBLOCK), :] = ls
                else:
                    c = idx[j] // per_class
                    res = c if dil == 4 else 4 * (c % 4) + c // 4
                    start = dil * QBLOCK * n[j] + res
                    o_ref[pl.ds(start, QBLOCK, stride=dil), :] = o
                    l_ref[pl.ds(start, QBLOCK, stride=dil), :] = ls
            return carry

        lax.fori_loop(0, nblk // ATTN_GROUP, body, 0)

    branch(0, 1, o1, l1)
    branch(1, 4, o4, l4)
    branch(2, 16, o16, l16)

    T = 256

    def mix(i, carry):
        r0 = pl.multiple_of(i * T, T)
        la, lb, lc = l1[pl.ds(r0, T), :], l4[pl.ds(r0, T), :], l16[pl.ds(r0, T), :]
        mx = jnp.maximum(jnp.maximum(la, lb), lc)
        ea, eb, ec = jnp.exp(la - mx), jnp.exp(lb - mx), jnp.exp(lc - mx)
        num = ea * o1[pl.ds(r0, T), :] + eb * o4[pl.ds(r0, T), :] + ec * o16[pl.ds(r0, T), :]
        out_ref[pl.ds(r0, T), :] = (num / (ea + eb + ec)).astype(out_ref.dtype)
        return carry

    lax.fori_loop(0, S // T, mix, 0)


def _attn_prompt(z3, bias):
    B, S, _ = z3.shape
    assert S % (16 * QBLOCK) == 0
    npair = N_ATTN_HEADS // 2
    col = lambda c0: pl.BlockSpec((None, S, LANES), lambda b, p: (b, 0, c0 // LANES + p))
    f32s = lambda: pltpu.VMEM((S, LANES), F32)
    return pl.pallas_call(
        functools.partial(_attn_prompt_kernel, seq=S),
        out_shape=jax.ShapeDtypeStruct((B, S, ATTN_WIDTH), BF16),
        grid=(B, npair),
        in_specs=[col(COL_AQ), col(COL_AK), col(COL_AV),
                  pl.BlockSpec((3, None, 2, 2 * QBLOCK, 2 * QBLOCK), lambda b, p: (0, p, 0, 0, 0))],
        out_specs=pl.BlockSpec((None, S, LANES), lambda b, p: (b, 0, p)),
        scratch_shapes=[f32s(),
                        pltpu.VMEM((3, S, LANES), BF16),
                        pltpu.VMEM((3, S + QBLOCK, LANES), BF16),
                        pltpu.VMEM((3, S // QBLOCK + 1, LANES, QBLOCK), BF16),
                        f32s(), f32s(), f32s(), f32s(), f32s(), f32s()],
        compiler_params=_cparams(("parallel", "parallel")),
        name="attn_prompt",
    )(z3, z3, z3, bias)


TPAD = SUBLANES
TAIL = 512


def _sample_tables(bias_dist, n_past, n_tok):
    assert n_past >= max(w for w, _ in DILATED_PATTERNS) and n_tok <= TPAD and TAIL >= DILATED_PATTERNS[1][0]
    H = N_ATTN_HEADS
    t = np.arange(TPAD)[:, None]

    def by_row(width):
        rows = [_dist_slice(bias_dist, tt + 1, tt + width) for tt in range(TPAD)]
        return jnp.stack(rows, axis=1).reshape(H * TPAD, width)

    bias_tail, bias_full = by_row(TAIL), by_row(n_past)
    zero_dist = _dist_slice(bias_dist, 0, 0)
    new_rows = [jnp.concatenate([_dist_slice(bias_dist, 0, tt),
                                 jnp.broadcast_to(zero_dist, (H, TPAD - tt - 1))], axis=1) for tt in range(TPAD)]
    bias_new = jnp.stack(new_rows, axis=1).reshape(H * TPAD, TPAD)

    def valid(dj, dil, nsub, lo):
        ok = (dj % dil == 0) & (dj // dil >= lo) & (dj // dil <= nsub) & (t < n_tok)
        return np.tile(ok, (H, 1)).astype(np.float32)

    tn = np.arange(TPAD)[None, :]
    m_tail, m_new = [], []
    for window, dil in DILATED_PATTERNS:
        nsub = window // dil
        if window <= TAIL:
            m_tail.append(valid(TAIL + t - np.arange(TAIL)[None, :], dil, nsub, 1))
        else:
            m_full = valid(n_past + t - np.arange(n_past)[None, :], dil, nsub, 1)
        m_new.append(valid(t - tn, dil, nsub, 0) * (tn < n_tok))
    return (bias_tail, bias_full, bias_new, jnp.asarray(np.stack(m_tail)), jnp.asarray(m_full),
            jnp.asarray(np.stack(m_new).astype(np.float32)))


def _attn_sample_kernel(q_ref, kn_ref, vn_ref, kt_ref, vt_ref, bt_ref, bf_ref, bn_ref, mt_ref, mf_ref, mn_ref,
                        out_ref, *, n_past):
    H = N_ATTN_HEADS
    rows = H * TPAD
    same_head = (_iota2((rows, ATTN_WIDTH), 0) >> 3) == (_iota2((rows, ATTN_WIDTH), 1) >> HEAD_SHIFT)
    q = q_ref[...] * (HEAD_DIM ** -0.5)
    qm = jnp.where(same_head, jnp.concatenate([q] * H, axis=0), 0.0).astype(BF16)
    nt = (((1,), (1,)), ((), ()))
    kn = kn_ref[...].astype(BF16)
    vn = vn_ref[...].astype(BF16)
    kt = kt_ref[...].astype(BF16)
    vt = vt_ref[...].astype(BF16)
    s_new = lax.dot_general(qm, kn, nt, preferred_element_type=F32)
    s_full = jnp.dot(qm, kt, preferred_element_type=F32)
    s_tail = s_full[:, n_past - TAIL:]

    outs, lses = [], []
    for di, (window, _) in enumerate(DILATED_PATTERNS):
        if window <= TAIL:
            sm = jnp.where(mt_ref[di] > 0.5, s_tail + bt_ref[...], NEG_BIG)
            vmain = vt[:, n_past - TAIL:]
        else:
            sm = jnp.where(mf_ref[...] > 0.5, s_full + bf_ref[...], NEG_BIG)
            vmain = vt
        sn = jnp.where(mn_ref[di] > 0.5, s_new + bn_ref[...], NEG_BIG)
        m = jnp.maximum(jnp.max(sm, axis=-1, keepdims=True), jnp.max(sn, axis=-1, keepdims=True))
        pm = jnp.exp(sm - m)
        pn = jnp.exp(sn - m)
        l = jnp.sum(pm, axis=-1, keepdims=True) + jnp.sum(pn, axis=-1, keepdims=True)
        o = jnp.dot(pn.astype(BF16), vn, preferred_element_type=F32)
        o = o + lax.dot_general(pm.astype(BF16), vmain, nt, preferred_element_type=F32)
        outs.append(o / l)
        lses.append(m + jnp.log(l))
    mx = jnp.maximum(jnp.maximum(lses[0], lses[1]), lses[2])
    es = [jnp.exp(ls - mx) for ls in lses]
    mixed = (es[0] * outs[0] + es[1] * outs[1] + es[2] * outs[2]) / (es[0] + es[1] + es[2])
    mixed = jnp.where(same_head, mixed, 0.0)
    acc = mixed[0:TPAD]
    for h in range(1, H):
        acc = acc + mixed[h * TPAD:(h + 1) * TPAD]
    out_ref[...] = acc.astype(out_ref.dtype)


def _attn_sample(z3, cache_kt, cache_vt, layer, tables):
    B, T, _ = z3.shape
    assert T == TPAD
    n_past = cache_kt.shape[-1]
    new = lambda c0: pl.BlockSpec((None, TPAD, ATTN_WIDTH), lambda b: (b, 0, c0 // ATTN_WIDTH))
    buf = pl.BlockSpec((None, None, ATTN_WIDTH, n_past), lambda b: (layer, b, 0, 0))
    return pl.pallas_call(
        functools.partial(_attn_sample_kernel, n_past=n_past),
        out_shape=jax.ShapeDtypeStruct((B, TPAD, ATTN_WIDTH), BF16),
        grid=(B,),
        in_specs=[new(COL_AQ), new(COL_AK), new(COL_AV), buf, buf] + [_const_spec(t.shape) for t in tables],
        out_specs=pl.BlockSpec((None, TPAD, ATTN_WIDTH), lambda b: (b, 0, 0)),
        compiler_params=_cparams(("parallel",)),
        name="attn_sample",
    )(z3, z3, z3, cache_kt, cache_vt, *tables)


QUAD = N_MLSTM_HEADS * HEAD_DIM
CHUNK_UNROLL = 2
CHUNK_GROUP = 4
HEAD_SHIFT = 6


def _iota2(shape, axis):
    return lax.broadcasted_iota(jnp.int32, shape, axis)


def _log2(n):
    k = int(n).bit_length() - 1
    assert 1 << k == n
    return k


def _seg_mask(rows, row_shift, cols, col_shift):
    return (_iota2((rows, cols), 0) >> row_shift) == (_iota2((rows, cols), 1) >> col_shift)


def _cumulate_rows(x, length, op, fill):
    row = _iota2(x.shape, 0) & (length - 1)
    sh = 1
    while sh < length:
        x = op(x, jnp.where(row >= sh, pltpu.roll(x, sh, axis=0), fill))
        sh *= 2
    return x


def _split2(x):
    hi = x.astype(BF16)
    lo = (x - hi.astype(F32)).astype(BF16)
    return hi, lo


def _dot2(x, w):
    hi, lo = _split2(x)
    return jnp.dot(hi, w, preferred_element_type=F32) + jnp.dot(lo, w, preferred_element_type=F32)


def _dot2r(w, x):
    hi, lo = _split2(x)
    return jnp.dot(w, hi, preferred_element_type=F32) + jnp.dot(w, lo, preferred_element_type=F32)


def _load_block_diag(ref, seg64):
    flat = ref[...].reshape(QUAD, HEAD_DIM)
    return jnp.where(seg64, jnp.concatenate([flat] * (QUAD // HEAD_DIM), axis=1), 0.0)


def _store_block_diag(ref, mat):
    for h in range(QUAD // HEAD_DIM):
        ref[h] = mat[h * HEAD_DIM:(h + 1) * HEAD_DIM, h * HEAD_DIM:(h + 1) * HEAD_DIM]


def _head_rmsnorm_gate(h, seg64b, g_row, gate_pre):
    ms = _dot2(h * h, seg64b) * (1.0 / HEAD_DIM)
    return jax.nn.sigmoid(gate_pre) * (h * lax.rsqrt(ms + EPS) * g_row)


def _mlstm_kernel(qk_ref, v_ref, o_ref, gi_ref, gf_ref, conv0_ref, c0_ref, n0_ref, m0_ref, gb_ref, cw_ref, ng_ref,
                  out_ref, c_out, n_out, m_out, xp, cs, ns, ms, *, tile, chunk, n_valid):
    TS, L = tile, chunk
    t = pl.program_id(1)
    PAD = SUBLANES

    @pl.when(t == 0)
    def _():
        cs[...] = _load_block_diag(c0_ref, _seg_mask(QUAD, HEAD_SHIFT, QUAD, HEAD_SHIFT))
        ns[...] = n0_ref[...]
        ms[...] = m0_ref[...]
        xp[0:PAD, :] = conv0_ref[...]

    @pl.when(t > 0)
    def _():
        xp[0:PAD, :] = xp[TS:TS + PAD, :]

    xp[PAD:PAD + TS, :] = qk_ref[...]

    LK = MLSTM_CHUNK
    seg64 = _seg_mask(QUAD, HEAD_SHIFT, QUAD, HEAD_SHIFT)
    seg64b = seg64.astype(BF16)
    dmask = (_iota2((LK, QUAD), 1) & (LK - 1)) == _iota2((LK, QUAD), 0)
    causal = (_iota2((L, QUAD), 1) & (LK - 1)) <= _iota2((L, QUAD), 0)
    tril = (_iota2((L, L), 1) <= _iota2((L, L), 0)).astype(BF16)
    ones_lk = jnp.ones((L, LK), BF16)
    row = _iota2((L, QUAD), 0)
    cw = cw_ref[...]
    gb = gb_ref[...]
    ng = ng_ref[...]

    def key_rows(x, fill):
        if L == LK:
            return x
        return jnp.concatenate([x, jnp.full((LK - L, QUAD), fill, x.dtype)], axis=0)

    G = min(CHUNK_GROUP, TS // L)
    assert TS % (L * G) == 0
    nt = (((1,), (1,)), ((), ()))
    tn = (((0,), (0,)), ((), ()))
    zb = jnp.zeros((QUAD, QUAD), BF16)

    def group_body(gi, carry):
        R = range(G)
        r0 = [pl.multiple_of((gi * G + j) * L, L) for j in R]
        q, k, v, ig, lf = [], [], [], [], []
        for j in R:
            win = xp[pl.ds(r0[j], L + PAD), :]
            acc = win[PAD:PAD + L] * cw[CONV_WIDTH - 1:CONV_WIDTH, :]
            for s in range(1, CONV_WIDTH):
                acc = acc + win[PAD - s:PAD - s + L] * cw[CONV_WIDTH - 1 - s:CONV_WIDTH - s, :]
            qk = acc * jax.nn.sigmoid(acc)
            q.append(qk[:, 0:QUAD])
            k.append(qk[:, QUAD:] * (HEAD_DIM ** -0.5))
            v.append(v_ref[pl.ds(r0[j], L), :])
            ig_j = gi_ref[pl.ds(r0[j], L), :] + gb[0:1, :]
            g2 = gf_ref[pl.ds(r0[j], L), :] + gb[1:2, :]
            lf_j = jnp.minimum(g2, 0.0) - jnp.log(1.0 + jnp.exp(-jnp.abs(g2)))
            if n_valid < TS:
                ok = (row + r0[j]) < n_valid
                ig_j = jnp.where(ok, ig_j, NEG_BIG)
                lf_j = jnp.where(ok, lf_j, 0.0)
            ig.append(ig_j)
            lf.append(lf_j)
        b = [_dot2r(tril, lf[j]) for j in R]
        a = [ig[j] - b[j] for j in R]
        cm = [_cumulate_rows(a[j], L, jnp.maximum, -jnp.inf) for j in R]
        arow = [_dot2r(ones_lk, jnp.where(dmask, key_rows(a[j], NEG_BIG), 0.0)) for j in R]
        qb = [q[j].astype(BF16) for j in R]
        vb = [v[j].astype(BF16) for j in R]
        kbd = [jnp.where(seg64, jnp.concatenate([key_rows(k[j], 0.0).astype(BF16)] * N_MLSTM_HEADS, axis=0), zb) for j in R]
        vbd = [jnp.where(seg64, jnp.concatenate([key_rows(v[j], 0.0).astype(BF16)] * N_MLSTM_HEADS, axis=0), zb) for j in R]
        qkt = [lax.dot_general(qb[j], kbd[j], nt, preferred_element_type=F32) for j in R]

        mprev = ms[...]
        M, gg, emt = [], [], []
        for j in R:
            M.append(jnp.maximum(cm[j], mprev))
            mt = b[j] + M[j]
            gg.append(jnp.exp(mprev - M[j]))
            emt.append(jnp.exp(-mt))
            mprev = mt[L - 1:L, :]
        ms[...] = mprev

        wts = [jnp.where(causal, jnp.exp(arow[j] - M[j]), 0.0) * qkt[j] for j in R]
        kd = [k[j] * jnp.exp(a[j] - M[j][L - 1:L, :]) for j in R]
        num = [jnp.dot(wts[j].astype(BF16), vbd[j], preferred_element_type=F32) for j in R]
        u = [lax.dot_general(kd[j].astype(BF16), vb[j], tn, preferred_element_type=F32) for j in R]

        cmat, nvec = [cs[...]], [ns[...]]
        for j in R:
            gl = gg[j][L - 1:L, :]
            cmat.append(gl * cmat[j] + jnp.where(seg64, u[j], 0.0))
            nvec.append(gl * nvec[j] + jnp.sum(kd[j], axis=0, keepdims=True))
        cs[...] = cmat[G]
        ns[...] = nvec[G]

        inter = [jnp.dot(qb[j], cmat[j].astype(BF16), preferred_element_type=F32) for j in R]
        den = [_dot2(wts[j] + gg[j] * (q[j] * nvec[j]), seg64b) for j in R]
        h = [(num[j] + gg[j] * inter[j]) / jnp.maximum(jnp.abs(den[j]), emt[j]) for j in R]
        msq = [_dot2(h[j] * h[j], seg64b) * (1.0 / HEAD_DIM) for j in R]
        for j in R:
            y = h[j] * lax.rsqrt(msq[j] + EPS) * ng
            out_ref[pl.ds(r0[j], L), :] = (jax.nn.sigmoid(o_ref[pl.ds(r0[j], L), :]) * y).astype(out_ref.dtype)
        return carry

    lax.fori_loop(0, TS // (L * G), group_body, 0)

    @pl.when(t == pl.num_programs(1) - 1)
    def _():
        _store_block_diag(c_out, cs[...])
        n_out[...] = ns[...]
        m_out[...] = ms[...]


def _mlstm(z3, conv0, c0bd, n0, m0, gate_b, conv_w, norm_g, n_valid, chunk):
    B, S, _ = z3.shape
    ts = min(512, S)
    assert S % ts == 0 and ts % chunk == 0
    blk = lambda w, c0: pl.BlockSpec((None, ts, w), lambda b, t: (b, t, c0 // w))
    per_b = lambda r, w: pl.BlockSpec((None, r, w), lambda b, t: (b, 0, 0))
    per_head = pl.BlockSpec((None, N_MLSTM_HEADS, HEAD_DIM, HEAD_DIM), lambda b, t: (b, 0, 0, 0))
    return pl.pallas_call(
        functools.partial(_mlstm_kernel, tile=ts, chunk=chunk, n_valid=n_valid),
        out_shape=[jax.ShapeDtypeStruct((B, S, QUAD), BF16),
                   jax.ShapeDtypeStruct((B, N_MLSTM_HEADS, HEAD_DIM, HEAD_DIM), F32),
                   jax.ShapeDtypeStruct((B, 1, QUAD), F32),
                   jax.ShapeDtypeStruct((B, 1, QUAD), F32)],
        grid=(B, S // ts),
        in_specs=[blk(2 * QUAD, COL_MQK), blk(QUAD, COL_MV), blk(QUAD, COL_MO), blk(QUAD, COL_GI), blk(QUAD, COL_GF),
                  per_b(SUBLANES, 2 * QUAD), per_head, per_b(1, QUAD), per_b(1, QUAD),
                  _const_spec((2, QUAD)), _const_spec((CONV_WIDTH, 2 * QUAD)), _const_spec((1, QUAD))],
        out_specs=[pl.BlockSpec((None, ts, QUAD), lambda b, t: (b, t, 0)),
                   per_head, per_b(1, QUAD), per_b(1, QUAD)],
        scratch_shapes=[pltpu.VMEM((ts + 2 * SUBLANES, 2 * QUAD), F32),
                        pltpu.VMEM((QUAD, QUAD), F32), pltpu.VMEM((1, QUAD), F32), pltpu.VMEM((1, QUAD), F32)],
        compiler_params=_cparams(("parallel", "arbitrary")),
        name="mlstm",
    )(z3, z3, z3, z3, z3, conv0, c0bd, n0, m0, gate_b, conv_w, norm_g)


HGRN_FAST_CHUNK = 64
HGRN_SAFE_DECAY = 80.0


def _hgrn_kernel(q_ref, f_ref, i_ref, g_ref, s0_ref, lb_ref, ng_ref, out_ref, s_out, st, qs, fs, ks, *,
                 tile, sub, n_valid, fast):
    TS = tile
    t = pl.program_id(1)

    seg64 = _seg_mask(QUAD, HEAD_SHIFT, QUAD, HEAD_SHIFT)
    seg64b = seg64.astype(BF16)

    @pl.when(t == 0)
    def _():
        st[...] = _load_block_diag(s0_ref, seg64)
    lb = lb_ref[...]
    ng = ng_ref[...]
    nt = (((1,), (1,)), ((), ()))
    tn = (((0,), (0,)), ((), ()))

    P = HGRN_FAST_CHUNK if fast else sub
    prow = _iota2((P, QUAD), 0)

    def prep(c, worst):
        r0 = pl.multiple_of(c * P, P)
        hq = q_ref[pl.ds(r0, P), :]
        f = lb + (1.0 - lb) * jax.nn.sigmoid(f_ref[pl.ds(r0, P), :])
        logf = jnp.log(f)
        kk = 1.0 - f
        if n_valid < TS:
            ok = (prow + r0) < n_valid
            logf = jnp.where(ok, logf, 0.0)
            kk = jnp.where(ok, kk, 0.0)
        qs[pl.ds(r0, P), :] = hq * jax.nn.sigmoid(hq)
        fs[pl.ds(r0, P), :] = logf
        ks[pl.ds(r0, P), :] = kk
        return jnp.minimum(worst, jnp.sum(logf, axis=0, keepdims=True))

    worst = lax.fori_loop(0, TS // P, prep, jnp.zeros((1, QUAD), F32))

    def finish(o, r0, rows, smat, b, kk, iv, qt):
        o = o + lax.dot_general(qt, smat.astype(BF16), nt, preferred_element_type=F32)
        out_ref[pl.ds(r0, rows), :] = _head_rmsnorm_gate(o, seg64b, ng, g_ref[pl.ds(r0, rows), :]).astype(out_ref.dtype)
        bl = b[rows - 1:rows, :]
        ktil = kk * jnp.exp(bl - b)
        u = lax.dot_general(iv.astype(BF16), ktil.astype(BF16), tn, preferred_element_type=F32)
        st[...] = smat * jnp.exp(bl) + jnp.where(seg64, u, 0.0)

    def fast_loop():
        L = HGRN_FAST_CHUNK
        tril = (_iota2((L, L), 1) <= _iota2((L, L), 0)).astype(BF16)
        causal = (_iota2((L, QUAD), 1) & (L - 1)) <= _iota2((L, QUAD), 0)
        zb = jnp.zeros((QUAD, QUAD), BF16)

        G = min(CHUNK_GROUP, TS // L)
        assert TS % (L * G) == 0

        def body(gi, carry):
            R = range(G)
            r0 = [pl.multiple_of((gi * G + j) * L, L) for j in R]
            kk = [ks[pl.ds(r0[j], L), :] for j in R]
            ivb = [i_ref[pl.ds(r0[j], L), :].astype(BF16) for j in R]
            b = [_dot2r(tril, fs[pl.ds(r0[j], L), :]) for j in R]
            qt = [(qs[pl.ds(r0[j], L), :] * jnp.exp(b[j])).astype(BF16) for j in R]
            kbd = [jnp.where(seg64, jnp.concatenate([(kk[j] * jnp.exp(-b[j])).astype(BF16)] * N_HGRN_HEADS, axis=0), zb)
                   for j in R]
            ibd = [jnp.where(seg64, jnp.concatenate([ivb[j]] * N_HGRN_HEADS, axis=0), zb) for j in R]
            amat = [lax.dot_general(qt[j], kbd[j], nt, preferred_element_type=F32) for j in R]
            ktil = [(kk[j] * jnp.exp(b[j][L - 1:L, :] - b[j])).astype(BF16) for j in R]
            u = [lax.dot_general(ivb[j], ktil[j], tn, preferred_element_type=F32) for j in R]
            o = [jnp.dot(jnp.where(causal, amat[j], 0.0).astype(BF16), ibd[j], preferred_element_type=F32) for j in R]
            smat = [st[...]]
            for j in R:
                smat.append(smat[j] * jnp.exp(b[j][L - 1:L, :]) + jnp.where(seg64, u[j], 0.0))
            st[...] = smat[G]
            inter = [lax.dot_general(qt[j], smat[j].astype(BF16), nt, preferred_element_type=F32) for j in R]
            o = [o[j] + inter[j] for j in R]
            msq = [_dot2(o[j] * o[j], seg64b) * (1.0 / HEAD_DIM) for j in R]
            for j in R:
                y = o[j] * lax.rsqrt(msq[j] + EPS) * ng
                out_ref[pl.ds(r0[j], L), :] = (jax.nn.sigmoid(g_ref[pl.ds(r0[j], L), :]) * y).astype(out_ref.dtype)
            return carry

        lax.fori_loop(0, TS // (L * G), body, 0)

    def exact_loop():
        L = sub
        row = _iota2((L, QUAD), 0)

        def body(c, carry):
            r0 = pl.multiple_of(c * L, L)
            q = qs[pl.ds(r0, L), :]
            kk = ks[pl.ds(r0, L), :]
            iv = i_ref[pl.ds(r0, L), :]
            b = _cumulate_rows(fs[pl.ds(r0, L), :], L, jnp.add, 0.0)
            parts = []
            for j in range(L):
                dec = jnp.exp(jnp.where(row >= j, b - b[j:j + 1, :], NEG_BIG))
                parts.append(dec * q * kk[j:j + 1, :])
            tstack = jnp.concatenate(parts, axis=0).astype(BF16)
            y = jnp.dot(tstack, seg64b, preferred_element_type=F32)
            o = y[0:L] * iv[0:1, :]
            for j in range(1, L):
                o = o + y[j * L:(j + 1) * L] * iv[j:j + 1, :]
            finish(o, r0, L, st[...], b, kk, iv, (q * jnp.exp(b)).astype(BF16))
            return carry

        lax.fori_loop(0, TS // L, body, 0, unroll=min(CHUNK_UNROLL, TS // L))

    if fast:
        safe = jnp.min(worst) > -HGRN_SAFE_DECAY
        pl.when(safe)(fast_loop)
        pl.when(jnp.logical_not(safe))(exact_loop)
    else:
        exact_loop()

    @pl.when(t == pl.num_programs(1) - 1)
    def _():
        _store_block_diag(s_out, st[...])


def _hgrn(z3, s0t, lb, norm_g, n_valid, sub):
    B, S, _ = z3.shape
    ts = min(512, S)
    fast = ts % HGRN_FAST_CHUNK == 0
    assert S % ts == 0 and ts % sub == 0
    blk = lambda c0: pl.BlockSpec((None, ts, QUAD), lambda b, t: (b, t, c0 // QUAD))
    per_b = pl.BlockSpec((None, N_HGRN_HEADS, HEAD_DIM, HEAD_DIM), lambda b, t: (b, 0, 0, 0))
    stage = lambda: pltpu.VMEM((ts, QUAD), F32)
    return pl.pallas_call(
        functools.partial(_hgrn_kernel, tile=ts, sub=sub, n_valid=n_valid, fast=fast),
        out_shape=[jax.ShapeDtypeStruct((B, S, QUAD), BF16),
                   jax.ShapeDtypeStruct((B, N_HGRN_HEADS, HEAD_DIM, HEAD_DIM), F32)],
        grid=(B, S // ts),
        in_specs=[blk(COL_HQ), blk(COL_HF), blk(COL_HI), blk(COL_HG), per_b,
                  _const_spec((1, QUAD)), _const_spec((1, QUAD))],
        out_specs=[pl.BlockSpec((None, ts, QUAD), lambda b, t: (b, t, 0)), per_b],
        scratch_shapes=[pltpu.VMEM((QUAD, QUAD), F32), stage(), stage(), stage()],
        compiler_params=_cparams(("parallel", "arbitrary")),
        name="hgrn",
    )(z3, z3, z3, z3, s0t, lb, norm_g)


PROJ_SPLITS = (ATTN_WIDTH, ATTN_WIDTH, ATTN_WIDTH,
               MLSTM_WIDTH, MLSTM_WIDTH, MLSTM_WIDTH, MLSTM_WIDTH, N_MLSTM_HEADS, N_MLSTM_HEADS,
               HGRN_WIDTH, HGRN_WIDTH, HGRN_WIDTH, HGRN_WIDTH)


def _prep_w_in(w):
    cuts = [int(c) for c in np.cumsum(PROJ_SPLITS)[:-1]]
    aq, ak, av, mq, mk, mv, mo, mi, mf, hq, hf, hi, hg = jnp.split(w, cuts, axis=1)
    gi, gf = jnp.repeat(mi, HEAD_DIM, axis=1), jnp.repeat(mf, HEAD_DIM, axis=1)
    return jnp.concatenate([aq, ak, av, mq, mk, mv, mo, hq, hf, hi, hg, gi, gf], axis=1).astype(BF16)


HGRN_SUBCHUNK = 16


class _LayerWeights(NamedTuple):
    g_mix: jax.Array
    w_in: jax.Array
    wkv_t: jax.Array
    gate_b: jax.Array
    conv_w: jax.Array
    mlstm_g: jax.Array
    lb: jax.Array
    hgrn_g: jax.Array
    w_out: jax.Array
    g_ffn: jax.Array
    w_up: jax.Array
    w_down: jax.Array
    g_final: jax.Array


def _gate_row(gate_b):
    return jnp.repeat(gate_b.astype(F32), HEAD_DIM, axis=1)


def _trunk_layer(x2d, B, S, n_valid, attn_fn, states, w, final, chunk_m, chunk_h, n_keep=None, stacked=None):
    conv_buf, C0, n0, m0, S0 = states
    if n_keep is None:
        z = _inproj(x2d, w.g_mix, w.w_in)
    else:
        z, kt, vt = _inproj(x2d, w.g_mix, w.w_in, w.wkv_t, S, n_keep, stacked)
    z3 = z.reshape(B, S, ZW)
    attn = attn_fn(z3)
    conv0 = jnp.zeros((B, SUBLANES, 2 * MLSTM_WIDTH), F32).at[:, SUBLANES - (CONV_WIDTH - 1):].set(conv_buf.astype(F32))
    m0r = jnp.repeat(m0.astype(F32), HEAD_DIM, axis=1).reshape(B, 1, MLSTM_WIDTH)
    ml, c_new, n, m = _mlstm(z3, conv0, C0.astype(F32), n0.astype(F32).reshape(B, 1, MLSTM_WIDTH), m0r,
                             w.gate_b, w.conv_w, w.mlstm_g, n_valid, chunk_m)
    hg, st = _hgrn(z3, jnp.swapaxes(S0.astype(F32), -1, -2), w.lb, w.hgrn_g, n_valid, chunk_h)
    n2 = B * S
    y = _outffn(x2d, attn.reshape(n2, ATTN_WIDTH), ml.reshape(n2, MLSTM_WIDTH), hg.reshape(n2, HGRN_WIDTH),
                w.w_out, w.g_ffn, w.w_up, w.w_down, w.g_final, final)
    if n_keep is None:
        k_rows = z3[:, :n_valid, COL_AK:COL_AK + ATTN_WIDTH].reshape(B, n_valid, N_ATTN_HEADS, HEAD_DIM)
        v_rows = z3[:, :n_valid, COL_AV:COL_AV + ATTN_WIDTH].reshape(B, n_valid, N_ATTN_HEADS, HEAD_DIM)
    else:
        k_rows, v_rows = kt, vt
    conv_new = z3[:, n_valid - (CONV_WIDTH - 1):n_valid, COL_MQK:COL_MQK + 2 * MLSTM_WIDTH]
    s_new = jnp.swapaxes(st, -1, -2)
    return y, (k_rows, v_rows, conv_new, c_new, n.reshape(B, N_MLSTM_HEADS, HEAD_DIM), m[:, 0, ::HEAD_DIM], s_new)


def kernel(x_prompt, x_sample, cache_attn_k, cache_attn_v, state_mlstm_conv, state_mlstm_C, state_mlstm_n, state_mlstm_m, state_hgrn_S, rel_bias, g_mix, w_in, mlstm_gate_b, mlstm_conv_w, mlstm_norm_g, hgrn_lb_raw, hgrn_norm_g, w_out, g_ffn, w_up, w_down, g_final):
    depth = w_in.shape[0]
    bp, sp, _ = x_prompt.shape
    bs, ts, _ = x_sample.shape
    n_keep = min(max(w for w, _ in DILATED_PATTERNS), sp)
    n_past = cache_attn_k.shape[2]
    lb_p = jax.nn.softmax(hgrn_lb_raw.astype(F32), axis=0)
    hgrn_lb = jnp.cumsum(lb_p, axis=0) - lb_p[0]
    bias_dist = _bias_by_distance(rel_bias)
    p_tables = _prompt_tables(bias_dist)
    s_tables = _sample_tables(bias_dist, n_past, ts)
    cache_kt = jnp.transpose(cache_attn_k, (0, 1, 3, 4, 2)).reshape(depth, bs, ATTN_WIDTH, n_past)
    cache_vt = jnp.transpose(cache_attn_v, (0, 1, 3, 4, 2)).reshape(depth, bs, ATTN_WIDTH, n_past)

    hp = x_prompt.reshape(bp * sp, D_MODEL)
    hs = jnp.zeros((bs, TPAD, D_MODEL), F32).at[:, :ts].set(x_sample).reshape(bs * TPAD, D_MODEL)
    zero_states = (jnp.zeros((bp, CONV_WIDTH - 1, 2 * MLSTM_WIDTH), F32),
                   jnp.zeros((bp, N_MLSTM_HEADS, HEAD_DIM, HEAD_DIM), F32),
                   jnp.zeros((bp, N_MLSTM_HEADS, HEAD_DIM), F32),
                   jnp.zeros((bp, N_MLSTM_HEADS), F32),
                   jnp.zeros((bp, N_HGRN_HEADS, HEAD_DIM, HEAD_DIM), F32))
    p_states, s_states = [], []
    kv_t = None
    for l in range(depth):
        final = l == depth - 1
        w_in_l = _prep_w_in(w_in[l])
        weights = _LayerWeights(
            g_mix=g_mix[l][None], w_in=w_in_l, wkv_t=w_in_l[:, COL_AK:COL_AV + ATTN_WIDTH].T,
            gate_b=_gate_row(mlstm_gate_b[l]),
            conv_w=mlstm_conv_w[l], mlstm_g=mlstm_norm_g[l][None], lb=hgrn_lb[l][None], hgrn_g=hgrn_norm_g[l][None],
            w_out=w_out[l].astype(BF16), g_ffn=g_ffn[l][None], w_up=w_up[l].astype(BF16),
            w_down=w_down[l].astype(BF16), g_final=g_final[None])
        hp, st = _trunk_layer(hp, bp, sp, sp, functools.partial(_attn_prompt, bias=p_tables),
                              zero_states, weights, final, MLSTM_CHUNK, HGRN_SUBCHUNK, n_keep, (l, depth, kv_t))
        kv_t = st[:2]
        p_states.append(st[2:])
        sample_attn = functools.partial(_attn_sample, cache_kt=cache_kt, cache_vt=cache_vt, layer=l, tables=s_tables)
        states = (state_mlstm_conv[l], state_mlstm_C[l], state_mlstm_n[l], state_mlstm_m[l], state_hgrn_S[l])
        hs, st = _trunk_layer(hs, bs, TPAD, ts, sample_attn, states, weights, final, TPAD, TPAD)
        s_states.append(st)
    y_prompt = hp.reshape(bp, sp, D_MODEL)
    y_sample = hs.reshape(bs, TPAD, D_MODEL)[:, :ts]
    to_rows = lambda t: jnp.transpose(t.reshape(depth, bp, N_ATTN_HEADS, HEAD_DIM, n_keep), (0, 1, 4, 2, 3))
    p_out = [to_rows(kv_t[0]), to_rows(kv_t[1])] + [jnp.stack(a) for a in zip(*p_states)]
    s_out = [jnp.stack(a) for a in zip(*s_states)]
    return (y_prompt, y_sample, *p_out, *s_out)
```

```python
import functools
from typing import Callable, NamedTuple

import jax
import jax.numpy as jnp
import numpy as np
from jax import lax
from jax.experimental import pallas as pl
from jax.experimental.pallas import tpu as pltpu

F32 = jnp.float32
BF16 = jnp.bfloat16

D_MODEL = 1024
HEAD_DIM = 64
N_ATTN_HEADS = 8
N_MLSTM_HEADS = 4
N_HGRN_HEADS = 4
ATTN_WIDTH = N_ATTN_HEADS * HEAD_DIM
MLSTM_WIDTH = N_MLSTM_HEADS * HEAD_DIM
HGRN_WIDTH = N_HGRN_HEADS * HEAD_DIM
DILATED_PATTERNS = ((128, 1), (512, 4), (2048, 16))
QBLOCK = 128
N_REL_BUCKETS = 32
REL_MAX_DISTANCE = 2048
CONV_WIDTH = 4
MLSTM_CHUNK = 64
D_FF = 4 * D_MODEL
EPS = 1e-6
NEG_BIG = -1e30

LANES = 128
SUBLANES = 8

COL_AQ = 0
COL_AK = COL_AQ + ATTN_WIDTH
COL_AV = COL_AK + ATTN_WIDTH
COL_MQK = COL_AV + ATTN_WIDTH
COL_MV = COL_MQK + 2 * MLSTM_WIDTH
COL_MO = COL_MV + MLSTM_WIDTH
COL_HQ = COL_MO + MLSTM_WIDTH
COL_HF = COL_HQ + HGRN_WIDTH
COL_HI = COL_HF + HGRN_WIDTH
COL_HG = COL_HI + HGRN_WIDTH
COL_GI = COL_HG + HGRN_WIDTH
COL_GF = COL_GI + MLSTM_WIDTH
ZW = COL_GF + MLSTM_WIDTH

VMEM_LIMIT = 56 * 1024 * 1024


def _cparams(sem, vmem=VMEM_LIMIT):
    return pltpu.CompilerParams(dimension_semantics=sem, vmem_limit_bytes=vmem)


def _const_spec(shape):
    nd = len(shape)
    return pl.BlockSpec(shape, lambda *_: (0,) * nd, pipeline_mode=pl.Buffered(1))


class _CallSpec(NamedTuple):
    body: Callable
    grid: tuple
    args: list
    in_specs: list
    out_shape: list
    out_specs: list
    scratch: list


def _run(specs, sem, name):
    grid = specs[0].grid
    assert all(s.grid == grid for s in specs)
    n_in = [len(s.args) for s in specs]
    n_out = [len(s.out_shape) for s in specs]
    n_scr = [len(s.scratch) for s in specs]

    def body(*refs):
        ins, outs, scrs = refs[:sum(n_in)], refs[sum(n_in):sum(n_in) + sum(n_out)], refs[sum(n_in) + sum(n_out):]
        for k, s in enumerate(specs):
            a, b, c = sum(n_in[:k]), sum(n_out[:k]), sum(n_scr[:k])
            s.body(*ins[a:a + n_in[k]], *outs[b:b + n_out[k]], *scrs[c:c + n_scr[k]])

    flat = pl.pallas_call(
        body, grid=grid,
        in_specs=[x for s in specs for x in s.in_specs],
        out_shape=[x for s in specs for x in s.out_shape],
        out_specs=[x for s in specs for x in s.out_specs],
        scratch_shapes=[x for s in specs for x in s.scratch],
        compiler_params=_cparams(sem), name=name,
    )(*[x for s in specs for x in s.args])
    out, k = [], 0
    for n in n_out:
        out.append(list(flat[k:k + n]))
        k += n
    return out


def _inproj_kernel(x_ref, g_ref, w_ref, *rest, tiles_per_seq, first_keep, n_prev=0):
    x = x_ref[...]
    ms = jnp.mean(x * x, axis=-1, keepdims=True)
    xn = ((x * lax.rsqrt(ms + EPS)) * g_ref[...]).astype(BF16)
    if first_keep is None:
        (z_ref,) = rest
        z_ref[...] = jnp.dot(xn, w_ref[...], preferred_element_type=F32)
        return
    wkv_ref = rest[0]
    z_ref, kt_ref, vt_ref = rest[1 + n_prev:]
    z_ref[...] = jnp.dot(xn, w_ref[...], preferred_element_type=F32)

    @pl.when(pl.program_id(0) % tiles_per_seq >= first_keep)
    def _():
        kv = lax.dot_general(wkv_ref[...], xn, (((1,), (1,)), ((), ())), preferred_element_type=F32)
        kt_ref[...] = kv[0:ATTN_WIDTH]
        vt_ref[...] = kv[ATTN_WIDTH:]


def _inproj(x2d, g, w, wkv_t=None, seq=None, n_keep=None, stacked=None):
    n = x2d.shape[0]
    tm = min(512, n)
    assert n % tm == 0
    x_spec = pl.BlockSpec((tm, D_MODEL), lambda i: (i, 0))
    z_spec = pl.BlockSpec((tm, ZW), lambda i: (i, 0))
    z_shape = jax.ShapeDtypeStruct((n, ZW), F32)
    if wkv_t is None:
        return pl.pallas_call(
            functools.partial(_inproj_kernel, tiles_per_seq=None, first_keep=None),
            out_shape=z_shape, grid=(n // tm,),
            in_specs=[x_spec, _const_spec((1, D_MODEL)), _const_spec((D_MODEL, ZW))],
            out_specs=z_spec, compiler_params=_cparams(("parallel",)), name="inproj",
        )(x2d, g, w)
    assert seq % tm == 0 and n_keep % tm == 0
    layer, depth, prev = stacked
    tps, first_keep = seq // tm, (seq - n_keep) // tm
    t_spec = pl.BlockSpec((None, None, ATTN_WIDTH, tm),
                          lambda i: (layer, i // tps, 0, jnp.maximum(i % tps - first_keep, 0)))
    t_shape = jax.ShapeDtypeStruct((depth, n // seq, ATTN_WIDTH, n_keep), F32)
    in_specs = [x_spec, _const_spec((1, D_MODEL)), _const_spec((D_MODEL, ZW)), _const_spec((2 * ATTN_WIDTH, D_MODEL))]
    args = [x2d, g, w, wkv_t]
    if prev is None:
        prev = (jnp.zeros(t_shape.shape, F32), jnp.zeros(t_shape.shape, F32))
    in_specs += [pl.BlockSpec(memory_space=pl.ANY)] * 2
    aliases = {len(args): 1, len(args) + 1: 2}
    args += list(prev)
    return pl.pallas_call(
        functools.partial(_inproj_kernel, tiles_per_seq=tps, first_keep=first_keep, n_prev=2),
        out_shape=[z_shape, t_shape, t_shape], grid=(n // tm,),
        in_specs=in_specs, out_specs=[z_spec, t_spec, t_spec], input_output_aliases=aliases,
        compiler_params=_cparams(("arbitrary",)), name="inproj_kt",
    )(*args)


FF_CHUNK = 1024


def _outffn_kernel(x_ref, a_ref, m_ref, h_ref, wo_ref, gf_ref, wu_ref, wd_ref, gl_ref, y_ref, xn_sc, *, final):
    x1 = x_ref[...]
    x1 = x1 + jnp.dot(a_ref[...], wo_ref[0:ATTN_WIDTH, :], preferred_element_type=F32)
    x1 = x1 + jnp.dot(m_ref[...], wo_ref[ATTN_WIDTH:ATTN_WIDTH + MLSTM_WIDTH, :], preferred_element_type=F32)
    x1 = x1 + jnp.dot(h_ref[...], wo_ref[ATTN_WIDTH + MLSTM_WIDTH:, :], preferred_element_type=F32)
    ms = jnp.mean(x1 * x1, axis=-1, keepdims=True)
    xn_sc[...] = ((x1 * lax.rsqrt(ms + EPS)) * gf_ref[...]).astype(BF16)
    y_ref[...] = x1
    for c in range(D_FF // FF_CHUNK):
        u = jnp.dot(xn_sc[...], wu_ref[:, c * FF_CHUNK:(c + 1) * FF_CHUNK], preferred_element_type=F32)
        hh = jnp.square(jnp.maximum(u, 0.0)).astype(BF16)
        y_ref[...] += jnp.dot(hh, wd_ref[c * FF_CHUNK:(c + 1) * FF_CHUNK, :], preferred_element_type=F32)
    if final:
        x2 = y_ref[...]
        ms2 = jnp.mean(x2 * x2, axis=-1, keepdims=True)
        y_ref[...] = (x2 * lax.rsqrt(ms2 + EPS)) * gl_ref[...]


def _outffn(x2d, attn, ml, hg, w_out, g_ffn, w_up, w_down, g_final, final):
    n = x2d.shape[0]
    tm = min(512, n)
    assert n % tm == 0
    row = lambda w: pl.BlockSpec((tm, w), lambda i: (i, 0))
    return pl.pallas_call(
        functools.partial(_outffn_kernel, final=final),
        out_shape=jax.ShapeDtypeStruct((n, D_MODEL), F32),
        grid=(n // tm,),
        in_specs=[row(D_MODEL), row(ATTN_WIDTH), row(MLSTM_WIDTH), row(HGRN_WIDTH),
                  _const_spec((D_MODEL, D_MODEL)), _const_spec((1, D_MODEL)),
                  _const_spec((D_MODEL, D_FF)), _const_spec((D_FF, D_MODEL)), _const_spec((1, D_MODEL))],
        out_specs=row(D_MODEL),
        scratch_shapes=[pltpu.VMEM((tm, D_MODEL), BF16)],
        compiler_params=_cparams(("parallel",)),
        name="outffn",
    )(x2d, attn, ml, hg, w_out, g_ffn, w_up, w_down, g_final)


def _t5_causal_bucket(dist):
    n = np.asarray(dist).astype(np.int32)
    max_exact = N_REL_BUCKETS // 2
    scaled = np.log(np.maximum(n, 1) / max_exact) / np.log(REL_MAX_DISTANCE / max_exact)
    large = np.minimum(max_exact + (scaled * (N_REL_BUCKETS - max_exact)).astype(np.int32), N_REL_BUCKETS - 1)
    return np.where(n < max_exact, n, large).astype(np.int32)


BIAS_DIST = 2304


def _bias_by_distance(rel_bias):
    assert BIAS_DIST > max(w for w, _ in DILATED_PATTERNS) + SUBLANES
    return rel_bias.astype(F32)[_t5_causal_bucket(np.arange(BIAS_DIST)[::-1])].T


def _dist_slice(bias_desc, lo, hi, step=1):
    last = BIAS_DIST - 1
    return bias_desc[:, last - hi:last - lo + 1:step]


def _prompt_tables(bias_dist):
    H = N_ATTN_HEADS
    a = np.arange(QBLOCK)[:, None]
    b = np.arange(2 * QBLOCK)[None, :]
    rel = QBLOCK + a - b
    band = (rel >= 0) & (rel <= QBLOCK)
    period = 3 * QBLOCK
    biases = []
    for window, dil in DILATED_PATTERNS:
        nsub = window // dil
        assert nsub == QBLOCK
        vd = _dist_slice(bias_dist, 0, nsub * dil, dil)
        rp = jnp.concatenate([vd, jnp.broadcast_to(vd[:, -1:], (H, QBLOCK)),
                              jnp.broadcast_to(vd[:, 0:1], (H, QBLOCK - 1))], axis=1)
        skew = jnp.tile(rp, (1, QBLOCK))[:, :QBLOCK * (period - 1)].reshape(H, QBLOCK, period - 1)
        biases.append(skew[:, :, :2 * QBLOCK].reshape(H // 2, 2 * QBLOCK, 2 * QBLOCK))
    mask = np.stack([np.tile(band, (2, 1)), np.tile(band & (b >= QBLOCK), (2, 1))])
    table = jnp.where(mask[None, None], jnp.stack(biases)[:, :, None], -jnp.inf)
    return jnp.swapaxes(table, -1, -2)


ATTN_GROUP = 8


def _attn_prompt_kernel(q_ref, k_ref, v_ref, bias_ref, out_ref,
                        x4, qs, ks, vts, o1, o4, o16, l1, l4, l16, p_scr, st_scr, *, seq):
    S = seq
    nblk = S // QBLOCK
    scale = HEAD_DIM ** -0.5
    zpad = jnp.zeros((QBLOCK, LANES), BF16)
    for di in range(3):
        ks[di, 0:QBLOCK, :] = zpad
        vts[di, 0] = zpad

    R = 512

    def put(src, di, row0, blk):
        if src is q_ref:
            qs[di, row0:row0 + blk.shape[0], :] = (blk * scale).astype(BF16)
        elif src is k_ref:
            ks[di, QBLOCK + row0:QBLOCK + row0 + blk.shape[0], :] = blk.astype(BF16)
        else:
            blk_t = blk.T.astype(BF16)
            for t in range(blk.shape[0] // QBLOCK):
                vts[di, 1 + row0 // QBLOCK + t] = blk_t[:, t * QBLOCK:(t + 1) * QBLOCK]

    for src in (q_ref, k_ref, v_ref):
        for i in range(S // R):
            put(src, 0, i * R, src[i * R:(i + 1) * R, :])
        for r in range(4):
            for i in range(S // 4 // R):
                blk = src[pl.ds(r + 4 * i * R, R, stride=4), :]
                row0 = r * (S // 4) + i * R
                x4[row0:row0 + R, :] = blk
                put(src, 1, row0, blk)
        n16 = S // 16
        for c4 in range(4):
            for a in range(4):
                put(src, 2, (c4 * 4 + a) * n16, x4[pl.ds(c4 * (S // 4) + a, n16, stride=4), :])

    lane = lax.broadcasted_iota(jnp.int32, (QBLOCK, LANES), 1)
    head_a = lane < HEAD_DIM
    row_a = lax.broadcasted_iota(jnp.int32, (QBLOCK, LANES), 0) < HEAD_DIM
    nt = (((1,), (1,)), ((), ()))

    def branch(di, dil, o_ref, l_ref):
        per_class = (S // dil) // QBLOCK

        R = range(ATTN_GROUP)
        zero = jnp.zeros((QBLOCK, LANES), BF16)

        def scores(g):
            out = []
            for j in R:
                i = g * ATTN_GROUP + j
                base = pl.multiple_of(i * QBLOCK, QBLOCK)
                q = qs[di, pl.ds(base, QBLOCK), :]
                kk = ks[di, pl.ds(base, 2 * QBLOCK), :]
                q2 = jnp.concatenate([jnp.where(head_a, q, zero), jnp.where(head_a, zero, q)], axis=0)
                out.append(lax.dot_general(kk, q2, nt, preferred_element_type=F32))
            return out

        def softmax(g, slot, st):
            for j in R:
                i = g * ATTN_GROUP + j
                s = st[j] + bias_ref[di, jnp.where(i % per_class == 0, 1, 0)]
                m = jnp.max(s, axis=0, keepdims=True)
                p = jnp.exp(s - m)
                p_scr[slot, j] = p.astype(BF16)
                st_scr[slot, j, 0:1, :] = m
                st_scr[slot, j, 1:2, :] = jnp.sum(p, axis=0, keepdims=True)

        def weighted_values(g, slot):
            out = []
            for j in R:
                i = g * ATTN_GROUP + j
                vt = jnp.concatenate([vts[di, i], vts[di, i + 1]], axis=1)
                ot = jnp.dot(vt, p_scr[slot, j], preferred_element_type=F32)
                out.append((ot, st_scr[slot, j, 0:1, :], st_scr[slot, j, 1:2, :]))
            return out

        def outputs(g, pv):
            for j in R:
                i = g * ATTN_GROUP + j
                base = pl.multiple_of(i * QBLOCK, QBLOCK)
                ot, m, l = pv[j]
                otn = ot / l
                lse = m + jnp.log(l)
                o = jnp.where(row_a, otn[:, 0:QBLOCK], otn[:, QBLOCK:]).T
                ls = jnp.where(row_a, jnp.broadcast_to(lse[:, 0:QBLOCK], (QBLOCK, LANES)),
                               jnp.broadcast_to(lse[:, QBLOCK:], (QBLOCK, LANES))).T
                if dil == 1:
                    o_ref[pl.ds(base, QBLOCK), :] = o
                    l_ref[pl.ds(base, QBLOCK), :] = ls
                else:
                    c, n = i // per_class, i % per_class
                    res = c if dil == 4 else 4 * (c % 4) + c // 4
                    start = dil * QBLOCK * n + res
                    o_ref[pl.ds(start, QBLOCK, stride=dil), :] = o
                    l_ref[pl.ds(start, QBLOCK, stride=dil), :] = ls

        ngroup = nblk // ATTN_GROUP
        softmax(0, 0, scores(0))

        def body(g, carry):
            slot = g % 2
            st = scores(g)
            pv = weighted_values(g - 1, 1 - slot)
            softmax(g, slot, st)
            outputs(g - 1, pv)
            return carry

        lax.fori_loop(1, ngroup, body, 0)
        outputs(ngroup - 1, weighted_values(ngroup - 1, (ngroup - 1) % 2))

    branch(0, 1, o1, l1)
    branch(1, 4, o4, l4)
    branch(2, 16, o16, l16)

    T = 256

    def mix(i, carry):
        r0 = pl.multiple_of(i * T, T)
        la, lb, lc = l1[pl.ds(r0, T), :], l4[pl.ds(r0, T), :], l16[pl.ds(r0, T), :]
        mx = jnp.maximum(jnp.maximum(la, lb), lc)
        ea, eb, ec = jnp.exp(la - mx), jnp.exp(lb - mx), jnp.exp(lc - mx)
        num = ea * o1[pl.ds(r0, T), :] + eb * o4[pl.ds(r0, T), :] + ec * o16[pl.ds(r0, T), :]
        out_ref[pl.ds(r0, T), :] = (num / (ea + eb + ec)).astype(out_ref.dtype)
        return carry

    lax.fori_loop(0, S // T, mix, 0)


def _attn_prompt(z3, bias):
    B, S, _ = z3.shape
    assert S % (16 * QBLOCK) == 0
    npair = N_ATTN_HEADS // 2
    col = lambda c0: pl.BlockSpec((None, S, LANES), lambda b, p: (b, 0, c0 // LANES + p))
    f32s = lambda: pltpu.VMEM((S, LANES), F32)
    return pl.pallas_call(
        functools.partial(_attn_prompt_kernel, seq=S),
        out_shape=jax.ShapeDtypeStruct((B, S, ATTN_WIDTH), BF16),
        grid=(B, npair),
        in_specs=[col(COL_AQ), col(COL_AK), col(COL_AV),
                  pl.BlockSpec((3, None, 2, 2 * QBLOCK, 2 * QBLOCK), lambda b, p: (0, p, 0, 0, 0))],
        out_specs=pl.BlockSpec((None, S, LANES), lambda b, p: (b, 0, p)),
        scratch_shapes=[f32s(),
                        pltpu.VMEM((3, S, LANES), BF16),
                        pltpu.VMEM((3, S + QBLOCK, LANES), BF16),
                        pltpu.VMEM((3, S // QBLOCK + 1, LANES, QBLOCK), BF16),
                        f32s(), f32s(), f32s(), f32s(), f32s(), f32s(),
                        pltpu.VMEM((2, ATTN_GROUP, 2 * QBLOCK, 2 * QBLOCK), BF16),
                        pltpu.VMEM((2, ATTN_GROUP, SUBLANES, 2 * QBLOCK), F32)],
        compiler_params=_cparams(("parallel", "parallel")),
        name="attn_prompt",
    )(z3, z3, z3, bias)


TPAD = SUBLANES
TAIL = 512


def _sample_tables(bias_dist, n_past, n_tok):
    assert n_past >= max(w for w, _ in DILATED_PATTERNS) and n_tok <= TPAD and TAIL >= DILATED_PATTERNS[1][0]
    H = N_ATTN_HEADS
    t = np.arange(TPAD)[:, None]

    def by_row(width):
        rows = [_dist_slice(bias_dist, tt + 1, tt + width) for tt in range(TPAD)]
        return jnp.stack(rows, axis=1).reshape(H * TPAD, width)

    bias_tail, bias_full = by_row(TAIL), by_row(n_past)
    zero_dist = _dist_slice(bias_dist, 0, 0)
    new_rows = [jnp.concatenate([_dist_slice(bias_dist, 0, tt),
                                 jnp.broadcast_to(zero_dist, (H, TPAD - tt - 1))], axis=1) for tt in range(TPAD)]
    bias_new = jnp.stack(new_rows, axis=1).reshape(H * TPAD, TPAD)

    def valid(dj, dil, nsub, lo):
        ok = (dj % dil == 0) & (dj // dil >= lo) & (dj // dil <= nsub) & (t < n_tok)
        return np.tile(ok, (H, 1)).astype(np.float32)

    tn = np.arange(TPAD)[None, :]
    m_tail, m_new = [], []
    for window, dil in DILATED_PATTERNS:
        nsub = window // dil
        if window <= TAIL:
            m_tail.append(valid(TAIL + t - np.arange(TAIL)[None, :], dil, nsub, 1))
        else:
            m_full = valid(n_past + t - np.arange(n_past)[None, :], dil, nsub, 1)
        m_new.append(valid(t - tn, dil, nsub, 0) * (tn < n_tok))
    return (bias_tail, bias_full, bias_new, jnp.asarray(np.stack(m_tail)), jnp.asarray(m_full),
            jnp.asarray(np.stack(m_new).astype(np.float32)))


def _attn_sample_kernel(q_ref, kn_ref, vn_ref, kt_ref, vt_ref, bt_ref, bf_ref, bn_ref, mt_ref, mf_ref, mn_ref,
                        out_ref, *, n_past):
    H = N_ATTN_HEADS
    rows = H * TPAD
    same_head = (_iota2((rows, ATTN_WIDTH), 0) >> 3) == (_iota2((rows, ATTN_WIDTH), 1) >> HEAD_SHIFT)
    q = q_ref[...] * (HEAD_DIM ** -0.5)
    qm = jnp.where(same_head, jnp.concatenate([q] * H, axis=0), 0.0).astype(BF16)
    nt = (((1,), (1,)), ((), ()))
    kn = kn_ref[...].astype(BF16)
    vn = vn_ref[...].astype(BF16)
    kt = kt_ref[...].astype(BF16)
    vt = vt_ref[...].astype(BF16)
    s_new = lax.dot_general(qm, kn, nt, preferred_element_type=F32)
    s_full = jnp.dot(qm, kt, preferred_element_type=F32)
    s_tail = s_full[:, n_past - TAIL:]

    outs, lses = [], []
    for di, (window, _) in enumerate(DILATED_PATTERNS):
        if window <= TAIL:
            sm = jnp.where(mt_ref[di] > 0.5, s_tail + bt_ref[...], NEG_BIG)
            vmain = vt[:, n_past - TAIL:]
        else:
            sm = jnp.where(mf_ref[...] > 0.5, s_full + bf_ref[...], NEG_BIG)
            vmain = vt
        sn = jnp.where(mn_ref[di] > 0.5, s_new + bn_ref[...], NEG_BIG)
        m = jnp.maximum(jnp.max(sm, axis=-1, keepdims=True), jnp.max(sn, axis=-1, keepdims=True))
        pm = jnp.exp(sm - m)
        pn = jnp.exp(sn - m)
        l = jnp.sum(pm, axis=-1, keepdims=True) + jnp.sum(pn, axis=-1, keepdims=True)
        o = jnp.dot(pn.astype(BF16), vn, preferred_element_type=F32)
        o = o + lax.dot_general(pm.astype(BF16), vmain, nt, preferred_element_type=F32)
        outs.append(o / l)
        lses.append(m + jnp.log(l))
    mx = jnp.maximum(jnp.maximum(lses[0], lses[1]), lses[2])
    es = [jnp.exp(ls - mx) for ls in lses]
    mixed = (es[0] * outs[0] + es[1] * outs[1] + es[2] * outs[2]) / (es[0] + es[1] + es[2])
    mixed = jnp.where(same_head, mixed, 0.0)
    acc = mixed[0:TPAD]
    for h in range(1, H):
        acc = acc + mixed[h * TPAD:(h + 1) * TPAD]
    out_ref[...] = acc.astype(out_ref.dtype)


def _attn_sample(z3, cache_kt, cache_vt, layer, tables):
    B, T, _ = z3.shape
    assert T == TPAD
    n_past = cache_kt.shape[-1]
    new = lambda c0: pl.BlockSpec((None, TPAD, ATTN_WIDTH), lambda b, t: (b, 0, c0 // ATTN_WIDTH))
    buf = pl.BlockSpec((None, None, ATTN_WIDTH, n_past), lambda b, t: (layer, b, 0, 0))
    return _CallSpec(
        body=functools.partial(_attn_sample_kernel, n_past=n_past),
        grid=(B, 1),
        args=[z3, z3, z3, cache_kt, cache_vt, *tables],
        in_specs=[new(COL_AQ), new(COL_AK), new(COL_AV), buf, buf] + [_const_spec(t.shape) for t in tables],
        out_shape=[jax.ShapeDtypeStruct((B, TPAD, ATTN_WIDTH), BF16)],
        out_specs=[pl.BlockSpec((None, TPAD, ATTN_WIDTH), lambda b, t: (b, 0, 0))],
        scratch=[])


QUAD = N_MLSTM_HEADS * HEAD_DIM
CHUNK_UNROLL = 2
CHUNK_GROUP = 4
HEAD_SHIFT = 6


def _iota2(shape, axis):
    return lax.broadcasted_iota(jnp.int32, shape, axis)


def _log2(n):
    k = int(n).bit_length() - 1
    assert 1 << k == n
    return k


def _seg_mask(rows, row_shift, cols, col_shift):
    return (_iota2((rows, cols), 0) >> row_shift) == (_iota2((rows, cols), 1) >> col_shift)


def _cumulate_rows(x, length, op, fill):
    row = _iota2(x.shape, 0) & (length - 1)
    sh = 1
    while sh < length:
        x = op(x, jnp.where(row >= sh, pltpu.roll(x, sh, axis=0), fill))
        sh *= 2
    return x


def _split2(x):
    hi = x.astype(BF16)
    lo = (x - hi.astype(F32)).astype(BF16)
    return hi, lo


def _dot2(x, w):
    hi, lo = _split2(x)
    return jnp.dot(hi, w, preferred_element_type=F32) + jnp.dot(lo, w, preferred_element_type=F32)


def _dot2r(w, x):
    hi, lo = _split2(x)
    return jnp.dot(w, hi, preferred_element_type=F32) + jnp.dot(w, lo, preferred_element_type=F32)


def _load_block_diag(ref, seg64):
    flat = ref[...].reshape(QUAD, HEAD_DIM)
    return jnp.where(seg64, jnp.concatenate([flat] * (QUAD // HEAD_DIM), axis=1), 0.0)


def _store_block_diag(ref, mat):
    for h in range(QUAD // HEAD_DIM):
        ref[h] = mat[h * HEAD_DIM:(h + 1) * HEAD_DIM, h * HEAD_DIM:(h + 1) * HEAD_DIM]


def _head_rmsnorm_gate(h, seg64b, g_row, gate_pre):
    ms = _dot2(h * h, seg64b) * (1.0 / HEAD_DIM)
    return jax.nn.sigmoid(gate_pre) * (h * lax.rsqrt(ms + EPS) * g_row)


def _mlstm_kernel(qk_ref, v_ref, o_ref, gi_ref, gf_ref, conv0_ref, c0_ref, n0_ref, m0_ref, gb_ref, cw_ref, ng_ref,
                  out_ref, c_out, n_out, m_out, xp, cs, ns, ms, *, tile, chunk, n_valid):
    TS, L = tile, chunk
    t = pl.program_id(1)
    PAD = SUBLANES

    @pl.when(t == 0)
    def _():
        cs[...] = _load_block_diag(c0_ref, _seg_mask(QUAD, HEAD_SHIFT, QUAD, HEAD_SHIFT))
        ns[...] = n0_ref[...]
        ms[...] = m0_ref[...]
        xp[0:PAD, :] = conv0_ref[...]

    @pl.when(t > 0)
    def _():
        xp[0:PAD, :] = xp[TS:TS + PAD, :]

    xp[PAD:PAD + TS, :] = qk_ref[...]

    LK = MLSTM_CHUNK
    seg64 = _seg_mask(QUAD, HEAD_SHIFT, QUAD, HEAD_SHIFT)
    seg64b = seg64.astype(BF16)
    dmask = (_iota2((LK, QUAD), 1) & (LK - 1)) == _iota2((LK, QUAD), 0)
    causal = (_iota2((L, QUAD), 1) & (LK - 1)) <= _iota2((L, QUAD), 0)
    tril = (_iota2((L, L), 1) <= _iota2((L, L), 0)).astype(BF16)
    ones_lk = jnp.ones((L, LK), BF16)
    row = _iota2((L, QUAD), 0)
    cw = cw_ref[...]
    gb = gb_ref[...]
    ng = ng_ref[...]

    def key_rows(x, fill):
        if L == LK:
            return x
        return jnp.concatenate([x, jnp.full((LK - L, QUAD), fill, x.dtype)], axis=0)

    G = min(CHUNK_GROUP, TS // L)
    assert TS % (L * G) == 0
    nt = (((1,), (1,)), ((), ()))
    tn = (((0,), (0,)), ((), ()))
    zb = jnp.zeros((QUAD, QUAD), BF16)

    def group_body(gi, carry):
        R = range(G)
        r0 = [pl.multiple_of((gi * G + j) * L, L) for j in R]
        q, k, v, ig, lf = [], [], [], [], []
        for j in R:
            win = xp[pl.ds(r0[j], L + PAD), :]
            acc = win[PAD:PAD + L] * cw[CONV_WIDTH - 1:CONV_WIDTH, :]
            for s in range(1, CONV_WIDTH):
                acc = acc + win[PAD - s:PAD - s + L] * cw[CONV_WIDTH - 1 - s:CONV_WIDTH - s, :]
            qk = acc * jax.nn.sigmoid(acc)
            q.append(qk[:, 0:QUAD])
            k.append(qk[:, QUAD:] * (HEAD_DIM ** -0.5))
            v.append(v_ref[pl.ds(r0[j], L), :])
            ig_j = gi_ref[pl.ds(r0[j], L), :] + gb[0:1, :]
            g2 = gf_ref[pl.ds(r0[j], L), :] + gb[1:2, :]
            lf_j = jnp.minimum(g2, 0.0) - jnp.log(1.0 + jnp.exp(-jnp.abs(g2)))
            if n_valid < TS:
                ok = (row + r0[j]) < n_valid
                ig_j = jnp.where(ok, ig_j, NEG_BIG)
                lf_j = jnp.where(ok, lf_j, 0.0)
            ig.append(ig_j)
            lf.append(lf_j)
        b = [_dot2r(tril, lf[j]) for j in R]
        a = [ig[j] - b[j] for j in R]
        cm = [_cumulate_rows(a[j], L, jnp.maximum, -jnp.inf) for j in R]
        arow = [_dot2r(ones_lk, jnp.where(dmask, key_rows(a[j], NEG_BIG), 0.0)) for j in R]
        qb = [q[j].astype(BF16) for j in R]
        vb = [v[j].astype(BF16) for j in R]
        kbd = [jnp.where(seg64, jnp.concatenate([key_rows(k[j], 0.0).astype(BF16)] * N_MLSTM_HEADS, axis=0), zb) for j in R]
        vbd = [jnp.where(seg64, jnp.concatenate([key_rows(v[j], 0.0).astype(BF16)] * N_MLSTM_HEADS, axis=0), zb) for j in R]
        qkt = [lax.dot_general(qb[j], kbd[j], nt, preferred_element_type=F32) for j in R]

        mprev = ms[...]
        M, gg, emt = [], [], []
        for j in R:
            M.append(jnp.maximum(cm[j], mprev))
            mt = b[j] + M[j]
            gg.append(jnp.exp(mprev - M[j]))
            emt.append(jnp.exp(-mt))
            mprev = mt[L - 1:L, :]
        ms[...] = mprev

        wts = [jnp.where(causal, jnp.exp(arow[j] - M[j]), 0.0) * qkt[j] for j in R]
        kd = [k[j] * jnp.exp(a[j] - M[j][L - 1:L, :]) for j in R]
        num = [jnp.dot(wts[j].astype(BF16), vbd[j], preferred_element_type=F32) for j in R]
        u = [lax.dot_general(kd[j].astype(BF16), vb[j], tn, preferred_element_type=F32) for j in R]

        cmat, nvec = [cs[...]], [ns[...]]
        for j in R:
            gl = gg[j][L - 1:L, :]
            cmat.append(gl * cmat[j] + jnp.where(seg64, u[j], 0.0))
            nvec.append(gl * nvec[j] + jnp.sum(kd[j], axis=0, keepdims=True))
        cs[...] = cmat[G]
        ns[...] = nvec[G]

        inter = [jnp.dot(qb[j], cmat[j].astype(BF16), preferred_element_type=F32) for j in R]
        den = [_dot2(wts[j] + gg[j] * (q[j] * nvec[j]), seg64b) for j in R]
        h = [(num[j] + gg[j] * inter[j]) / jnp.maximum(jnp.abs(den[j]), emt[j]) for j in R]
        msq = [_dot2(h[j] * h[j], seg64b) * (1.0 / HEAD_DIM) for j in R]
        for j in R:
            y = h[j] * lax.rsqrt(msq[j] + EPS) * ng
            out_ref[pl.ds(r0[j], L), :] = (jax.nn.sigmoid(o_ref[pl.ds(r0[j], L), :]) * y).astype(out_ref.dtype)
        return carry

    lax.fori_loop(0, TS // (L * G), group_body, 0)

    @pl.when(t == pl.num_programs(1) - 1)
    def _():
        _store_block_diag(c_out, cs[...])
        n_out[...] = ns[...]
        m_out[...] = ms[...]


def _mlstm(z3, conv0, c0bd, n0, m0, gate_b, conv_w, norm_g, n_valid, chunk):
    B, S, _ = z3.shape
    ts = min(512, S)
    assert S % ts == 0 and ts % chunk == 0
    blk = lambda w, c0: pl.BlockSpec((None, ts, w), lambda b, t: (b, t, c0 // w))
    per_b = lambda r, w: pl.BlockSpec((None, r, w), lambda b, t: (b, 0, 0))
    per_head = pl.BlockSpec((None, N_MLSTM_HEADS, HEAD_DIM, HEAD_DIM), lambda b, t: (b, 0, 0, 0))
    return _CallSpec(
        body=functools.partial(_mlstm_kernel, tile=ts, chunk=chunk, n_valid=n_valid),
        grid=(B, S // ts),
        args=[z3, z3, z3, z3, z3, conv0, c0bd, n0, m0, gate_b, conv_w, norm_g],
        in_specs=[blk(2 * QUAD, COL_MQK), blk(QUAD, COL_MV), blk(QUAD, COL_MO), blk(QUAD, COL_GI), blk(QUAD, COL_GF),
                  per_b(SUBLANES, 2 * QUAD), per_head, per_b(1, QUAD), per_b(1, QUAD),
                  _const_spec((2, QUAD)), _const_spec((CONV_WIDTH, 2 * QUAD)), _const_spec((1, QUAD))],
        out_shape=[jax.ShapeDtypeStruct((B, S, QUAD), BF16),
                   jax.ShapeDtypeStruct((B, N_MLSTM_HEADS, HEAD_DIM, HEAD_DIM), F32),
                   jax.ShapeDtypeStruct((B, 1, QUAD), F32),
                   jax.ShapeDtypeStruct((B, 1, QUAD), F32)],
        out_specs=[pl.BlockSpec((None, ts, QUAD), lambda b, t: (b, t, 0)),
                   per_head, per_b(1, QUAD), per_b(1, QUAD)],
        scratch=[pltpu.VMEM((ts + 2 * SUBLANES, 2 * QUAD), F32),
                 pltpu.VMEM((QUAD, QUAD), F32), pltpu.VMEM((1, QUAD), F32), pltpu.VMEM((1, QUAD), F32)])


HGRN_FAST_CHUNK = 64
HGRN_SAFE_DECAY = 80.0


def _hgrn_kernel(q_ref, f_ref, i_ref, g_ref, s0_ref, lb_ref, ng_ref, out_ref, s_out, st, qs, fs, ks, *,
                 tile, sub, n_valid, fast):
    TS = tile
    t = pl.program_id(1)

    seg64 = _seg_mask(QUAD, HEAD_SHIFT, QUAD, HEAD_SHIFT)
    seg64b = seg64.astype(BF16)

    @pl.when(t == 0)
    def _():
        st[...] = _load_block_diag(s0_ref, seg64)
    lb = lb_ref[...]
    ng = ng_ref[...]
    nt = (((1,), (1,)), ((), ()))
    tn = (((0,), (0,)), ((), ()))

    P = HGRN_FAST_CHUNK if fast else sub
    prow = _iota2((P, QUAD), 0)

    def prep(c, worst):
        r0 = pl.multiple_of(c * P, P)
        hq = q_ref[pl.ds(r0, P), :]
        f = lb + (1.0 - lb) * jax.nn.sigmoid(f_ref[pl.ds(r0, P), :])
        logf = jnp.log(f)
        kk = 1.0 - f
        if n_valid < TS:
            ok = (prow + r0) < n_valid
            logf = jnp.where(ok, logf, 0.0)
            kk = jnp.where(ok, kk, 0.0)
        qs[pl.ds(r0, P), :] = hq * jax.nn.sigmoid(hq)
        fs[pl.ds(r0, P), :] = logf
        ks[pl.ds(r0, P), :] = kk
        return jnp.minimum(worst, jnp.sum(logf, axis=0, keepdims=True))

    worst = lax.fori_loop(0, TS // P, prep, jnp.zeros((1, QUAD), F32))

    def finish(o, r0, rows, smat, b, kk, iv, qt):
        o = o + lax.dot_general(qt, smat.astype(BF16), nt, preferred_element_type=F32)
        out_ref[pl.ds(r0, rows), :] = _head_rmsnorm_gate(o, seg64b, ng, g_ref[pl.ds(r0, rows), :]).astype(out_ref.dtype)
        bl = b[rows - 1:rows, :]
        ktil = kk * jnp.exp(bl - b)
        u = lax.dot_general(iv.astype(BF16), ktil.astype(BF16), tn, preferred_element_type=F32)
        st[...] = smat * jnp.exp(bl) + jnp.where(seg64, u, 0.0)

    def fast_loop():
        L = HGRN_FAST_CHUNK
        tril = (_iota2((L, L), 1) <= _iota2((L, L), 0)).astype(BF16)
        causal = (_iota2((L, QUAD), 1) & (L - 1)) <= _iota2((L, QUAD), 0)
        zb = jnp.zeros((QUAD, QUAD), BF16)

        G = min(CHUNK_GROUP, TS // L)
        assert TS % (L * G) == 0

        def body(gi, carry):
            R = range(G)
            r0 = [pl.multiple_of((gi * G + j) * L, L) for j in R]
            kk = [ks[pl.ds(r0[j], L), :] for j in R]
            ivb = [i_ref[pl.ds(r0[j], L), :].astype(BF16) for j in R]
            b = [_dot2r(tril, fs[pl.ds(r0[j], L), :]) for j in R]
            qt = [(qs[pl.ds(r0[j], L), :] * jnp.exp(b[j])).astype(BF16) for j in R]
            kbd = [jnp.where(seg64, jnp.concatenate([(kk[j] * jnp.exp(-b[j])).astype(BF16)] * N_HGRN_HEADS, axis=0), zb)
                   for j in R]
            ibd = [jnp.where(seg64, jnp.concatenate([ivb[j]] * N_HGRN_HEADS, axis=0), zb) for j in R]
            amat = [lax.dot_general(qt[j], kbd[j], nt, preferred_element_type=F32) for j in R]
            ktil = [(kk[j] * jnp.exp(b[j][L - 1:L, :] - b[j])).astype(BF16) for j in R]
            u = [lax.dot_general(ivb[j], ktil[j], tn, preferred_element_type=F32) for j in R]
            o = [jnp.dot(jnp.where(causal, amat[j], 0.0).astype(BF16), ibd[j], preferred_element_type=F32) for j in R]
            smat = [st[...]]
            for j in R:
                smat.append(smat[j] * jnp.exp(b[j][L - 1:L, :]) + jnp.where(seg64, u[j], 0.0))
            st[...] = smat[G]
            inter = [lax.dot_general(qt[j], smat[j].astype(BF16), nt, preferred_element_type=F32) for j in R]
            o = [o[j] + inter[j] for j in R]
            msq = [_dot2(o[j] * o[j], seg64b) * (1.0 / HEAD_DIM) for j in R]
            for j in R:
                y = o[j] * lax.rsqrt(msq[j] + EPS) * ng
                out_ref[pl.ds(r0[j], L), :] = (jax.nn.sigmoid(g_ref[pl.ds(r0[j], L), :]) * y).astype(out_ref.dtype)
            return carry

        lax.fori_loop(0, TS // (L * G), body, 0)

    def exact_loop():
        L = sub
        row = _iota2((L, QUAD), 0)

        def body(c, carry):
            r0 = pl.multiple_of(c * L, L)
            q = qs[pl.ds(r0, L), :]
            kk = ks[pl.ds(r0, L), :]
            iv = i_ref[pl.ds(r0, L), :]
            b = _cumulate_rows(fs[pl.ds(r0, L), :], L, jnp.add, 0.0)
            parts = []
            for j in range(L):
                dec = jnp.exp(jnp.where(row >= j, b - b[j:j + 1, :], NEG_BIG))
                parts.append(dec * q * kk[j:j + 1, :])
            tstack = jnp.concatenate(parts, axis=0).astype(BF16)
            y = jnp.dot(tstack, seg64b, preferred_element_type=F32)
            o = y[0:L] * iv[0:1, :]
            for j in range(1, L):
                o = o + y[j * L:(j + 1) * L] * iv[j:j + 1, :]
            finish(o, r0, L, st[...], b, kk, iv, (q * jnp.exp(b)).astype(BF16))
            return carry

        lax.fori_loop(0, TS // L, body, 0, unroll=min(CHUNK_UNROLL, TS // L))

    if fast:
        safe = jnp.min(worst) > -HGRN_SAFE_DECAY
        pl.when(safe)(fast_loop)
        pl.when(jnp.logical_not(safe))(exact_loop)
    else:
        exact_loop()

    @pl.when(t == pl.num_programs(1) - 1)
    def _():
        _store_block_diag(s_out, st[...])


def _hgrn(z3, s0t, lb, norm_g, n_valid, sub):
    B, S, _ = z3.shape
    ts = min(512, S)
    fast = ts % HGRN_FAST_CHUNK == 0
    assert S % ts == 0 and ts % sub == 0
    blk = lambda c0: pl.BlockSpec((None, ts, QUAD), lambda b, t: (b, t, c0 // QUAD))
    per_b = pl.BlockSpec((None, N_HGRN_HEADS, HEAD_DIM, HEAD_DIM), lambda b, t: (b, 0, 0, 0))
    stage = lambda: pltpu.VMEM((ts, QUAD), F32)
    return _CallSpec(
        body=functools.partial(_hgrn_kernel, tile=ts, sub=sub, n_valid=n_valid, fast=fast),
        grid=(B, S // ts),
        args=[z3, z3, z3, z3, s0t, lb, norm_g],
        in_specs=[blk(COL_HQ), blk(COL_HF), blk(COL_HI), blk(COL_HG), per_b,
                  _const_spec((1, QUAD)), _const_spec((1, QUAD))],
        out_shape=[jax.ShapeDtypeStruct((B, S, QUAD), BF16),
                   jax.ShapeDtypeStruct((B, N_HGRN_HEADS, HEAD_DIM, HEAD_DIM), F32)],
        out_specs=[pl.BlockSpec((None, ts, QUAD), lambda b, t: (b, t, 0)), per_b],
        scratch=[pltpu.VMEM((QUAD, QUAD), F32), stage(), stage(), stage()])


PROJ_SPLITS = (ATTN_WIDTH, ATTN_WIDTH, ATTN_WIDTH,
               MLSTM_WIDTH, MLSTM_WIDTH, MLSTM_WIDTH, MLSTM_WIDTH, N_MLSTM_HEADS, N_MLSTM_HEADS,
               HGRN_WIDTH, HGRN_WIDTH, HGRN_WIDTH, HGRN_WIDTH)


def _prep_w_in(w):
    cuts = [int(c) for c in np.cumsum(PROJ_SPLITS)[:-1]]
    aq, ak, av, mq, mk, mv, mo, mi, mf, hq, hf, hi, hg = jnp.split(w, cuts, axis=1)
    gi, gf = jnp.repeat(mi, HEAD_DIM, axis=1), jnp.repeat(mf, HEAD_DIM, axis=1)
    return jnp.concatenate([aq, ak, av, mq, mk, mv, mo, hq, hf, hi, hg, gi, gf], axis=1).astype(BF16)


HGRN_SUBCHUNK = 16


class _LayerWeights(NamedTuple):
    g_mix: jax.Array
    w_in: jax.Array
    wkv_t: jax.Array
    gate_b: jax.Array
    conv_w: jax.Array
    mlstm_g: jax.Array
    lb: jax.Array
    hgrn_g: jax.Array
    w_out: jax.Array
    g_ffn: jax.Array
    w_up: jax.Array
    w_down: jax.Array
    g_final: jax.Array


def _gate_row(gate_b):
    return jnp.repeat(gate_b.astype(F32), HEAD_DIM, axis=1)


def _trunk_layer(x2d, B, S, n_valid, attn_fn, states, w, final, chunk_m, chunk_h, n_keep=None, stacked=None):
    conv_buf, C0, n0, m0, S0 = states
    if n_keep is None:
        z = _inproj(x2d, w.g_mix, w.w_in)
    else:
        z, kt, vt = _inproj(x2d, w.g_mix, w.w_in, w.wkv_t, S, n_keep, stacked)
    z3 = z.reshape(B, S, ZW)
    conv0 = jnp.zeros((B, SUBLANES, 2 * MLSTM_WIDTH), F32).at[:, SUBLANES - (CONV_WIDTH - 1):].set(conv_buf.astype(F32))
    m0r = jnp.repeat(m0.astype(F32), HEAD_DIM, axis=1).reshape(B, 1, MLSTM_WIDTH)
    mlstm = _mlstm(z3, conv0, C0.astype(F32), n0.astype(F32).reshape(B, 1, MLSTM_WIDTH), m0r,
                   w.gate_b, w.conv_w, w.mlstm_g, n_valid, chunk_m)
    hgrn = _hgrn(z3, jnp.swapaxes(S0.astype(F32), -1, -2), w.lb, w.hgrn_g, n_valid, chunk_h)
    attn = attn_fn(z3)
    sem = ("parallel", "arbitrary")
    if isinstance(attn, _CallSpec):
        (attn,), (ml, c_new, n, m), (hg, st) = _run([attn, mlstm, hgrn], sem, "sample_mixers")
    else:
        (ml, c_new, n, m), = _run([mlstm], sem, "mlstm")
        (hg, st), = _run([hgrn], sem, "hgrn")
    n2 = B * S
    y = _outffn(x2d, attn.reshape(n2, ATTN_WIDTH), ml.reshape(n2, MLSTM_WIDTH), hg.reshape(n2, HGRN_WIDTH),
                w.w_out, w.g_ffn, w.w_up, w.w_down, w.g_final, final)
    if n_keep is None:
        k_rows = z3[:, :n_valid, COL_AK:COL_AK + ATTN_WIDTH].reshape(B, n_valid, N_ATTN_HEADS, HEAD_DIM)
        v_rows = z3[:, :n_valid, COL_AV:COL_AV + ATTN_WIDTH].reshape(B, n_valid, N_ATTN_HEADS, HEAD_DIM)
    else:
        k_rows, v_rows = kt, vt
    conv_new = z3[:, n_valid - (CONV_WIDTH - 1):n_valid, COL_MQK:COL_MQK + 2 * MLSTM_WIDTH]
    s_new = jnp.swapaxes(st, -1, -2)
    return y, (k_rows, v_rows, conv_new, c_new, n.reshape(B, N_MLSTM_HEADS, HEAD_DIM), m[:, 0, ::HEAD_DIM], s_new)


def kernel(x_prompt, x_sample, cache_attn_k, cache_attn_v, state_mlstm_conv, state_mlstm_C, state_mlstm_n, state_mlstm_m, state_hgrn_S, rel_bias, g_mix, w_in, mlstm_gate_b, mlstm_conv_w, mlstm_norm_g, hgrn_lb_raw, hgrn_norm_g, w_out, g_ffn, w_up, w_down, g_final):
    depth = w_in.shape[0]
    bp, sp, _ = x_prompt.shape
    bs, ts, _ = x_sample.shape
    n_keep = min(max(w for w, _ in DILATED_PATTERNS), sp)
    n_past = cache_attn_k.shape[2]
    lb_p = jax.nn.softmax(hgrn_lb_raw.astype(F32), axis=0)
    hgrn_lb = jnp.cumsum(lb_p, axis=0) - lb_p[0]
    bias_dist = _bias_by_distance(rel_bias)
    p_tables = _prompt_tables(bias_dist)
    s_tables = _sample_tables(bias_dist, n_past, ts)
    cache_kt = jnp.transpose(cache_attn_k, (0, 1, 3, 4, 2)).reshape(depth, bs, ATTN_WIDTH, n_past)
    cache_vt = jnp.transpose(cache_attn_v, (0, 1, 3, 4, 2)).reshape(depth, bs, ATTN_WIDTH, n_past)

    hp = x_prompt.reshape(bp * sp, D_MODEL)
    hs = jnp.zeros((bs, TPAD, D_MODEL), F32).at[:, :ts].set(x_sample).reshape(bs * TPAD, D_MODEL)
    zero_states = (jnp.zeros((bp, CONV_WIDTH - 1, 2 * MLSTM_WIDTH), F32),
                   jnp.zeros((bp, N_MLSTM_HEADS, HEAD_DIM, HEAD_DIM), F32),
                   jnp.zeros((bp, N_MLSTM_HEADS, HEAD_DIM), F32),
                   jnp.zeros((bp, N_MLSTM_HEADS), F32),
                   jnp.zeros((bp, N_HGRN_HEADS, HEAD_DIM, HEAD_DIM), F32))
    p_states, s_states = [], []
    kv_t = None
    for l in range(depth):
        final = l == depth - 1
        w_in_l = _prep_w_in(w_in[l])
        weights = _LayerWeights(
            g_mix=g_mix[l][None], w_in=w_in_l, wkv_t=w_in_l[:, COL_AK:COL_AV + ATTN_WIDTH].T,
            gate_b=_gate_row(mlstm_gate_b[l]),
            conv_w=mlstm_conv_w[l], mlstm_g=mlstm_norm_g[l][None], lb=hgrn_lb[l][None], hgrn_g=hgrn_norm_g[l][None],
            w_out=w_out[l].astype(BF16), g_ffn=g_ffn[l][None], w_up=w_up[l].astype(BF16),
            w_down=w_down[l].astype(BF16), g_final=g_final[None])
        hp, st = _trunk_layer(hp, bp, sp, sp, functools.partial(_attn_prompt, bias=p_tables),
                              zero_states, weights, final, MLSTM_CHUNK, HGRN_SUBCHUNK, n_keep, (l, depth, kv_t))
        kv_t = st[:2]
        p_states.append(st[2:])
        sample_attn = functools.partial(_attn_sample, cache_kt=cache_kt, cache_vt=cache_vt, layer=l, tables=s_tables)
        states = (state_mlstm_conv[l], state_mlstm_C[l], state_mlstm_n[l], state_mlstm_m[l], state_hgrn_S[l])
        hs, st = _trunk_layer(hs, bs, TPAD, ts, sample_attn, states, weights, final, TPAD, TPAD)
        s_states.append(st)
    y_prompt = hp.reshape(bp, sp, D_MODEL)
    y_sample = hs.reshape(bs, TPAD, D_MODEL)[:, :ts]
    to_rows = lambda t: jnp.transpose(t.reshape(depth, bp, N_ATTN_HEADS, HEAD_DIM, n_keep), (0, 1, 4, 2, 3))
    p_out = [to_rows(kv_t[0]), to_rows(kv_t[1])] + [jnp.stack(a) for a in zip(*p_states)]
    s_out = [jnp.stack(a) for a in zip(*s_states)]
    return (y_prompt, y_sample, *p_out, *s_out)
```

```python
import functools
from typing import Callable, NamedTuple

import jax
import jax.numpy as jnp
import numpy as np
from jax import lax
from jax.experimental import pallas as pl
from jax.experimental.pallas import tpu as pltpu

F32 = jnp.float32
BF16 = jnp.bfloat16

D_MODEL = 1024
HEAD_DIM = 64
N_ATTN_HEADS = 8
N_MLSTM_HEADS = 4
N_HGRN_HEADS = 4
ATTN_WIDTH = N_ATTN_HEADS * HEAD_DIM
MLSTM_WIDTH = N_MLSTM_HEADS * HEAD_DIM
HGRN_WIDTH = N_HGRN_HEADS * HEAD_DIM
DILATED_PATTERNS = ((128, 1), (512, 4), (2048, 16))
QBLOCK = 128
N_REL_BUCKETS = 32
REL_MAX_DISTANCE = 2048
CONV_WIDTH = 4
MLSTM_CHUNK = 64
D_FF = 4 * D_MODEL
EPS = 1e-6
NEG_BIG = -1e30

LANES = 128
SUBLANES = 8

COL_AQ = 0
COL_AK = COL_AQ + ATTN_WIDTH
COL_AV = COL_AK + ATTN_WIDTH
COL_MQK = COL_AV + ATTN_WIDTH
COL_MV = COL_MQK + 2 * MLSTM_WIDTH
COL_MO = COL_MV + MLSTM_WIDTH
COL_HQ = COL_MO + MLSTM_WIDTH
COL_HF = COL_HQ + HGRN_WIDTH
COL_HI = COL_HF + HGRN_WIDTH
COL_HG = COL_HI + HGRN_WIDTH
COL_GATE = COL_HG + HGRN_WIDTH
ZW = COL_GATE + LANES

VMEM_LIMIT = 56 * 1024 * 1024


def _cparams(sem, vmem=VMEM_LIMIT):
    return pltpu.CompilerParams(dimension_semantics=sem, vmem_limit_bytes=vmem)


def _const_spec(shape):
    nd = len(shape)
    return pl.BlockSpec(shape, lambda *_: (0,) * nd, pipeline_mode=pl.Buffered(1))


class _CallSpec(NamedTuple):
    body: Callable
    grid: tuple
    args: list
    in_specs: list
    out_shape: list
    out_specs: list
    scratch: list


def _run(specs, sem, name):
    grid = specs[0].grid
    assert all(s.grid == grid for s in specs)
    n_in = [len(s.args) for s in specs]
    n_out = [len(s.out_shape) for s in specs]
    n_scr = [len(s.scratch) for s in specs]

    def body(*refs):
        ins, outs, scrs = refs[:sum(n_in)], refs[sum(n_in):sum(n_in) + sum(n_out)], refs[sum(n_in) + sum(n_out):]
        for k, s in enumerate(specs):
            a, b, c = sum(n_in[:k]), sum(n_out[:k]), sum(n_scr[:k])
            s.body(*ins[a:a + n_in[k]], *outs[b:b + n_out[k]], *scrs[c:c + n_scr[k]])

    flat = pl.pallas_call(
        body, grid=grid,
        in_specs=[x for s in specs for x in s.in_specs],
        out_shape=[x for s in specs for x in s.out_shape],
        out_specs=[x for s in specs for x in s.out_specs],
        scratch_shapes=[x for s in specs for x in s.scratch],
        compiler_params=_cparams(sem), name=name,
    )(*[x for s in specs for x in s.args])
    out, k = [], 0
    for n in n_out:
        out.append(list(flat[k:k + n]))
        k += n
    return out


def _inproj_kernel(x_ref, g_ref, w_ref, *rest, tiles_per_seq, first_keep, n_prev=0):
    x = x_ref[...]
    ms = jnp.mean(x * x, axis=-1, keepdims=True)
    xn = ((x * lax.rsqrt(ms + EPS)) * g_ref[...]).astype(BF16)
    if first_keep is None:
        (z_ref,) = rest
        z_ref[...] = jnp.dot(xn, w_ref[...], preferred_element_type=F32)
        return
    z_ref, kt_ref, vt_ref = rest[n_prev:]
    z_ref[...] = jnp.dot(xn, w_ref[...], preferred_element_type=F32)

    @pl.when(pl.program_id(0) % tiles_per_seq >= first_keep)
    def _():
        kt_ref[...] = z_ref[:, COL_AK:COL_AK + ATTN_WIDTH].T
        vt_ref[...] = z_ref[:, COL_AV:COL_AV + ATTN_WIDTH].T


def _inproj(x2d, g, w, seq=None, n_keep=None, stacked=None):
    n = x2d.shape[0]
    tm = min(512, n)
    assert n % tm == 0
    x_spec = pl.BlockSpec((tm, D_MODEL), lambda i: (i, 0))
    z_spec = pl.BlockSpec((tm, ZW), lambda i: (i, 0))
    z_shape = jax.ShapeDtypeStruct((n, ZW), F32)
    if stacked is None:
        return pl.pallas_call(
            functools.partial(_inproj_kernel, tiles_per_seq=None, first_keep=None),
            out_shape=z_shape, grid=(n // tm,),
            in_specs=[x_spec, _const_spec((1, D_MODEL)), _const_spec((D_MODEL, ZW))],
            out_specs=z_spec, compiler_params=_cparams(("parallel",)), name="inproj",
        )(x2d, g, w)
    assert seq % tm == 0 and n_keep % tm == 0
    layer, depth, prev = stacked
    tps, first_keep = seq // tm, (seq - n_keep) // tm
    t_spec = pl.BlockSpec((None, None, ATTN_WIDTH, tm),
                          lambda i: (layer, i // tps, 0, jnp.maximum(i % tps - first_keep, 0)))
    t_shape = jax.ShapeDtypeStruct((depth, n // seq, ATTN_WIDTH, n_keep), F32)
    in_specs = [x_spec, _const_spec((1, D_MODEL)), _const_spec((D_MODEL, ZW))]
    args = [x2d, g, w]
    if prev is None:
        prev = (jnp.zeros(t_shape.shape, F32), jnp.zeros(t_shape.shape, F32))
    in_specs += [pl.BlockSpec(memory_space=pl.ANY)] * 2
    aliases = {len(args): 1, len(args) + 1: 2}
    args += list(prev)
    return pl.pallas_call(
        functools.partial(_inproj_kernel, tiles_per_seq=tps, first_keep=first_keep, n_prev=2),
        out_shape=[z_shape, t_shape, t_shape], grid=(n // tm,),
        in_specs=in_specs, out_specs=[z_spec, t_spec, t_spec], input_output_aliases=aliases,
        compiler_params=_cparams(("arbitrary",)), name="inproj_kt",
    )(*args)


FF_CHUNK = 1024
FFN_ROWS = 512


def _outffn_kernel(x_ref, a_ref, m_ref, h_ref, wo_ref, gf_ref, wu_ref, wd_ref, gl_ref, y_ref, xn_sc, *, final):
    x1 = x_ref[...]
    x1 = x1 + jnp.dot(a_ref[...], wo_ref[0:ATTN_WIDTH, :], preferred_element_type=F32)
    x1 = x1 + jnp.dot(m_ref[...], wo_ref[ATTN_WIDTH:ATTN_WIDTH + MLSTM_WIDTH, :], preferred_element_type=F32)
    x1 = x1 + jnp.dot(h_ref[...], wo_ref[ATTN_WIDTH + MLSTM_WIDTH:, :], preferred_element_type=F32)
    ms = jnp.mean(x1 * x1, axis=-1, keepdims=True)
    xn_sc[...] = ((x1 * lax.rsqrt(ms + EPS)) * gf_ref[...]).astype(BF16)
    y_ref[...] = x1
    nchunk = D_FF // FF_CHUNK
    up = lambda c: jnp.dot(xn_sc[...], wu_ref[:, c * FF_CHUNK:(c + 1) * FF_CHUNK], preferred_element_type=F32)
    u = up(0)
    for c in range(nchunk):
        u_next = up(c + 1) if c + 1 < nchunk else None
        hh = jnp.square(jnp.maximum(u, 0.0)).astype(BF16)
        y_ref[...] += jnp.dot(hh, wd_ref[c * FF_CHUNK:(c + 1) * FF_CHUNK, :], preferred_element_type=F32)
        u = u_next
    if final:
        x2 = y_ref[...]
        ms2 = jnp.mean(x2 * x2, axis=-1, keepdims=True)
        y_ref[...] = (x2 * lax.rsqrt(ms2 + EPS)) * gl_ref[...]


def _outffn(x2d, attn, ml, hg, w_out, g_ffn, w_up, w_down, g_final, final):
    n = x2d.shape[0]
    tm = min(FFN_ROWS, n)
    assert n % tm == 0
    row = lambda w: pl.BlockSpec((tm, w), lambda i: (i, 0))
    return pl.pallas_call(
        functools.partial(_outffn_kernel, final=final),
        out_shape=jax.ShapeDtypeStruct((n, D_MODEL), F32),
        grid=(n // tm,),
        in_specs=[row(D_MODEL), row(ATTN_WIDTH), row(MLSTM_WIDTH), row(HGRN_WIDTH),
                  _const_spec((D_MODEL, D_MODEL)), _const_spec((1, D_MODEL)),
                  _const_spec((D_MODEL, D_FF)), _const_spec((D_FF, D_MODEL)), _const_spec((1, D_MODEL))],
        out_specs=row(D_MODEL),
        scratch_shapes=[pltpu.VMEM((tm, D_MODEL), BF16)],
        compiler_params=_cparams(("parallel",)),
        name="outffn",
    )(x2d, attn, ml, hg, w_out, g_ffn, w_up, w_down, g_final)


def _t5_causal_bucket(dist):
    n = np.asarray(dist).astype(np.int32)
    max_exact = N_REL_BUCKETS // 2
    scaled = np.log(np.maximum(n, 1) / max_exact) / np.log(REL_MAX_DISTANCE / max_exact)
    large = np.minimum(max_exact + (scaled * (N_REL_BUCKETS - max_exact)).astype(np.int32), N_REL_BUCKETS - 1)
    return np.where(n < max_exact, n, large).astype(np.int32)


BIAS_DIST = 2304


def _bias_by_distance(rel_bias):
    assert BIAS_DIST > max(w for w, _ in DILATED_PATTERNS) + SUBLANES
    return rel_bias.astype(F32)[_t5_causal_bucket(np.arange(BIAS_DIST)[::-1])].T


def _dist_slice(bias_desc, lo, hi, step=1):
    last = BIAS_DIST - 1
    return bias_desc[:, last - hi:last - lo + 1:step]


def _prompt_tables(bias_dist):
    H = N_ATTN_HEADS
    a = np.arange(QBLOCK)[:, None]
    b = np.arange(2 * QBLOCK)[None, :]
    rel = QBLOCK + a - b
    band = (rel >= 0) & (rel <= QBLOCK)
    period = 3 * QBLOCK
    biases = []
    for window, dil in DILATED_PATTERNS:
        nsub = window // dil
        assert nsub == QBLOCK
        vd = _dist_slice(bias_dist, 0, nsub * dil, dil)
        rp = jnp.concatenate([vd, jnp.broadcast_to(vd[:, -1:], (H, QBLOCK)),
                              jnp.broadcast_to(vd[:, 0:1], (H, QBLOCK - 1))], axis=1)
        skew = jnp.tile(rp, (1, QBLOCK))[:, :QBLOCK * (period - 1)].reshape(H, QBLOCK, period - 1)
        biases.append(skew[:, :, :2 * QBLOCK].reshape(H // 2, 2 * QBLOCK, 2 * QBLOCK))
    mask = np.stack([np.tile(band, (2, 1)), np.tile(band & (b >= QBLOCK), (2, 1))])
    table = jnp.where(mask[None, None], jnp.stack(biases)[:, :, None], -jnp.inf)
    return jnp.swapaxes(table, -1, -2)


ATTN_GROUP = 8


def _attn_prompt_kernel(q_ref, k_ref, v_ref, bias_ref, out_ref,
                        x4, qs, ks, vts, o1, o4, o16, l1, l4, l16, p_scr, st_scr, *, seq):
    S = seq
    nblk = S // QBLOCK
    scale = HEAD_DIM ** -0.5
    zpad = jnp.zeros((QBLOCK, LANES), BF16)
    for di in range(3):
        ks[di, 0:QBLOCK, :] = zpad
        vts[di, 0] = zpad

    R = 512

    def put(src, di, row0, blk):
        if src is q_ref:
            qs[di, row0:row0 + blk.shape[0], :] = (blk * scale).astype(BF16)
        elif src is k_ref:
            ks[di, QBLOCK + row0:QBLOCK + row0 + blk.shape[0], :] = blk.astype(BF16)
        else:
            blk_t = blk.T.astype(BF16)
            for t in range(blk.shape[0] // QBLOCK):
                vts[di, 1 + row0 // QBLOCK + t] = blk_t[:, t * QBLOCK:(t + 1) * QBLOCK]

    for src in (q_ref, k_ref, v_ref):
        for i in range(S // R):
            put(src, 0, i * R, src[i * R:(i + 1) * R, :])
        for r in range(4):
            for i in range(S // 4 // R):
                blk = src[pl.ds(r + 4 * i * R, R, stride=4), :]
                row0 = r * (S // 4) + i * R
                x4[row0:row0 + R, :] = blk
                put(src, 1, row0, blk)
        n16 = S // 16
        for c4 in range(4):
            for a in range(4):
                put(src, 2, (c4 * 4 + a) * n16, x4[pl.ds(c4 * (S // 4) + a, n16, stride=4), :])

    lane = lax.broadcasted_iota(jnp.int32, (QBLOCK, LANES), 1)
    head_a = lane < HEAD_DIM
    row_a = lax.broadcasted_iota(jnp.int32, (QBLOCK, LANES), 0) < HEAD_DIM
    nt = (((1,), (1,)), ((), ()))

    def branch(di, dil, o_ref, l_ref):
        per_class = (S // dil) // QBLOCK

        R = range(ATTN_GROUP)
        zero = jnp.zeros((QBLOCK, LANES), BF16)

        def scores(g):
            out = []
            for j in R:
                i = g * ATTN_GROUP + j
                base = pl.multiple_of(i * QBLOCK, QBLOCK)
                q = qs[di, pl.ds(base, QBLOCK), :]
                kk = ks[di, pl.ds(base, 2 * QBLOCK), :]
                q2 = jnp.concatenate([jnp.where(head_a, q, zero), jnp.where(head_a, zero, q)], axis=0)
                out.append(lax.dot_general(kk, q2, nt, preferred_element_type=F32))
            return out

        def softmax(g, slot, st):
            for j in R:
                i = g * ATTN_GROUP + j
                s = st[j] + bias_ref[di, jnp.where(i % per_class == 0, 1, 0)]
                m = jnp.max(s, axis=0, keepdims=True)
                p = jnp.exp(s - m)
                p_scr[slot, j] = p.astype(BF16)
                st_scr[slot, j, 0:1, :] = m
                st_scr[slot, j, 1:2, :] = jnp.sum(p, axis=0, keepdims=True)

        def weighted_values(g, slot):
            out = []
            for j in R:
                i = g * ATTN_GROUP + j
                vt = jnp.concatenate([vts[di, i], vts[di, i + 1]], axis=1)
                ot = jnp.dot(vt, p_scr[slot, j], preferred_element_type=F32)
                out.append((ot, st_scr[slot, j, 0:1, :], st_scr[slot, j, 1:2, :]))
            return out

        def outputs(g, pv):
            for j in R:
                i = g * ATTN_GROUP + j
                base = pl.multiple_of(i * QBLOCK, QBLOCK)
                ot, m, l = pv[j]
                otn = ot / l
                lse = m + jnp.log(l)
                o = jnp.where(row_a, otn[:, 0:QBLOCK], otn[:, QBLOCK:]).T
                ls = jnp.where(row_a, jnp.broadcast_to(lse[:, 0:QBLOCK], (QBLOCK, LANES)),
                               jnp.broadcast_to(lse[:, QBLOCK:], (QBLOCK, LANES))).T
                if dil == 1:
                    o_ref[pl.ds(base, QBLOCK), :] = o
                    l_ref[pl.ds(base, QBLOCK), :] = ls
                else:
                    c, n = i // per_class, i % per_class
                    res = c if dil == 4 else 4 * (c % 4) + c // 4
                    start = dil * QBLOCK * n + res
                    o_ref[pl.ds(start, QBLOCK, stride=dil), :] = o
                    l_ref[pl.ds(start, QBLOCK, stride=dil), :] = ls

        ngroup = nblk // ATTN_GROUP
        softmax(0, 0, scores(0))

        def body(g, carry):
            slot = g % 2
            st = scores(g)
            pv = weighted_values(g - 1, 1 - slot)
            softmax(g, slot, st)
            outputs(g - 1, pv)
            return carry

        lax.fori_loop(1, ngroup, body, 0)
        outputs(ngroup - 1, weighted_values(ngroup - 1, (ngroup - 1) % 2))

    branch(0, 1, o1, l1)
    branch(1, 4, o4, l4)
    branch(2, 16, o16, l16)

    T = 256

    def mix(i, carry):
        r0 = pl.multiple_of(i * T, T)
        la, lb, lc = l1[pl.ds(r0, T), :], l4[pl.ds(r0, T), :], l16[pl.ds(r0, T), :]
        mx = jnp.maximum(jnp.maximum(la, lb), lc)
        ea, eb, ec = jnp.exp(la - mx), jnp.exp(lb - mx), jnp.exp(lc - mx)
        num = ea * o1[pl.ds(r0, T), :] + eb * o4[pl.ds(r0, T), :] + ec * o16[pl.ds(r0, T), :]
        out_ref[pl.ds(r0, T), :] = (num / (ea + eb + ec)).astype(out_ref.dtype)
        return carry

    lax.fori_loop(0, S // T, mix, 0)


def _attn_prompt(z3, bias):
    B, S, _ = z3.shape
    assert S % (16 * QBLOCK) == 0
    npair = N_ATTN_HEADS // 2
    col = lambda c0: pl.BlockSpec((None, S, LANES), lambda b, p: (b, 0, c0 // LANES + p))
    f32s = lambda: pltpu.VMEM((S, LANES), F32)
    return pl.pallas_call(
        functools.partial(_attn_prompt_kernel, seq=S),
        out_shape=jax.ShapeDtypeStruct((B, S, ATTN_WIDTH), BF16),
        grid=(B, npair),
        in_specs=[col(COL_AQ), col(COL_AK), col(COL_AV),
                  pl.BlockSpec((3, None, 2, 2 * QBLOCK, 2 * QBLOCK), lambda b, p: (0, p, 0, 0, 0))],
        out_specs=pl.BlockSpec((None, S, LANES), lambda b, p: (b, 0, p)),
        scratch_shapes=[f32s(),
                        pltpu.VMEM((3, S, LANES), BF16),
                        pltpu.VMEM((3, S + QBLOCK, LANES), BF16),
                        pltpu.VMEM((3, S // QBLOCK + 1, LANES, QBLOCK), BF16),
                        f32s(), f32s(), f32s(), f32s(), f32s(), f32s(),
                        pltpu.VMEM((2, ATTN_GROUP, 2 * QBLOCK, 2 * QBLOCK), BF16),
                        pltpu.VMEM((2, ATTN_GROUP, SUBLANES, 2 * QBLOCK), F32)],
        compiler_params=_cparams(("parallel", "parallel")),
        name="attn_prompt",
    )(z3, z3, z3, bias)


TPAD = SUBLANES
TAIL = 512


def _sample_tables(bias_dist, n_past, n_tok):
    assert n_past >= max(w for w, _ in DILATED_PATTERNS) and n_tok <= TPAD and TAIL >= DILATED_PATTERNS[1][0]
    H = N_ATTN_HEADS
    t = np.arange(TPAD)[:, None]

    def by_row(width):
        rows = [_dist_slice(bias_dist, tt + 1, tt + width) for tt in range(TPAD)]
        return jnp.stack(rows, axis=1).reshape(H * TPAD, width)

    bias_tail, bias_full = by_row(TAIL), by_row(n_past)
    zero_dist = _dist_slice(bias_dist, 0, 0)
    new_rows = [jnp.concatenate([_dist_slice(bias_dist, 0, tt),
                                 jnp.broadcast_to(zero_dist, (H, TPAD - tt - 1))], axis=1) for tt in range(TPAD)]
    bias_new = jnp.stack(new_rows, axis=1).reshape(H * TPAD, TPAD)

    def valid(dj, dil, nsub, lo):
        ok = (dj % dil == 0) & (dj // dil >= lo) & (dj // dil <= nsub) & (t < n_tok)
        return np.tile(ok, (H, 1)).astype(np.float32)

    tn = np.arange(TPAD)[None, :]
    m_tail, m_new = [], []
    for window, dil in DILATED_PATTERNS:
        nsub = window // dil
        if window <= TAIL:
            m_tail.append(valid(TAIL + t - np.arange(TAIL)[None, :], dil, nsub, 1))
        else:
            m_full = valid(n_past + t - np.arange(n_past)[None, :], dil, nsub, 1)
        m_new.append(valid(t - tn, dil, nsub, 0) * (tn < n_tok))
    return (bias_tail, bias_full, bias_new, jnp.asarray(np.stack(m_tail)), jnp.asarray(m_full),
            jnp.asarray(np.stack(m_new).astype(np.float32)))


def _attn_sample_kernel(q_ref, kn_ref, vn_ref, kt_ref, vt_ref, bt_ref, bf_ref, bn_ref, mt_ref, mf_ref, mn_ref,
                        out_ref, *, n_past):
    H = N_ATTN_HEADS
    rows = H * TPAD
    same_head = (_iota2((rows, ATTN_WIDTH), 0) >> 3) == (_iota2((rows, ATTN_WIDTH), 1) >> HEAD_SHIFT)
    q = q_ref[...] * (HEAD_DIM ** -0.5)
    qm = jnp.where(same_head, jnp.concatenate([q] * H, axis=0), 0.0).astype(BF16)
    nt = (((1,), (1,)), ((), ()))
    kn = kn_ref[...].astype(BF16)
    vn = vn_ref[...].astype(BF16)
    kt = kt_ref[...].astype(BF16)
    vt = vt_ref[...].astype(BF16)
    s_new = lax.dot_general(qm, kn, nt, preferred_element_type=F32)
    s_full = jnp.dot(qm, kt, preferred_element_type=F32)
    s_tail = s_full[:, n_past - TAIL:]

    outs, lses = [], []
    for di, (window, _) in enumerate(DILATED_PATTERNS):
        if window <= TAIL:
            sm = jnp.where(mt_ref[di] > 0.5, s_tail + bt_ref[...], NEG_BIG)
            vmain = vt[:, n_past - TAIL:]
        else:
            sm = jnp.where(mf_ref[...] > 0.5, s_full + bf_ref[...], NEG_BIG)
            vmain = vt
        sn = jnp.where(mn_ref[di] > 0.5, s_new + bn_ref[...], NEG_BIG)
        m = jnp.maximum(jnp.max(sm, axis=-1, keepdims=True), jnp.max(sn, axis=-1, keepdims=True))
        pm = jnp.exp(sm - m)
        pn = jnp.exp(sn - m)
        l = jnp.sum(pm, axis=-1, keepdims=True) + jnp.sum(pn, axis=-1, keepdims=True)
        o = jnp.dot(pn.astype(BF16), vn, preferred_element_type=F32)
        o = o + lax.dot_general(pm.astype(BF16), vmain, nt, preferred_element_type=F32)
        outs.append(o / l)
        lses.append(m + jnp.log(l))
    mx = jnp.maximum(jnp.maximum(lses[0], lses[1]), lses[2])
    es = [jnp.exp(ls - mx) for ls in lses]
    mixed = (es[0] * outs[0] + es[1] * outs[1] + es[2] * outs[2]) / (es[0] + es[1] + es[2])
    mixed = jnp.where(same_head, mixed, 0.0)
    acc = mixed[0:TPAD]
    for h in range(1, H):
        acc = acc + mixed[h * TPAD:(h + 1) * TPAD]
    out_ref[...] = acc.astype(out_ref.dtype)


def _attn_sample(z3, cache_kt, cache_vt, layer, tables):
    B, T, _ = z3.shape
    assert T == TPAD
    n_past = cache_kt.shape[-1]
    new = lambda c0: pl.BlockSpec((None, TPAD, ATTN_WIDTH), lambda b, t: (b, 0, c0 // ATTN_WIDTH))
    buf = pl.BlockSpec((None, None, ATTN_WIDTH, n_past), lambda b, t: (layer, b, 0, 0))
    return _CallSpec(
        body=functools.partial(_attn_sample_kernel, n_past=n_past),
        grid=(B, 1),
        args=[z3, z3, z3, cache_kt, cache_vt, *tables],
        in_specs=[new(COL_AQ), new(COL_AK), new(COL_AV), buf, buf] + [_const_spec(t.shape) for t in tables],
        out_shape=[jax.ShapeDtypeStruct((B, TPAD, ATTN_WIDTH), BF16)],
        out_specs=[pl.BlockSpec((None, TPAD, ATTN_WIDTH), lambda b, t: (b, 0, 0))],
        scratch=[])


QUAD = N_MLSTM_HEADS * HEAD_DIM
CHUNK_UNROLL = 2
CHUNK_GROUP = 4
HEAD_SHIFT = 6


def _iota2(shape, axis):
    return lax.broadcasted_iota(jnp.int32, shape, axis)


def _log2(n):
    k = int(n).bit_length() - 1
    assert 1 << k == n
    return k


def _seg_mask(rows, row_shift, cols, col_shift):
    return (_iota2((rows, cols), 0) >> row_shift) == (_iota2((rows, cols), 1) >> col_shift)


def _cumulate_rows(x, length, op, fill):
    row = _iota2(x.shape, 0) & (length - 1)
    sh = 1
    while sh < length:
        x = op(x, jnp.where(row >= sh, pltpu.roll(x, sh, axis=0), fill))
        sh *= 2
    return x


def _split2(x):
    hi = x.astype(BF16)
    lo = (x - hi.astype(F32)).astype(BF16)
    return hi, lo


def _dot2(x, w):
    hi, lo = _split2(x)
    return jnp.dot(hi, w, preferred_element_type=F32) + jnp.dot(lo, w, preferred_element_type=F32)


def _dot2r(w, x):
    hi, lo = _split2(x)
    return jnp.dot(w, hi, preferred_element_type=F32) + jnp.dot(w, lo, preferred_element_type=F32)


def _load_block_diag(ref, seg64):
    flat = ref[...].reshape(QUAD, HEAD_DIM)
    return jnp.where(seg64, jnp.concatenate([flat] * (QUAD // HEAD_DIM), axis=1), 0.0)


def _store_block_diag(ref, mat):
    for h in range(QUAD // HEAD_DIM):
        ref[h] = mat[h * HEAD_DIM:(h + 1) * HEAD_DIM, h * HEAD_DIM:(h + 1) * HEAD_DIM]


def _head_rmsnorm_gate(h, seg64b, g_row, gate_pre):
    ms = _dot2(h * h, seg64b) * (1.0 / HEAD_DIM)
    return jax.nn.sigmoid(gate_pre) * (h * lax.rsqrt(ms + EPS) * g_row)


def _mlstm_kernel(qk_ref, v_ref, o_ref, gate_ref, conv0_ref, c0_ref, n0_ref, m0_ref, gb_ref, cw_ref, ng_ref,
                  out_ref, c_out, n_out, m_out, xp, cs, ns, ms, *, tile, chunk, n_valid):
    TS, L = tile, chunk
    t = pl.program_id(1)
    PAD = SUBLANES

    @pl.when(t == 0)
    def _():
        cs[...] = _load_block_diag(c0_ref, _seg_mask(QUAD, HEAD_SHIFT, QUAD, HEAD_SHIFT))
        ns[...] = n0_ref[...]
        ms[...] = m0_ref[...]
        xp[0:PAD, :] = conv0_ref[...]

    @pl.when(t > 0)
    def _():
        xp[0:PAD, :] = xp[TS:TS + PAD, :]

    xp[PAD:PAD + TS, :] = qk_ref[...]

    LK = MLSTM_CHUNK
    seg64 = _seg_mask(QUAD, HEAD_SHIFT, QUAD, HEAD_SHIFT)
    seg64b = seg64.astype(BF16)
    dmask = (_iota2((LK, QUAD), 1) & (LK - 1)) == _iota2((LK, QUAD), 0)
    causal = (_iota2((L, QUAD), 1) & (LK - 1)) <= _iota2((L, QUAD), 0)
    tril = (_iota2((L, L), 1) <= _iota2((L, L), 0)).astype(BF16)
    ones_lk = jnp.ones((L, LK), BF16)
    row = _iota2((L, QUAD), 0)
    cw = cw_ref[...]
    gb = gb_ref[...]
    ng = ng_ref[...]

    first_half = _iota2((L, LANES), 1) < HEAD_DIM

    def per_head_lanes(g, lane0):
        col = [jnp.broadcast_to(g[:, lane0 + h:lane0 + h + 1], (L, LANES)) for h in range(N_MLSTM_HEADS)]
        return jnp.concatenate([jnp.where(first_half, col[0], col[1]), jnp.where(first_half, col[2], col[3])], axis=1)

    def key_rows(x, fill):
        if L == LK:
            return x
        return jnp.concatenate([x, jnp.full((LK - L, QUAD), fill, x.dtype)], axis=0)

    G = min(CHUNK_GROUP, TS // L)
    assert TS % (L * G) == 0
    nt = (((1,), (1,)), ((), ()))
    tn = (((0,), (0,)), ((), ()))
    zb = jnp.zeros((QUAD, QUAD), BF16)

    def group_body(gi, carry):
        R = range(G)
        r0 = [pl.multiple_of((gi * G + j) * L, L) for j in R]
        q, k, v, ig, lf = [], [], [], [], []
        for j in R:
            win = xp[pl.ds(r0[j], L + PAD), :]
            acc = win[PAD:PAD + L] * cw[CONV_WIDTH - 1:CONV_WIDTH, :]
            for s in range(1, CONV_WIDTH):
                acc = acc + win[PAD - s:PAD - s + L] * cw[CONV_WIDTH - 1 - s:CONV_WIDTH - s, :]
            qk = acc * jax.nn.sigmoid(acc)
            q.append(qk[:, 0:QUAD])
            k.append(qk[:, QUAD:] * (HEAD_DIM ** -0.5))
            v.append(v_ref[pl.ds(r0[j], L), :])
            gate = gate_ref[pl.ds(r0[j], L), :]
            ig_j = per_head_lanes(gate, 0) + gb[0:1, :]
            g2 = per_head_lanes(gate, N_MLSTM_HEADS) + gb[1:2, :]
            lf_j = jnp.minimum(g2, 0.0) - jnp.log(1.0 + jnp.exp(-jnp.abs(g2)))
            if n_valid < TS:
                ok = (row + r0[j]) < n_valid
                ig_j = jnp.where(ok, ig_j, NEG_BIG)
                lf_j = jnp.where(ok, lf_j, 0.0)
            ig.append(ig_j)
            lf.append(lf_j)
        b = [_dot2r(tril, lf[j]) for j in R]
        a = [ig[j] - b[j] for j in R]
        cm = [_cumulate_rows(a[j], L, jnp.maximum, -jnp.inf) for j in R]
        arow = [_dot2r(ones_lk, jnp.where(dmask, key_rows(a[j], NEG_BIG), 0.0)) for j in R]
        qb = [q[j].astype(BF16) for j in R]
        vb = [v[j].astype(BF16) for j in R]
        kbd = [jnp.where(seg64, jnp.concatenate([key_rows(k[j], 0.0).astype(BF16)] * N_MLSTM_HEADS, axis=0), zb) for j in R]
        vbd = [jnp.where(seg64, jnp.concatenate([key_rows(v[j], 0.0).astype(BF16)] * N_MLSTM_HEADS, axis=0), zb) for j in R]
        qkt = [lax.dot_general(qb[j], kbd[j], nt, preferred_element_type=F32) for j in R]

        mprev = ms[...]
        M, gg, emt = [], [], []
        for j in R:
            M.append(jnp.maximum(cm[j], mprev))
            mt = b[j] + M[j]
            gg.append(jnp.exp(mprev - M[j]))
            emt.append(jnp.exp(-mt))
            mprev = mt[L - 1:L, :]
        ms[...] = mprev

        wts = [jnp.where(causal, jnp.exp(arow[j] - M[j]), 0.0) * qkt[j] for j in R]
        kd = [k[j] * jnp.exp(a[j] - M[j][L - 1:L, :]) for j in R]
        num = [jnp.dot(wts[j].astype(BF16), vbd[j], preferred_element_type=F32) for j in R]
        u = [lax.dot_general(kd[j].astype(BF16), vb[j], tn, preferred_element_type=F32) for j in R]

        cmat, nvec = [cs[...]], [ns[...]]
        for j in R:
            gl = gg[j][L - 1:L, :]
            cmat.append(gl * cmat[j] + jnp.where(seg64, u[j], 0.0))
            nvec.append(gl * nvec[j] + jnp.sum(kd[j], axis=0, keepdims=True))
        cs[...] = cmat[G]
        ns[...] = nvec[G]

        inter = [jnp.dot(qb[j], cmat[j].astype(BF16), preferred_element_type=F32) for j in R]
        den = [_dot2(wts[j] + gg[j] * (q[j] * nvec[j]), seg64b) for j in R]
        h = [(num[j] + gg[j] * inter[j]) / jnp.maximum(jnp.abs(den[j]), emt[j]) for j in R]
        msq = [_dot2(h[j] * h[j], seg64b) * (1.0 / HEAD_DIM) for j in R]
        for j in R:
            y = h[j] * lax.rsqrt(msq[j] + EPS) * ng
            out_ref[pl.ds(r0[j], L), :] = (jax.nn.sigmoid(o_ref[pl.ds(r0[j], L), :]) * y).astype(out_ref.dtype)
        return carry

    lax.fori_loop(0, TS // (L * G), group_body, 0)

    @pl.when(t == pl.num_programs(1) - 1)
    def _():
        _store_block_diag(c_out, cs[...])
        n_out[...] = ns[...]
        m_out[...] = ms[...]


def _mlstm(z3, conv0, c0bd, n0, m0, gate_b, conv_w, norm_g, n_valid, chunk):
    B, S, _ = z3.shape
    ts = min(512, S)
    assert S % ts == 0 and ts % chunk == 0
    blk = lambda w, c0: pl.BlockSpec((None, ts, w), lambda b, t: (b, t, c0 // w))
    per_b = lambda r, w: pl.BlockSpec((None, r, w), lambda b, t: (b, 0, 0))
    per_head = pl.BlockSpec((None, N_MLSTM_HEADS, HEAD_DIM, HEAD_DIM), lambda b, t: (b, 0, 0, 0))
    return _CallSpec(
        body=functools.partial(_mlstm_kernel, tile=ts, chunk=chunk, n_valid=n_valid),
        grid=(B, S // ts),
        args=[z3, z3, z3, z3, conv0, c0bd, n0, m0, gate_b, conv_w, norm_g],
        in_specs=[blk(2 * QUAD, COL_MQK), blk(QUAD, COL_MV), blk(QUAD, COL_MO), blk(LANES, COL_GATE),
                  per_b(SUBLANES, 2 * QUAD), per_head, per_b(1, QUAD), per_b(1, QUAD),
                  _const_spec((2, QUAD)), _const_spec((CONV_WIDTH, 2 * QUAD)), _const_spec((1, QUAD))],
        out_shape=[jax.ShapeDtypeStruct((B, S, QUAD), BF16),
                   jax.ShapeDtypeStruct((B, N_MLSTM_HEADS, HEAD_DIM, HEAD_DIM), F32),
                   jax.ShapeDtypeStruct((B, 1, QUAD), F32),
                   jax.ShapeDtypeStruct((B, 1, QUAD), F32)],
        out_specs=[pl.BlockSpec((None, ts, QUAD), lambda b, t: (b, t, 0)),
                   per_head, per_b(1, QUAD), per_b(1, QUAD)],
        scratch=[pltpu.VMEM((ts + 2 * SUBLANES, 2 * QUAD), F32),
                 pltpu.VMEM((QUAD, QUAD), F32), pltpu.VMEM((1, QUAD), F32), pltpu.VMEM((1, QUAD), F32)])


HGRN_FAST_CHUNK = 64
HGRN_SAFE_DECAY = 80.0


def _hgrn_kernel(q_ref, f_ref, i_ref, g_ref, s0_ref, lb_ref, ng_ref, out_ref, s_out, st, qs, fs, ks, *,
                 tile, sub, n_valid, fast):
    TS = tile
    t = pl.program_id(1)

    seg64 = _seg_mask(QUAD, HEAD_SHIFT, QUAD, HEAD_SHIFT)
    seg64b = seg64.astype(BF16)

    @pl.when(t == 0)
    def _():
        st[...] = _load_block_diag(s0_ref, seg64)
    lb = lb_ref[...]
    ng = ng_ref[...]
    nt = (((1,), (1,)), ((), ()))
    tn = (((0,), (0,)), ((), ()))

    P = HGRN_FAST_CHUNK if fast else sub
    prow = _iota2((P, QUAD), 0)

    def prep(c, worst):
        r0 = pl.multiple_of(c * P, P)
        hq = q_ref[pl.ds(r0, P), :]
        f = lb + (1.0 - lb) * jax.nn.sigmoid(f_ref[pl.ds(r0, P), :])
        logf = jnp.log(f)
        kk = 1.0 - f
        if n_valid < TS:
            ok = (prow + r0) < n_valid
            logf = jnp.where(ok, logf, 0.0)
            kk = jnp.where(ok, kk, 0.0)
        qs[pl.ds(r0, P), :] = hq * jax.nn.sigmoid(hq)
        fs[pl.ds(r0, P), :] = logf
        ks[pl.ds(r0, P), :] = kk
        return jnp.minimum(worst, jnp.sum(logf, axis=0, keepdims=True))

    worst = lax.fori_loop(0, TS // P, prep, jnp.zeros((1, QUAD), F32))

    def finish(o, r0, rows, smat, b, kk, iv, qt):
        o = o + lax.dot_general(qt, smat.astype(BF16), nt, preferred_element_type=F32)
        out_ref[pl.ds(r0, rows), :] = _head_rmsnorm_gate(o, seg64b, ng, g_ref[pl.ds(r0, rows), :]).astype(out_ref.dtype)
        bl = b[rows - 1:rows, :]
        ktil = kk * jnp.exp(bl - b)
        u = lax.dot_general(iv.astype(BF16), ktil.astype(BF16), tn, preferred_element_type=F32)
        st[...] = smat * jnp.exp(bl) + jnp.where(seg64, u, 0.0)

    def fast_loop():
        L = HGRN_FAST_CHUNK
        tril = (_iota2((L, L), 1) <= _iota2((L, L), 0)).astype(BF16)
        causal = (_iota2((L, QUAD), 1) & (L - 1)) <= _iota2((L, QUAD), 0)
        zb = jnp.zeros((QUAD, QUAD), BF16)

        G = min(CHUNK_GROUP, TS // L)
        assert TS % (L * G) == 0

        def body(gi, carry):
            R = range(G)
            r0 = [pl.multiple_of((gi * G + j) * L, L) for j in R]
            kk = [ks[pl.ds(r0[j], L), :] for j in R]
            ivb = [i_ref[pl.ds(r0[j], L), :].astype(BF16) for j in R]
            b = [_dot2r(tril, fs[pl.ds(r0[j], L), :]) for j in R]
            qt = [(qs[pl.ds(r0[j], L), :] * jnp.exp(b[j])).astype(BF16) for j in R]
            kbd = [jnp.where(seg64, jnp.concatenate([(kk[j] * jnp.exp(-b[j])).astype(BF16)] * N_HGRN_HEADS, axis=0), zb)
                   for j in R]
            ibd = [jnp.where(seg64, jnp.concatenate([ivb[j]] * N_HGRN_HEADS, axis=0), zb) for j in R]
            amat = [lax.dot_general(qt[j], kbd[j], nt, preferred_element_type=F32) for j in R]
            ktil = [(kk[j] * jnp.exp(b[j][L - 1:L, :] - b[j])).astype(BF16) for j in R]
            u = [lax.dot_general(ivb[j], ktil[j], tn, preferred_element_type=F32) for j in R]
            o = [jnp.dot(jnp.where(causal, amat[j], 0.0).astype(BF16), ibd[j], preferred_element_type=F32) for j in R]
            smat = [st[...]]
            for j in R:
                smat.append(smat[j] * jnp.exp(b[j][L - 1:L, :]) + jnp.where(seg64, u[j], 0.0))
            st[...] = smat[G]
            inter = [lax.dot_general(qt[j], smat[j].astype(BF16), nt, preferred_element_type=F32) for j in R]
            o = [o[j] + inter[j] for j in R]
            msq = [_dot2(o[j] * o[j], seg64b) * (1.0 / HEAD_DIM) for j in R]
            for j in R:
                y = o[j] * lax.rsqrt(msq[j] + EPS) * ng
                out_ref[pl.ds(r0[j], L), :] = (jax.nn.sigmoid(g_ref[pl.ds(r0[j], L), :]) * y).astype(out_ref.dtype)
            return carry

        lax.fori_loop(0, TS // (L * G), body, 0)

    def exact_loop():
        L = sub
        row = _iota2((L, QUAD), 0)

        def body(c, carry):
            r0 = pl.multiple_of(c * L, L)
            q = qs[pl.ds(r0, L), :]
            kk = ks[pl.ds(r0, L), :]
            iv = i_ref[pl.ds(r0, L), :]
            b = _cumulate_rows(fs[pl.ds(r0, L), :], L, jnp.add, 0.0)
            parts = []
            for j in range(L):
                dec = jnp.exp(jnp.where(row >= j, b - b[j:j + 1, :], NEG_BIG))
                parts.append(dec * q * kk[j:j + 1, :])
            tstack = jnp.concatenate(parts, axis=0).astype(BF16)
            y = jnp.dot(tstack, seg64b, preferred_element_type=F32)
            o = y[0:L] * iv[0:1, :]
            for j in range(1, L):
                o = o + y[j * L:(j + 1) * L] * iv[j:j + 1, :]
            finish(o, r0, L, st[...], b, kk, iv, (q * jnp.exp(b)).astype(BF16))
            return carry

        lax.fori_loop(0, TS // L, body, 0, unroll=min(CHUNK_UNROLL, TS // L))

    if fast:
        safe = jnp.min(worst) > -HGRN_SAFE_DECAY
        pl.when(safe)(fast_loop)
        pl.when(jnp.logical_not(safe))(exact_loop)
    else:
        exact_loop()

    @pl.when(t == pl.num_programs(1) - 1)
    def _():
        _store_block_diag(s_out, st[...])


def _hgrn(z3, s0t, lb, norm_g, n_valid, sub):
    B, S, _ = z3.shape
    ts = min(512, S)
    fast = ts % HGRN_FAST_CHUNK == 0
    assert S % ts == 0 and ts % sub == 0
    blk = lambda c0: pl.BlockSpec((None, ts, QUAD), lambda b, t: (b, t, c0 // QUAD))
    per_b = pl.BlockSpec((None, N_HGRN_HEADS, HEAD_DIM, HEAD_DIM), lambda b, t: (b, 0, 0, 0))
    stage = lambda: pltpu.VMEM((ts, QUAD), F32)
    return _CallSpec(
        body=functools.partial(_hgrn_kernel, tile=ts, sub=sub, n_valid=n_valid, fast=fast),
        grid=(B, S // ts),
        args=[z3, z3, z3, z3, s0t, lb, norm_g],
        in_specs=[blk(COL_HQ), blk(COL_HF), blk(COL_HI), blk(COL_HG), per_b,
                  _const_spec((1, QUAD)), _const_spec((1, QUAD))],
        out_shape=[jax.ShapeDtypeStruct((B, S, QUAD), BF16),
                   jax.ShapeDtypeStruct((B, N_HGRN_HEADS, HEAD_DIM, HEAD_DIM), F32)],
        out_specs=[pl.BlockSpec((None, ts, QUAD), lambda b, t: (b, t, 0)), per_b],
        scratch=[pltpu.VMEM((QUAD, QUAD), F32), stage(), stage(), stage()])


PROJ_SPLITS = (ATTN_WIDTH, ATTN_WIDTH, ATTN_WIDTH,
               MLSTM_WIDTH, MLSTM_WIDTH, MLSTM_WIDTH, MLSTM_WIDTH, N_MLSTM_HEADS, N_MLSTM_HEADS,
               HGRN_WIDTH, HGRN_WIDTH, HGRN_WIDTH, HGRN_WIDTH)


def _prep_w_in(w):
    cuts = [int(c) for c in np.cumsum(PROJ_SPLITS)[:-1]]
    aq, ak, av, mq, mk, mv, mo, mi, mf, hq, hf, hi, hg = jnp.split(w, cuts, axis=1)
    gate = jnp.concatenate([mi, mf, jnp.zeros((w.shape[0], LANES - 2 * N_MLSTM_HEADS), w.dtype)], axis=1)
    return jnp.concatenate([aq, ak, av, mq, mk, mv, mo, hq, hf, hi, hg, gate], axis=1).astype(BF16)


HGRN_SUBCHUNK = 16


class _LayerWeights(NamedTuple):
    g_mix: jax.Array
    w_in: jax.Array
    gate_b: jax.Array
    conv_w: jax.Array
    mlstm_g: jax.Array
    lb: jax.Array
    hgrn_g: jax.Array
    w_out: jax.Array
    g_ffn: jax.Array
    w_up: jax.Array
    w_down: jax.Array
    g_final: jax.Array


def _gate_row(gate_b):
    return jnp.repeat(gate_b.astype(F32), HEAD_DIM, axis=1)


def _trunk_layer(x2d, B, S, n_valid, attn_fn, states, w, final, chunk_m, chunk_h, n_keep=None, stacked=None):
    conv_buf, C0, n0, m0, S0 = states
    if n_keep is None:
        z = _inproj(x2d, w.g_mix, w.w_in)
    else:
        z, kt, vt = _inproj(x2d, w.g_mix, w.w_in, S, n_keep, stacked)
    z3 = z.reshape(B, S, ZW)
    conv0 = jnp.zeros((B, SUBLANES, 2 * MLSTM_WIDTH), F32).at[:, SUBLANES - (CONV_WIDTH - 1):].set(conv_buf.astype(F32))
    m0r = jnp.repeat(m0.astype(F32), HEAD_DIM, axis=1).reshape(B, 1, MLSTM_WIDTH)
    mlstm = _mlstm(z3, conv0, C0.astype(F32), n0.astype(F32).reshape(B, 1, MLSTM_WIDTH), m0r,
                   w.gate_b, w.conv_w, w.mlstm_g, n_valid, chunk_m)
    hgrn = _hgrn(z3, jnp.swapaxes(S0.astype(F32), -1, -2), w.lb, w.hgrn_g, n_valid, chunk_h)
    attn = attn_fn(z3)
    sem = ("parallel", "arbitrary")
    if isinstance(attn, _CallSpec):
        (attn,), (ml, c_new, n, m), (hg, st) = _run([attn, mlstm, hgrn], sem, "sample_mixers")
    else:
        (ml, c_new, n, m), = _run([mlstm], sem, "mlstm")
        (hg, st), = _run([hgrn], sem, "hgrn")
    n2 = B * S
    y = _outffn(x2d, attn.reshape(n2, ATTN_WIDTH), ml.reshape(n2, MLSTM_WIDTH), hg.reshape(n2, HGRN_WIDTH),
                w.w_out, w.g_ffn, w.w_up, w.w_down, w.g_final, final)
    if n_keep is None:
        k_rows = z3[:, :n_valid, COL_AK:COL_AK + ATTN_WIDTH].reshape(B, n_valid, N_ATTN_HEADS, HEAD_DIM)
        v_rows = z3[:, :n_valid, COL_AV:COL_AV + ATTN_WIDTH].reshape(B, n_valid, N_ATTN_HEADS, HEAD_DIM)
    else:
        k_rows, v_rows = kt, vt
    conv_new = z3[:, n_valid - (CONV_WIDTH - 1):n_valid, COL_MQK:COL_MQK + 2 * MLSTM_WIDTH]
    s_new = jnp.swapaxes(st, -1, -2)
    return y, (k_rows, v_rows, conv_new, c_new, n.reshape(B, N_MLSTM_HEADS, HEAD_DIM), m[:, 0, ::HEAD_DIM], s_new)


def kernel(x_prompt, x_sample, cache_attn_k, cache_attn_v, state_mlstm_conv, state_mlstm_C, state_mlstm_n, state_mlstm_m, state_hgrn_S, rel_bias, g_mix, w_in, mlstm_gate_b, mlstm_conv_w, mlstm_norm_g, hgrn_lb_raw, hgrn_norm_g, w_out, g_ffn, w_up, w_down, g_final):
    depth = w_in.shape[0]
    bp, sp, _ = x_prompt.shape
    bs, ts, _ = x_sample.shape
    n_keep = min(max(w for w, _ in DILATED_PATTERNS), sp)
    n_past = cache_attn_k.shape[2]
    lb_p = jax.nn.softmax(hgrn_lb_raw.astype(F32), axis=0)
    hgrn_lb = jnp.cumsum(lb_p, axis=0) - lb_p[0]
    bias_dist = _bias_by_distance(rel_bias)
    p_tables = _prompt_tables(bias_dist)
    s_tables = _sample_tables(bias_dist, n_past, ts)
    cache_kt = jnp.transpose(cache_attn_k, (0, 1, 3, 4, 2)).reshape(depth, bs, ATTN_WIDTH, n_past)
    cache_vt = jnp.transpose(cache_attn_v, (0, 1, 3, 4, 2)).reshape(depth, bs, ATTN_WIDTH, n_past)

    hp = x_prompt.reshape(bp * sp, D_MODEL)
    hs = jnp.zeros((bs, TPAD, D_MODEL), F32).at[:, :ts].set(x_sample).reshape(bs * TPAD, D_MODEL)
    zero_states = (jnp.zeros((bp, CONV_WIDTH - 1, 2 * MLSTM_WIDTH), F32),
                   jnp.zeros((bp, N_MLSTM_HEADS, HEAD_DIM, HEAD_DIM), F32),
                   jnp.zeros((bp, N_MLSTM_HEADS, HEAD_DIM), F32),
                   jnp.zeros((bp, N_MLSTM_HEADS), F32),
                   jnp.zeros((bp, N_HGRN_HEADS, HEAD_DIM, HEAD_DIM), F32))
    p_states, s_states = [], []
    kv_t = None
    for l in range(depth):
        final = l == depth - 1
        w_in_l = _prep_w_in(w_in[l])
        weights = _LayerWeights(
            g_mix=g_mix[l][None], w_in=w_in_l,
            gate_b=_gate_row(mlstm_gate_b[l]),
            conv_w=mlstm_conv_w[l], mlstm_g=mlstm_norm_g[l][None], lb=hgrn_lb[l][None], hgrn_g=hgrn_norm_g[l][None],
            w_out=w_out[l].astype(BF16), g_ffn=g_ffn[l][None], w_up=w_up[l].astype(BF16),
            w_down=w_down[l].astype(BF16), g_final=g_final[None])
        hp, st = _trunk_layer(hp, bp, sp, sp, functools.partial(_attn_prompt, bias=p_tables),
                              zero_states, weights, final, MLSTM_CHUNK, HGRN_SUBCHUNK, n_keep, (l, depth, kv_t))
        kv_t = st[:2]
        p_states.append(st[2:])
        sample_attn = functools.partial(_attn_sample, cache_kt=cache_kt, cache_vt=cache_vt, layer=l, tables=s_tables)
        states = (state_mlstm_conv[l], state_mlstm_C[l], state_mlstm_n[l], state_mlstm_m[l], state_hgrn_S[l])
        hs, st = _trunk_layer(hs, bs, TPAD, ts, sample_attn, states, weights, final, TPAD, TPAD)
        s_states.append(st)
    y_prompt = hp.reshape(bp, sp, D_MODEL)
    y_sample = hs.reshape(bs, TPAD, D_MODEL)[:, :ts]
    to_rows = lambda t: jnp.transpose(t.reshape(depth, bp, N_ATTN_HEADS, HEAD_DIM, n_keep), (0, 1, 4, 2, 3))
    p_out = [to_rows(kv_t[0]), to_rows(kv_t[1])] + [jnp.stack(a) for a in zip(*p_states)]
    s_out = [jnp.stack(a) for a in zip(*s_states)]
    return (y_prompt, y_sample, *p_out, *s_out)
```

```python
import functools
from typing import Callable, NamedTuple

import jax
import jax.numpy as jnp
import numpy as np
from jax import lax
from jax.experimental import pallas as pl
from jax.experimental.pallas import tpu as pltpu

F32 = jnp.float32
BF16 = jnp.bfloat16

D_MODEL = 1024
HEAD_DIM = 64
N_ATTN_HEADS = 8
N_MLSTM_HEADS = 4
N_HGRN_HEADS = 4
ATTN_WIDTH = N_ATTN_HEADS * HEAD_DIM
MLSTM_WIDTH = N_MLSTM_HEADS * HEAD_DIM
HGRN_WIDTH = N_HGRN_HEADS * HEAD_DIM
DILATED_PATTERNS = ((128, 1), (512, 4), (2048, 16))
QBLOCK = 128
N_REL_BUCKETS = 32
REL_MAX_DISTANCE = 2048
CONV_WIDTH = 4
MLSTM_CHUNK = 64
D_FF = 4 * D_MODEL
EPS = 1e-6
NEG_BIG = -1e30
LOG2E = 1.4426950408889634

LANES = 128
SUBLANES = 8

COL_AQ = 0
COL_AK = COL_AQ + ATTN_WIDTH
COL_AV = COL_AK + ATTN_WIDTH
COL_MQK = COL_AV + ATTN_WIDTH
COL_MV = COL_MQK + 2 * MLSTM_WIDTH
COL_MO = COL_MV + MLSTM_WIDTH
COL_HQ = COL_MO + MLSTM_WIDTH
COL_HF = COL_HQ + HGRN_WIDTH
COL_HI = COL_HF + HGRN_WIDTH
COL_HG = COL_HI + HGRN_WIDTH
COL_GATE = COL_HG + HGRN_WIDTH
ZW = COL_GATE + LANES

VMEM_LIMIT = 56 * 1024 * 1024


def _cparams(sem, vmem=VMEM_LIMIT):
    return pltpu.CompilerParams(dimension_semantics=sem, vmem_limit_bytes=vmem)


def _const_spec(shape):
    nd = len(shape)
    return pl.BlockSpec(shape, lambda *_: (0,) * nd, pipeline_mode=pl.Buffered(1))


class _CallSpec(NamedTuple):
    body: Callable
    grid: tuple
    args: list
    in_specs: list
    out_shape: list
    out_specs: list
    scratch: list


def _run(specs, sem, name):
    grid = specs[0].grid
    assert all(s.grid == grid for s in specs)
    n_in = [len(s.args) for s in specs]
    n_out = [len(s.out_shape) for s in specs]
    n_scr = [len(s.scratch) for s in specs]

    def body(*refs):
        ins, outs, scrs = refs[:sum(n_in)], refs[sum(n_in):sum(n_in) + sum(n_out)], refs[sum(n_in) + sum(n_out):]
        for k, s in enumerate(specs):
            a, b, c = sum(n_in[:k]), sum(n_out[:k]), sum(n_scr[:k])
            s.body(*ins[a:a + n_in[k]], *outs[b:b + n_out[k]], *scrs[c:c + n_scr[k]])

    flat = pl.pallas_call(
        body, grid=grid,
        in_specs=[x for s in specs for x in s.in_specs],
        out_shape=[x for s in specs for x in s.out_shape],
        out_specs=[x for s in specs for x in s.out_specs],
        scratch_shapes=[x for s in specs for x in s.scratch],
        compiler_params=_cparams(sem), name=name,
    )(*[x for s in specs for x in s.args])
    out, k = [], 0
    for n in n_out:
        out.append(list(flat[k:k + n]))
        k += n
    return out


def _inproj_kernel(x_ref, g_ref, w_ref, *rest, tiles_per_seq, first_keep, n_prev=0):
    x = x_ref[...]
    ms = jnp.mean(x * x, axis=-1, keepdims=True)
    xn = ((x * lax.rsqrt(ms + EPS)) * g_ref[...]).astype(BF16)
    if first_keep is None:
        (z_ref,) = rest
        z_ref[...] = jnp.dot(xn, w_ref[...], preferred_element_type=F32)
        return
    z_ref, kt_ref, vt_ref = rest[n_prev:]
    z_ref[...] = jnp.dot(xn, w_ref[...], preferred_element_type=F32)

    @pl.when(pl.program_id(0) % tiles_per_seq >= first_keep)
    def _():
        kt_ref[...] = z_ref[:, COL_AK:COL_AK + ATTN_WIDTH].T
        vt_ref[...] = z_ref[:, COL_AV:COL_AV + ATTN_WIDTH].T


def _inproj(x2d, g, w, seq=None, n_keep=None, stacked=None):
    n = x2d.shape[0]
    tm = min(512, n)
    assert n % tm == 0
    x_spec = pl.BlockSpec((tm, D_MODEL), lambda i: (i, 0))
    z_spec = pl.BlockSpec((tm, ZW), lambda i: (i, 0))
    z_shape = jax.ShapeDtypeStruct((n, ZW), F32)
    if stacked is None:
        return pl.pallas_call(
            functools.partial(_inproj_kernel, tiles_per_seq=None, first_keep=None),
            out_shape=z_shape, grid=(n // tm,),
            in_specs=[x_spec, _const_spec((1, D_MODEL)), _const_spec((D_MODEL, ZW))],
            out_specs=z_spec, compiler_params=_cparams(("parallel",)), name="inproj",
        )(x2d, g, w)
    assert seq % tm == 0 and n_keep % tm == 0
    layer, depth, prev = stacked
    tps, first_keep = seq // tm, (seq - n_keep) // tm
    t_spec = pl.BlockSpec((None, None, ATTN_WIDTH, tm),
                          lambda i: (layer, i // tps, 0, jnp.maximum(i % tps - first_keep, 0)))
    t_shape = jax.ShapeDtypeStruct((depth, n // seq, ATTN_WIDTH, n_keep), F32)
    in_specs = [x_spec, _const_spec((1, D_MODEL)), _const_spec((D_MODEL, ZW))]
    args = [x2d, g, w]
    if prev is None:
        prev = (jnp.zeros(t_shape.shape, F32), jnp.zeros(t_shape.shape, F32))
    in_specs += [pl.BlockSpec(memory_space=pl.ANY)] * 2
    aliases = {len(args): 1, len(args) + 1: 2}
    args += list(prev)
    return pl.pallas_call(
        functools.partial(_inproj_kernel, tiles_per_seq=tps, first_keep=first_keep, n_prev=2),
        out_shape=[z_shape, t_shape, t_shape], grid=(n // tm,),
        in_specs=in_specs, out_specs=[z_spec, t_spec, t_spec], input_output_aliases=aliases,
        compiler_params=_cparams(("arbitrary",)), name="inproj_kt",
    )(*args)


FF_CHUNK = 1024
FFN_ROWS = 512


def _outffn_kernel(x_ref, a_ref, m_ref, h_ref, wo_ref, gf_ref, wu_ref, wd_ref, gl_ref, y_ref, xn_sc, *, final):
    x1 = x_ref[...]
    x1 = x1 + jnp.dot(a_ref[...], wo_ref[0:ATTN_WIDTH, :], preferred_element_type=F32)
    x1 = x1 + jnp.dot(m_ref[...], wo_ref[ATTN_WIDTH:ATTN_WIDTH + MLSTM_WIDTH, :], preferred_element_type=F32)
    x1 = x1 + jnp.dot(h_ref[...], wo_ref[ATTN_WIDTH + MLSTM_WIDTH:, :], preferred_element_type=F32)
    ms = jnp.mean(x1 * x1, axis=-1, keepdims=True)
    xn_sc[...] = ((x1 * lax.rsqrt(ms + EPS)) * gf_ref[...]).astype(BF16)
    y_ref[...] = x1
    nchunk = D_FF // FF_CHUNK
    up = lambda c: jnp.dot(xn_sc[...], wu_ref[:, c * FF_CHUNK:(c + 1) * FF_CHUNK], preferred_element_type=F32)
    u = up(0)
    for c in range(nchunk):
        u_next = up(c + 1) if c + 1 < nchunk else None
        hh = jnp.square(jnp.maximum(u, 0.0)).astype(BF16)
        y_ref[...] += jnp.dot(hh, wd_ref[c * FF_CHUNK:(c + 1) * FF_CHUNK, :], preferred_element_type=F32)
        u = u_next
    if final:
        x2 = y_ref[...]
        ms2 = jnp.mean(x2 * x2, axis=-1, keepdims=True)
        y_ref[...] = (x2 * lax.rsqrt(ms2 + EPS)) * gl_ref[...]


def _outffn(x2d, attn, ml, hg, w_out, g_ffn, w_up, w_down, g_final, final):
    n = x2d.shape[0]
    tm = min(FFN_ROWS, n)
    assert n % tm == 0
    row = lambda w: pl.BlockSpec((tm, w), lambda i: (i, 0))
    return pl.pallas_call(
        functools.partial(_outffn_kernel, final=final),
        out_shape=jax.ShapeDtypeStruct((n, D_MODEL), F32),
        grid=(n // tm,),
        in_specs=[row(D_MODEL), row(ATTN_WIDTH), row(MLSTM_WIDTH), row(HGRN_WIDTH),
                  _const_spec((D_MODEL, D_MODEL)), _const_spec((1, D_MODEL)),
                  _const_spec((D_MODEL, D_FF)), _const_spec((D_FF, D_MODEL)), _const_spec((1, D_MODEL))],
        out_specs=row(D_MODEL),
        scratch_shapes=[pltpu.VMEM((tm, D_MODEL), BF16)],
        compiler_params=_cparams(("parallel",)),
        name="outffn",
    )(x2d, attn, ml, hg, w_out, g_ffn, w_up, w_down, g_final)


def _t5_causal_bucket(dist):
    n = np.asarray(dist).astype(np.int32)
    max_exact = N_REL_BUCKETS // 2
    scaled = np.log(np.maximum(n, 1) / max_exact) / np.log(REL_MAX_DISTANCE / max_exact)
    large = np.minimum(max_exact + (scaled * (N_REL_BUCKETS - max_exact)).astype(np.int32), N_REL_BUCKETS - 1)
    return np.where(n < max_exact, n, large).astype(np.int32)


BIAS_DIST = 2304


def _bias_by_distance(rel_bias):
    assert BIAS_DIST > max(w for w, _ in DILATED_PATTERNS) + SUBLANES
    return rel_bias.astype(F32)[_t5_causal_bucket(np.arange(BIAS_DIST)[::-1])].T


def _dist_slice(bias_desc, lo, hi, step=1):
    last = BIAS_DIST - 1
    return bias_desc[:, last - hi:last - lo + 1:step]


def _prompt_tables(bias_dist):
    H = N_ATTN_HEADS
    a = np.arange(QBLOCK)[:, None]
    b = np.arange(2 * QBLOCK)[None, :]
    rel = QBLOCK + a - b
    band = (rel >= 0) & (rel <= QBLOCK)
    period = 3 * QBLOCK
    biases = []
    for window, dil in DILATED_PATTERNS:
        nsub = window // dil
        assert nsub == QBLOCK
        vd = _dist_slice(bias_dist, 0, nsub * dil, dil)
        rp = jnp.concatenate([vd, jnp.broadcast_to(vd[:, -1:], (H, QBLOCK)),
                              jnp.broadcast_to(vd[:, 0:1], (H, QBLOCK - 1))], axis=1)
        skew = jnp.tile(rp, (1, QBLOCK))[:, :QBLOCK * (period - 1)].reshape(H, QBLOCK, period - 1)
        biases.append(skew[:, :, :2 * QBLOCK].reshape(H // 2, 2 * QBLOCK, 2 * QBLOCK))
    mask = np.stack([np.tile(band, (2, 1)), np.tile(band & (b >= QBLOCK), (2, 1))])
    table = jnp.where(mask[None, None], jnp.stack(biases)[:, :, None] * LOG2E, -jnp.inf)
    return jnp.swapaxes(table, -1, -2)


ATTN_GROUP = 8


def _attn_prompt_kernel(q_ref, k_ref, v_ref, bias_ref, out_ref,
                        x4, qs, ks, vts, o1, o4, o16, l1, l4, l16, p_scr, st_scr, *, seq):
    S = seq
    nblk = S // QBLOCK
    scale = HEAD_DIM ** -0.5 * LOG2E
    zpad = jnp.zeros((QBLOCK, LANES), BF16)
    for di in range(3):
        ks[di, 0:QBLOCK, :] = zpad
        vts[di, 0] = zpad

    R = 512

    def put(src, di, row0, blk):
        if src is q_ref:
            qs[di, row0:row0 + blk.shape[0], :] = (blk * scale).astype(BF16)
        elif src is k_ref:
            ks[di, QBLOCK + row0:QBLOCK + row0 + blk.shape[0], :] = blk.astype(BF16)
        else:
            blk_t = blk.T.astype(BF16)
            for t in range(blk.shape[0] // QBLOCK):
                vts[di, 1 + row0 // QBLOCK + t] = blk_t[:, t * QBLOCK:(t + 1) * QBLOCK]

    for src in (q_ref, k_ref, v_ref):
        for i in range(S // R):
            put(src, 0, i * R, src[i * R:(i + 1) * R, :])
        for r in range(4):
            for i in range(S // 4 // R):
                blk = src[pl.ds(r + 4 * i * R, R, stride=4), :]
                row0 = r * (S // 4) + i * R
                x4[row0:row0 + R, :] = blk
                put(src, 1, row0, blk)
        n16 = S // 16
        for c4 in range(4):
            for a in range(4):
                put(src, 2, (c4 * 4 + a) * n16, x4[pl.ds(c4 * (S // 4) + a, n16, stride=4), :])

    lane = lax.broadcasted_iota(jnp.int32, (QBLOCK, LANES), 1)
    head_a = lane < HEAD_DIM
    row_a = lax.broadcasted_iota(jnp.int32, (QBLOCK, LANES), 0) < HEAD_DIM
    nt = (((1,), (1,)), ((), ()))

    def branch(di, dil, o_ref, l_ref):
        per_class = (S // dil) // QBLOCK

        R = range(ATTN_GROUP)
        zero = jnp.zeros((QBLOCK, LANES), BF16)

        def scores(g, js):
            out = {}
            for j in js:
                i = g * ATTN_GROUP + j
                base = pl.multiple_of(i * QBLOCK, QBLOCK)
                q = qs[di, pl.ds(base, QBLOCK), :]
                kk = ks[di, pl.ds(base, 2 * QBLOCK), :]
                q2 = jnp.concatenate([jnp.where(head_a, q, zero), jnp.where(head_a, zero, q)], axis=0)
                out[j] = lax.dot_general(kk, q2, nt, preferred_element_type=F32)
            return out

        def softmax(g, slot, st):
            for j in st:
                i = g * ATTN_GROUP + j
                s = st[j] + bias_ref[di, jnp.where(i % per_class == 0, 1, 0)]
                m = jnp.max(s, axis=0, keepdims=True)
                p = jnp.exp2(s - m)
                p_scr[slot, j] = p.astype(BF16)
                st_scr[slot, j, 0:1, :] = m
                st_scr[slot, j, 1:2, :] = jnp.sum(p, axis=0, keepdims=True)

        def weighted_values(g, slot, js):
            out = {}
            for j in js:
                i = g * ATTN_GROUP + j
                vt = jnp.concatenate([vts[di, i], vts[di, i + 1]], axis=1)
                ot = jnp.dot(vt, p_scr[slot, j], preferred_element_type=F32)
                out[j] = (ot, st_scr[slot, j, 0:1, :], st_scr[slot, j, 1:2, :])
            return out

        def outputs(g, pv):
            for j in pv:
                i = g * ATTN_GROUP + j
                base = pl.multiple_of(i * QBLOCK, QBLOCK)
                ot, m, l = pv[j]
                otn = ot / l
                lse = m + jnp.log2(l)
                o = jnp.where(row_a, otn[:, 0:QBLOCK], otn[:, QBLOCK:]).T
                ls = jnp.where(row_a, jnp.broadcast_to(lse[:, 0:QBLOCK], (QBLOCK, LANES)),
                               jnp.broadcast_to(lse[:, QBLOCK:], (QBLOCK, LANES))).T
                if dil == 1:
                    o_ref[pl.ds(base, QBLOCK), :] = o
                    l_ref[pl.ds(base, QBLOCK), :] = ls
                else:
                    c, n = i // per_class, i % per_class
                    res = c if dil == 4 else 4 * (c % 4) + c // 4
                    start = dil * QBLOCK * n + res
                    o_ref[pl.ds(start, QBLOCK, stride=dil), :] = o
                    l_ref[pl.ds(start, QBLOCK, stride=dil), :] = ls

        ngroup = nblk // ATTN_GROUP
        softmax(0, 0, scores(0, R))

        def body(g, carry):
            slot = g % 2
            st = scores(g, R)
            pv = weighted_values(g - 1, 1 - slot, R)
            softmax(g, slot, st)
            outputs(g - 1, pv)
            return carry

        lax.fori_loop(1, ngroup, body, 0)
        outputs(ngroup - 1, weighted_values(ngroup - 1, (ngroup - 1) % 2, R))

    branch(0, 1, o1, l1)
    branch(1, 4, o4, l4)
    branch(2, 16, o16, l16)

    T = 256

    def mix(i, carry):
        r0 = pl.multiple_of(i * T, T)
        la, lb, lc = l1[pl.ds(r0, T), :], l4[pl.ds(r0, T), :], l16[pl.ds(r0, T), :]
        mx = jnp.maximum(jnp.maximum(la, lb), lc)
        ea, eb, ec = jnp.exp2(la - mx), jnp.exp2(lb - mx), jnp.exp2(lc - mx)
        num = ea * o1[pl.ds(r0, T), :] + eb * o4[pl.ds(r0, T), :] + ec * o16[pl.ds(r0, T), :]
        out_ref[pl.ds(r0, T), :] = (num / (ea + eb + ec)).astype(out_ref.dtype)
        return carry

    lax.fori_loop(0, S // T, mix, 0)


def _attn_prompt(z3, bias):
    B, S, _ = z3.shape
    assert S % (16 * QBLOCK) == 0
    npair = N_ATTN_HEADS // 2
    col = lambda c0: pl.BlockSpec((None, S, LANES), lambda b, p: (b, 0, c0 // LANES + p))
    f32s = lambda: pltpu.VMEM((S, LANES), F32)
    return pl.pallas_call(
        functools.partial(_attn_prompt_kernel, seq=S),
        out_shape=jax.ShapeDtypeStruct((B, S, ATTN_WIDTH), BF16),
        grid=(B, npair),
        in_specs=[col(COL_AQ), col(COL_AK), col(COL_AV),
                  pl.BlockSpec((3, None, 2, 2 * QBLOCK, 2 * QBLOCK), lambda b, p: (0, p, 0, 0, 0))],
        out_specs=pl.BlockSpec((None, S, LANES), lambda b, p: (b, 0, p)),
        scratch_shapes=[f32s(),
                        pltpu.VMEM((3, S, LANES), BF16),
                        pltpu.VMEM((3, S + QBLOCK, LANES), BF16),
                        pltpu.VMEM((3, S // QBLOCK + 1, LANES, QBLOCK), BF16),
                        f32s(), f32s(), f32s(), f32s(), f32s(), f32s(),
                        pltpu.VMEM((2, ATTN_GROUP, 2 * QBLOCK, 2 * QBLOCK), BF16),
                        pltpu.VMEM((2, ATTN_GROUP, SUBLANES, 2 * QBLOCK), F32)],
        compiler_params=_cparams(("parallel", "parallel")),
        name="attn_prompt",
    )(z3, z3, z3, bias)


TPAD = SUBLANES
TAIL = 512


def _sample_tables(bias_dist, n_past, n_tok):
    assert n_past >= max(w for w, _ in DILATED_PATTERNS) and n_tok <= TPAD and TAIL >= DILATED_PATTERNS[1][0]
    H = N_ATTN_HEADS
    t = np.arange(TPAD)[:, None]

    def by_row(width):
        rows = [_dist_slice(bias_dist, tt + 1, tt + width) for tt in range(TPAD)]
        return jnp.stack(rows, axis=1).reshape(H * TPAD, width)

    bias_tail, bias_full = by_row(TAIL), by_row(n_past)
    zero_dist = _dist_slice(bias_dist, 0, 0)
    new_rows = [jnp.concatenate([_dist_slice(bias_dist, 0, tt),
                                 jnp.broadcast_to(zero_dist, (H, TPAD - tt - 1))], axis=1) for tt in range(TPAD)]
    bias_new = jnp.stack(new_rows, axis=1).reshape(H * TPAD, TPAD)

    def valid(dj, dil, nsub, lo):
        ok = (dj % dil == 0) & (dj // dil >= lo) & (dj // dil <= nsub) & (t < n_tok)
        return np.tile(ok, (H, 1)).astype(np.float32)

    tn = np.arange(TPAD)[None, :]
    m_tail, m_new = [], []
    for window, dil in DILATED_PATTERNS:
        nsub = window // dil
        if window <= TAIL:
            m_tail.append(valid(TAIL + t - np.arange(TAIL)[None, :], dil, nsub, 1))
        else:
            m_full = valid(n_past + t - np.arange(n_past)[None, :], dil, nsub, 1)
        m_new.append(valid(t - tn, dil, nsub, 0) * (tn < n_tok))
    return (bias_tail, bias_full, bias_new, jnp.asarray(np.stack(m_tail)), jnp.asarray(m_full),
            jnp.asarray(np.stack(m_new).astype(np.float32)))


def _attn_sample_kernel(q_ref, kn_ref, vn_ref, kt_ref, vt_ref, bt_ref, bf_ref, bn_ref, mt_ref, mf_ref, mn_ref,
                        out_ref, *, n_past):
    H = N_ATTN_HEADS
    rows = H * TPAD
    same_head = (_iota2((rows, ATTN_WIDTH), 0) >> 3) == (_iota2((rows, ATTN_WIDTH), 1) >> HEAD_SHIFT)
    q = q_ref[...] * (HEAD_DIM ** -0.5)
    qm = jnp.where(same_head, jnp.concatenate([q] * H, axis=0), 0.0).astype(BF16)
    nt = (((1,), (1,)), ((), ()))
    kn = kn_ref[...].astype(BF16)
    vn = vn_ref[...].astype(BF16)
    kt = kt_ref[...].astype(BF16)
    vt = vt_ref[...].astype(BF16)
    s_new = lax.dot_general(qm, kn, nt, preferred_element_type=F32)
    s_full = jnp.dot(qm, kt, preferred_element_type=F32)
    s_tail = s_full[:, n_past - TAIL:]

    outs, lses = [], []
    for di, (window, _) in enumerate(DILATED_PATTERNS):
        if window <= TAIL:
            sm = jnp.where(mt_ref[di] > 0.5, s_tail + bt_ref[...], NEG_BIG)
            vmain = vt[:, n_past - TAIL:]
        else:
            sm = jnp.where(mf_ref[...] > 0.5, s_full + bf_ref[...], NEG_BIG)
            vmain = vt
        sn = jnp.where(mn_ref[di] > 0.5, s_new + bn_ref[...], NEG_BIG)
        m = jnp.maximum(jnp.max(sm, axis=-1, keepdims=True), jnp.max(sn, axis=-1, keepdims=True))
        pm = jnp.exp(sm - m)
        pn = jnp.exp(sn - m)
        l = jnp.sum(pm, axis=-1, keepdims=True) + jnp.sum(pn, axis=-1, keepdims=True)
        o = jnp.dot(pn.astype(BF16), vn, preferred_element_type=F32)
        o = o + lax.dot_general(pm.astype(BF16), vmain, nt, preferred_element_type=F32)
        outs.append(o / l)
        lses.append(m + jnp.log(l))
    mx = jnp.maximum(jnp.maximum(lses[0], lses[1]), lses[2])
    es = [jnp.exp(ls - mx) for ls in lses]
    mixed = (es[0] * outs[0] + es[1] * outs[1] + es[2] * outs[2]) / (es[0] + es[1] + es[2])
    mixed = jnp.where(same_head, mixed, 0.0)
    acc = mixed[0:TPAD]
    for h in range(1, H):
        acc = acc + mixed[h * TPAD:(h + 1) * TPAD]
    out_ref[...] = acc.astype(out_ref.dtype)


def _attn_sample(z3, cache_kt, cache_vt, layer, tables):
    B, T, _ = z3.shape
    assert T == TPAD
    n_past = cache_kt.shape[-1]
    new = lambda c0: pl.BlockSpec((None, TPAD, ATTN_WIDTH), lambda b, t: (b, 0, c0 // ATTN_WIDTH))
    buf = pl.BlockSpec((None, None, ATTN_WIDTH, n_past), lambda b, t: (layer, b, 0, 0))
    return _CallSpec(
        body=functools.partial(_attn_sample_kernel, n_past=n_past),
        grid=(B, 1),
        args=[z3, z3, z3, cache_kt, cache_vt, *tables],
        in_specs=[new(COL_AQ), new(COL_AK), new(COL_AV), buf, buf] + [_const_spec(t.shape) for t in tables],
        out_shape=[jax.ShapeDtypeStruct((B, TPAD, ATTN_WIDTH), BF16)],
        out_specs=[pl.BlockSpec((None, TPAD, ATTN_WIDTH), lambda b, t: (b, 0, 0))],
        scratch=[])


QUAD = N_MLSTM_HEADS * HEAD_DIM
CHUNK_UNROLL = 2
CHUNK_GROUP = 4
HEAD_SHIFT = 6


def _iota2(shape, axis):
    return lax.broadcasted_iota(jnp.int32, shape, axis)


def _log2(n):
    k = int(n).bit_length() - 1
    assert 1 << k == n
    return k


def _seg_mask(rows, row_shift, cols, col_shift):
    return (_iota2((rows, cols), 0) >> row_shift) == (_iota2((rows, cols), 1) >> col_shift)


def _cumulate_rows(x, length, op, fill):
    row = _iota2(x.shape, 0) & (length - 1)
    sh = 1
    while sh < length:
        x = op(x, jnp.where(row >= sh, pltpu.roll(x, sh, axis=0), fill))
        sh *= 2
    return x


def _split2(x):
    hi = x.astype(BF16)
    lo = (x - hi.astype(F32)).astype(BF16)
    return hi, lo


def _dot2(x, w):
    hi, lo = _split2(x)
    return jnp.dot(hi, w, preferred_element_type=F32) + jnp.dot(lo, w, preferred_element_type=F32)


def _dot2r(w, x):
    hi, lo = _split2(x)
    return jnp.dot(w, hi, preferred_element_type=F32) + jnp.dot(w, lo, preferred_element_type=F32)


def _load_block_diag(ref, seg64):
    flat = ref[...].reshape(QUAD, HEAD_DIM)
    return jnp.where(seg64, jnp.concatenate([flat] * (QUAD // HEAD_DIM), axis=1), 0.0)


def _store_block_diag(ref, mat):
    for h in range(QUAD // HEAD_DIM):
        ref[h] = mat[h * HEAD_DIM:(h + 1) * HEAD_DIM, h * HEAD_DIM:(h + 1) * HEAD_DIM]


def _head_rmsnorm_gate(h, seg64b, g_row, gate_pre):
    ms = _dot2(h * h, seg64b) * (1.0 / HEAD_DIM)
    return jax.nn.sigmoid(gate_pre) * (h * lax.rsqrt(ms + EPS) * g_row)


def _mlstm_kernel(qk_ref, v_ref, o_ref, gate_ref, conv0_ref, c0_ref, n0_ref, m0_ref, gb_ref, cw_ref, ng_ref,
                  out_ref, c_out, n_out, m_out, xp, cs, ns, ms, *, tile, chunk, n_valid):
    TS, L = tile, chunk
    t = pl.program_id(1)
    PAD = SUBLANES

    @pl.when(t == 0)
    def _():
        cs[...] = _load_block_diag(c0_ref, _seg_mask(QUAD, HEAD_SHIFT, QUAD, HEAD_SHIFT))
        ns[...] = n0_ref[...]
        ms[...] = m0_ref[...]
        xp[0:PAD, :] = conv0_ref[...]

    @pl.when(t > 0)
    def _():
        xp[0:PAD, :] = xp[TS:TS + PAD, :]

    xp[PAD:PAD + TS, :] = qk_ref[...]

    LK = MLSTM_CHUNK
    seg64 = _seg_mask(QUAD, HEAD_SHIFT, QUAD, HEAD_SHIFT)
    seg64b = seg64.astype(BF16)
    dmask = (_iota2((LK, QUAD), 1) & (LK - 1)) == _iota2((LK, QUAD), 0)
    causal = (_iota2((L, QUAD), 1) & (LK - 1)) <= _iota2((L, QUAD), 0)
    tril = (_iota2((L, L), 1) <= _iota2((L, L), 0)).astype(BF16)
    ones_lk = jnp.ones((L, LK), BF16)
    row = _iota2((L, QUAD), 0)
    cw = cw_ref[...]
    gb = gb_ref[...]
    ng = ng_ref[...]

    first_half = _iota2((L, LANES), 1) < HEAD_DIM

    def per_head_lanes(g, lane0):
        col = [jnp.broadcast_to(g[:, lane0 + h:lane0 + h + 1], (L, LANES)) for h in range(N_MLSTM_HEADS)]
        return jnp.concatenate([jnp.where(first_half, col[0], col[1]), jnp.where(first_half, col[2], col[3])], axis=1)

    def key_rows(x, fill):
        if L == LK:
            return x
        return jnp.concatenate([x, jnp.full((LK - L, QUAD), fill, x.dtype)], axis=0)

    G = min(CHUNK_GROUP, TS // L)
    assert TS % (L * G) == 0
    nt = (((1,), (1,)), ((), ()))
    tn = (((0,), (0,)), ((), ()))
    zb = jnp.zeros((QUAD, QUAD), BF16)

    def group_body(gi, carry):
        R = range(G)
        r0 = [pl.multiple_of((gi * G + j) * L, L) for j in R]
        q, k, v, ig, lf = [], [], [], [], []
        for j in R:
            win = xp[pl.ds(r0[j], L + PAD), :]
            acc = win[PAD:PAD + L] * cw[CONV_WIDTH - 1:CONV_WIDTH, :]
            for s in range(1, CONV_WIDTH):
                acc = acc + win[PAD - s:PAD - s + L] * cw[CONV_WIDTH - 1 - s:CONV_WIDTH - s, :]
            qk = acc * jax.nn.sigmoid(acc)
            q.append(qk[:, 0:QUAD])
            k.append(qk[:, QUAD:] * (HEAD_DIM ** -0.5))
            v.append(v_ref[pl.ds(r0[j], L), :])
            gate = gate_ref[pl.ds(r0[j], L), :]
            ig_j = per_head_lanes(gate, 0) + gb[0:1, :]
            g2 = per_head_lanes(gate, N_MLSTM_HEADS) + gb[1:2, :]
            lf_j = jnp.minimum(g2, 0.0) - jnp.log(1.0 + jnp.exp(-jnp.abs(g2)))
            if n_valid < TS:
                ok = (row + r0[j]) < n_valid
                ig_j = jnp.where(ok, ig_j, NEG_BIG)
                lf_j = jnp.where(ok, lf_j, 0.0)
            ig.append(ig_j)
            lf.append(lf_j)
        b = [_dot2r(tril, lf[j]) for j in R]
        a = [ig[j] - b[j] for j in R]
        cm = [_cumulate_rows(a[j], L, jnp.maximum, -jnp.inf) for j in R]
        arow = [_dot2r(ones_lk, jnp.where(dmask, key_rows(a[j], NEG_BIG), 0.0)) for j in R]
        qb = [q[j].astype(BF16) for j in R]
        vb = [v[j].astype(BF16) for j in R]
        kbd = [jnp.where(seg64, jnp.concatenate([key_rows(k[j], 0.0).astype(BF16)] * N_MLSTM_HEADS, axis=0), zb) for j in R]
        vbd = [jnp.where(seg64, jnp.concatenate([key_rows(v[j], 0.0).astype(BF16)] * N_MLSTM_HEADS, axis=0), zb) for j in R]
        qkt = [lax.dot_general(qb[j], kbd[j], nt, preferred_element_type=F32) for j in R]

        mprev = ms[...]
        M, gg, emt = [], [], []
        for j in R:
            M.append(jnp.maximum(cm[j], mprev))
            mt = b[j] + M[j]
            gg.append(jnp.exp(mprev - M[j]))
            emt.append(jnp.exp(-mt))
            mprev = mt[L - 1:L, :]
        ms[...] = mprev

        wts = [jnp.where(causal, jnp.exp(arow[j] - M[j]), 0.0) * qkt[j] for j in R]
        kd = [k[j] * jnp.exp(a[j] - M[j][L - 1:L, :]) for j in R]
        num = [jnp.dot(wts[j].astype(BF16), vbd[j], preferred_element_type=F32) for j in R]
        u = [lax.dot_general(kd[j].astype(BF16), vb[j], tn, preferred_element_type=F32) for j in R]

        cmat, nvec = [cs[...]], [ns[...]]
        for j in R:
            gl = gg[j][L - 1:L, :]
            cmat.append(gl * cmat[j] + jnp.where(seg64, u[j], 0.0))
            nvec.append(gl * nvec[j] + jnp.sum(kd[j], axis=0, keepdims=True))
        cs[...] = cmat[G]
        ns[...] = nvec[G]

        inter = [jnp.dot(qb[j], cmat[j].astype(BF16), preferred_element_type=F32) for j in R]
        den = [_dot2(wts[j] + gg[j] * (q[j] * nvec[j]), seg64b) for j in R]
        h = [(num[j] + gg[j] * inter[j]) / jnp.maximum(jnp.abs(den[j]), emt[j]) for j in R]
        msq = [_dot2(h[j] * h[j], seg64b) * (1.0 / HEAD_DIM) for j in R]
        for j in R:
            y = h[j] * lax.rsqrt(msq[j] + EPS) * ng
            out_ref[pl.ds(r0[j], L), :] = (jax.nn.sigmoid(o_ref[pl.ds(r0[j], L), :]) * y).astype(out_ref.dtype)
        return carry

    lax.fori_loop(0, TS // (L * G), group_body, 0)

    @pl.when(t == pl.num_programs(1) - 1)
    def _():
        _store_block_diag(c_out, cs[...])
        n_out[...] = ns[...]
        m_out[...] = ms[...]


def _mlstm(z3, conv0, c0bd, n0, m0, gate_b, conv_w, norm_g, n_valid, chunk):
    B, S, _ = z3.shape
    ts = min(512, S)
    assert S % ts == 0 and ts % chunk == 0
    blk = lambda w, c0: pl.BlockSpec((None, ts, w), lambda b, t: (b, t, c0 // w))
    per_b = lambda r, w: pl.BlockSpec((None, r, w), lambda b, t: (b, 0, 0))
    per_head = pl.BlockSpec((None, N_MLSTM_HEADS, HEAD_DIM, HEAD_DIM), lambda b, t: (b, 0, 0, 0))
    return _CallSpec(
        body=functools.partial(_mlstm_kernel, tile=ts, chunk=chunk, n_valid=n_valid),
        grid=(B, S // ts),
        args=[z3, z3, z3, z3, conv0, c0bd, n0, m0, gate_b, conv_w, norm_g],
        in_specs=[blk(2 * QUAD, COL_MQK), blk(QUAD, COL_MV), blk(QUAD, COL_MO), blk(LANES, COL_GATE),
                  per_b(SUBLANES, 2 * QUAD), per_head, per_b(1, QUAD), per_b(1, QUAD),
                  _const_spec((2, QUAD)), _const_spec((CONV_WIDTH, 2 * QUAD)), _const_spec((1, QUAD))],
        out_shape=[jax.ShapeDtypeStruct((B, S, QUAD), BF16),
                   jax.ShapeDtypeStruct((B, N_MLSTM_HEADS, HEAD_DIM, HEAD_DIM), F32),
                   jax.ShapeDtypeStruct((B, 1, QUAD), F32),
                   jax.ShapeDtypeStruct((B, 1, QUAD), F32)],
        out_specs=[pl.BlockSpec((None, ts, QUAD), lambda b, t: (b, t, 0)),
                   per_head, per_b(1, QUAD), per_b(1, QUAD)],
        scratch=[pltpu.VMEM((ts + 2 * SUBLANES, 2 * QUAD), F32),
                 pltpu.VMEM((QUAD, QUAD), F32), pltpu.VMEM((1, QUAD), F32), pltpu.VMEM((1, QUAD), F32)])


HGRN_FAST_CHUNK = 64
HGRN_GROUP = 8
HGRN_SAFE_DECAY = 80.0


def _hgrn_kernel(q_ref, f_ref, i_ref, g_ref, s0_ref, lb_ref, ng_ref, out_ref, s_out, st, qs, fs, ks, *,
                 tile, sub, n_valid, fast):
    TS = tile
    t = pl.program_id(1)

    seg64 = _seg_mask(QUAD, HEAD_SHIFT, QUAD, HEAD_SHIFT)
    seg64b = seg64.astype(BF16)

    @pl.when(t == 0)
    def _():
        st[...] = _load_block_diag(s0_ref, seg64)
    lb = lb_ref[...]
    ng = ng_ref[...]
    nt = (((1,), (1,)), ((), ()))
    tn = (((0,), (0,)), ((), ()))

    P = HGRN_FAST_CHUNK if fast else sub
    prow = _iota2((P, QUAD), 0)

    def prep(c, worst):
        r0 = pl.multiple_of(c * P, P)
        hq = q_ref[pl.ds(r0, P), :]
        f = lb + (1.0 - lb) * jax.nn.sigmoid(f_ref[pl.ds(r0, P), :])
        logf = jnp.log(f)
        kk = 1.0 - f
        if n_valid < TS:
            ok = (prow + r0) < n_valid
            logf = jnp.where(ok, logf, 0.0)
            kk = jnp.where(ok, kk, 0.0)
        qs[pl.ds(r0, P), :] = hq * jax.nn.sigmoid(hq)
        fs[pl.ds(r0, P), :] = logf
        ks[pl.ds(r0, P), :] = kk
        return jnp.minimum(worst, jnp.sum(logf, axis=0, keepdims=True))

    worst = lax.fori_loop(0, TS // P, prep, jnp.zeros((1, QUAD), F32))

    def finish(o, r0, rows, smat, b, kk, iv, qt):
        o = o + lax.dot_general(qt, smat.astype(BF16), nt, preferred_element_type=F32)
        out_ref[pl.ds(r0, rows), :] = _head_rmsnorm_gate(o, seg64b, ng, g_ref[pl.ds(r0, rows), :]).astype(out_ref.dtype)
        bl = b[rows - 1:rows, :]
        ktil = kk * jnp.exp(bl - b)
        u = lax.dot_general(iv.astype(BF16), ktil.astype(BF16), tn, preferred_element_type=F32)
        st[...] = smat * jnp.exp(bl) + jnp.where(seg64, u, 0.0)

    def fast_loop():
        L = HGRN_FAST_CHUNK
        tril = (_iota2((L, L), 1) <= _iota2((L, L), 0)).astype(BF16)
        causal = (_iota2((L, QUAD), 1) & (L - 1)) <= _iota2((L, QUAD), 0)
        zb = jnp.zeros((QUAD, QUAD), BF16)

        G = min(HGRN_GROUP, TS // L)
        assert TS % (L * G) == 0

        def body(gi, carry):
            R = range(G)
            r0 = [pl.multiple_of((gi * G + j) * L, L) for j in R]
            kk = [ks[pl.ds(r0[j], L), :] for j in R]
            ivb = [i_ref[pl.ds(r0[j], L), :].astype(BF16) for j in R]
            b = [_dot2r(tril, fs[pl.ds(r0[j], L), :]) for j in R]
            qt = [(qs[pl.ds(r0[j], L), :] * jnp.exp(b[j])).astype(BF16) for j in R]
            kbd = [jnp.where(seg64, jnp.concatenate([(kk[j] * jnp.exp(-b[j])).astype(BF16)] * N_HGRN_HEADS, axis=0), zb)
                   for j in R]
            ibd = [jnp.where(seg64, jnp.concatenate([ivb[j]] * N_HGRN_HEADS, axis=0), zb) for j in R]
            amat = [lax.dot_general(qt[j], kbd[j], nt, preferred_element_type=F32) for j in R]
            ktil = [(kk[j] * jnp.exp(b[j][L - 1:L, :] - b[j])).astype(BF16) for j in R]
            u = [lax.dot_general(ivb[j], ktil[j], tn, preferred_element_type=F32) for j in R]
            o = [jnp.dot(jnp.where(causal, amat[j], 0.0).astype(BF16), ibd[j], preferred_element_type=F32) for j in R]
            smat = [st[...]]
            for j in R:
                smat.append(smat[j] * jnp.exp(b[j][L - 1:L, :]) + jnp.where(seg64, u[j], 0.0))
            st[...] = smat[G]
            inter = [lax.dot_general(qt[j], smat[j].astype(BF16), nt, preferred_element_type=F32) for j in R]
            o = [o[j] + inter[j] for j in R]
            msq = [_dot2(o[j] * o[j], seg64b) * (1.0 / HEAD_DIM) for j in R]
            for j in R:
                y = o[j] * lax.rsqrt(msq[j] + EPS) * ng
                out_ref[pl.ds(r0[j], L), :] = (jax.nn.sigmoid(g_ref[pl.ds(r0[j], L), :]) * y).astype(out_ref.dtype)
            return carry

        lax.fori_loop(0, TS // (L * G), body, 0)

    def exact_loop():
        L = sub
        row = _iota2((L, QUAD), 0)

        def body(c, carry):
            r0 = pl.multiple_of(c * L, L)
            q = qs[pl.ds(r0, L), :]
            kk = ks[pl.ds(r0, L), :]
            iv = i_ref[pl.ds(r0, L), :]
            b = _cumulate_rows(fs[pl.ds(r0, L), :], L, jnp.add, 0.0)
            parts = []
            for j in range(L):
                dec = jnp.exp(jnp.where(row >= j, b - b[j:j + 1, :], NEG_BIG))
                parts.append(dec * q * kk[j:j + 1, :])
            tstack = jnp.concatenate(parts, axis=0).astype(BF16)
            y = jnp.dot(tstack, seg64b, preferred_element_type=F32)
            o = y[0:L] * iv[0:1, :]
            for j in range(1, L):
                o = o + y[j * L:(j + 1) * L] * iv[j:j + 1, :]
            finish(o, r0, L, st[...], b, kk, iv, (q * jnp.exp(b)).astype(BF16))
            return carry

        lax.fori_loop(0, TS // L, body, 0, unroll=min(CHUNK_UNROLL, TS // L))

    if fast:
        safe = jnp.min(worst) > -HGRN_SAFE_DECAY
        pl.when(safe)(fast_loop)
        pl.when(jnp.logical_not(safe))(exact_loop)
    else:
        exact_loop()

    @pl.when(t == pl.num_programs(1) - 1)
    def _():
        _store_block_diag(s_out, st[...])


def _hgrn(z3, s0t, lb, norm_g, n_valid, sub):
    B, S, _ = z3.shape
    ts = min(512, S)
    fast = ts % HGRN_FAST_CHUNK == 0
    assert S % ts == 0 and ts % sub == 0
    blk = lambda c0: pl.BlockSpec((None, ts, QUAD), lambda b, t: (b, t, c0 // QUAD))
    per_b = pl.BlockSpec((None, N_HGRN_HEADS, HEAD_DIM, HEAD_DIM), lambda b, t: (b, 0, 0, 0))
    stage = lambda: pltpu.VMEM((ts, QUAD), F32)
    return _CallSpec(
        body=functools.partial(_hgrn_kernel, tile=ts, sub=sub, n_valid=n_valid, fast=fast),
        grid=(B, S // ts),
        args=[z3, z3, z3, z3, s0t, lb, norm_g],
        in_specs=[blk(COL_HQ), blk(COL_HF), blk(COL_HI), blk(COL_HG), per_b,
                  _const_spec((1, QUAD)), _const_spec((1, QUAD))],
        out_shape=[jax.ShapeDtypeStruct((B, S, QUAD), BF16),
                   jax.ShapeDtypeStruct((B, N_HGRN_HEADS, HEAD_DIM, HEAD_DIM), F32)],
        out_specs=[pl.BlockSpec((None, ts, QUAD), lambda b, t: (b, t, 0)), per_b],
        scratch=[pltpu.VMEM((QUAD, QUAD), F32), stage(), stage(), stage()])


PROJ_SPLITS = (ATTN_WIDTH, ATTN_WIDTH, ATTN_WIDTH,
               MLSTM_WIDTH, MLSTM_WIDTH, MLSTM_WIDTH, MLSTM_WIDTH, N_MLSTM_HEADS, N_MLSTM_HEADS,
               HGRN_WIDTH, HGRN_WIDTH, HGRN_WIDTH, HGRN_WIDTH)


def _prep_w_in(w):
    cuts = [int(c) for c in np.cumsum(PROJ_SPLITS)[:-1]]
    aq, ak, av, mq, mk, mv, mo, mi, mf, hq, hf, hi, hg = jnp.split(w, cuts, axis=1)
    gate = jnp.concatenate([mi, mf, jnp.zeros((w.shape[0], LANES - 2 * N_MLSTM_HEADS), w.dtype)], axis=1)
    return jnp.concatenate([aq, ak, av, mq, mk, mv, mo, hq, hf, hi, hg, gate], axis=1).astype(BF16)


HGRN_SUBCHUNK = 16


class _LayerWeights(NamedTuple):
    g_mix: jax.Array
    w_in: jax.Array
    gate_b: jax.Array
    conv_w: jax.Array
    mlstm_g: jax.Array
    lb: jax.Array
    hgrn_g: jax.Array
    w_out: jax.Array
    g_ffn: jax.Array
    w_up: jax.Array
    w_down: jax.Array
    g_final: jax.Array


def _gate_row(gate_b):
    return jnp.repeat(gate_b.astype(F32), HEAD_DIM, axis=1)


def _trunk_layer(x2d, B, S, n_valid, attn_fn, states, w, final, chunk_m, chunk_h, n_keep=None, stacked=None):
    conv_buf, C0, n0, m0, S0 = states
    if n_keep is None:
        z = _inproj(x2d, w.g_mix, w.w_in)
    else:
        z, kt, vt = _inproj(x2d, w.g_mix, w.w_in, S, n_keep, stacked)
    z3 = z.reshape(B, S, ZW)
    conv0 = jnp.zeros((B, SUBLANES, 2 * MLSTM_WIDTH), F32).at[:, SUBLANES - (CONV_WIDTH - 1):].set(conv_buf.astype(F32))
    m0r = jnp.repeat(m0.astype(F32), HEAD_DIM, axis=1).reshape(B, 1, MLSTM_WIDTH)
    mlstm = _mlstm(z3, conv0, C0.astype(F32), n0.astype(F32).reshape(B, 1, MLSTM_WIDTH), m0r,
                   w.gate_b, w.conv_w, w.mlstm_g, n_valid, chunk_m)
    hgrn = _hgrn(z3, jnp.swapaxes(S0.astype(F32), -1, -2), w.lb, w.hgrn_g, n_valid, chunk_h)
    attn = attn_fn(z3)
    sem = ("parallel", "arbitrary")
    if isinstance(attn, _CallSpec):
        (attn,), (ml, c_new, n, m), (hg, st) = _run([attn, mlstm, hgrn], sem, "sample_mixers")
    else:
        (ml, c_new, n, m), = _run([mlstm], sem, "mlstm")
        (hg, st), = _run([hgrn], sem, "hgrn")
    n2 = B * S
    y = _outffn(x2d, attn.reshape(n2, ATTN_WIDTH), ml.reshape(n2, MLSTM_WIDTH), hg.reshape(n2, HGRN_WIDTH),
                w.w_out, w.g_ffn, w.w_up, w.w_down, w.g_final, final)
    if n_keep is None:
        k_rows = z3[:, :n_valid, COL_AK:COL_AK + ATTN_WIDTH].reshape(B, n_valid, N_ATTN_HEADS, HEAD_DIM)
        v_rows = z3[:, :n_valid, COL_AV:COL_AV + ATTN_WIDTH].reshape(B, n_valid, N_ATTN_HEADS, HEAD_DIM)
    else:
        k_rows, v_rows = kt, vt
    conv_new = z3[:, n_valid - (CONV_WIDTH - 1):n_valid, COL_MQK:COL_MQK + 2 * MLSTM_WIDTH]
    s_new = jnp.swapaxes(st, -1, -2)
    return y, (k_rows, v_rows, conv_new, c_new, n.reshape(B, N_MLSTM_HEADS, HEAD_DIM), m[:, 0, ::HEAD_DIM], s_new)


def kernel(x_prompt, x_sample, cache_attn_k, cache_attn_v, state_mlstm_conv, state_mlstm_C, state_mlstm_n, state_mlstm_m, state_hgrn_S, rel_bias, g_mix, w_in, mlstm_gate_b, mlstm_conv_w, mlstm_norm_g, hgrn_lb_raw, hgrn_norm_g, w_out, g_ffn, w_up, w_down, g_final):
    depth = w_in.shape[0]
    bp, sp, _ = x_prompt.shape
    bs, ts, _ = x_sample.shape
    n_keep = min(max(w for w, _ in DILATED_PATTERNS), sp)
    n_past = cache_attn_k.shape[2]
    lb_p = jax.nn.softmax(hgrn_lb_raw.astype(F32), axis=0)
    hgrn_lb = jnp.cumsum(lb_p, axis=0) - lb_p[0]
    bias_dist = _bias_by_distance(rel_bias)
    p_tables = _prompt_tables(bias_dist)
    s_tables = _sample_tables(bias_dist, n_past, ts)
    cache_kt = jnp.transpose(cache_attn_k, (0, 1, 3, 4, 2)).reshape(depth, bs, ATTN_WIDTH, n_past)
    cache_vt = jnp.transpose(cache_attn_v, (0, 1, 3, 4, 2)).reshape(depth, bs, ATTN_WIDTH, n_past)

    hp = x_prompt.reshape(bp * sp, D_MODEL)
    hs = jnp.zeros((bs, TPAD, D_MODEL), F32).at[:, :ts].set(x_sample).reshape(bs * TPAD, D_MODEL)
    zero_states = (jnp.zeros((bp, CONV_WIDTH - 1, 2 * MLSTM_WIDTH), F32),
                   jnp.zeros((bp, N_MLSTM_HEADS, HEAD_DIM, HEAD_DIM), F32),
                   jnp.zeros((bp, N_MLSTM_HEADS, HEAD_DIM), F32),
                   jnp.zeros((bp, N_MLSTM_HEADS), F32),
                   jnp.zeros((bp, N_HGRN_HEADS, HEAD_DIM, HEAD_DIM), F32))
    p_states, s_states = [], []
    kv_t = None
    for l in range(depth):
        final = l == depth - 1
        w_in_l = _prep_w_in(w_in[l])
        weights = _LayerWeights(
            g_mix=g_mix[l][None], w_in=w_in_l,
            gate_b=_gate_row(mlstm_gate_b[l]),
            conv_w=mlstm_conv_w[l], mlstm_g=mlstm_norm_g[l][None], lb=hgrn_lb[l][None], hgrn_g=hgrn_norm_g[l][None],
            w_out=w_out[l].astype(BF16), g_ffn=g_ffn[l][None], w_up=w_up[l].astype(BF16),
            w_down=w_down[l].astype(BF16), g_final=g_final[None])
        hp, st = _trunk_layer(hp, bp, sp, sp, functools.partial(_attn_prompt, bias=p_tables),
                              zero_states, weights, final, MLSTM_CHUNK, HGRN_SUBCHUNK, n_keep, (l, depth, kv_t))
        kv_t = st[:2]
        p_states.append(st[2:])
        sample_attn = functools.partial(_attn_sample, cache_kt=cache_kt, cache_vt=cache_vt, layer=l, tables=s_tables)
        states = (state_mlstm_conv[l], state_mlstm_C[l], state_mlstm_n[l], state_mlstm_m[l], state_hgrn_S[l])
        hs, st = _trunk_layer(hs, bs, TPAD, ts, sample_attn, states, weights, final, TPAD, TPAD)
        s_states.append(st)
    y_prompt = hp.reshape(bp, sp, D_MODEL)
    y_sample = hs.reshape(bs, TPAD, D_MODEL)[:, :ts]
    to_rows = lambda t: jnp.transpose(t.reshape(depth, bp, N_ATTN_HEADS, HEAD_DIM, n_keep), (0, 1, 4, 2, 3))
    p_out = [to_rows(kv_t[0]), to_rows(kv_t[1])] + [jnp.stack(a) for a in zip(*p_states)]
    s_out = [jnp.stack(a) for a in zip(*s_states)]
    return (y_prompt, y_sample, *p_out, *s_out)
```

```python
import functools
from typing import Callable, NamedTuple

import jax
import jax.numpy as jnp
import numpy as np
from jax import lax
from jax.experimental import pallas as pl
from jax.experimental.pallas import tpu as pltpu

F32 = jnp.float32
BF16 = jnp.bfloat16

D_MODEL = 1024
HEAD_DIM = 64
N_ATTN_HEADS = 8
N_MLSTM_HEADS = 4
N_HGRN_HEADS = 4
ATTN_WIDTH = N_ATTN_HEADS * HEAD_DIM
MLSTM_WIDTH = N_MLSTM_HEADS * HEAD_DIM
HGRN_WIDTH = N_HGRN_HEADS * HEAD_DIM
DILATED_PATTERNS = ((128, 1), (512, 4), (2048, 16))
QBLOCK = 128
N_REL_BUCKETS = 32
REL_MAX_DISTANCE = 2048
CONV_WIDTH = 4
MLSTM_CHUNK = 64
D_FF = 4 * D_MODEL
EPS = 1e-6
NEG_BIG = -1e30
LOG2E = 1.4426950408889634

LANES = 128
SUBLANES = 8

COL_AQ = 0
COL_AK = COL_AQ + ATTN_WIDTH
COL_AV = COL_AK + ATTN_WIDTH
COL_MQK = COL_AV + ATTN_WIDTH
COL_MV = COL_MQK + 2 * MLSTM_WIDTH
COL_MO = COL_MV + MLSTM_WIDTH
COL_HQ = COL_MO + MLSTM_WIDTH
COL_HF = COL_HQ + HGRN_WIDTH
COL_HI = COL_HF + HGRN_WIDTH
COL_HG = COL_HI + HGRN_WIDTH
COL_GATE = COL_HG + HGRN_WIDTH
ZW = COL_GATE + LANES

VMEM_LIMIT = 56 * 1024 * 1024


def _cparams(sem, vmem=VMEM_LIMIT):
    return pltpu.CompilerParams(dimension_semantics=sem, vmem_limit_bytes=vmem)


def _const_spec(shape):
    nd = len(shape)
    return pl.BlockSpec(shape, lambda *_: (0,) * nd, pipeline_mode=pl.Buffered(1))


def _layer_spec(shape, layer):
    nd = len(shape)
    return pl.BlockSpec((None,) + tuple(shape), lambda *_: (layer,) + (0,) * nd, pipeline_mode=pl.Buffered(1))


class _CallSpec(NamedTuple):
    body: Callable
    grid: tuple
    args: list
    in_specs: list
    out_shape: list
    out_specs: list
    scratch: list


def _run(specs, sem, name):
    grid = specs[0].grid
    assert all(s.grid == grid for s in specs)
    n_in = [len(s.args) for s in specs]
    n_out = [len(s.out_shape) for s in specs]
    n_scr = [len(s.scratch) for s in specs]

    def body(*refs):
        ins, outs, scrs = refs[:sum(n_in)], refs[sum(n_in):sum(n_in) + sum(n_out)], refs[sum(n_in) + sum(n_out):]

        for k, s in enumerate(specs):
            a, b, c = sum(n_in[:k]), sum(n_out[:k]), sum(n_scr[:k])
            s.body(*ins[a:a + n_in[k]], *outs[b:b + n_out[k]], *scrs[c:c + n_scr[k]])

    flat = pl.pallas_call(
        body, grid=grid,
        in_specs=[x for s in specs for x in s.in_specs],
        out_shape=[x for s in specs for x in s.out_shape],
        out_specs=[x for s in specs for x in s.out_specs],
        scratch_shapes=[x for s in specs for x in s.scratch],
        compiler_params=_cparams(sem), name=name,
    )(*[x for s in specs for x in s.args])
    out, k = [], 0
    for n in n_out:
        out.append(list(flat[k:k + n]))
        k += n
    return out


def _inproj_kernel(x_ref, g_ref, w_ref, *rest, tiles_per_seq, first_keep, n_prev=0):
    x = x_ref[...]
    ms = jnp.mean(x * x, axis=-1, keepdims=True)
    xn = ((x * lax.rsqrt(ms + EPS)) * g_ref[...]).astype(BF16)
    if first_keep is None:
        (z_ref,) = rest
        z_ref[...] = jnp.dot(xn, w_ref[...], preferred_element_type=F32)
        return
    z_ref, kt_ref, vt_ref = rest[n_prev:]
    z_ref[...] = jnp.dot(xn, w_ref[...], preferred_element_type=F32)

    @pl.when(pl.program_id(0) % tiles_per_seq >= first_keep)
    def _():
        kt_ref[...] = z_ref[:, COL_AK:COL_AK + ATTN_WIDTH].T
        vt_ref[...] = z_ref[:, COL_AV:COL_AV + ATTN_WIDTH].T


def _inproj(x2d, g, w, layer, seq=None, n_keep=None, stacked=None):
    n = x2d.shape[0]
    tm = min(512, n)
    assert n % tm == 0
    x_spec = pl.BlockSpec((tm, D_MODEL), lambda i: (i, 0))
    z_spec = pl.BlockSpec((tm, ZW), lambda i: (i, 0))
    z_shape = jax.ShapeDtypeStruct((n, ZW), F32)
    if stacked is None:
        return pl.pallas_call(
            functools.partial(_inproj_kernel, tiles_per_seq=None, first_keep=None),
            out_shape=z_shape, grid=(n // tm,),
            in_specs=[x_spec, _const_spec((1, D_MODEL)), _layer_spec((D_MODEL, ZW), layer)],
            out_specs=z_spec, compiler_params=_cparams(("parallel",)), name="inproj",
        )(x2d, g, w)
    assert seq % tm == 0 and n_keep % tm == 0
    layer, depth, prev = stacked
    tps, first_keep = seq // tm, (seq - n_keep) // tm
    t_spec = pl.BlockSpec((None, None, ATTN_WIDTH, tm),
                          lambda i: (layer, i // tps, 0, jnp.maximum(i % tps - first_keep, 0)))
    t_shape = jax.ShapeDtypeStruct((depth, n // seq, ATTN_WIDTH, n_keep), F32)
    in_specs = [x_spec, _const_spec((1, D_MODEL)), _layer_spec((D_MODEL, ZW), layer)]
    args = [x2d, g, w]
    if prev is None:
        prev = (jnp.zeros(t_shape.shape, F32), jnp.zeros(t_shape.shape, F32))
    in_specs += [pl.BlockSpec(memory_space=pl.ANY)] * 2
    aliases = {len(args): 1, len(args) + 1: 2}
    args += list(prev)
    return pl.pallas_call(
        functools.partial(_inproj_kernel, tiles_per_seq=tps, first_keep=first_keep, n_prev=2),
        out_shape=[z_shape, t_shape, t_shape], grid=(n // tm,),
        in_specs=in_specs, out_specs=[z_spec, t_spec, t_spec], input_output_aliases=aliases,
        compiler_params=_cparams(("arbitrary",)), name="inproj_kt",
    )(*args)


FF_CHUNK = 1024
FFN_ROWS = 512


def _outffn_kernel(x_ref, a_ref, m_ref, h_ref, wo_ref, gf_ref, wu_ref, wd_ref, gl_ref, y_ref, xn_sc, *, final):
    x1 = x_ref[...]
    x1 = x1 + jnp.dot(a_ref[...], wo_ref[0:ATTN_WIDTH, :], preferred_element_type=F32)
    x1 = x1 + jnp.dot(m_ref[...], wo_ref[ATTN_WIDTH:ATTN_WIDTH + MLSTM_WIDTH, :], preferred_element_type=F32)
    x1 = x1 + jnp.dot(h_ref[...], wo_ref[ATTN_WIDTH + MLSTM_WIDTH:, :], preferred_element_type=F32)
    ms = jnp.mean(x1 * x1, axis=-1, keepdims=True)
    xn_sc[...] = ((x1 * lax.rsqrt(ms + EPS)) * gf_ref[...]).astype(BF16)
    y_ref[...] = x1
    nchunk = D_FF // FF_CHUNK
    up = lambda c: jnp.dot(xn_sc[...], wu_ref[:, c * FF_CHUNK:(c + 1) * FF_CHUNK], preferred_element_type=F32)
    u = up(0)
    for c in range(nchunk):
        u_next = up(c + 1) if c + 1 < nchunk else None
        hh = jnp.square(jnp.maximum(u, 0.0)).astype(BF16)
        y_ref[...] += jnp.dot(hh, wd_ref[c * FF_CHUNK:(c + 1) * FF_CHUNK, :], preferred_element_type=F32)
        u = u_next
    if final:
        x2 = y_ref[...]
        ms2 = jnp.mean(x2 * x2, axis=-1, keepdims=True)
        y_ref[...] = (x2 * lax.rsqrt(ms2 + EPS)) * gl_ref[...]


def _outffn(x2d, attn, ml, hg, w_out, g_ffn, w_up, w_down, g_final, final, layer):
    n = x2d.shape[0]
    tm = min(FFN_ROWS, n)
    assert n % tm == 0
    row = lambda w: pl.BlockSpec((tm, w), lambda i: (i, 0))
    return pl.pallas_call(
        functools.partial(_outffn_kernel, final=final),
        out_shape=jax.ShapeDtypeStruct((n, D_MODEL), F32),
        grid=(n // tm,),
        in_specs=[row(D_MODEL), row(ATTN_WIDTH), row(MLSTM_WIDTH), row(HGRN_WIDTH),
                  _layer_spec((D_MODEL, D_MODEL), layer), _const_spec((1, D_MODEL)),
                  _layer_spec((D_MODEL, D_FF), layer), _layer_spec((D_FF, D_MODEL), layer), _const_spec((1, D_MODEL))],
        out_specs=row(D_MODEL),
        scratch_shapes=[pltpu.VMEM((tm, D_MODEL), BF16)],
        compiler_params=_cparams(("parallel",)),
        name="outffn",
    )(x2d, attn, ml, hg, w_out, g_ffn, w_up, w_down, g_final)


def _t5_causal_bucket(dist):
    n = np.asarray(dist).astype(np.int32)
    max_exact = N_REL_BUCKETS // 2
    scaled = np.log(np.maximum(n, 1) / max_exact) / np.log(REL_MAX_DISTANCE / max_exact)
    large = np.minimum(max_exact + (scaled * (N_REL_BUCKETS - max_exact)).astype(np.int32), N_REL_BUCKETS - 1)
    return np.where(n < max_exact, n, large).astype(np.int32)


BIAS_DIST = 2304


def _bias_by_distance(rel_bias):
    assert BIAS_DIST > max(w for w, _ in DILATED_PATTERNS) + SUBLANES
    return rel_bias.astype(F32)[_t5_causal_bucket(np.arange(BIAS_DIST)[::-1])].T


def _dist_slice(bias_desc, lo, hi, step=1):
    last = BIAS_DIST - 1
    return bias_desc[:, last - hi:last - lo + 1:step]


def _prompt_tables(bias_dist):
    H = N_ATTN_HEADS
    a = np.arange(QBLOCK)[:, None]
    b = np.arange(2 * QBLOCK)[None, :]
    rel = QBLOCK + a - b
    band = (rel >= 0) & (rel <= QBLOCK)
    period = 3 * QBLOCK
    biases = []
    for window, dil in DILATED_PATTERNS:
        nsub = window // dil
        assert nsub == QBLOCK
        vd = _dist_slice(bias_dist, 0, nsub * dil, dil)
        rp = jnp.concatenate([vd, jnp.broadcast_to(vd[:, -1:], (H, QBLOCK)),
                              jnp.broadcast_to(vd[:, 0:1], (H, QBLOCK - 1))], axis=1)
        skew = jnp.tile(rp, (1, QBLOCK))[:, :QBLOCK * (period - 1)].reshape(H, QBLOCK, period - 1)
        biases.append(skew[:, :, :2 * QBLOCK].reshape(H // 2, 2 * QBLOCK, 2 * QBLOCK))
    mask = np.stack([np.tile(band, (2, 1)), np.tile(band & (b >= QBLOCK), (2, 1))])
    table = jnp.where(mask[None, None], jnp.stack(biases)[:, :, None] * LOG2E, -jnp.inf)
    return jnp.swapaxes(table, -1, -2)


ATTN_GROUP = 8


def _attn_prompt_kernel(q_ref, k_ref, v_ref, bias_ref, out_ref,
                        x4, qs, ks, vts, o1, o4, o16, l1, l4, l16, p_scr, st_scr, *, seq):
    S = seq
    nblk = S // QBLOCK
    scale = HEAD_DIM ** -0.5 * LOG2E
    zpad = jnp.zeros((QBLOCK, LANES), BF16)
    for di in range(3):
        ks[di, 0:QBLOCK, :] = zpad
        vts[di, 0] = zpad

    R = 512

    def put(src, di, row0, blk):
        if src is q_ref:
            qs[di, row0:row0 + blk.shape[0], :] = (blk * scale).astype(BF16)
        elif src is k_ref:
            ks[di, QBLOCK + row0:QBLOCK + row0 + blk.shape[0], :] = blk.astype(BF16)
        else:
            blk_t = blk.T.astype(BF16)
            for t in range(blk.shape[0] // QBLOCK):
                vts[di, 1 + row0 // QBLOCK + t] = blk_t[:, t * QBLOCK:(t + 1) * QBLOCK]

    for src in (q_ref, k_ref, v_ref):
        for i in range(S // R):
            put(src, 0, i * R, src[i * R:(i + 1) * R, :])
        for r in range(4):
            for i in range(S // 4 // R):
                blk = src[pl.ds(r + 4 * i * R, R, stride=4), :]
                row0 = r * (S // 4) + i * R
                x4[row0:row0 + R, :] = blk
                put(src, 1, row0, blk)
        n16 = S // 16
        for c4 in range(4):
            for a in range(4):
                put(src, 2, (c4 * 4 + a) * n16, x4[pl.ds(c4 * (S // 4) + a, n16, stride=4), :])

    lane = lax.broadcasted_iota(jnp.int32, (QBLOCK, LANES), 1)
    head_a = lane < HEAD_DIM
    row_a = lax.broadcasted_iota(jnp.int32, (QBLOCK, LANES), 0) < HEAD_DIM
    nt = (((1,), (1,)), ((), ()))

    def branch(di, dil, o_ref, l_ref):
        per_class = (S // dil) // QBLOCK

        R = range(ATTN_GROUP)
        zero = jnp.zeros((QBLOCK, LANES), BF16)

        def scores(g, js):
            out = {}
            for j in js:
                i = g * ATTN_GROUP + j
                base = pl.multiple_of(i * QBLOCK, QBLOCK)
                q = qs[di, pl.ds(base, QBLOCK), :]
                kk = ks[di, pl.ds(base, 2 * QBLOCK), :]
                q2 = jnp.concatenate([jnp.where(head_a, q, zero), jnp.where(head_a, zero, q)], axis=0)
                out[j] = lax.dot_general(kk, q2, nt, preferred_element_type=F32)
            return out

        def softmax(g, slot, st):
            for j in st:
                i = g * ATTN_GROUP + j
                s = st[j] + bias_ref[di, jnp.where(i % per_class == 0, 1, 0)]
                m = jnp.max(s, axis=0, keepdims=True)
                p = jnp.exp2(s - m)
                p_scr[slot, j] = p.astype(BF16)
                st_scr[slot, j, 0:1, :] = m
                st_scr[slot, j, 1:2, :] = jnp.sum(p, axis=0, keepdims=True)

        def weighted_values(g, slot, js):
            out = {}
            for j in js:
                i = g * ATTN_GROUP + j
                vt = jnp.concatenate([vts[di, i], vts[di, i + 1]], axis=1)
                ot = jnp.dot(vt, p_scr[slot, j], preferred_element_type=F32)
                out[j] = (ot, st_scr[slot, j, 0:1, :], st_scr[slot, j, 1:2, :])
            return out

        def outputs(g, pv):
            for j in pv:
                i = g * ATTN_GROUP + j
                base = pl.multiple_of(i * QBLOCK, QBLOCK)
                ot, m, l = pv[j]
                otn = ot / l
                lse = m + jnp.log2(l)
                o = jnp.where(row_a, otn[:, 0:QBLOCK], otn[:, QBLOCK:]).T
                ls = jnp.where(row_a, jnp.broadcast_to(lse[:, 0:QBLOCK], (QBLOCK, LANES)),
                               jnp.broadcast_to(lse[:, QBLOCK:], (QBLOCK, LANES))).T
                if dil == 1:
                    o_ref[pl.ds(base, QBLOCK), :] = o
                    l_ref[pl.ds(base, QBLOCK), :] = ls
                else:
                    c, n = i // per_class, i % per_class
                    res = c if dil == 4 else 4 * (c % 4) + c // 4
                    start = dil * QBLOCK * n + res
                    o_ref[pl.ds(start, QBLOCK, stride=dil), :] = o
                    l_ref[pl.ds(start, QBLOCK, stride=dil), :] = ls

        ngroup = nblk // ATTN_GROUP
        softmax(0, 0, scores(0, R))

        def body(g, carry):
            slot = g % 2
            st = scores(g, R)
            pv = weighted_values(g - 1, 1 - slot, R)
            softmax(g, slot, st)
            outputs(g - 1, pv)
            return carry

        lax.fori_loop(1, ngroup, body, 0)
        outputs(ngroup - 1, weighted_values(ngroup - 1, (ngroup - 1) % 2, R))

    branch(0, 1, o1, l1)
    branch(1, 4, o4, l4)
    branch(2, 16, o16, l16)

    T = 256

    def mix(i, carry):
        r0 = pl.multiple_of(i * T, T)
        la, lb, lc = l1[pl.ds(r0, T), :], l4[pl.ds(r0, T), :], l16[pl.ds(r0, T), :]
        mx = jnp.maximum(jnp.maximum(la, lb), lc)
        ea, eb, ec = jnp.exp2(la - mx), jnp.exp2(lb - mx), jnp.exp2(lc - mx)
        num = ea * o1[pl.ds(r0, T), :] + eb * o4[pl.ds(r0, T), :] + ec * o16[pl.ds(r0, T), :]
        out_ref[pl.ds(r0, T), :] = (num / (ea + eb + ec)).astype(out_ref.dtype)
        return carry

    lax.fori_loop(0, S // T, mix, 0)


def _attn_prompt(z3, bias):
    B, S, _ = z3.shape
    assert S % (16 * QBLOCK) == 0
    npair = N_ATTN_HEADS // 2
    col = lambda c0: pl.BlockSpec((None, S, LANES), lambda b, p: (b, 0, c0 // LANES + p))
    f32s = lambda: pltpu.VMEM((S, LANES), F32)
    return pl.pallas_call(
        functools.partial(_attn_prompt_kernel, seq=S),
        out_shape=jax.ShapeDtypeStruct((B, S, ATTN_WIDTH), BF16),
        grid=(B, npair),
        in_specs=[col(COL_AQ), col(COL_AK), col(COL_AV),
                  pl.BlockSpec((3, None, 2, 2 * QBLOCK, 2 * QBLOCK), lambda b, p: (0, p, 0, 0, 0))],
        out_specs=pl.BlockSpec((None, S, LANES), lambda b, p: (b, 0, p)),
        scratch_shapes=[f32s(),
                        pltpu.VMEM((3, S, LANES), BF16),
                        pltpu.VMEM((3, S + QBLOCK, LANES), BF16),
                        pltpu.VMEM((3, S // QBLOCK + 1, LANES, QBLOCK), BF16),
                        f32s(), f32s(), f32s(), f32s(), f32s(), f32s(),
                        pltpu.VMEM((2, ATTN_GROUP, 2 * QBLOCK, 2 * QBLOCK), BF16),
                        pltpu.VMEM((2, ATTN_GROUP, SUBLANES, 2 * QBLOCK), F32)],
        compiler_params=_cparams(("parallel", "parallel")),
        name="attn_prompt",
    )(z3, z3, z3, bias)


TPAD = SUBLANES
TAIL = 512


def _sample_tables(bias_dist, n_past, n_tok):
    assert n_past >= max(w for w, _ in DILATED_PATTERNS) and n_tok <= TPAD and TAIL >= DILATED_PATTERNS[1][0]
    H = N_ATTN_HEADS
    t = np.arange(TPAD)[:, None]

    def by_row(width):
        rows = [_dist_slice(bias_dist, tt + 1, tt + width) for tt in range(TPAD)]
        return jnp.stack(rows, axis=1).reshape(H * TPAD, width)

    bias_tail, bias_full = by_row(TAIL), by_row(n_past)
    zero_dist = _dist_slice(bias_dist, 0, 0)
    new_rows = [jnp.concatenate([_dist_slice(bias_dist, 0, tt),
                                 jnp.broadcast_to(zero_dist, (H, TPAD - tt - 1))], axis=1) for tt in range(TPAD)]
    bias_new = jnp.stack(new_rows, axis=1).reshape(H * TPAD, TPAD)

    def valid(dj, dil, nsub, lo):
        ok = (dj % dil == 0) & (dj // dil >= lo) & (dj // dil <= nsub) & (t < n_tok)
        return np.tile(ok, (H, 1)).astype(np.float32)

    tn = np.arange(TPAD)[None, :]
    m_tail, m_new = [], []
    for window, dil in DILATED_PATTERNS:
        nsub = window // dil
        if window <= TAIL:
            m_tail.append(valid(TAIL + t - np.arange(TAIL)[None, :], dil, nsub, 1))
        else:
            m_full = valid(n_past + t - np.arange(n_past)[None, :], dil, nsub, 1)
        m_new.append(valid(t - tn, dil, nsub, 0) * (tn < n_tok))
    return (bias_tail, bias_full, bias_new, jnp.asarray(np.stack(m_tail)), jnp.asarray(m_full),
            jnp.asarray(np.stack(m_new).astype(np.float32)))


def _attn_sample_kernel(q_ref, kn_ref, vn_ref, kt_ref, vt_ref, bt_ref, bf_ref, bn_ref, mt_ref, mf_ref, mn_ref,
                        out_ref, *, n_past):
    H = N_ATTN_HEADS
    rows = H * TPAD
    same_head = (_iota2((rows, ATTN_WIDTH), 0) >> 3) == (_iota2((rows, ATTN_WIDTH), 1) >> HEAD_SHIFT)
    q = q_ref[...] * (HEAD_DIM ** -0.5)
    qm = jnp.where(same_head, jnp.concatenate([q] * H, axis=0), 0.0).astype(BF16)
    nt = (((1,), (1,)), ((), ()))
    kn = kn_ref[...].astype(BF16)
    vn = vn_ref[...].astype(BF16)
    kt = kt_ref[...].astype(BF16)
    vt = vt_ref[...].astype(BF16)
    s_new = lax.dot_general(qm, kn, nt, preferred_element_type=F32)
    s_full = jnp.dot(qm, kt, preferred_element_type=F32)
    s_tail = s_full[:, n_past - TAIL:]

    outs, lses = [], []
    for di, (window, _) in enumerate(DILATED_PATTERNS):
        if window <= TAIL:
            sm = jnp.where(mt_ref[di] > 0.5, s_tail + bt_ref[...], NEG_BIG)
            vmain = vt[:, n_past - TAIL:]
        else:
            sm = jnp.where(mf_ref[...] > 0.5, s_full + bf_ref[...], NEG_BIG)
            vmain = vt
        sn = jnp.where(mn_ref[di] > 0.5, s_new + bn_ref[...], NEG_BIG)
        m = jnp.maximum(jnp.max(sm, axis=-1, keepdims=True), jnp.max(sn, axis=-1, keepdims=True))
        pm = jnp.exp(sm - m)
        pn = jnp.exp(sn - m)
        l = jnp.sum(pm, axis=-1, keepdims=True) + jnp.sum(pn, axis=-1, keepdims=True)
        o = jnp.dot(pn.astype(BF16), vn, preferred_element_type=F32)
        o = o + lax.dot_general(pm.astype(BF16), vmain, nt, preferred_element_type=F32)
        outs.append(o / l)
        lses.append(m + jnp.log(l))
    mx = jnp.maximum(jnp.maximum(lses[0], lses[1]), lses[2])
    es = [jnp.exp(ls - mx) for ls in lses]
    mixed = (es[0] * outs[0] + es[1] * outs[1] + es[2] * outs[2]) / (es[0] + es[1] + es[2])
    mixed = jnp.where(same_head, mixed, 0.0)
    acc = mixed[0:TPAD]
    for h in range(1, H):
        acc = acc + mixed[h * TPAD:(h + 1) * TPAD]
    out_ref[...] = acc.astype(out_ref.dtype)


def _attn_sample(z3, cache_kt, cache_vt, layer, tables):
    B, T, _ = z3.shape
    assert T == TPAD
    n_past = cache_kt.shape[-1]
    new = lambda c0: pl.BlockSpec((None, TPAD, ATTN_WIDTH), lambda b, t: (b, 0, c0 // ATTN_WIDTH))
    buf = pl.BlockSpec((None, None, ATTN_WIDTH, n_past), lambda b, t: (layer, b, 0, 0))
    return _CallSpec(
        body=functools.partial(_attn_sample_kernel, n_past=n_past),
        grid=(B, 1),
        args=[z3, z3, z3, cache_kt, cache_vt, *tables],
        in_specs=[new(COL_AQ), new(COL_AK), new(COL_AV), buf, buf] + [_const_spec(t.shape) for t in tables],
        out_shape=[jax.ShapeDtypeStruct((B, TPAD, ATTN_WIDTH), BF16)],
        out_specs=[pl.BlockSpec((None, TPAD, ATTN_WIDTH), lambda b, t: (b, 0, 0))],
        scratch=[])


QUAD = N_MLSTM_HEADS * HEAD_DIM
CHUNK_UNROLL = 2
CHUNK_GROUP = 8
HEAD_SHIFT = 6


def _iota2(shape, axis):
    return lax.broadcasted_iota(jnp.int32, shape, axis)


def _log2(n):
    k = int(n).bit_length() - 1
    assert 1 << k == n
    return k


def _seg_mask(rows, row_shift, cols, col_shift):
    return (_iota2((rows, cols), 0) >> row_shift) == (_iota2((rows, cols), 1) >> col_shift)


def _cumulate_rows(x, length, op, fill):
    row = _iota2(x.shape, 0) & (length - 1)
    sh = 1
    while sh < length:
        x = op(x, jnp.where(row >= sh, pltpu.roll(x, sh, axis=0), fill))
        sh *= 2
    return x


def _split2(x):
    hi = x.astype(BF16)
    lo = (x - hi.astype(F32)).astype(BF16)
    return hi, lo


def _dot2(x, w):
    hi, lo = _split2(x)
    return jnp.dot(hi, w, preferred_element_type=F32) + jnp.dot(lo, w, preferred_element_type=F32)


def _dot2r(w, x):
    hi, lo = _split2(x)
    return jnp.dot(w, hi, preferred_element_type=F32) + jnp.dot(w, lo, preferred_element_type=F32)


def _load_block_diag(ref, seg64):
    flat = ref[...].reshape(QUAD, HEAD_DIM)
    return jnp.where(seg64, jnp.concatenate([flat] * (QUAD // HEAD_DIM), axis=1), 0.0)


def _store_block_diag(ref, mat):
    for h in range(QUAD // HEAD_DIM):
        ref[h] = mat[h * HEAD_DIM:(h + 1) * HEAD_DIM, h * HEAD_DIM:(h + 1) * HEAD_DIM]


def _head_rmsnorm_gate(h, seg64b, g_row, gate_pre):
    ms = _dot2(h * h, seg64b) * (1.0 / HEAD_DIM)
    return jax.nn.sigmoid(gate_pre) * (h * lax.rsqrt(ms + EPS) * g_row)


def _mlstm_kernel(qk_ref, v_ref, o_ref, gate_ref, conv0_ref, c0_ref, n0_ref, m0_ref, gb_ref, cw_ref, ng_ref,
                  out_ref, c_out, n_out, m_out, xp, cs, ns, ms, *, tile, chunk, n_valid):
    TS, L = tile, chunk
    t = pl.program_id(1)
    PAD = SUBLANES

    @pl.when(t == 0)
    def _():
        cs[...] = _load_block_diag(c0_ref, _seg_mask(QUAD, HEAD_SHIFT, QUAD, HEAD_SHIFT))
        ns[...] = n0_ref[...]
        ms[...] = m0_ref[...]
        xp[0:PAD, :] = conv0_ref[...]

    @pl.when(t > 0)
    def _():
        xp[0:PAD, :] = xp[TS:TS + PAD, :]

    xp[PAD:PAD + TS, :] = qk_ref[...]

    LK = MLSTM_CHUNK
    seg64 = _seg_mask(QUAD, HEAD_SHIFT, QUAD, HEAD_SHIFT)
    seg64b = seg64.astype(BF16)
    dmask = (_iota2((LK, QUAD), 1) & (LK - 1)) == _iota2((LK, QUAD), 0)
    causal = (_iota2((L, QUAD), 1) & (LK - 1)) <= _iota2((L, QUAD), 0)
    tril = (_iota2((L, L), 1) <= _iota2((L, L), 0)).astype(BF16)
    ones_lk = jnp.ones((L, LK), BF16)
    row = _iota2((L, QUAD), 0)
    cw = cw_ref[...]
    gb = gb_ref[...]
    ng = ng_ref[...]

    first_half = _iota2((L, LANES), 1) < HEAD_DIM

    def per_head_lanes(g, lane0):
        col = [jnp.broadcast_to(g[:, lane0 + h:lane0 + h + 1], (L, LANES)) for h in range(N_MLSTM_HEADS)]
        return jnp.concatenate([jnp.where(first_half, col[0], col[1]), jnp.where(first_half, col[2], col[3])], axis=1)

    def key_rows(x, fill):
        if L == LK:
            return x
        return jnp.concatenate([x, jnp.full((LK - L, QUAD), fill, x.dtype)], axis=0)

    G = min(CHUNK_GROUP, TS // L)
    assert TS % (L * G) == 0
    nt = (((1,), (1,)), ((), ()))
    tn = (((0,), (0,)), ((), ()))
    zb = jnp.zeros((QUAD, QUAD), BF16)

    def group_body(gi, carry):
        R = range(G)
        r0 = [pl.multiple_of((gi * G + j) * L, L) for j in R]
        q, k, v, ig, lf = [], [], [], [], []
        for j in R:
            win = xp[pl.ds(r0[j], L + PAD), :]
            acc = win[PAD:PAD + L] * cw[CONV_WIDTH - 1:CONV_WIDTH, :]
            for s in range(1, CONV_WIDTH):
                acc = acc + win[PAD - s:PAD - s + L] * cw[CONV_WIDTH - 1 - s:CONV_WIDTH - s, :]
            qk = acc * jax.nn.sigmoid(acc)
            q.append(qk[:, 0:QUAD])
            k.append(qk[:, QUAD:] * (HEAD_DIM ** -0.5))
            v.append(v_ref[pl.ds(r0[j], L), :])
            gate = gate_ref[pl.ds(r0[j], L), :] + gb
            logsig = jnp.minimum(gate, 0.0) - jnp.log(1.0 + jnp.exp(-jnp.abs(gate)))
            ig_j = per_head_lanes(gate, 0)
            lf_j = per_head_lanes(logsig, N_MLSTM_HEADS)
            if n_valid < TS:
                ok = (row + r0[j]) < n_valid
                ig_j = jnp.where(ok, ig_j, NEG_BIG)
                lf_j = jnp.where(ok, lf_j, 0.0)
            ig.append(ig_j)
            lf.append(lf_j)
        b = [_dot2r(tril, lf[j]) for j in R]
        a = [ig[j] - b[j] for j in R]
        cm = [_cumulate_rows(a[j], L, jnp.maximum, -jnp.inf) for j in R]
        arow = [_dot2r(ones_lk, jnp.where(dmask, key_rows(a[j], NEG_BIG), 0.0)) for j in R]
        qb = [q[j].astype(BF16) for j in R]
        vb = [v[j].astype(BF16) for j in R]
        kbd = [jnp.where(seg64, jnp.concatenate([key_rows(k[j], 0.0).astype(BF16)] * N_MLSTM_HEADS, axis=0), zb) for j in R]
        vbd = [jnp.where(seg64, jnp.concatenate([key_rows(v[j], 0.0).astype(BF16)] * N_MLSTM_HEADS, axis=0), zb) for j in R]
        qkt = [lax.dot_general(qb[j], kbd[j], nt, preferred_element_type=F32) for j in R]

        mprev = ms[...]
        M, gg, emt = [], [], []
        for j in R:
            M.append(jnp.maximum(cm[j], mprev))
            mt = b[j] + M[j]
            gg.append(jnp.exp(mprev - M[j]))
            emt.append(jnp.exp(-mt))
            mprev = mt[L - 1:L, :]
        ms[...] = mprev

        wts = [jnp.where(causal, jnp.exp(arow[j] - M[j]), 0.0) * qkt[j] for j in R]
        kd = [k[j] * jnp.exp(a[j] - M[j][L - 1:L, :]) for j in R]
        num = [jnp.dot(wts[j].astype(BF16), vbd[j], preferred_element_type=F32) for j in R]
        u = [lax.dot_general(kd[j].astype(BF16), vb[j], tn, preferred_element_type=F32) for j in R]

        cmat, nvec = [cs[...]], [ns[...]]
        for j in R:
            gl = gg[j][L - 1:L, :]
            cmat.append(gl * cmat[j] + jnp.where(seg64, u[j], 0.0))
            nvec.append(gl * nvec[j] + jnp.sum(kd[j], axis=0, keepdims=True))
        cs[...] = cmat[G]
        ns[...] = nvec[G]

        inter = [jnp.dot(qb[j], cmat[j].astype(BF16), preferred_element_type=F32) for j in R]
        den = [_dot2(wts[j] + gg[j] * (q[j] * nvec[j]), seg64b) for j in R]
        h = [(num[j] + gg[j] * inter[j]) / jnp.maximum(jnp.abs(den[j]), emt[j]) for j in R]
        msq = [_dot2(h[j] * h[j], seg64b) * (1.0 / HEAD_DIM) for j in R]
        for j in R:
            y = h[j] * lax.rsqrt(msq[j] + EPS) * ng
            out_ref[pl.ds(r0[j], L), :] = (jax.nn.sigmoid(o_ref[pl.ds(r0[j], L), :]) * y).astype(out_ref.dtype)
        return carry

    lax.fori_loop(0, TS // (L * G), group_body, 0)

    @pl.when(t == pl.num_programs(1) - 1)
    def _():
        _store_block_diag(c_out, cs[...])
        n_out[...] = ns[...]
        m_out[...] = ms[...]


def _mlstm(z3, conv0, c0bd, n0, m0, gate_b, conv_w, norm_g, n_valid, chunk):
    B, S, _ = z3.shape
    ts = min(512, S)
    assert S % ts == 0 and ts % chunk == 0
    blk = lambda w, c0: pl.BlockSpec((None, ts, w), lambda b, t: (b, t, c0 // w))
    per_b = lambda r, w: pl.BlockSpec((None, r, w), lambda b, t: (b, 0, 0))
    per_head = pl.BlockSpec((None, N_MLSTM_HEADS, HEAD_DIM, HEAD_DIM), lambda b, t: (b, 0, 0, 0))
    return _CallSpec(
        body=functools.partial(_mlstm_kernel, tile=ts, chunk=chunk, n_valid=n_valid),
        grid=(B, S // ts),
        args=[z3, z3, z3, z3, conv0, c0bd, n0, m0, gate_b, conv_w, norm_g],
        in_specs=[blk(2 * QUAD, COL_MQK), blk(QUAD, COL_MV), blk(QUAD, COL_MO), blk(LANES, COL_GATE),
                  per_b(SUBLANES, 2 * QUAD), per_head, per_b(1, QUAD), per_b(1, QUAD),
                  _const_spec((1, LANES)), _const_spec((CONV_WIDTH, 2 * QUAD)), _const_spec((1, QUAD))],
        out_shape=[jax.ShapeDtypeStruct((B, S, QUAD), BF16),
                   jax.ShapeDtypeStruct((B, N_MLSTM_HEADS, HEAD_DIM, HEAD_DIM), F32),
                   jax.ShapeDtypeStruct((B, 1, QUAD), F32),
                   jax.ShapeDtypeStruct((B, 1, QUAD), F32)],
        out_specs=[pl.BlockSpec((None, ts, QUAD), lambda b, t: (b, t, 0)),
                   per_head, per_b(1, QUAD), per_b(1, QUAD)],
        scratch=[pltpu.VMEM((ts + 2 * SUBLANES, 2 * QUAD), F32),
                 pltpu.VMEM((QUAD, QUAD), F32), pltpu.VMEM((1, QUAD), F32), pltpu.VMEM((1, QUAD), F32)])


HGRN_FAST_CHUNK = 64
HGRN_GROUP = 8
HGRN_SAFE_DECAY = 80.0


def _hgrn_kernel(q_ref, f_ref, i_ref, g_ref, s0_ref, lb_ref, ng_ref, out_ref, s_out, st, qs, fs, ks, *,
                 tile, sub, n_valid, fast):
    TS = tile
    t = pl.program_id(1)

    seg64 = _seg_mask(QUAD, HEAD_SHIFT, QUAD, HEAD_SHIFT)
    seg64b = seg64.astype(BF16)

    @pl.when(t == 0)
    def _():
        st[...] = _load_block_diag(s0_ref, seg64)
    lb = lb_ref[...]
    ng = ng_ref[...]
    nt = (((1,), (1,)), ((), ()))
    tn = (((0,), (0,)), ((), ()))

    P = HGRN_FAST_CHUNK if fast else sub
    prow = _iota2((P, QUAD), 0)

    def prep(c, worst):
        r0 = pl.multiple_of(c * P, P)
        hq = q_ref[pl.ds(r0, P), :]
        f = lb + (1.0 - lb) * jax.nn.sigmoid(f_ref[pl.ds(r0, P), :])
        logf = jnp.log(f)
        kk = 1.0 - f
        if n_valid < TS:
            ok = (prow + r0) < n_valid
            logf = jnp.where(ok, logf, 0.0)
            kk = jnp.where(ok, kk, 0.0)
        qs[pl.ds(r0, P), :] = hq * jax.nn.sigmoid(hq)
        fs[pl.ds(r0, P), :] = logf
        ks[pl.ds(r0, P), :] = kk
        return jnp.minimum(worst, jnp.sum(logf, axis=0, keepdims=True))

    worst = lax.fori_loop(0, TS // P, prep, jnp.zeros((1, QUAD), F32))

    def finish(o, r0, rows, smat, b, kk, iv, qt):
        o = o + lax.dot_general(qt, smat.astype(BF16), nt, preferred_element_type=F32)
        out_ref[pl.ds(r0, rows), :] = _head_rmsnorm_gate(o, seg64b, ng, g_ref[pl.ds(r0, rows), :]).astype(out_ref.dtype)
        bl = b[rows - 1:rows, :]
        ktil = kk * jnp.exp(bl - b)
        u = lax.dot_general(iv.astype(BF16), ktil.astype(BF16), tn, preferred_element_type=F32)
        st[...] = smat * jnp.exp(bl) + jnp.where(seg64, u, 0.0)

    def fast_loop():
        L = HGRN_FAST_CHUNK
        tril = (_iota2((L, L), 1) <= _iota2((L, L), 0)).astype(BF16)
        causal = (_iota2((L, QUAD), 1) & (L - 1)) <= _iota2((L, QUAD), 0)
        zb = jnp.zeros((QUAD, QUAD), BF16)

        G = min(HGRN_GROUP, TS // L)
        assert TS % (L * G) == 0

        def body(gi, carry):
            R = range(G)
            r0 = [pl.multiple_of((gi * G + j) * L, L) for j in R]
            kk = [ks[pl.ds(r0[j], L), :] for j in R]
            ivb = [i_ref[pl.ds(r0[j], L), :].astype(BF16) for j in R]
            b = [_dot2r(tril, fs[pl.ds(r0[j], L), :]) for j in R]
            qt = [(qs[pl.ds(r0[j], L), :] * jnp.exp(b[j])).astype(BF16) for j in R]
            kbd = [jnp.where(seg64, jnp.concatenate([(kk[j] * jnp.exp(-b[j])).astype(BF16)] * N_HGRN_HEADS, axis=0), zb)
                   for j in R]
            ibd = [jnp.where(seg64, jnp.concatenate([ivb[j]] * N_HGRN_HEADS, axis=0), zb) for j in R]
            amat = [lax.dot_general(qt[j], kbd[j], nt, preferred_element_type=F32) for j in R]
            ktil = [(kk[j] * jnp.exp(b[j][L - 1:L, :] - b[j])).astype(BF16) for j in R]
            u = [lax.dot_general(ivb[j], ktil[j], tn, preferred_element_type=F32) for j in R]
            o = [jnp.dot(jnp.where(causal, amat[j], 0.0).astype(BF16), ibd[j], preferred_element_type=F32) for j in R]
            smat = [st[...]]
            for j in R:
                smat.append(smat[j] * jnp.exp(b[j][L - 1:L, :]) + jnp.where(seg64, u[j], 0.0))
            st[...] = smat[G]
            inter = [lax.dot_general(qt[j], smat[j].astype(BF16), nt, preferred_element_type=F32) for j in R]
            o = [o[j] + inter[j] for j in R]
            msq = [_dot2(o[j] * o[j], seg64b) * (1.0 / HEAD_DIM) for j in R]
            for j in R:
                y = o[j] * lax.rsqrt(msq[j] + EPS) * ng
                out_ref[pl.ds(r0[j], L), :] = (jax.nn.sigmoid(g_ref[pl.ds(r0[j], L), :]) * y).astype(out_ref.dtype)
            return carry

        lax.fori_loop(0, TS // (L * G), body, 0)

    def exact_loop():
        L = sub
        row = _iota2((L, QUAD), 0)

        def body(c, carry):
            r0 = pl.multiple_of(c * L, L)
            q = qs[pl.ds(r0, L), :]
            kk = ks[pl.ds(r0, L), :]
            iv = i_ref[pl.ds(r0, L), :]
            b = _cumulate_rows(fs[pl.ds(r0, L), :], L, jnp.add, 0.0)
            parts = []
            for j in range(L):
                dec = jnp.exp(jnp.where(row >= j, b - b[j:j + 1, :], NEG_BIG))
                parts.append(dec * q * kk[j:j + 1, :])
            tstack = jnp.concatenate(parts, axis=0).astype(BF16)
            y = jnp.dot(tstack, seg64b, preferred_element_type=F32)
            o = y[0:L] * iv[0:1, :]
            for j in range(1, L):
                o = o + y[j * L:(j + 1) * L] * iv[j:j + 1, :]
            finish(o, r0, L, st[...], b, kk, iv, (q * jnp.exp(b)).astype(BF16))
            return carry

        lax.fori_loop(0, TS // L, body, 0, unroll=min(CHUNK_UNROLL, TS // L))

    if fast:
        safe = jnp.min(worst) > -HGRN_SAFE_DECAY
        pl.when(safe)(fast_loop)
        pl.when(jnp.logical_not(safe))(exact_loop)
    else:
        exact_loop()

    @pl.when(t == pl.num_programs(1) - 1)
    def _():
        _store_block_diag(s_out, st[...])


def _hgrn(z3, s0t, lb, norm_g, n_valid, sub):
    B, S, _ = z3.shape
    ts = min(512, S)
    fast = ts % HGRN_FAST_CHUNK == 0
    assert S % ts == 0 and ts % sub == 0
    blk = lambda c0: pl.BlockSpec((None, ts, QUAD), lambda b, t: (b, t, c0 // QUAD))
    per_b = pl.BlockSpec((None, N_HGRN_HEADS, HEAD_DIM, HEAD_DIM), lambda b, t: (b, 0, 0, 0))
    stage = lambda: pltpu.VMEM((ts, QUAD), F32)
    return _CallSpec(
        body=functools.partial(_hgrn_kernel, tile=ts, sub=sub, n_valid=n_valid, fast=fast),
        grid=(B, S // ts),
        args=[z3, z3, z3, z3, s0t, lb, norm_g],
        in_specs=[blk(COL_HQ), blk(COL_HF), blk(COL_HI), blk(COL_HG), per_b,
                  _const_spec((1, QUAD)), _const_spec((1, QUAD))],
        out_shape=[jax.ShapeDtypeStruct((B, S, QUAD), BF16),
                   jax.ShapeDtypeStruct((B, N_HGRN_HEADS, HEAD_DIM, HEAD_DIM), F32)],
        out_specs=[pl.BlockSpec((None, ts, QUAD), lambda b, t: (b, t, 0)), per_b],
        scratch=[pltpu.VMEM((QUAD, QUAD), F32), stage(), stage(), stage()])


PROJ_SPLITS = (ATTN_WIDTH, ATTN_WIDTH, ATTN_WIDTH,
               MLSTM_WIDTH, MLSTM_WIDTH, MLSTM_WIDTH, MLSTM_WIDTH, N_MLSTM_HEADS, N_MLSTM_HEADS,
               HGRN_WIDTH, HGRN_WIDTH, HGRN_WIDTH, HGRN_WIDTH)


def _prep_w_in(w):
    g0 = int(np.sum(PROJ_SPLITS[:7]))
    g1 = g0 + 2 * N_MLSTM_HEADS
    assert (g0, w.shape[-1] - g1) == (COL_HQ, COL_GATE - COL_HQ)
    wb = w.astype(BF16)
    pad = jnp.zeros(w.shape[:-1] + (LANES - 2 * N_MLSTM_HEADS,), BF16)
    return jnp.concatenate([wb[..., :g0], wb[..., g1:], wb[..., g0:g1], pad], axis=-1)


HGRN_SUBCHUNK = 16


class _LayerWeights(NamedTuple):
    layer: int
    g_mix: jax.Array
    w_in: jax.Array
    gate_b: jax.Array
    conv_w: jax.Array
    mlstm_g: jax.Array
    lb: jax.Array
    hgrn_g: jax.Array
    w_out: jax.Array
    g_ffn: jax.Array
    w_up: jax.Array
    w_down: jax.Array
    g_final: jax.Array


def _gate_row(gate_b):
    flat = gate_b.astype(F32).reshape(1, 2 * N_MLSTM_HEADS)
    return jnp.concatenate([flat, jnp.zeros((1, LANES - 2 * N_MLSTM_HEADS), F32)], axis=1)


def _trunk_layer(x2d, B, S, n_valid, attn_fn, states, w, final, chunk_m, chunk_h, n_keep=None, stacked=None):
    conv_buf, C0, n0, m0, S0 = states
    if n_keep is None:
        z = _inproj(x2d, w.g_mix, w.w_in, w.layer)
    else:
        z, kt, vt = _inproj(x2d, w.g_mix, w.w_in, w.layer, S, n_keep, stacked)
    z3 = z.reshape(B, S, ZW)
    conv0 = jnp.zeros((B, SUBLANES, 2 * MLSTM_WIDTH), F32).at[:, SUBLANES - (CONV_WIDTH - 1):].set(conv_buf.astype(F32))
    m0r = jnp.repeat(m0.astype(F32), HEAD_DIM, axis=1).reshape(B, 1, MLSTM_WIDTH)
    mlstm = _mlstm(z3, conv0, C0.astype(F32), n0.astype(F32).reshape(B, 1, MLSTM_WIDTH), m0r,
                   w.gate_b, w.conv_w, w.mlstm_g, n_valid, chunk_m)
    hgrn = _hgrn(z3, jnp.swapaxes(S0.astype(F32), -1, -2), w.lb, w.hgrn_g, n_valid, chunk_h)
    attn = attn_fn(z3)
    sem = ("parallel", "arbitrary")
    if isinstance(attn, _CallSpec):
        (attn,), (ml, c_new, n, m), (hg, st) = _run([attn, mlstm, hgrn], sem, "sample_mixers")
    else:
        (ml, c_new, n, m), = _run([mlstm], sem, "mlstm")
        (hg, st), = _run([hgrn], sem, "hgrn")
    n2 = B * S
    y = _outffn(x2d, attn.reshape(n2, ATTN_WIDTH), ml.reshape(n2, MLSTM_WIDTH), hg.reshape(n2, HGRN_WIDTH),
                w.w_out, w.g_ffn, w.w_up, w.w_down, w.g_final, final, w.layer)
    if n_keep is None:
        k_rows = z3[:, :n_valid, COL_AK:COL_AK + ATTN_WIDTH].reshape(B, n_valid, N_ATTN_HEADS, HEAD_DIM)
        v_rows = z3[:, :n_valid, COL_AV:COL_AV + ATTN_WIDTH].reshape(B, n_valid, N_ATTN_HEADS, HEAD_DIM)
    else:
        k_rows, v_rows = kt, vt
    conv_new = z3[:, n_valid - (CONV_WIDTH - 1):n_valid, COL_MQK:COL_MQK + 2 * MLSTM_WIDTH]
    s_new = jnp.swapaxes(st, -1, -2)
    return y, (k_rows, v_rows, conv_new, c_new, n.reshape(B, N_MLSTM_HEADS, HEAD_DIM), m[:, 0, ::HEAD_DIM], s_new)


def kernel(x_prompt, x_sample, cache_attn_k, cache_attn_v, state_mlstm_conv, state_mlstm_C, state_mlstm_n, state_mlstm_m, state_hgrn_S, rel_bias, g_mix, w_in, mlstm_gate_b, mlstm_conv_w, mlstm_norm_g, hgrn_lb_raw, hgrn_norm_g, w_out, g_ffn, w_up, w_down, g_final):
    depth = w_in.shape[0]
    bp, sp, _ = x_prompt.shape
    bs, ts, _ = x_sample.shape
    n_keep = min(max(w for w, _ in DILATED_PATTERNS), sp)
    n_past = cache_attn_k.shape[2]
    lb_p = jax.nn.softmax(hgrn_lb_raw.astype(F32), axis=0)
    hgrn_lb = jnp.cumsum(lb_p, axis=0) - lb_p[0]
    bias_dist = _bias_by_distance(rel_bias)
    p_tables = _prompt_tables(bias_dist)
    s_tables = _sample_tables(bias_dist, n_past, ts)
    cache_kt = jnp.transpose(cache_attn_k, (0, 1, 3, 4, 2)).reshape(depth, bs, ATTN_WIDTH, n_past)
    cache_vt = jnp.transpose(cache_attn_v, (0, 1, 3, 4, 2)).reshape(depth, bs, ATTN_WIDTH, n_past)

    hp = x_prompt.reshape(bp * sp, D_MODEL)
    hs = jnp.zeros((bs, TPAD, D_MODEL), F32).at[:, :ts].set(x_sample).reshape(bs * TPAD, D_MODEL)
    zero_states = (jnp.zeros((bp, CONV_WIDTH - 1, 2 * MLSTM_WIDTH), F32),
                   jnp.zeros((bp, N_MLSTM_HEADS, HEAD_DIM, HEAD_DIM), F32),
                   jnp.zeros((bp, N_MLSTM_HEADS, HEAD_DIM), F32),
                   jnp.zeros((bp, N_MLSTM_HEADS), F32),
                   jnp.zeros((bp, N_HGRN_HEADS, HEAD_DIM, HEAD_DIM), F32))
    p_states, s_states = [], []
    kv_t = None
    w_in_b, w_out_b, w_up_b, w_down_b = _prep_w_in(w_in), w_out.astype(BF16), w_up.astype(BF16), w_down.astype(BF16)
    for l in range(depth):
        final = l == depth - 1
        weights = _LayerWeights(
            layer=l, g_mix=g_mix[l][None], w_in=w_in_b,
            gate_b=_gate_row(mlstm_gate_b[l]),
            conv_w=mlstm_conv_w[l], mlstm_g=mlstm_norm_g[l][None], lb=hgrn_lb[l][None], hgrn_g=hgrn_norm_g[l][None],
            w_out=w_out_b, g_ffn=g_ffn[l][None], w_up=w_up_b, w_down=w_down_b, g_final=g_final[None])
        hp, st = _trunk_layer(hp, bp, sp, sp, functools.partial(_attn_prompt, bias=p_tables),
                              zero_states, weights, final, MLSTM_CHUNK, HGRN_SUBCHUNK, n_keep, (l, depth, kv_t))
        kv_t = st[:2]
        p_states.append(st[2:])
        sample_attn = functools.partial(_attn_sample, cache_kt=cache_kt, cache_vt=cache_vt, layer=l, tables=s_tables)
        states = (state_mlstm_conv[l], state_mlstm_C[l], state_mlstm_n[l], state_mlstm_m[l], state_hgrn_S[l])
        hs, st = _trunk_layer(hs, bs, TPAD, ts, sample_attn, states, weights, final, TPAD, TPAD)
        s_states.append(st)
    y_prompt = hp.reshape(bp, sp, D_MODEL)
    y_sample = hs.reshape(bs, TPAD, D_MODEL)[:, :ts]
    to_rows = lambda t: jnp.transpose(t.reshape(depth, bp, N_ATTN_HEADS, HEAD_DIM, n_keep), (0, 1, 4, 2, 3))
    p_out = [to_rows(kv_t[0]), to_rows(kv_t[1])] + [jnp.stack(a) for a in zip(*p_states)]
    s_out = [jnp.stack(a) for a in zip(*s_states)]
    return (y_prompt, y_sample, *p_out, *s_out)
```

```python
import functools
from typing import Callable, NamedTuple

import jax
import jax.numpy as jnp
import numpy as np
from jax import lax
from jax.experimental import pallas as pl
from jax.experimental.pallas import tpu as pltpu

F32 = jnp.float32
BF16 = jnp.bfloat16

D_MODEL = 1024
HEAD_DIM = 64
N_ATTN_HEADS = 8
N_MLSTM_HEADS = 4
N_HGRN_HEADS = 4
ATTN_WIDTH = N_ATTN_HEADS * HEAD_DIM
MLSTM_WIDTH = N_MLSTM_HEADS * HEAD_DIM
HGRN_WIDTH = N_HGRN_HEADS * HEAD_DIM
DILATED_PATTERNS = ((128, 1), (512, 4), (2048, 16))
QBLOCK = 128
N_REL_BUCKETS = 32
REL_MAX_DISTANCE = 2048
CONV_WIDTH = 4
MLSTM_CHUNK = 64
D_FF = 4 * D_MODEL
EPS = 1e-6
NEG_BIG = -1e30
LOG2E = 1.4426950408889634

LANES = 128
SUBLANES = 8

COL_AQ = 0
COL_AK = COL_AQ + ATTN_WIDTH
COL_AV = COL_AK + ATTN_WIDTH
COL_MQK = COL_AV + ATTN_WIDTH
COL_MV = COL_MQK + 2 * MLSTM_WIDTH
COL_MO = COL_MV + MLSTM_WIDTH
COL_HQ = COL_MO + MLSTM_WIDTH
COL_HF = COL_HQ + HGRN_WIDTH
COL_HI = COL_HF + HGRN_WIDTH
COL_HG = COL_HI + HGRN_WIDTH
COL_GATE = COL_HG + HGRN_WIDTH
ZW = COL_GATE + LANES

VMEM_LIMIT = 56 * 1024 * 1024


def _cparams(sem, vmem=VMEM_LIMIT):
    return pltpu.CompilerParams(dimension_semantics=sem, vmem_limit_bytes=vmem)


def _const_spec(shape):
    nd = len(shape)
    return pl.BlockSpec(shape, lambda *_: (0,) * nd, pipeline_mode=pl.Buffered(1))


def _layer_spec(shape, layer):
    nd = len(shape)
    return pl.BlockSpec((None,) + tuple(shape), lambda *_: (layer,) + (0,) * nd, pipeline_mode=pl.Buffered(1))


class _CallSpec(NamedTuple):
    body: Callable
    grid: tuple
    args: list
    in_specs: list
    out_shape: list
    out_specs: list
    scratch: list


def _run(specs, sem, name):
    grid = specs[0].grid
    assert all(s.grid == grid for s in specs)
    n_in = [len(s.args) for s in specs]
    n_out = [len(s.out_shape) for s in specs]
    n_scr = [len(s.scratch) for s in specs]

    def body(*refs):
        ins, outs, scrs = refs[:sum(n_in)], refs[sum(n_in):sum(n_in) + sum(n_out)], refs[sum(n_in) + sum(n_out):]

        for k, s in enumerate(specs):
            a, b, c = sum(n_in[:k]), sum(n_out[:k]), sum(n_scr[:k])
            s.body(*ins[a:a + n_in[k]], *outs[b:b + n_out[k]], *scrs[c:c + n_scr[k]])

    flat = pl.pallas_call(
        body, grid=grid,
        in_specs=[x for s in specs for x in s.in_specs],
        out_shape=[x for s in specs for x in s.out_shape],
        out_specs=[x for s in specs for x in s.out_specs],
        scratch_shapes=[x for s in specs for x in s.scratch],
        compiler_params=_cparams(sem), name=name,
    )(*[x for s in specs for x in s.args])
    out, k = [], 0
    for n in n_out:
        out.append(list(flat[k:k + n]))
        k += n
    return out


def _inproj_kernel(x_ref, g_ref, wa_ref, wb_ref, *rest, tiles_per_seq, first_keep, n_prev=0):
    x = x_ref[...]
    ms = jnp.mean(x * x, axis=-1, keepdims=True)
    xn = ((x * lax.rsqrt(ms + EPS)) * g_ref[...]).astype(BF16)
    z_ref = rest[n_prev]
    z_ref[:, 0:COL_HQ] = jnp.dot(xn, wa_ref[...], preferred_element_type=F32)
    z_ref[:, COL_HQ:] = jnp.dot(xn, wb_ref[...], preferred_element_type=F32)
    if first_keep is None:
        return
    kt_ref, vt_ref = rest[n_prev + 1:]

    @pl.when(pl.program_id(0) % tiles_per_seq >= first_keep)
    def _():
        kt_ref[...] = z_ref[:, COL_AK:COL_AK + ATTN_WIDTH].T
        vt_ref[...] = z_ref[:, COL_AV:COL_AV + ATTN_WIDTH].T


def _inproj(x2d, g, w, layer, seq=None, n_keep=None, stacked=None):
    n = x2d.shape[0]
    tm = min(512, n)
    assert n % tm == 0
    x_spec = pl.BlockSpec((tm, D_MODEL), lambda i: (i, 0))
    z_spec = pl.BlockSpec((tm, ZW), lambda i: (i, 0))
    z_shape = jax.ShapeDtypeStruct((n, ZW), F32)
    w_specs = [_layer_spec((D_MODEL, COL_HQ), layer), _layer_spec((D_MODEL, ZW - COL_HQ), layer)]
    if stacked is None:
        return pl.pallas_call(
            functools.partial(_inproj_kernel, tiles_per_seq=None, first_keep=None),
            out_shape=z_shape, grid=(n // tm,),
            in_specs=[x_spec, _const_spec((1, D_MODEL))] + w_specs,
            out_specs=z_spec, compiler_params=_cparams(("parallel",)), name="inproj",
        )(x2d, g, *w)
    assert seq % tm == 0 and n_keep % tm == 0
    layer, depth, prev = stacked
    tps, first_keep = seq // tm, (seq - n_keep) // tm
    t_spec = pl.BlockSpec((None, None, ATTN_WIDTH, tm),
                          lambda i: (layer, i // tps, 0, jnp.maximum(i % tps - first_keep, 0)))
    t_shape = jax.ShapeDtypeStruct((depth, n // seq, ATTN_WIDTH, n_keep), F32)
    in_specs = [x_spec, _const_spec((1, D_MODEL))] + w_specs
    args = [x2d, g, *w]
    if prev is None:
        prev = (jnp.zeros(t_shape.shape, F32), jnp.zeros(t_shape.shape, F32))
    in_specs += [pl.BlockSpec(memory_space=pl.ANY)] * 2
    aliases = {len(args): 1, len(args) + 1: 2}
    args += list(prev)
    return pl.pallas_call(
        functools.partial(_inproj_kernel, tiles_per_seq=tps, first_keep=first_keep, n_prev=len(prev)),
        out_shape=[z_shape, t_shape, t_shape], grid=(n // tm,),
        in_specs=in_specs, out_specs=[z_spec, t_spec, t_spec], input_output_aliases=aliases,
        compiler_params=_cparams(("arbitrary",)), name="inproj_kt",
    )(*args)


FF_CHUNK = 1024
FFN_ROWS = 512


def _outffn_kernel(x_ref, a_ref, m_ref, h_ref, wo_ref, gf_ref, wu_ref, wd_ref, gl_ref, y_ref, xn_sc, *, final):
    x1 = x_ref[...]
    x1 = x1 + jnp.dot(a_ref[...], wo_ref[0:ATTN_WIDTH, :], preferred_element_type=F32)
    x1 = x1 + jnp.dot(m_ref[...], wo_ref[ATTN_WIDTH:ATTN_WIDTH + MLSTM_WIDTH, :], preferred_element_type=F32)
    x1 = x1 + jnp.dot(h_ref[...], wo_ref[ATTN_WIDTH + MLSTM_WIDTH:, :], preferred_element_type=F32)
    ms = jnp.mean(x1 * x1, axis=-1, keepdims=True)
    xn_sc[...] = ((x1 * lax.rsqrt(ms + EPS)) * gf_ref[...]).astype(BF16)
    y_ref[...] = x1
    nchunk = D_FF // FF_CHUNK
    up = lambda c: jnp.dot(xn_sc[...], wu_ref[:, c * FF_CHUNK:(c + 1) * FF_CHUNK], preferred_element_type=F32)
    u = up(0)
    for c in range(nchunk):
        u_next = up(c + 1) if c + 1 < nchunk else None
        hh = jnp.square(jnp.maximum(u, 0.0)).astype(BF16)
        y_ref[...] += jnp.dot(hh, wd_ref[c * FF_CHUNK:(c + 1) * FF_CHUNK, :], preferred_element_type=F32)
        u = u_next
    if final:
        x2 = y_ref[...]
        ms2 = jnp.mean(x2 * x2, axis=-1, keepdims=True)
        y_ref[...] = (x2 * lax.rsqrt(ms2 + EPS)) * gl_ref[...]


def _outffn(x2d, attn, ml, hg, w_out, g_ffn, w_up, w_down, g_final, final, layer):
    n = x2d.shape[0]
    tm = min(FFN_ROWS, n)
    assert n % tm == 0
    row = lambda w: pl.BlockSpec((tm, w), lambda i: (i, 0))
    return pl.pallas_call(
        functools.partial(_outffn_kernel, final=final),
        out_shape=jax.ShapeDtypeStruct((n, D_MODEL), F32),
        grid=(n // tm,),
        in_specs=[row(D_MODEL), row(ATTN_WIDTH), row(MLSTM_WIDTH), row(HGRN_WIDTH),
                  _layer_spec((D_MODEL, D_MODEL), layer), _const_spec((1, D_MODEL)),
                  _layer_spec((D_MODEL, D_FF), layer), _layer_spec((D_FF, D_MODEL), layer), _const_spec((1, D_MODEL))],
        out_specs=row(D_MODEL),
        scratch_shapes=[pltpu.VMEM((tm, D_MODEL), BF16)],
        compiler_params=_cparams(("parallel",)),
        name="outffn",
    )(x2d, attn, ml, hg, w_out, g_ffn, w_up, w_down, g_final)


def _t5_causal_bucket(dist):
    n = np.asarray(dist).astype(np.int32)
    max_exact = N_REL_BUCKETS // 2
    scaled = np.log(np.maximum(n, 1) / max_exact) / np.log(REL_MAX_DISTANCE / max_exact)
    large = np.minimum(max_exact + (scaled * (N_REL_BUCKETS - max_exact)).astype(np.int32), N_REL_BUCKETS - 1)
    return np.where(n < max_exact, n, large).astype(np.int32)


BIAS_DIST = 2304


def _bias_by_distance(rel_bias):
    assert BIAS_DIST > max(w for w, _ in DILATED_PATTERNS) + SUBLANES
    return rel_bias.astype(F32)[_t5_causal_bucket(np.arange(BIAS_DIST)[::-1])].T


def _dist_slice(bias_desc, lo, hi, step=1):
    last = BIAS_DIST - 1
    return bias_desc[:, last - hi:last - lo + 1:step]


def _prompt_tables(bias_dist):
    H = N_ATTN_HEADS
    a = np.arange(QBLOCK)[:, None]
    b = np.arange(2 * QBLOCK)[None, :]
    rel = QBLOCK + a - b
    band = (rel >= 0) & (rel <= QBLOCK)
    period = 3 * QBLOCK
    biases = []
    for window, dil in DILATED_PATTERNS:
        nsub = window // dil
        assert nsub == QBLOCK
        vd = _dist_slice(bias_dist, 0, nsub * dil, dil)
        rp = jnp.concatenate([vd, jnp.broadcast_to(vd[:, -1:], (H, QBLOCK)),
                              jnp.broadcast_to(vd[:, 0:1], (H, QBLOCK - 1))], axis=1)
        skew = jnp.tile(rp, (1, QBLOCK))[:, :QBLOCK * (period - 1)].reshape(H, QBLOCK, period - 1)
        biases.append(skew[:, :, :2 * QBLOCK].reshape(H // 2, 2 * QBLOCK, 2 * QBLOCK))
    mask = np.stack([np.tile(band, (2, 1)), np.tile(band & (b >= QBLOCK), (2, 1))])
    table = jnp.where(mask[None, None], jnp.stack(biases)[:, :, None] * LOG2E, -jnp.inf)
    return jnp.swapaxes(table, -1, -2)


ATTN_GROUP = 8


def _attn_prompt_kernel(q_ref, k_ref, v_ref, bias_ref, out_ref,
                        x4, qs, ks, vts, o1, o4, o16, l1, l4, l16, p_scr, st_scr, *, seq):
    S = seq
    nblk = S // QBLOCK
    scale = HEAD_DIM ** -0.5 * LOG2E
    zpad = jnp.zeros((QBLOCK, LANES), BF16)
    for di in range(3):
        ks[di, 0:QBLOCK, :] = zpad
        vts[di, 0] = zpad

    R = 512

    def put(src, di, row0, blk):
        if src is q_ref:
            qs[di, row0:row0 + blk.shape[0], :] = (blk * scale).astype(BF16)
        elif src is k_ref:
            ks[di, QBLOCK + row0:QBLOCK + row0 + blk.shape[0], :] = blk.astype(BF16)
        else:
            blk_t = blk.T.astype(BF16)
            for t in range(blk.shape[0] // QBLOCK):
                vts[di, 1 + row0 // QBLOCK + t] = blk_t[:, t * QBLOCK:(t + 1) * QBLOCK]

    for src in (q_ref, k_ref, v_ref):
        for i in range(S // R):
            put(src, 0, i * R, src[i * R:(i + 1) * R, :])
        for r in range(4):
            for i in range(S // 4 // R):
                blk = src[pl.ds(r + 4 * i * R, R, stride=4), :]
                row0 = r * (S // 4) + i * R
                x4[row0:row0 + R, :] = blk
                put(src, 1, row0, blk)
        n16 = S // 16
        for c4 in range(4):
            for a in range(4):
                put(src, 2, (c4 * 4 + a) * n16, x4[pl.ds(c4 * (S // 4) + a, n16, stride=4), :])

    lane = lax.broadcasted_iota(jnp.int32, (QBLOCK, LANES), 1)
    head_a = lane < HEAD_DIM
    row_a = lax.broadcasted_iota(jnp.int32, (QBLOCK, LANES), 0) < HEAD_DIM
    nt = (((1,), (1,)), ((), ()))

    def branch(di, dil, o_ref, l_ref):
        per_class = (S // dil) // QBLOCK

        R = range(ATTN_GROUP)
        zero = jnp.zeros((QBLOCK, LANES), BF16)

        def scores(g, js):
            out = {}
            for j in js:
                i = g * ATTN_GROUP + j
                base = pl.multiple_of(i * QBLOCK, QBLOCK)
                q = qs[di, pl.ds(base, QBLOCK), :]
                kk = ks[di, pl.ds(base, 2 * QBLOCK), :]
                q2 = jnp.concatenate([jnp.where(head_a, q, zero), jnp.where(head_a, zero, q)], axis=0)
                out[j] = lax.dot_general(kk, q2, nt, preferred_element_type=F32)
            return out

        def softmax(g, slot, st):
            for j in st:
                i = g * ATTN_GROUP + j
                s = st[j] + bias_ref[di, jnp.where(i % per_class == 0, 1, 0)]
                m = jnp.max(s, axis=0, keepdims=True)
                p = jnp.exp2(s - m)
                p_scr[slot, j] = p.astype(BF16)
                st_scr[slot, j, 0:1, :] = m
                st_scr[slot, j, 1:2, :] = jnp.sum(p, axis=0, keepdims=True)

        def weighted_values(g, slot, js):
            out = {}
            for j in js:
                i = g * ATTN_GROUP + j
                vt = jnp.concatenate([vts[di, i], vts[di, i + 1]], axis=1)
                ot = jnp.dot(vt, p_scr[slot, j], preferred_element_type=F32)
                out[j] = (ot, st_scr[slot, j, 0:1, :], st_scr[slot, j, 1:2, :])
            return out

        def outputs(g, pv):
            for j in pv:
                i = g * ATTN_GROUP + j
                base = pl.multiple_of(i * QBLOCK, QBLOCK)
                ot, m, l = pv[j]
                otn = ot / l
                lse = m + jnp.log2(l)
                o = jnp.where(row_a, otn[:, 0:QBLOCK], otn[:, QBLOCK:]).T
                ls = jnp.where(row_a, jnp.broadcast_to(lse[:, 0:QBLOCK], (QBLOCK, LANES)),
                               jnp.broadcast_to(lse[:, QBLOCK:], (QBLOCK, LANES))).T
                if dil == 1:
                    o_ref[pl.ds(base, QBLOCK), :] = o
                    l_ref[pl.ds(base, QBLOCK), :] = ls
                else:
                    c, n = i // per_class, i % per_class
                    res = c if dil == 4 else 4 * (c % 4) + c // 4
                    start = dil * QBLOCK * n + res
                    o_ref[pl.ds(start, QBLOCK, stride=dil), :] = o
                    l_ref[pl.ds(start, QBLOCK, stride=dil), :] = ls

        ngroup = nblk // ATTN_GROUP
        softmax(0, 0, scores(0, R))

        def body(g, carry):
            slot = g % 2
            st = scores(g, R)
            pv = weighted_values(g - 1, 1 - slot, R)
            softmax(g, slot, st)
            outputs(g - 1, pv)
            return carry

        lax.fori_loop(1, ngroup, body, 0)
        outputs(ngroup - 1, weighted_values(ngroup - 1, (ngroup - 1) % 2, R))

    branch(0, 1, o1, l1)
    branch(1, 4, o4, l4)
    branch(2, 16, o16, l16)

    T = 256

    def mix(i, carry):
        r0 = pl.multiple_of(i * T, T)
        la, lb, lc = l1[pl.ds(r0, T), :], l4[pl.ds(r0, T), :], l16[pl.ds(r0, T), :]
        mx = jnp.maximum(jnp.maximum(la, lb), lc)
        ea, eb, ec = jnp.exp2(la - mx), jnp.exp2(lb - mx), jnp.exp2(lc - mx)
        num = ea * o1[pl.ds(r0, T), :] + eb * o4[pl.ds(r0, T), :] + ec * o16[pl.ds(r0, T), :]
        out_ref[pl.ds(r0, T), :] = (num / (ea + eb + ec)).astype(out_ref.dtype)
        return carry

    lax.fori_loop(0, S // T, mix, 0)


def _attn_prompt(z3, bias):
    B, S, _ = z3.shape
    assert S % (16 * QBLOCK) == 0
    npair = N_ATTN_HEADS // 2
    col = lambda c0: pl.BlockSpec((None, S, LANES), lambda b, p: (b, 0, c0 // LANES + p))
    f32s = lambda: pltpu.VMEM((S, LANES), F32)
    return pl.pallas_call(
        functools.partial(_attn_prompt_kernel, seq=S),
        out_shape=jax.ShapeDtypeStruct((B, S, ATTN_WIDTH), BF16),
        grid=(B, npair),
        in_specs=[col(COL_AQ), col(COL_AK), col(COL_AV),
                  pl.BlockSpec((3, None, 2, 2 * QBLOCK, 2 * QBLOCK), lambda b, p: (0, p, 0, 0, 0))],
        out_specs=pl.BlockSpec((None, S, LANES), lambda b, p: (b, 0, p)),
        scratch_shapes=[f32s(),
                        pltpu.VMEM((3, S, LANES), BF16),
                        pltpu.VMEM((3, S + QBLOCK, LANES), BF16),
                        pltpu.VMEM((3, S // QBLOCK + 1, LANES, QBLOCK), BF16),
                        f32s(), f32s(), f32s(), f32s(), f32s(), f32s(),
                        pltpu.VMEM((2, ATTN_GROUP, 2 * QBLOCK, 2 * QBLOCK), BF16),
                        pltpu.VMEM((2, ATTN_GROUP, SUBLANES, 2 * QBLOCK), F32)],
        compiler_params=_cparams(("parallel", "parallel")),
        name="attn_prompt",
    )(z3, z3, z3, bias)


TPAD = SUBLANES
TAIL = 512


def _sample_tables(bias_dist, n_past, n_tok):
    assert n_past >= max(w for w, _ in DILATED_PATTERNS) and n_tok <= TPAD and TAIL >= DILATED_PATTERNS[1][0]
    H = N_ATTN_HEADS
    t = np.arange(TPAD)[:, None]

    def by_row(width):
        rows = [_dist_slice(bias_dist, tt + 1, tt + width) for tt in range(TPAD)]
        return jnp.stack(rows, axis=1).reshape(H * TPAD, width)

    bias_tail, bias_full = by_row(TAIL), by_row(n_past)
    zero_dist = _dist_slice(bias_dist, 0, 0)
    new_rows = [jnp.concatenate([_dist_slice(bias_dist, 0, tt),
                                 jnp.broadcast_to(zero_dist, (H, TPAD - tt - 1))], axis=1) for tt in range(TPAD)]
    bias_new = jnp.stack(new_rows, axis=1).reshape(H * TPAD, TPAD)

    def valid(dj, dil, nsub, lo):
        ok = (dj % dil == 0) & (dj // dil >= lo) & (dj // dil <= nsub) & (t < n_tok)
        return np.tile(ok, (H, 1)).astype(np.float32)

    tn = np.arange(TPAD)[None, :]
    m_tail, m_new = [], []
    for window, dil in DILATED_PATTERNS:
        nsub = window // dil
        if window <= TAIL:
            m_tail.append(valid(TAIL + t - np.arange(TAIL)[None, :], dil, nsub, 1))
        else:
            m_full = valid(n_past + t - np.arange(n_past)[None, :], dil, nsub, 1)
        m_new.append(valid(t - tn, dil, nsub, 0) * (tn < n_tok))
    return (bias_tail, bias_full, bias_new, jnp.asarray(np.stack(m_tail)), jnp.asarray(m_full),
            jnp.asarray(np.stack(m_new).astype(np.float32)))


def _attn_sample_kernel(q_ref, kn_ref, vn_ref, kt_ref, vt_ref, bt_ref, bf_ref, bn_ref, mt_ref, mf_ref, mn_ref,
                        out_ref, *, n_past):
    H = N_ATTN_HEADS
    rows = H * TPAD
    same_head = (_iota2((rows, ATTN_WIDTH), 0) >> 3) == (_iota2((rows, ATTN_WIDTH), 1) >> HEAD_SHIFT)
    q = q_ref[...] * (HEAD_DIM ** -0.5)
    qm = jnp.where(same_head, jnp.concatenate([q] * H, axis=0), 0.0).astype(BF16)
    nt = (((1,), (1,)), ((), ()))
    kn = kn_ref[...].astype(BF16)
    vn = vn_ref[...].astype(BF16)
    kt = kt_ref[...].astype(BF16)
    vt = vt_ref[...].astype(BF16)
    s_new = lax.dot_general(qm, kn, nt, preferred_element_type=F32)
    s_full = jnp.dot(qm, kt, preferred_element_type=F32)
    s_tail = s_full[:, n_past - TAIL:]

    outs, lses = [], []
    for di, (window, _) in enumerate(DILATED_PATTERNS):
        if window <= TAIL:
            sm = jnp.where(mt_ref[di] > 0.5, s_tail + bt_ref[...], NEG_BIG)
            vmain = vt[:, n_past - TAIL:]
        else:
            sm = jnp.where(mf_ref[...] > 0.5, s_full + bf_ref[...], NEG_BIG)
            vmain = vt
        sn = jnp.where(mn_ref[di] > 0.5, s_new + bn_ref[...], NEG_BIG)
        m = jnp.maximum(jnp.max(sm, axis=-1, keepdims=True), jnp.max(sn, axis=-1, keepdims=True))
        pm = jnp.exp(sm - m)
        pn = jnp.exp(sn - m)
        l = jnp.sum(pm, axis=-1, keepdims=True) + jnp.sum(pn, axis=-1, keepdims=True)
        o = jnp.dot(pn.astype(BF16), vn, preferred_element_type=F32)
        o = o + lax.dot_general(pm.astype(BF16), vmain, nt, preferred_element_type=F32)
        outs.append(o / l)
        lses.append(m + jnp.log(l))
    mx = jnp.maximum(jnp.maximum(lses[0], lses[1]), lses[2])
    es = [jnp.exp(ls - mx) for ls in lses]
    mixed = (es[0] * outs[0] + es[1] * outs[1] + es[2] * outs[2]) / (es[0] + es[1] + es[2])
    mixed = jnp.where(same_head, mixed, 0.0)
    acc = mixed[0:TPAD]
    for h in range(1, H):
        acc = acc + mixed[h * TPAD:(h + 1) * TPAD]
    out_ref[...] = acc.astype(out_ref.dtype)


def _attn_sample(z3, cache_kt, cache_vt, layer, tables):
    B, T, _ = z3.shape
    assert T == TPAD
    n_past = cache_kt.shape[-1]
    new = lambda c0: pl.BlockSpec((None, TPAD, ATTN_WIDTH), lambda b, t: (b, 0, c0 // ATTN_WIDTH))
    buf = pl.BlockSpec((None, None, ATTN_WIDTH, n_past), lambda b, t: (layer, b, 0, 0))
    return _CallSpec(
        body=functools.partial(_attn_sample_kernel, n_past=n_past),
        grid=(B, 1),
        args=[z3, z3, z3, cache_kt, cache_vt, *tables],
        in_specs=[new(COL_AQ), new(COL_AK), new(COL_AV), buf, buf] + [_const_spec(t.shape) for t in tables],
        out_shape=[jax.ShapeDtypeStruct((B, TPAD, ATTN_WIDTH), BF16)],
        out_specs=[pl.BlockSpec((None, TPAD, ATTN_WIDTH), lambda b, t: (b, 0, 0))],
        scratch=[])


QUAD = N_MLSTM_HEADS * HEAD_DIM
CHUNK_UNROLL = 2
CHUNK_GROUP = 8
HEAD_SHIFT = 6


def _iota2(shape, axis):
    return lax.broadcasted_iota(jnp.int32, shape, axis)


def _log2(n):
    k = int(n).bit_length() - 1
    assert 1 << k == n
    return k


def _seg_mask(rows, row_shift, cols, col_shift):
    return (_iota2((rows, cols), 0) >> row_shift) == (_iota2((rows, cols), 1) >> col_shift)


def _cumulate_rows(x, length, op, fill):
    row = _iota2(x.shape, 0) & (length - 1)
    sh = 1
    while sh < length:
        x = op(x, jnp.where(row >= sh, pltpu.roll(x, sh, axis=0), fill))
        sh *= 2
    return x


def _split2(x):
    hi = x.astype(BF16)
    lo = (x - hi.astype(F32)).astype(BF16)
    return hi, lo


def _dot2(x, w):
    hi, lo = _split2(x)
    return jnp.dot(hi, w, preferred_element_type=F32) + jnp.dot(lo, w, preferred_element_type=F32)


def _dot2r(w, x):
    hi, lo = _split2(x)
    return jnp.dot(w, hi, preferred_element_type=F32) + jnp.dot(w, lo, preferred_element_type=F32)


def _load_block_diag(ref, seg64):
    flat = ref[...].reshape(QUAD, HEAD_DIM)
    return jnp.where(seg64, jnp.concatenate([flat] * (QUAD // HEAD_DIM), axis=1), 0.0)


def _store_block_diag(ref, mat):
    for h in range(QUAD // HEAD_DIM):
        ref[h] = mat[h * HEAD_DIM:(h + 1) * HEAD_DIM, h * HEAD_DIM:(h + 1) * HEAD_DIM]


def _head_rmsnorm_gate(h, seg64b, g_row, gate_pre):
    ms = _dot2(h * h, seg64b) * (1.0 / HEAD_DIM)
    return jax.nn.sigmoid(gate_pre) * (h * lax.rsqrt(ms + EPS) * g_row)


def _mlstm_kernel(qk_ref, v_ref, o_ref, gate_ref, conv0_ref, c0_ref, n0_ref, m0_ref, gb_ref, cw_ref, ng_ref,
                  out_ref, c_out, n_out, m_out, xp, cs, ns, ms, *, tile, chunk, n_valid):
    TS, L = tile, chunk
    t = pl.program_id(1)
    PAD = SUBLANES

    @pl.when(t == 0)
    def _():
        cs[...] = _load_block_diag(c0_ref, _seg_mask(QUAD, HEAD_SHIFT, QUAD, HEAD_SHIFT))
        ns[...] = n0_ref[...]
        ms[...] = m0_ref[...]
        xp[0:PAD, :] = conv0_ref[...]

    @pl.when(t > 0)
    def _():
        xp[0:PAD, :] = xp[TS:TS + PAD, :]

    xp[PAD:PAD + TS, :] = qk_ref[...]

    LK = MLSTM_CHUNK
    seg64 = _seg_mask(QUAD, HEAD_SHIFT, QUAD, HEAD_SHIFT)
    seg64b = seg64.astype(BF16)
    dmask = (_iota2((LK, QUAD), 1) & (LK - 1)) == _iota2((LK, QUAD), 0)
    causal = (_iota2((L, QUAD), 1) & (LK - 1)) <= _iota2((L, QUAD), 0)
    tril = (_iota2((L, L), 1) <= _iota2((L, L), 0)).astype(BF16)
    ones_lk = jnp.ones((L, LK), BF16)
    row = _iota2((L, QUAD), 0)
    cw = cw_ref[...]
    gb = gb_ref[...]
    ng = ng_ref[...]

    first_half = _iota2((L, LANES), 1) < HEAD_DIM

    def per_head_lanes(g, lane0):
        col = [jnp.broadcast_to(g[:, lane0 + h:lane0 + h + 1], (L, LANES)) for h in range(N_MLSTM_HEADS)]
        return jnp.concatenate([jnp.where(first_half, col[0], col[1]), jnp.where(first_half, col[2], col[3])], axis=1)

    def key_rows(x, fill):
        if L == LK:
            return x
        return jnp.concatenate([x, jnp.full((LK - L, QUAD), fill, x.dtype)], axis=0)

    G = min(CHUNK_GROUP, TS // L)
    assert TS % (L * G) == 0
    nt = (((1,), (1,)), ((), ()))
    tn = (((0,), (0,)), ((), ()))
    zb = jnp.zeros((QUAD, QUAD), BF16)

    def group_body(gi, carry):
        R = range(G)
        r0 = [pl.multiple_of((gi * G + j) * L, L) for j in R]
        q, k, v, ig, lf = [], [], [], [], []
        for j in R:
            win = xp[pl.ds(r0[j], L + PAD), :]
            acc = win[PAD:PAD + L] * cw[CONV_WIDTH - 1:CONV_WIDTH, :]
            for s in range(1, CONV_WIDTH):
                acc = acc + win[PAD - s:PAD - s + L] * cw[CONV_WIDTH - 1 - s:CONV_WIDTH - s, :]
            qk = acc * jax.nn.sigmoid(acc)
            q.append(qk[:, 0:QUAD])
            k.append(qk[:, QUAD:] * (HEAD_DIM ** -0.5))
            v.append(v_ref[pl.ds(r0[j], L), :])
            gate = gate_ref[pl.ds(r0[j], L), :] + gb
            logsig = jnp.minimum(gate, 0.0) - jnp.log(1.0 + jnp.exp(-jnp.abs(gate)))
            ig_j = per_head_lanes(gate, 0)
            lf_j = per_head_lanes(logsig, N_MLSTM_HEADS)
            if n_valid < TS:
                ok = (row + r0[j]) < n_valid
                ig_j = jnp.where(ok, ig_j, NEG_BIG)
                lf_j = jnp.where(ok, lf_j, 0.0)
            ig.append(ig_j)
            lf.append(lf_j)
        b = [_dot2r(tril, lf[j]) for j in R]
        a = [ig[j] - b[j] for j in R]
        cm = [_cumulate_rows(a[j], L, jnp.maximum, -jnp.inf) for j in R]
        arow = [_dot2r(ones_lk, jnp.where(dmask, key_rows(a[j], NEG_BIG), 0.0)) for j in R]
        qb = [q[j].astype(BF16) for j in R]
        vb = [v[j].astype(BF16) for j in R]
        kbd = [jnp.where(seg64, jnp.concatenate([key_rows(k[j], 0.0).astype(BF16)] * N_MLSTM_HEADS, axis=0), zb) for j in R]
        vbd = [jnp.where(seg64, jnp.concatenate([key_rows(v[j], 0.0).astype(BF16)] * N_MLSTM_HEADS, axis=0), zb) for j in R]
        qkt = [lax.dot_general(qb[j], kbd[j], nt, preferred_element_type=F32) for j in R]

        mprev = ms[...]
        M, gg, emt = [], [], []
        for j in R:
            M.append(jnp.maximum(cm[j], mprev))
            mt = b[j] + M[j]
            gg.append(jnp.exp(mprev - M[j]))
            emt.append(jnp.exp(-mt))
            mprev = mt[L - 1:L, :]
        ms[...] = mprev

        wts = [jnp.where(causal, jnp.exp(arow[j] - M[j]), 0.0) * qkt[j] for j in R]
        kd = [k[j] * jnp.exp(a[j] - M[j][L - 1:L, :]) for j in R]
        num = [jnp.dot(wts[j].astype(BF16), vbd[j], preferred_element_type=F32) for j in R]
        u = [lax.dot_general(kd[j].astype(BF16), vb[j], tn, preferred_element_type=F32) for j in R]

        cmat, nvec = [cs[...]], [ns[...]]
        for j in R:
            gl = gg[j][L - 1:L, :]
            cmat.append(gl * cmat[j] + jnp.where(seg64, u[j], 0.0))
            nvec.append(gl * nvec[j] + jnp.sum(kd[j], axis=0, keepdims=True))
        cs[...] = cmat[G]
        ns[...] = nvec[G]

        inter = [jnp.dot(qb[j], cmat[j].astype(BF16), preferred_element_type=F32) for j in R]
        den = [_dot2(wts[j] + gg[j] * (q[j] * nvec[j]), seg64b) for j in R]
        h = [(num[j] + gg[j] * inter[j]) / jnp.maximum(jnp.abs(den[j]), emt[j]) for j in R]
        msq = [_dot2(h[j] * h[j], seg64b) * (1.0 / HEAD_DIM) for j in R]
        for j in R:
            y = h[j] * lax.rsqrt(msq[j] + EPS) * ng
            out_ref[pl.ds(r0[j], L), :] = (jax.nn.sigmoid(o_ref[pl.ds(r0[j], L), :]) * y).astype(out_ref.dtype)
        return carry

    lax.fori_loop(0, TS // (L * G), group_body, 0)

    @pl.when(t == pl.num_programs(1) - 1)
    def _():
        _store_block_diag(c_out, cs[...])
        n_out[...] = ns[...]
        m_out[...] = ms[...]


def _mlstm(z3, conv0, c0bd, n0, m0, gate_b, conv_w, norm_g, n_valid, chunk):
    B, S, _ = z3.shape
    ts = min(512, S)
    assert S % ts == 0 and ts % chunk == 0
    blk = lambda w, c0: pl.BlockSpec((None, ts, w), lambda b, t: (b, t, c0 // w))
    per_b = lambda r, w: pl.BlockSpec((None, r, w), lambda b, t: (b, 0, 0))
    per_head = pl.BlockSpec((None, N_MLSTM_HEADS, HEAD_DIM, HEAD_DIM), lambda b, t: (b, 0, 0, 0))
    return _CallSpec(
        body=functools.partial(_mlstm_kernel, tile=ts, chunk=chunk, n_valid=n_valid),
        grid=(B, S // ts),
        args=[z3, z3, z3, z3, conv0, c0bd, n0, m0, gate_b, conv_w, norm_g],
        in_specs=[blk(2 * QUAD, COL_MQK), blk(QUAD, COL_MV), blk(QUAD, COL_MO), blk(LANES, COL_GATE),
                  per_b(SUBLANES, 2 * QUAD), per_head, per_b(1, QUAD), per_b(1, QUAD),
                  _const_spec((1, LANES)), _const_spec((CONV_WIDTH, 2 * QUAD)), _const_spec((1, QUAD))],
        out_shape=[jax.ShapeDtypeStruct((B, S, QUAD), BF16),
                   jax.ShapeDtypeStruct((B, N_MLSTM_HEADS, HEAD_DIM, HEAD_DIM), F32),
                   jax.ShapeDtypeStruct((B, 1, QUAD), F32),
                   jax.ShapeDtypeStruct((B, 1, QUAD), F32)],
        out_specs=[pl.BlockSpec((None, ts, QUAD), lambda b, t: (b, t, 0)),
                   per_head, per_b(1, QUAD), per_b(1, QUAD)],
        scratch=[pltpu.VMEM((ts + 2 * SUBLANES, 2 * QUAD), F32),
                 pltpu.VMEM((QUAD, QUAD), F32), pltpu.VMEM((1, QUAD), F32), pltpu.VMEM((1, QUAD), F32)])


HGRN_FAST_CHUNK = 64
HGRN_GROUP = 8
HGRN_SAFE_DECAY = 80.0


def _hgrn_kernel(q_ref, f_ref, i_ref, g_ref, s0_ref, lb_ref, ng_ref, out_ref, s_out, st, qs, fs, ks, *,
                 tile, sub, n_valid, fast):
    TS = tile
    t = pl.program_id(1)

    seg64 = _seg_mask(QUAD, HEAD_SHIFT, QUAD, HEAD_SHIFT)
    seg64b = seg64.astype(BF16)

    @pl.when(t == 0)
    def _():
        st[...] = _load_block_diag(s0_ref, seg64)
    lb = lb_ref[...]
    ng = ng_ref[...]
    nt = (((1,), (1,)), ((), ()))
    tn = (((0,), (0,)), ((), ()))

    P = HGRN_FAST_CHUNK if fast else sub
    prow = _iota2((P, QUAD), 0)

    def prep(c, worst):
        r0 = pl.multiple_of(c * P, P)
        hq = q_ref[pl.ds(r0, P), :]
        f = lb + (1.0 - lb) * jax.nn.sigmoid(f_ref[pl.ds(r0, P), :])
        logf = jnp.log(f)
        kk = 1.0 - f
        if n_valid < TS:
            ok = (prow + r0) < n_valid
            logf = jnp.where(ok, logf, 0.0)
            kk = jnp.where(ok, kk, 0.0)
        qs[pl.ds(r0, P), :] = hq * jax.nn.sigmoid(hq)
        fs[pl.ds(r0, P), :] = logf
        ks[pl.ds(r0, P), :] = kk
        return jnp.minimum(worst, jnp.sum(logf, axis=0, keepdims=True))

    worst = lax.fori_loop(0, TS // P, prep, jnp.zeros((1, QUAD), F32))

    def finish(o, r0, rows, smat, b, kk, iv, qt):
        o = o + lax.dot_general(qt, smat.astype(BF16), nt, preferred_element_type=F32)
        out_ref[pl.ds(r0, rows), :] = _head_rmsnorm_gate(o, seg64b, ng, g_ref[pl.ds(r0, rows), :]).astype(out_ref.dtype)
        bl = b[rows - 1:rows, :]
        ktil = kk * jnp.exp(bl - b)
        u = lax.dot_general(iv.astype(BF16), ktil.astype(BF16), tn, preferred_element_type=F32)
        st[...] = smat * jnp.exp(bl) + jnp.where(seg64, u, 0.0)

    def fast_loop():
        L = HGRN_FAST_CHUNK
        tril = (_iota2((L, L), 1) <= _iota2((L, L), 0)).astype(BF16)
        causal = (_iota2((L, QUAD), 1) & (L - 1)) <= _iota2((L, QUAD), 0)
        zb = jnp.zeros((QUAD, QUAD), BF16)

        G = min(HGRN_GROUP, TS // L)
        assert TS % (L * G) == 0

        def body(gi, carry):
            R = range(G)
            r0 = [pl.multiple_of((gi * G + j) * L, L) for j in R]
            kk = [ks[pl.ds(r0[j], L), :] for j in R]
            ivb = [i_ref[pl.ds(r0[j], L), :].astype(BF16) for j in R]
            b = [_dot2r(tril, fs[pl.ds(r0[j], L), :]) for j in R]
            qt = [(qs[pl.ds(r0[j], L), :] * jnp.exp(b[j])).astype(BF16) for j in R]
            kbd = [jnp.where(seg64, jnp.concatenate([(kk[j] * jnp.exp(-b[j])).astype(BF16)] * N_HGRN_HEADS, axis=0), zb)
                   for j in R]
            ibd = [jnp.where(seg64, jnp.concatenate([ivb[j]] * N_HGRN_HEADS, axis=0), zb) for j in R]
            amat = [lax.dot_general(qt[j], kbd[j], nt, preferred_element_type=F32) for j in R]
            ktil = [(kk[j] * jnp.exp(b[j][L - 1:L, :] - b[j])).astype(BF16) for j in R]
            u = [lax.dot_general(ivb[j], ktil[j], tn, preferred_element_type=F32) for j in R]
            o = [jnp.dot(jnp.where(causal, amat[j], 0.0).astype(BF16), ibd[j], preferred_element_type=F32) for j in R]
            smat = [st[...]]
            for j in R:
                smat.append(smat[j] * jnp.exp(b[j][L - 1:L, :]) + jnp.where(seg64, u[j], 0.0))
            st[...] = smat[G]
            inter = [lax.dot_general(qt[j], smat[j].astype(BF16), nt, preferred_element_type=F32) for j in R]
            o = [o[j] + inter[j] for j in R]
            msq = [_dot2(o[j] * o[j], seg64b) * (1.0 / HEAD_DIM) for j in R]
            for j in R:
                y = o[j] * lax.rsqrt(msq[j] + EPS) * ng
                out_ref[pl.ds(r0[j], L), :] = (jax.nn.sigmoid(g_ref[pl.ds(r0[j], L), :]) * y).astype(out_ref.dtype)
            return carry

        lax.fori_loop(0, TS // (L * G), body, 0)

    def exact_loop():
        L = sub
        row = _iota2((L, QUAD), 0)

        def body(c, carry):
            r0 = pl.multiple_of(c * L, L)
            q = qs[pl.ds(r0, L), :]
            kk = ks[pl.ds(r0, L), :]
            iv = i_ref[pl.ds(r0, L), :]
            b = _cumulate_rows(fs[pl.ds(r0, L), :], L, jnp.add, 0.0)
            parts = []
            for j in range(L):
                dec = jnp.exp(jnp.where(row >= j, b - b[j:j + 1, :], NEG_BIG))
                parts.append(dec * q * kk[j:j + 1, :])
            tstack = jnp.concatenate(parts, axis=0).astype(BF16)
            y = jnp.dot(tstack, seg64b, preferred_element_type=F32)
            o = y[0:L] * iv[0:1, :]
            for j in range(1, L):
                o = o + y[j * L:(j + 1) * L] * iv[j:j + 1, :]
            finish(o, r0, L, st[...], b, kk, iv, (q * jnp.exp(b)).astype(BF16))
            return carry

        lax.fori_loop(0, TS // L, body, 0, unroll=min(CHUNK_UNROLL, TS // L))

    if fast:
        safe = jnp.min(worst) > -HGRN_SAFE_DECAY
        pl.when(safe)(fast_loop)
        pl.when(jnp.logical_not(safe))(exact_loop)
    else:
        exact_loop()

    @pl.when(t == pl.num_programs(1) - 1)
    def _():
        _store_block_diag(s_out, st[...])


def _hgrn(z3, s0t, lb, norm_g, n_valid, sub):
    B, S, _ = z3.shape
    ts = min(512, S)
    fast = ts % HGRN_FAST_CHUNK == 0
    assert S % ts == 0 and ts % sub == 0
    blk = lambda c0: pl.BlockSpec((None, ts, QUAD), lambda b, t: (b, t, c0 // QUAD))
    per_b = pl.BlockSpec((None, N_HGRN_HEADS, HEAD_DIM, HEAD_DIM), lambda b, t: (b, 0, 0, 0))
    stage = lambda: pltpu.VMEM((ts, QUAD), F32)
    return _CallSpec(
        body=functools.partial(_hgrn_kernel, tile=ts, sub=sub, n_valid=n_valid, fast=fast),
        grid=(B, S // ts),
        args=[z3, z3, z3, z3, s0t, lb, norm_g],
        in_specs=[blk(COL_HQ), blk(COL_HF), blk(COL_HI), blk(COL_HG), per_b,
                  _const_spec((1, QUAD)), _const_spec((1, QUAD))],
        out_shape=[jax.ShapeDtypeStruct((B, S, QUAD), BF16),
                   jax.ShapeDtypeStruct((B, N_HGRN_HEADS, HEAD_DIM, HEAD_DIM), F32)],
        out_specs=[pl.BlockSpec((None, ts, QUAD), lambda b, t: (b, t, 0)), per_b],
        scratch=[pltpu.VMEM((QUAD, QUAD), F32), stage(), stage(), stage()])


PROJ_SPLITS = (ATTN_WIDTH, ATTN_WIDTH, ATTN_WIDTH,
               MLSTM_WIDTH, MLSTM_WIDTH, MLSTM_WIDTH, MLSTM_WIDTH, N_MLSTM_HEADS, N_MLSTM_HEADS,
               HGRN_WIDTH, HGRN_WIDTH, HGRN_WIDTH, HGRN_WIDTH)


def _prep_w_in(w):
    g0 = int(np.sum(PROJ_SPLITS[:7]))
    g1 = g0 + 2 * N_MLSTM_HEADS
    assert (g0, w.shape[-1] - g1) == (COL_HQ, COL_GATE - COL_HQ)
    wb = w.astype(BF16)
    pad = jnp.zeros(w.shape[:-1] + (LANES - 2 * N_MLSTM_HEADS,), BF16)
    return wb, jnp.concatenate([wb[..., g1:], wb[..., g0:g1], pad], axis=-1)


HGRN_SUBCHUNK = 16


class _LayerWeights(NamedTuple):
    layer: int
    g_mix: jax.Array
    w_in: jax.Array
    gate_b: jax.Array
    conv_w: jax.Array
    mlstm_g: jax.Array
    lb: jax.Array
    hgrn_g: jax.Array
    w_out: jax.Array
    g_ffn: jax.Array
    w_up: jax.Array
    w_down: jax.Array
    g_final: jax.Array


def _gate_row(gate_b):
    flat = gate_b.astype(F32).reshape(1, 2 * N_MLSTM_HEADS)
    return jnp.concatenate([flat, jnp.zeros((1, LANES - 2 * N_MLSTM_HEADS), F32)], axis=1)


def _trunk_layer(x2d, B, S, n_valid, attn_fn, states, w, final, chunk_m, chunk_h, n_keep=None, stacked=None):
    conv_buf, C0, n0, m0, S0 = states
    if n_keep is None:
        z = _inproj(x2d, w.g_mix, w.w_in, w.layer)
    else:
        z, kt, vt = _inproj(x2d, w.g_mix, w.w_in, w.layer, S, n_keep, stacked)
    z3 = z.reshape(B, S, ZW)
    conv0 = jnp.zeros((B, SUBLANES, 2 * MLSTM_WIDTH), F32).at[:, SUBLANES - (CONV_WIDTH - 1):].set(conv_buf.astype(F32))
    m0r = jnp.repeat(m0.astype(F32), HEAD_DIM, axis=1).reshape(B, 1, MLSTM_WIDTH)
    mlstm = _mlstm(z3, conv0, C0.astype(F32), n0.astype(F32).reshape(B, 1, MLSTM_WIDTH), m0r,
                   w.gate_b, w.conv_w, w.mlstm_g, n_valid, chunk_m)
    hgrn = _hgrn(z3, jnp.swapaxes(S0.astype(F32), -1, -2), w.lb, w.hgrn_g, n_valid, chunk_h)
    attn = attn_fn(z3)
    sem = ("parallel", "arbitrary")
    if isinstance(attn, _CallSpec):
        (attn,), (ml, c_new, n, m), (hg, st) = _run([attn, mlstm, hgrn], sem, "sample_mixers")
    else:
        (ml, c_new, n, m), = _run([mlstm], sem, "mlstm")
        (hg, st), = _run([hgrn], sem, "hgrn")
    n2 = B * S
    y = _outffn(x2d, attn.reshape(n2, ATTN_WIDTH), ml.reshape(n2, MLSTM_WIDTH), hg.reshape(n2, HGRN_WIDTH),
                w.w_out, w.g_ffn, w.w_up, w.w_down, w.g_final, final, w.layer)
    if n_keep is None:
        k_rows = z3[:, :n_valid, COL_AK:COL_AK + ATTN_WIDTH].reshape(B, n_valid, N_ATTN_HEADS, HEAD_DIM)
        v_rows = z3[:, :n_valid, COL_AV:COL_AV + ATTN_WIDTH].reshape(B, n_valid, N_ATTN_HEADS, HEAD_DIM)
    else:
        k_rows, v_rows = kt, vt
    conv_new = z3[:, n_valid - (CONV_WIDTH - 1):n_valid, COL_MQK:COL_MQK + 2 * MLSTM_WIDTH]
    s_new = jnp.swapaxes(st, -1, -2)
    return y, (k_rows, v_rows, conv_new, c_new, n.reshape(B, N_MLSTM_HEADS, HEAD_DIM), m[:, 0, ::HEAD_DIM], s_new)


def kernel(x_prompt, x_sample, cache_attn_k, cache_attn_v, state_mlstm_conv, state_mlstm_C, state_mlstm_n, state_mlstm_m, state_hgrn_S, rel_bias, g_mix, w_in, mlstm_gate_b, mlstm_conv_w, mlstm_norm_g, hgrn_lb_raw, hgrn_norm_g, w_out, g_ffn, w_up, w_down, g_final):
    depth = w_in.shape[0]
    bp, sp, _ = x_prompt.shape
    bs, ts, _ = x_sample.shape
    n_keep = min(max(w for w, _ in DILATED_PATTERNS), sp)
    n_past = cache_attn_k.shape[2]
    lb_p = jax.nn.softmax(hgrn_lb_raw.astype(F32), axis=0)
    hgrn_lb = jnp.cumsum(lb_p, axis=0) - lb_p[0]
    bias_dist = _bias_by_distance(rel_bias)
    p_tables = _prompt_tables(bias_dist)
    s_tables = _sample_tables(bias_dist, n_past, ts)
    cache_kt = jnp.transpose(cache_attn_k, (0, 1, 3, 4, 2)).reshape(depth, bs, ATTN_WIDTH, n_past)
    cache_vt = jnp.transpose(cache_attn_v, (0, 1, 3, 4, 2)).reshape(depth, bs, ATTN_WIDTH, n_past)

    hp = x_prompt.reshape(bp * sp, D_MODEL)
    hs = jnp.zeros((bs, TPAD, D_MODEL), F32).at[:, :ts].set(x_sample).reshape(bs * TPAD, D_MODEL)
    zero_states = (jnp.zeros((bp, CONV_WIDTH - 1, 2 * MLSTM_WIDTH), F32),
                   jnp.zeros((bp, N_MLSTM_HEADS, HEAD_DIM, HEAD_DIM), F32),
                   jnp.zeros((bp, N_MLSTM_HEADS, HEAD_DIM), F32),
                   jnp.zeros((bp, N_MLSTM_HEADS), F32),
                   jnp.zeros((bp, N_HGRN_HEADS, HEAD_DIM, HEAD_DIM), F32))
    p_states, s_states = [], []
    kv_t = None
    w_in_b, w_out_b, w_up_b, w_down_b = _prep_w_in(w_in), w_out.astype(BF16), w_up.astype(BF16), w_down.astype(BF16)
    for l in range(depth):
        final = l == depth - 1
        weights = _LayerWeights(
            layer=l, g_mix=g_mix[l][None], w_in=w_in_b,
            gate_b=_gate_row(mlstm_gate_b[l]),
            conv_w=mlstm_conv_w[l], mlstm_g=mlstm_norm_g[l][None], lb=hgrn_lb[l][None], hgrn_g=hgrn_norm_g[l][None],
            w_out=w_out_b, g_ffn=g_ffn[l][None], w_up=w_up_b, w_down=w_down_b, g_final=g_final[None])
        hp, st = _trunk_layer(hp, bp, sp, sp, functools.partial(_attn_prompt, bias=p_tables),
                              zero_states, weights, final, MLSTM_CHUNK, HGRN_SUBCHUNK, n_keep, (l, depth, kv_t))
        kv_t = st[:2]
        p_states.append(st[2:])
        sample_attn = functools.partial(_attn_sample, cache_kt=cache_kt, cache_vt=cache_vt, layer=l, tables=s_tables)
        states = (state_mlstm_conv[l], state_mlstm_C[l], state_mlstm_n[l], state_mlstm_m[l], state_hgrn_S[l])
        hs, st = _trunk_layer(hs, bs, TPAD, ts, sample_attn, states, weights, final, TPAD, TPAD)
        s_states.append(st)
    y_prompt = hp.reshape(bp, sp, D_MODEL)
    y_sample = hs.reshape(bs, TPAD, D_MODEL)[:, :ts]
    to_rows = lambda t: jnp.transpose(t.reshape(depth, bp, N_ATTN_HEADS, HEAD_DIM, n_keep), (0, 1, 4, 2, 3))
    p_out = [to_rows(kv_t[0]), to_rows(kv_t[1])] + [jnp.stack(a) for a in zip(*p_states)]
    s_out = [jnp.stack(a) for a in zip(*s_states)]
    return (y_prompt, y_sample, *p_out, *s_out)
```

```python
import functools
from typing import Callable, NamedTuple

import jax
import jax.numpy as jnp
import numpy as np
from jax import lax
from jax.experimental import pallas as pl
from jax.experimental.pallas import tpu as pltpu

F32 = jnp.float32
BF16 = jnp.bfloat16

D_MODEL = 1024
HEAD_DIM = 64
N_ATTN_HEADS = 8
N_MLSTM_HEADS = 4
N_HGRN_HEADS = 4
ATTN_WIDTH = N_ATTN_HEADS * HEAD_DIM
MLSTM_WIDTH = N_MLSTM_HEADS * HEAD_DIM
HGRN_WIDTH = N_HGRN_HEADS * HEAD_DIM
DILATED_PATTERNS = ((128, 1), (512, 4), (2048, 16))
QBLOCK = 128
N_REL_BUCKETS = 32
REL_MAX_DISTANCE = 2048
CONV_WIDTH = 4
MLSTM_CHUNK = 64
D_FF = 4 * D_MODEL
EPS = 1e-6
NEG_BIG = -1e30
LOG2E = 1.4426950408889634

LANES = 128
SUBLANES = 8

COL_AQ = 0
COL_AK = COL_AQ + ATTN_WIDTH
COL_AV = COL_AK + ATTN_WIDTH
COL_MQK = COL_AV + ATTN_WIDTH
COL_MV = COL_MQK + 2 * MLSTM_WIDTH
COL_MO = COL_MV + MLSTM_WIDTH
COL_HQ = COL_MO + MLSTM_WIDTH
COL_HF = COL_HQ + HGRN_WIDTH
COL_HI = COL_HF + HGRN_WIDTH
COL_HG = COL_HI + HGRN_WIDTH
COL_GATE = COL_HG + HGRN_WIDTH
ZW = COL_GATE + LANES

VMEM_LIMIT = 56 * 1024 * 1024


def _cparams(sem, vmem=VMEM_LIMIT):
    return pltpu.CompilerParams(dimension_semantics=sem, vmem_limit_bytes=vmem)


def _const_spec(shape):
    nd = len(shape)
    return pl.BlockSpec(shape, lambda *_: (0,) * nd, pipeline_mode=pl.Buffered(1))


def _layer_spec(shape, layer):
    nd = len(shape)
    return pl.BlockSpec((None,) + tuple(shape), lambda *_: (layer,) + (0,) * nd, pipeline_mode=pl.Buffered(1))


class _CallSpec(NamedTuple):
    body: Callable
    grid: tuple
    args: list
    in_specs: list
    out_shape: list
    out_specs: list
    scratch: list


def _run(specs, sem, name):
    grid = specs[0].grid
    assert all(s.grid == grid for s in specs)
    n_in = [len(s.args) for s in specs]
    n_out = [len(s.out_shape) for s in specs]
    n_scr = [len(s.scratch) for s in specs]

    def body(*refs):
        ins, outs, scrs = refs[:sum(n_in)], refs[sum(n_in):sum(n_in) + sum(n_out)], refs[sum(n_in) + sum(n_out):]

        for k, s in enumerate(specs):
            a, b, c = sum(n_in[:k]), sum(n_out[:k]), sum(n_scr[:k])
            s.body(*ins[a:a + n_in[k]], *outs[b:b + n_out[k]], *scrs[c:c + n_scr[k]])

    flat = pl.pallas_call(
        body, grid=grid,
        in_specs=[x for s in specs for x in s.in_specs],
        out_shape=[x for s in specs for x in s.out_shape],
        out_specs=[x for s in specs for x in s.out_specs],
        scratch_shapes=[x for s in specs for x in s.scratch],
        compiler_params=_cparams(sem), name=name,
    )(*[x for s in specs for x in s.args])
    out, k = [], 0
    for n in n_out:
        out.append(list(flat[k:k + n]))
        k += n
    return out


def _inproj_kernel(x_ref, g_ref, wa_ref, wb_ref, *rest, tiles_per_seq, first_keep, n_prev=0):
    x = x_ref[...]
    ms = jnp.mean(x * x, axis=-1, keepdims=True)
    xn = ((x * lax.rsqrt(ms + EPS)) * g_ref[...]).astype(BF16)
    z_ref = rest[n_prev]
    z_ref[:, 0:COL_HQ] = jnp.dot(xn, wa_ref[...], preferred_element_type=F32)
    z_ref[:, COL_HQ:] = jnp.dot(xn, wb_ref[...], preferred_element_type=F32)
    if first_keep is None:
        return
    kt_ref, vt_ref = rest[n_prev + 1:]

    @pl.when(pl.program_id(0) % tiles_per_seq >= first_keep)
    def _():
        kt_ref[...] = z_ref[:, COL_AK:COL_AK + ATTN_WIDTH].T
        vt_ref[...] = z_ref[:, COL_AV:COL_AV + ATTN_WIDTH].T


def _inproj(x2d, g, w, layer, seq=None, n_keep=None, stacked=None):
    n = x2d.shape[0]
    tm = min(512, n)
    assert n % tm == 0
    x_spec = pl.BlockSpec((tm, D_MODEL), lambda i: (i, 0))
    z_spec = pl.BlockSpec((tm, ZW), lambda i: (i, 0))
    z_shape = jax.ShapeDtypeStruct((n, ZW), F32)
    w_specs = [_layer_spec((D_MODEL, COL_HQ), layer), _layer_spec((D_MODEL, ZW - COL_HQ), layer)]
    if stacked is None:
        return pl.pallas_call(
            functools.partial(_inproj_kernel, tiles_per_seq=None, first_keep=None),
            out_shape=z_shape, grid=(n // tm,),
            in_specs=[x_spec, _const_spec((1, D_MODEL))] + w_specs,
            out_specs=z_spec, compiler_params=_cparams(("parallel",)), name="inproj",
        )(x2d, g, *w)
    assert seq % tm == 0 and n_keep % tm == 0
    layer, depth, prev = stacked
    tps, first_keep = seq // tm, (seq - n_keep) // tm
    t_spec = pl.BlockSpec((None, None, ATTN_WIDTH, tm),
                          lambda i: (layer, i // tps, 0, jnp.maximum(i % tps - first_keep, 0)))
    t_shape = jax.ShapeDtypeStruct((depth, n // seq, ATTN_WIDTH, n_keep), F32)
    in_specs = [x_spec, _const_spec((1, D_MODEL))] + w_specs
    args = [x2d, g, *w]
    if prev is None:
        prev = (jnp.zeros(t_shape.shape, F32), jnp.zeros(t_shape.shape, F32))
    in_specs += [pl.BlockSpec(memory_space=pl.ANY)] * 2
    aliases = {len(args): 1, len(args) + 1: 2}
    args += list(prev)
    return pl.pallas_call(
        functools.partial(_inproj_kernel, tiles_per_seq=tps, first_keep=first_keep, n_prev=len(prev)),
        out_shape=[z_shape, t_shape, t_shape], grid=(n // tm,),
        in_specs=in_specs, out_specs=[z_spec, t_spec, t_spec], input_output_aliases=aliases,
        compiler_params=_cparams(("arbitrary",)), name="inproj_kt",
    )(*args)


FF_CHUNK = 1024
FFN_ROWS = 512


def _outffn_kernel(x_ref, a_ref, m_ref, h_ref, wo_ref, gf_ref, wu_ref, wd_ref, gl_ref, y_ref, xn_sc, *, final):
    x1 = x_ref[...]
    x1 = x1 + jnp.dot(a_ref[...], wo_ref[0:ATTN_WIDTH, :], preferred_element_type=F32)
    x1 = x1 + jnp.dot(m_ref[...], wo_ref[ATTN_WIDTH:ATTN_WIDTH + MLSTM_WIDTH, :], preferred_element_type=F32)
    x1 = x1 + jnp.dot(h_ref[...], wo_ref[ATTN_WIDTH + MLSTM_WIDTH:, :], preferred_element_type=F32)
    ms = jnp.mean(x1 * x1, axis=-1, keepdims=True)
    xn_sc[...] = ((x1 * lax.rsqrt(ms + EPS)) * gf_ref[...]).astype(BF16)
    y_ref[...] = x1
    nchunk = D_FF // FF_CHUNK
    up = lambda c: jnp.dot(xn_sc[...], wu_ref[:, c * FF_CHUNK:(c + 1) * FF_CHUNK], preferred_element_type=F32)
    u = up(0)
    for c in range(nchunk):
        u_next = up(c + 1) if c + 1 < nchunk else None
        hh = jnp.square(jnp.maximum(u, 0.0)).astype(BF16)
        y_ref[...] += jnp.dot(hh, wd_ref[c * FF_CHUNK:(c + 1) * FF_CHUNK, :], preferred_element_type=F32)
        u = u_next
    if final:
        x2 = y_ref[...]
        ms2 = jnp.mean(x2 * x2, axis=-1, keepdims=True)
        y_ref[...] = (x2 * lax.rsqrt(ms2 + EPS)) * gl_ref[...]


def _outffn(x2d, attn, ml, hg, w_out, g_ffn, w_up, w_down, g_final, final, layer):
    n = x2d.shape[0]
    tm = min(FFN_ROWS, n)
    assert n % tm == 0
    row = lambda w: pl.BlockSpec((tm, w), lambda i: (i, 0))
    return pl.pallas_call(
        functools.partial(_outffn_kernel, final=final),
        out_shape=jax.ShapeDtypeStruct((n, D_MODEL), F32),
        grid=(n // tm,),
        in_specs=[row(D_MODEL), row(ATTN_WIDTH), row(MLSTM_WIDTH), row(HGRN_WIDTH),
                  _layer_spec((D_MODEL, D_MODEL), layer), _const_spec((1, D_MODEL)),
                  _layer_spec((D_MODEL, D_FF), layer), _layer_spec((D_FF, D_MODEL), layer), _const_spec((1, D_MODEL))],
        out_specs=row(D_MODEL),
        scratch_shapes=[pltpu.VMEM((tm, D_MODEL), BF16)],
        compiler_params=_cparams(("parallel",)),
        name="outffn",
    )(x2d, attn, ml, hg, w_out, g_ffn, w_up, w_down, g_final)


def _t5_causal_bucket(dist):
    n = np.asarray(dist).astype(np.int32)
    max_exact = N_REL_BUCKETS // 2
    scaled = np.log(np.maximum(n, 1) / max_exact) / np.log(REL_MAX_DISTANCE / max_exact)
    large = np.minimum(max_exact + (scaled * (N_REL_BUCKETS - max_exact)).astype(np.int32), N_REL_BUCKETS - 1)
    return np.where(n < max_exact, n, large).astype(np.int32)


BIAS_DIST = 2304


def _bias_by_distance(rel_bias):
    assert BIAS_DIST > max(w for w, _ in DILATED_PATTERNS) + SUBLANES
    return rel_bias.astype(F32)[_t5_causal_bucket(np.arange(BIAS_DIST)[::-1])].T


def _dist_slice(bias_desc, lo, hi, step=1):
    last = BIAS_DIST - 1
    return bias_desc[:, last - hi:last - lo + 1:step]


def _prompt_tables(bias_dist):
    H = N_ATTN_HEADS
    a = np.arange(QBLOCK)[:, None]
    b = np.arange(2 * QBLOCK)[None, :]
    rel = QBLOCK + a - b
    band = (rel >= 0) & (rel <= QBLOCK)
    period = 3 * QBLOCK
    biases = []
    for window, dil in DILATED_PATTERNS:
        nsub = window // dil
        assert nsub == QBLOCK
        vd = _dist_slice(bias_dist, 0, nsub * dil, dil)
        rp = jnp.concatenate([vd, jnp.broadcast_to(vd[:, -1:], (H, QBLOCK)),
                              jnp.broadcast_to(vd[:, 0:1], (H, QBLOCK - 1))], axis=1)
        skew = jnp.tile(rp, (1, QBLOCK))[:, :QBLOCK * (period - 1)].reshape(H, QBLOCK, period - 1)
        biases.append(skew[:, :, :2 * QBLOCK].reshape(H // 2, 2 * QBLOCK, 2 * QBLOCK))
    mask = np.stack([np.tile(band, (2, 1)), np.tile(band & (b >= QBLOCK), (2, 1))])
    table = jnp.where(mask[None, None], jnp.stack(biases)[:, :, None] * LOG2E, -jnp.inf)
    return jnp.swapaxes(table, -1, -2)


ATTN_GROUP = 8


def _attn_prompt_kernel(q_ref, k_ref, v_ref, bias_ref, out_ref,
                        x4, qs, ks, vts, o1, o4, o16, l1, l4, l16, p_scr, st_scr, *, seq):
    S = seq
    nblk = S // QBLOCK
    scale = HEAD_DIM ** -0.5 * LOG2E
    zpad = jnp.zeros((QBLOCK, LANES), BF16)
    for di in range(3):
        ks[di, 0:QBLOCK, :] = zpad
        vts[di, 0] = zpad

    R = 512

    def put(src, di, row0, blk):
        if src is q_ref:
            qs[di, row0:row0 + blk.shape[0], :] = (blk * scale).astype(BF16)
        elif src is k_ref:
            ks[di, QBLOCK + row0:QBLOCK + row0 + blk.shape[0], :] = blk.astype(BF16)
        else:
            blk_t = blk.T.astype(BF16)
            for t in range(blk.shape[0] // QBLOCK):
                vts[di, 1 + row0 // QBLOCK + t] = blk_t[:, t * QBLOCK:(t + 1) * QBLOCK]

    for src in (q_ref, k_ref, v_ref):
        for i in range(S // R):
            put(src, 0, i * R, src[i * R:(i + 1) * R, :])
        for r in range(4):
            for i in range(S // 4 // R):
                blk = src[pl.ds(r + 4 * i * R, R, stride=4), :]
                row0 = r * (S // 4) + i * R
                x4[row0:row0 + R, :] = blk
                put(src, 1, row0, blk)
        n16 = S // 16
        for c4 in range(4):
            for a in range(4):
                put(src, 2, (c4 * 4 + a) * n16, x4[pl.ds(c4 * (S // 4) + a, n16, stride=4), :])

    lane = lax.broadcasted_iota(jnp.int32, (QBLOCK, LANES), 1)
    head_a = lane < HEAD_DIM
    row_a = lax.broadcasted_iota(jnp.int32, (QBLOCK, LANES), 0) < HEAD_DIM
    nt = (((1,), (1,)), ((), ()))

    def branch(di, dil, o_ref, l_ref):
        per_class = (S // dil) // QBLOCK

        R = range(ATTN_GROUP)
        zero = jnp.zeros((QBLOCK, LANES), BF16)

        def head_block(j):
            return per_class <= ATTN_GROUP and j % per_class == 0

        assert per_class % ATTN_GROUP == 0 or ATTN_GROUP % per_class == 0

        def scores(g, js):
            out = {}
            for j in js:
                i = g * ATTN_GROUP + j
                base = pl.multiple_of(i * QBLOCK, QBLOCK)
                q = qs[di, pl.ds(base, QBLOCK), :]
                q2 = jnp.concatenate([jnp.where(head_a, q, zero), jnp.where(head_a, zero, q)], axis=0)
                if head_block(j):
                    kk = ks[di, pl.ds(base + QBLOCK, QBLOCK), :]
                else:
                    kk = ks[di, pl.ds(base, 2 * QBLOCK), :]
                out[j] = lax.dot_general(kk, q2, nt, preferred_element_type=F32)
            return out

        def softmax(g, slot, st):
            for j in st:
                i = g * ATTN_GROUP + j
                if head_block(j):
                    s = st[j] + bias_ref[di, 1, QBLOCK:, :]
                else:
                    s = st[j] + bias_ref[di, jnp.where(i % per_class == 0, 1, 0)]
                m = jnp.max(s, axis=0, keepdims=True)
                p = jnp.exp2(s - m)
                if head_block(j):
                    p_scr[slot, j, QBLOCK:, :] = p.astype(BF16)
                else:
                    p_scr[slot, j] = p.astype(BF16)
                st_scr[slot, j, 0:1, :] = m
                st_scr[slot, j, 1:2, :] = jnp.sum(p, axis=0, keepdims=True)

        def weighted_values(g, slot, js):
            out = {}
            for j in js:
                i = g * ATTN_GROUP + j
                if head_block(j):
                    ot = jnp.dot(vts[di, i + 1], p_scr[slot, j, QBLOCK:, :], preferred_element_type=F32)
                else:
                    vt = jnp.concatenate([vts[di, i], vts[di, i + 1]], axis=1)
                    ot = jnp.dot(vt, p_scr[slot, j], preferred_element_type=F32)
                out[j] = (ot, st_scr[slot, j, 0:1, :], st_scr[slot, j, 1:2, :])
            return out

        def outputs(g, pv):
            for j in pv:
                i = g * ATTN_GROUP + j
                base = pl.multiple_of(i * QBLOCK, QBLOCK)
                ot, m, l = pv[j]
                otn = ot / l
                lse = m + jnp.log2(l)
                o = jnp.where(row_a, otn[:, 0:QBLOCK], otn[:, QBLOCK:]).T
                ls = jnp.where(row_a, jnp.broadcast_to(lse[:, 0:QBLOCK], (QBLOCK, LANES)),
                               jnp.broadcast_to(lse[:, QBLOCK:], (QBLOCK, LANES))).T
                if dil == 1:
                    o_ref[pl.ds(base, QBLOCK), :] = o
                    l_ref[pl.ds(base, QBLOCK), :] = ls
                else:
                    c, n = i // per_class, i % per_class
                    res = c if dil == 4 else 4 * (c % 4) + c // 4
                    start = dil * QBLOCK * n + res
                    o_ref[pl.ds(start, QBLOCK, stride=dil), :] = o
                    l_ref[pl.ds(start, QBLOCK, stride=dil), :] = ls

        ngroup = nblk // ATTN_GROUP
        softmax(0, 0, scores(0, R))

        def body(g, carry):
            slot = g % 2
            st = scores(g, R)
            pv = weighted_values(g - 1, 1 - slot, R)
            softmax(g, slot, st)
            outputs(g - 1, pv)
            return carry

        lax.fori_loop(1, ngroup, body, 0)
        outputs(ngroup - 1, weighted_values(ngroup - 1, (ngroup - 1) % 2, R))

    branch(0, 1, o1, l1)
    branch(1, 4, o4, l4)
    branch(2, 16, o16, l16)

    T = 256

    def mix(i, carry):
        r0 = pl.multiple_of(i * T, T)
        la, lb, lc = l1[pl.ds(r0, T), :], l4[pl.ds(r0, T), :], l16[pl.ds(r0, T), :]
        mx = jnp.maximum(jnp.maximum(la, lb), lc)
        ea, eb, ec = jnp.exp2(la - mx), jnp.exp2(lb - mx), jnp.exp2(lc - mx)
        num = ea * o1[pl.ds(r0, T), :] + eb * o4[pl.ds(r0, T), :] + ec * o16[pl.ds(r0, T), :]
        out_ref[pl.ds(r0, T), :] = (num / (ea + eb + ec)).astype(out_ref.dtype)
        return carry

    lax.fori_loop(0, S // T, mix, 0)


def _attn_prompt(z3, bias):
    B, S, _ = z3.shape
    assert S % (16 * QBLOCK) == 0
    npair = N_ATTN_HEADS // 2
    col = lambda c0: pl.BlockSpec((None, S, LANES), lambda b, p: (b, 0, c0 // LANES + p))
    f32s = lambda: pltpu.VMEM((S, LANES), F32)
    return pl.pallas_call(
        functools.partial(_attn_prompt_kernel, seq=S),
        out_shape=jax.ShapeDtypeStruct((B, S, ATTN_WIDTH), BF16),
        grid=(B, npair),
        in_specs=[col(COL_AQ), col(COL_AK), col(COL_AV),
                  pl.BlockSpec((3, None, 2, 2 * QBLOCK, 2 * QBLOCK), lambda b, p: (0, p, 0, 0, 0))],
        out_specs=pl.BlockSpec((None, S, LANES), lambda b, p: (b, 0, p)),
        scratch_shapes=[f32s(),
                        pltpu.VMEM((3, S, LANES), BF16),
                        pltpu.VMEM((3, S + QBLOCK, LANES), BF16),
                        pltpu.VMEM((3, S // QBLOCK + 1, LANES, QBLOCK), BF16),
                        f32s(), f32s(), f32s(), f32s(), f32s(), f32s(),
                        pltpu.VMEM((2, ATTN_GROUP, 2 * QBLOCK, 2 * QBLOCK), BF16),
                        pltpu.VMEM((2, ATTN_GROUP, SUBLANES, 2 * QBLOCK), F32)],
        compiler_params=_cparams(("parallel", "parallel")),
        name="attn_prompt",
    )(z3, z3, z3, bias)


TPAD = SUBLANES
TAIL = 512


def _sample_tables(bias_dist, n_past, n_tok):
    assert n_past >= max(w for w, _ in DILATED_PATTERNS) and n_tok <= TPAD and TAIL >= DILATED_PATTERNS[1][0]
    H = N_ATTN_HEADS
    t = np.arange(TPAD)[:, None]

    def by_row(width):
        rows = [_dist_slice(bias_dist, tt + 1, tt + width) for tt in range(TPAD)]
        return jnp.stack(rows, axis=1).reshape(H * TPAD, width)

    bias_tail, bias_full = by_row(TAIL), by_row(n_past)
    zero_dist = _dist_slice(bias_dist, 0, 0)
    new_rows = [jnp.concatenate([_dist_slice(bias_dist, 0, tt),
                                 jnp.broadcast_to(zero_dist, (H, TPAD - tt - 1))], axis=1) for tt in range(TPAD)]
    bias_new = jnp.stack(new_rows, axis=1).reshape(H * TPAD, TPAD)

    def valid(dj, dil, nsub, lo):
        ok = (dj % dil == 0) & (dj // dil >= lo) & (dj // dil <= nsub) & (t < n_tok)
        return np.tile(ok, (H, 1)).astype(np.float32)

    tn = np.arange(TPAD)[None, :]
    m_tail, m_new = [], []
    for window, dil in DILATED_PATTERNS:
        nsub = window // dil
        if window <= TAIL:
            m_tail.append(valid(TAIL + t - np.arange(TAIL)[None, :], dil, nsub, 1))
        else:
            m_full = valid(n_past + t - np.arange(n_past)[None, :], dil, nsub, 1)
        m_new.append(valid(t - tn, dil, nsub, 0) * (tn < n_tok))
    return (bias_tail, bias_full, bias_new, jnp.asarray(np.stack(m_tail)), jnp.asarray(m_full),
            jnp.asarray(np.stack(m_new).astype(np.float32)))


def _attn_sample_kernel(q_ref, kn_ref, vn_ref, kt_ref, vt_ref, bt_ref, bf_ref, bn_ref, mt_ref, mf_ref, mn_ref,
                        out_ref, *, n_past):
    H = N_ATTN_HEADS
    rows = H * TPAD
    same_head = (_iota2((rows, ATTN_WIDTH), 0) >> 3) == (_iota2((rows, ATTN_WIDTH), 1) >> HEAD_SHIFT)
    q = q_ref[...] * (HEAD_DIM ** -0.5)
    qm = jnp.where(same_head, jnp.concatenate([q] * H, axis=0), 0.0).astype(BF16)
    nt = (((1,), (1,)), ((), ()))
    kn = kn_ref[...].astype(BF16)
    vn = vn_ref[...].astype(BF16)
    kt = kt_ref[...].astype(BF16)
    vt = vt_ref[...].astype(BF16)
    s_new = lax.dot_general(qm, kn, nt, preferred_element_type=F32)
    s_full = jnp.dot(qm, kt, preferred_element_type=F32)
    s_tail = s_full[:, n_past - TAIL:]

    outs, lses = [], []
    for di, (window, _) in enumerate(DILATED_PATTERNS):
        if window <= TAIL:
            sm = jnp.where(mt_ref[di] > 0.5, s_tail + bt_ref[...], NEG_BIG)
            vmain = vt[:, n_past - TAIL:]
        else:
            sm = jnp.where(mf_ref[...] > 0.5, s_full + bf_ref[...], NEG_BIG)
            vmain = vt
        sn = jnp.where(mn_ref[di] > 0.5, s_new + bn_ref[...], NEG_BIG)
        m = jnp.maximum(jnp.max(sm, axis=-1, keepdims=True), jnp.max(sn, axis=-1, keepdims=True))
        pm = jnp.exp(sm - m)
        pn = jnp.exp(sn - m)
        l = jnp.sum(pm, axis=-1, keepdims=True) + jnp.sum(pn, axis=-1, keepdims=True)
        o = jnp.dot(pn.astype(BF16), vn, preferred_element_type=F32)
        o = o + lax.dot_general(pm.astype(BF16), vmain, nt, preferred_element_type=F32)
        outs.append(o / l)
        lses.append(m + jnp.log(l))
    mx = jnp.maximum(jnp.maximum(lses[0], lses[1]), lses[2])
    es = [jnp.exp(ls - mx) for ls in lses]
    mixed = (es[0] * outs[0] + es[1] * outs[1] + es[2] * outs[2]) / (es[0] + es[1] + es[2])
    mixed = jnp.where(same_head, mixed, 0.0)
    acc = mixed[0:TPAD]
    for h in range(1, H):
        acc = acc + mixed[h * TPAD:(h + 1) * TPAD]
    out_ref[...] = acc.astype(out_ref.dtype)


def _attn_sample(z3, cache_kt, cache_vt, layer, tables):
    B, T, _ = z3.shape
    assert T == TPAD
    n_past = cache_kt.shape[-1]
    new = lambda c0: pl.BlockSpec((None, TPAD, ATTN_WIDTH), lambda b, t: (b, 0, c0 // ATTN_WIDTH))
    buf = pl.BlockSpec((None, None, ATTN_WIDTH, n_past), lambda b, t: (layer, b, 0, 0))
    return _CallSpec(
        body=functools.partial(_attn_sample_kernel, n_past=n_past),
        grid=(B, 1),
        args=[z3, z3, z3, cache_kt, cache_vt, *tables],
        in_specs=[new(COL_AQ), new(COL_AK), new(COL_AV), buf, buf] + [_const_spec(t.shape) for t in tables],
        out_shape=[jax.ShapeDtypeStruct((B, TPAD, ATTN_WIDTH), BF16)],
        out_specs=[pl.BlockSpec((None, TPAD, ATTN_WIDTH), lambda b, t: (b, 0, 0))],
        scratch=[])


QUAD = N_MLSTM_HEADS * HEAD_DIM
CHUNK_UNROLL = 2
CHUNK_GROUP = 8
HEAD_SHIFT = 6


def _iota2(shape, axis):
    return lax.broadcasted_iota(jnp.int32, shape, axis)


def _log2(n):
    k = int(n).bit_length() - 1
    assert 1 << k == n
    return k


def _seg_mask(rows, row_shift, cols, col_shift):
    return (_iota2((rows, cols), 0) >> row_shift) == (_iota2((rows, cols), 1) >> col_shift)


def _cumulate_rows(x, length, op, fill):
    row = _iota2(x.shape, 0) & (length - 1)
    sh = 1
    while sh < length:
        x = op(x, jnp.where(row >= sh, pltpu.roll(x, sh, axis=0), fill))
        sh *= 2
    return x


def _split2(x):
    hi = x.astype(BF16)
    lo = (x - hi.astype(F32)).astype(BF16)
    return hi, lo


def _dot2(x, w):
    hi, lo = _split2(x)
    return jnp.dot(hi, w, preferred_element_type=F32) + jnp.dot(lo, w, preferred_element_type=F32)


def _dot2r(w, x):
    hi, lo = _split2(x)
    return jnp.dot(w, hi, preferred_element_type=F32) + jnp.dot(w, lo, preferred_element_type=F32)


def _load_block_diag(ref, seg64):
    flat = ref[...].reshape(QUAD, HEAD_DIM)
    return jnp.where(seg64, jnp.concatenate([flat] * (QUAD // HEAD_DIM), axis=1), 0.0)


def _store_block_diag(ref, mat):
    for h in range(QUAD // HEAD_DIM):
        ref[h] = mat[h * HEAD_DIM:(h + 1) * HEAD_DIM, h * HEAD_DIM:(h + 1) * HEAD_DIM]


def _head_rmsnorm_gate(h, seg64b, g_row, gate_pre):
    ms = _dot2(h * h, seg64b) * (1.0 / HEAD_DIM)
    return jax.nn.sigmoid(gate_pre) * (h * lax.rsqrt(ms + EPS) * g_row)


def _mlstm_kernel(qk_ref, v_ref, o_ref, gate_ref, conv0_ref, c0_ref, n0_ref, m0_ref, gb_ref, cw_ref, ng_ref,
                  out_ref, c_out, n_out, m_out, xp, cs, ns, ms, *, tile, chunk, n_valid):
    TS, L = tile, chunk
    t = pl.program_id(1)
    PAD = SUBLANES

    @pl.when(t == 0)
    def _():
        cs[...] = _load_block_diag(c0_ref, _seg_mask(QUAD, HEAD_SHIFT, QUAD, HEAD_SHIFT))
        ns[...] = n0_ref[...]
        ms[...] = m0_ref[...]
        xp[0:PAD, :] = conv0_ref[...]

    @pl.when(t > 0)
    def _():
        xp[0:PAD, :] = xp[TS:TS + PAD, :]

    xp[PAD:PAD + TS, :] = qk_ref[...]

    LK = MLSTM_CHUNK
    seg64 = _seg_mask(QUAD, HEAD_SHIFT, QUAD, HEAD_SHIFT)
    seg64b = seg64.astype(BF16)
    dmask = (_iota2((LK, QUAD), 1) & (LK - 1)) == _iota2((LK, QUAD), 0)
    causal = (_iota2((L, QUAD), 1) & (LK - 1)) <= _iota2((L, QUAD), 0)
    tril = (_iota2((L, L), 1) <= _iota2((L, L), 0)).astype(BF16)
    ones_lk = jnp.ones((L, LK), BF16)
    row = _iota2((L, QUAD), 0)
    cw = cw_ref[...]
    gb = gb_ref[...]
    ng = ng_ref[...]

    first_half = _iota2((L, LANES), 1) < HEAD_DIM

    def per_head_lanes(g, lane0):
        col = [jnp.broadcast_to(g[:, lane0 + h:lane0 + h + 1], (L, LANES)) for h in range(N_MLSTM_HEADS)]
        return jnp.concatenate([jnp.where(first_half, col[0], col[1]), jnp.where(first_half, col[2], col[3])], axis=1)

    def key_rows(x, fill):
        if L == LK:
            return x
        return jnp.concatenate([x, jnp.full((LK - L, QUAD), fill, x.dtype)], axis=0)

    G = min(CHUNK_GROUP, TS // L)
    assert TS % (L * G) == 0
    nt = (((1,), (1,)), ((), ()))
    tn = (((0,), (0,)), ((), ()))
    zb = jnp.zeros((QUAD, QUAD), BF16)

    def group_body(gi, carry):
        R = range(G)
        r0 = [pl.multiple_of((gi * G + j) * L, L) for j in R]
        q, k, v, ig, lf = [], [], [], [], []
        for j in R:
            win = xp[pl.ds(r0[j], L + PAD), :]
            acc = win[PAD:PAD + L] * cw[CONV_WIDTH - 1:CONV_WIDTH, :]
            for s in range(1, CONV_WIDTH):
                acc = acc + win[PAD - s:PAD - s + L] * cw[CONV_WIDTH - 1 - s:CONV_WIDTH - s, :]
            qk = acc * jax.nn.sigmoid(acc)
            q.append(qk[:, 0:QUAD])
            k.append(qk[:, QUAD:] * (HEAD_DIM ** -0.5))
            v.append(v_ref[pl.ds(r0[j], L), :])
            gate = gate_ref[pl.ds(r0[j], L), :] + gb
            logsig = jnp.minimum(gate, 0.0) - jnp.log(1.0 + jnp.exp(-jnp.abs(gate)))
            ig_j = per_head_lanes(gate, 0)
            lf_j = per_head_lanes(logsig, N_MLSTM_HEADS)
            if n_valid < TS:
                ok = (row + r0[j]) < n_valid
                ig_j = jnp.where(ok, ig_j, NEG_BIG)
                lf_j = jnp.where(ok, lf_j, 0.0)
            ig.append(ig_j)
            lf.append(lf_j)
        b = [_dot2r(tril, lf[j]) for j in R]
        a = [ig[j] - b[j] for j in R]
        cm = [_cumulate_rows(a[j], L, jnp.maximum, -jnp.inf) for j in R]
        arow = [_dot2r(ones_lk, jnp.where(dmask, key_rows(a[j], NEG_BIG), 0.0)) for j in R]
        qb = [q[j].astype(BF16) for j in R]
        vb = [v[j].astype(BF16) for j in R]
        kbd = [jnp.where(seg64, jnp.concatenate([key_rows(k[j], 0.0).astype(BF16)] * N_MLSTM_HEADS, axis=0), zb) for j in R]
        vbd = [jnp.where(seg64, jnp.concatenate([key_rows(v[j], 0.0).astype(BF16)] * N_MLSTM_HEADS, axis=0), zb) for j in R]
        qkt = [lax.dot_general(qb[j], kbd[j], nt, preferred_element_type=F32) for j in R]

        mprev = ms[...]
        M, gg, emt = [], [], []
        for j in R:
            M.append(jnp.maximum(cm[j], mprev))
            mt = b[j] + M[j]
            gg.append(jnp.exp(mprev - M[j]))
            emt.append(jnp.exp(-mt))
            mprev = mt[L - 1:L, :]
        ms[...] = mprev

        wts = [jnp.where(causal, jnp.exp(arow[j] - M[j]), 0.0) * qkt[j] for j in R]
        kd = [k[j] * jnp.exp(a[j] - M[j][L - 1:L, :]) for j in R]
        num = [jnp.dot(wts[j].astype(BF16), vbd[j], preferred_element_type=F32) for j in R]
        u = [lax.dot_general(kd[j].astype(BF16), vb[j], tn, preferred_element_type=F32) for j in R]

        cmat, nvec = [cs[...]], [ns[...]]
        for j in R:
            gl = gg[j][L - 1:L, :]
            cmat.append(gl * cmat[j] + jnp.where(seg64, u[j], 0.0))
            nvec.append(gl * nvec[j] + jnp.sum(kd[j], axis=0, keepdims=True))
        cs[...] = cmat[G]
        ns[...] = nvec[G]

        inter = [jnp.dot(qb[j], cmat[j].astype(BF16), preferred_element_type=F32) for j in R]
        den = [_dot2(wts[j] + gg[j] * (q[j] * nvec[j]), seg64b) for j in R]
        h = [(num[j] + gg[j] * inter[j]) / jnp.maximum(jnp.abs(den[j]), emt[j]) for j in R]
        msq = [_dot2(h[j] * h[j], seg64b) * (1.0 / HEAD_DIM) for j in R]
        for j in R:
            y = h[j] * lax.rsqrt(msq[j] + EPS) * ng
            out_ref[pl.ds(r0[j], L), :] = (jax.nn.sigmoid(o_ref[pl.ds(r0[j], L), :]) * y).astype(out_ref.dtype)
        return carry

    lax.fori_loop(0, TS // (L * G), group_body, 0)

    @pl.when(t == pl.num_programs(1) - 1)
    def _():
        _store_block_diag(c_out, cs[...])
        n_out[...] = ns[...]
        m_out[...] = ms[...]


def _mlstm(z3, conv0, c0bd, n0, m0, gate_b, conv_w, norm_g, n_valid, chunk):
    B, S, _ = z3.shape
    ts = min(512, S)
    assert S % ts == 0 and ts % chunk == 0
    blk = lambda w, c0: pl.BlockSpec((None, ts, w), lambda b, t: (b, t, c0 // w))
    per_b = lambda r, w: pl.BlockSpec((None, r, w), lambda b, t: (b, 0, 0))
    per_head = pl.BlockSpec((None, N_MLSTM_HEADS, HEAD_DIM, HEAD_DIM), lambda b, t: (b, 0, 0, 0))
    return _CallSpec(
        body=functools.partial(_mlstm_kernel, tile=ts, chunk=chunk, n_valid=n_valid),
        grid=(B, S // ts),
        args=[z3, z3, z3, z3, conv0, c0bd, n0, m0, gate_b, conv_w, norm_g],
        in_specs=[blk(2 * QUAD, COL_MQK), blk(QUAD, COL_MV), blk(QUAD, COL_MO), blk(LANES, COL_GATE),
                  per_b(SUBLANES, 2 * QUAD), per_head, per_b(1, QUAD), per_b(1, QUAD),
                  _const_spec((1, LANES)), _const_spec((CONV_WIDTH, 2 * QUAD)), _const_spec((1, QUAD))],
        out_shape=[jax.ShapeDtypeStruct((B, S, QUAD), BF16),
                   jax.ShapeDtypeStruct((B, N_MLSTM_HEADS, HEAD_DIM, HEAD_DIM), F32),
                   jax.ShapeDtypeStruct((B, 1, QUAD), F32),
                   jax.ShapeDtypeStruct((B, 1, QUAD), F32)],
        out_specs=[pl.BlockSpec((None, ts, QUAD), lambda b, t: (b, t, 0)),
                   per_head, per_b(1, QUAD), per_b(1, QUAD)],
        scratch=[pltpu.VMEM((ts + 2 * SUBLANES, 2 * QUAD), F32),
                 pltpu.VMEM((QUAD, QUAD), F32), pltpu.VMEM((1, QUAD), F32), pltpu.VMEM((1, QUAD), F32)])


HGRN_FAST_CHUNK = 64
HGRN_GROUP = 8
HGRN_SAFE_DECAY = 80.0


def _hgrn_kernel(q_ref, f_ref, i_ref, g_ref, s0_ref, lb_ref, ng_ref, out_ref, s_out, st, qs, fs, ks, *,
                 tile, sub, n_valid, fast):
    TS = tile
    t = pl.program_id(1)

    seg64 = _seg_mask(QUAD, HEAD_SHIFT, QUAD, HEAD_SHIFT)
    seg64b = seg64.astype(BF16)

    @pl.when(t == 0)
    def _():
        st[...] = _load_block_diag(s0_ref, seg64)
    lb = lb_ref[...]
    ng = ng_ref[...]
    nt = (((1,), (1,)), ((), ()))
    tn = (((0,), (0,)), ((), ()))

    P = HGRN_FAST_CHUNK if fast else sub
    prow = _iota2((P, QUAD), 0)

    def prep(c, worst):
        r0 = pl.multiple_of(c * P, P)
        hq = q_ref[pl.ds(r0, P), :]
        f = lb + (1.0 - lb) * jax.nn.sigmoid(f_ref[pl.ds(r0, P), :])
        logf = jnp.log(f)
        kk = 1.0 - f
        if n_valid < TS:
            ok = (prow + r0) < n_valid
            logf = jnp.where(ok, logf, 0.0)
            kk = jnp.where(ok, kk, 0.0)
        qs[pl.ds(r0, P), :] = hq * jax.nn.sigmoid(hq)
        fs[pl.ds(r0, P), :] = logf
        ks[pl.ds(r0, P), :] = kk
        return jnp.minimum(worst, jnp.sum(logf, axis=0, keepdims=True))

    worst = lax.fori_loop(0, TS // P, prep, jnp.zeros((1, QUAD), F32))

    def finish(o, r0, rows, smat, b, kk, iv, qt):
        o = o + lax.dot_general(qt, smat.astype(BF16), nt, preferred_element_type=F32)
        out_ref[pl.ds(r0, rows), :] = _head_rmsnorm_gate(o, seg64b, ng, g_ref[pl.ds(r0, rows), :]).astype(out_ref.dtype)
        bl = b[rows - 1:rows, :]
        ktil = kk * jnp.exp(bl - b)
        u = lax.dot_general(iv.astype(BF16), ktil.astype(BF16), tn, preferred_element_type=F32)
        st[...] = smat * jnp.exp(bl) + jnp.where(seg64, u, 0.0)

    def fast_loop():
        L = HGRN_FAST_CHUNK
        tril = (_iota2((L, L), 1) <= _iota2((L, L), 0)).astype(BF16)
        causal = (_iota2((L, QUAD), 1) & (L - 1)) <= _iota2((L, QUAD), 0)
        zb = jnp.zeros((QUAD, QUAD), BF16)

        G = min(HGRN_GROUP, TS // L)
        assert TS % (L * G) == 0

        def body(gi, carry):
            R = range(G)
            r0 = [pl.multiple_of((gi * G + j) * L, L) for j in R]
            kk = [ks[pl.ds(r0[j], L), :] for j in R]
            ivb = [i_ref[pl.ds(r0[j], L), :].astype(BF16) for j in R]
            b = [_dot2r(tril, fs[pl.ds(r0[j], L), :]) for j in R]
            qt = [(qs[pl.ds(r0[j], L), :] * jnp.exp(b[j])).astype(BF16) for j in R]
            kbd = [jnp.where(seg64, jnp.concatenate([(kk[j] * jnp.exp(-b[j])).astype(BF16)] * N_HGRN_HEADS, axis=0), zb)
                   for j in R]
            ibd = [jnp.where(seg64, jnp.concatenate([ivb[j]] * N_HGRN_HEADS, axis=0), zb) for j in R]
            amat = [lax.dot_general(qt[j], kbd[j], nt, preferred_element_type=F32) for j in R]
            ktil = [(kk[j] * jnp.exp(b[j][L - 1:L, :] - b[j])).astype(BF16) for j in R]
            u = [lax.dot_general(ivb[j], ktil[j], tn, preferred_element_type=F32) for j in R]
            o = [jnp.dot(jnp.where(causal, amat[j], 0.0).astype(BF16), ibd[j], preferred_element_type=F32) for j in R]
            smat = [st[...]]
            for j in R:
                smat.append(smat[j] * jnp.exp(b[j][L - 1:L, :]) + jnp.where(seg64, u[j], 0.0))
            st[...] = smat[G]
            inter = [lax.dot_general(qt[j], smat[j].astype(BF16), nt, preferred_element_type=F32) for j in R]
            o = [o[j] + inter[j] for j in R]
            msq = [_dot2(o[j] * o[j], seg64b) * (1.0 / HEAD_DIM) for j in R]
            for j in R:
                y = o[j] * lax.rsqrt(msq[j] + EPS) * ng
                out_ref[pl.ds(r0[j], L), :] = (jax.nn.sigmoid(g_ref[pl.ds(r0[j], L), :]) * y).astype(out_ref.dtype)
            return carry

        lax.fori_loop(0, TS // (L * G), body, 0)

    def exact_loop():
        L = sub
        row = _iota2((L, QUAD), 0)

        def body(c, carry):
            r0 = pl.multiple_of(c * L, L)
            q = qs[pl.ds(r0, L), :]
            kk = ks[pl.ds(r0, L), :]
            iv = i_ref[pl.ds(r0, L), :]
            b = _cumulate_rows(fs[pl.ds(r0, L), :], L, jnp.add, 0.0)
            parts = []
            for j in range(L):
                dec = jnp.exp(jnp.where(row >= j, b - b[j:j + 1, :], NEG_BIG))
                parts.append(dec * q * kk[j:j + 1, :])
            tstack = jnp.concatenate(parts, axis=0).astype(BF16)
            y = jnp.dot(tstack, seg64b, preferred_element_type=F32)
            o = y[0:L] * iv[0:1, :]
            for j in range(1, L):
                o = o + y[j * L:(j + 1) * L] * iv[j:j + 1, :]
            finish(o, r0, L, st[...], b, kk, iv, (q * jnp.exp(b)).astype(BF16))
            return carry

        lax.fori_loop(0, TS // L, body, 0, unroll=min(CHUNK_UNROLL, TS // L))

    if fast:
        safe = jnp.min(worst) > -HGRN_SAFE_DECAY
        pl.when(safe)(fast_loop)
        pl.when(jnp.logical_not(safe))(exact_loop)
    else:
        exact_loop()

    @pl.when(t == pl.num_programs(1) - 1)
    def _():
        _store_block_diag(s_out, st[...])


def _hgrn(z3, s0t, lb, norm_g, n_valid, sub):
    B, S, _ = z3.shape
    ts = min(512, S)
    fast = ts % HGRN_FAST_CHUNK == 0
    assert S % ts == 0 and ts % sub == 0
    blk = lambda c0: pl.BlockSpec((None, ts, QUAD), lambda b, t: (b, t, c0 // QUAD))
    per_b = pl.BlockSpec((None, N_HGRN_HEADS, HEAD_DIM, HEAD_DIM), lambda b, t: (b, 0, 0, 0))
    stage = lambda: pltpu.VMEM((ts, QUAD), F32)
    return _CallSpec(
        body=functools.partial(_hgrn_kernel, tile=ts, sub=sub, n_valid=n_valid, fast=fast),
        grid=(B, S // ts),
        args=[z3, z3, z3, z3, s0t, lb, norm_g],
        in_specs=[blk(COL_HQ), blk(COL_HF), blk(COL_HI), blk(COL_HG), per_b,
                  _const_spec((1, QUAD)), _const_spec((1, QUAD))],
        out_shape=[jax.ShapeDtypeStruct((B, S, QUAD), BF16),
                   jax.ShapeDtypeStruct((B, N_HGRN_HEADS, HEAD_DIM, HEAD_DIM), F32)],
        out_specs=[pl.BlockSpec((None, ts, QUAD), lambda b, t: (b, t, 0)), per_b],
        scratch=[pltpu.VMEM((QUAD, QUAD), F32), stage(), stage(), stage()])


PROJ_SPLITS = (ATTN_WIDTH, ATTN_WIDTH, ATTN_WIDTH,
               MLSTM_WIDTH, MLSTM_WIDTH, MLSTM_WIDTH, MLSTM_WIDTH, N_MLSTM_HEADS, N_MLSTM_HEADS,
               HGRN_WIDTH, HGRN_WIDTH, HGRN_WIDTH, HGRN_WIDTH)


def _prep_w_in(w):
    g0 = int(np.sum(PROJ_SPLITS[:7]))
    g1 = g0 + 2 * N_MLSTM_HEADS
    assert (g0, w.shape[-1] - g1) == (COL_HQ, COL_GATE - COL_HQ)
    wb = w.astype(BF16)
    pad = jnp.zeros(w.shape[:-1] + (LANES - 2 * N_MLSTM_HEADS,), BF16)
    return wb, jnp.concatenate([wb[..., g1:], wb[..., g0:g1], pad], axis=-1)


HGRN_SUBCHUNK = 16


class _LayerWeights(NamedTuple):
    layer: int
    g_mix: jax.Array
    w_in: jax.Array
    gate_b: jax.Array
    conv_w: jax.Array
    mlstm_g: jax.Array
    lb: jax.Array
    hgrn_g: jax.Array
    w_out: jax.Array
    g_ffn: jax.Array
    w_up: jax.Array
    w_down: jax.Array
    g_final: jax.Array


def _gate_row(gate_b):
    flat = gate_b.astype(F32).reshape(1, 2 * N_MLSTM_HEADS)
    return jnp.concatenate([flat, jnp.zeros((1, LANES - 2 * N_MLSTM_HEADS), F32)], axis=1)


def _trunk_layer(x2d, B, S, n_valid, attn_fn, states, w, final, chunk_m, chunk_h, n_keep=None, stacked=None):
    conv_buf, C0, n0, m0, S0 = states
    if n_keep is None:
        z = _inproj(x2d, w.g_mix, w.w_in, w.layer)
    else:
        z, kt, vt = _inproj(x2d, w.g_mix, w.w_in, w.layer, S, n_keep, stacked)
    z3 = z.reshape(B, S, ZW)
    conv0 = jnp.zeros((B, SUBLANES, 2 * MLSTM_WIDTH), F32).at[:, SUBLANES - (CONV_WIDTH - 1):].set(conv_buf.astype(F32))
    m0r = jnp.repeat(m0.astype(F32), HEAD_DIM, axis=1).reshape(B, 1, MLSTM_WIDTH)
    mlstm = _mlstm(z3, conv0, C0.astype(F32), n0.astype(F32).reshape(B, 1, MLSTM_WIDTH), m0r,
                   w.gate_b, w.conv_w, w.mlstm_g, n_valid, chunk_m)
    hgrn = _hgrn(z3, jnp.swapaxes(S0.astype(F32), -1, -2), w.lb, w.hgrn_g, n_valid, chunk_h)
    attn = attn_fn(z3)
    sem = ("parallel", "arbitrary")
    if isinstance(attn, _CallSpec):
        (attn,), (ml, c_new, n, m), (hg, st) = _run([attn, mlstm, hgrn], sem, "sample_mixers")
    else:
        (ml, c_new, n, m), = _run([mlstm], sem, "mlstm")
        (hg, st), = _run([hgrn], sem, "hgrn")
    n2 = B * S
    y = _outffn(x2d, attn.reshape(n2, ATTN_WIDTH), ml.reshape(n2, MLSTM_WIDTH), hg.reshape(n2, HGRN_WIDTH),
                w.w_out, w.g_ffn, w.w_up, w.w_down, w.g_final, final, w.layer)
    if n_keep is None:
        k_rows = z3[:, :n_valid, COL_AK:COL_AK + ATTN_WIDTH].reshape(B, n_valid, N_ATTN_HEADS, HEAD_DIM)
        v_rows = z3[:, :n_valid, COL_AV:COL_AV + ATTN_WIDTH].reshape(B, n_valid, N_ATTN_HEADS, HEAD_DIM)
    else:
        k_rows, v_rows = kt, vt
    conv_new = z3[:, n_valid - (CONV_WIDTH - 1):n_valid, COL_MQK:COL_MQK + 2 * MLSTM_WIDTH]
    s_new = jnp.swapaxes(st, -1, -2)
    return y, (k_rows, v_rows, conv_new, c_new, n.reshape(B, N_MLSTM_HEADS, HEAD_DIM), m[:, 0, ::HEAD_DIM], s_new)


def kernel(x_prompt, x_sample, cache_attn_k, cache_attn_v, state_mlstm_conv, state_mlstm_C, state_mlstm_n, state_mlstm_m, state_hgrn_S, rel_bias, g_mix, w_in, mlstm_gate_b, mlstm_conv_w, mlstm_norm_g, hgrn_lb_raw, hgrn_norm_g, w_out, g_ffn, w_up, w_down, g_final):
    depth = w_in.shape[0]
    bp, sp, _ = x_prompt.shape
    bs, ts, _ = x_sample.shape
    n_keep = min(max(w for w, _ in DILATED_PATTERNS), sp)
    n_past = cache_attn_k.shape[2]
    lb_p = jax.nn.softmax(hgrn_lb_raw.astype(F32), axis=0)
    hgrn_lb = jnp.cumsum(lb_p, axis=0) - lb_p[0]
    bias_dist = _bias_by_distance(rel_bias)
    p_tables = _prompt_tables(bias_dist)
    s_tables = _sample_tables(bias_dist, n_past, ts)
    cache_kt = jnp.transpose(cache_attn_k, (0, 1, 3, 4, 2)).reshape(depth, bs, ATTN_WIDTH, n_past)
    cache_vt = jnp.transpose(cache_attn_v, (0, 1, 3, 4, 2)).reshape(depth, bs, ATTN_WIDTH, n_past)

    hp = x_prompt.reshape(bp * sp, D_MODEL)
    hs = jnp.zeros((bs, TPAD, D_MODEL), F32).at[:, :ts].set(x_sample).reshape(bs * TPAD, D_MODEL)
    zero_states = (jnp.zeros((bp, CONV_WIDTH - 1, 2 * MLSTM_WIDTH), F32),
                   jnp.zeros((bp, N_MLSTM_HEADS, HEAD_DIM, HEAD_DIM), F32),
                   jnp.zeros((bp, N_MLSTM_HEADS, HEAD_DIM), F32),
                   jnp.zeros((bp, N_MLSTM_HEADS), F32),
                   jnp.zeros((bp, N_HGRN_HEADS, HEAD_DIM, HEAD_DIM), F32))
    p_states, s_states = [], []
    kv_t = None
    w_in_b, w_out_b, w_up_b, w_down_b = _prep_w_in(w_in), w_out.astype(BF16), w_up.astype(BF16), w_down.astype(BF16)
    for l in range(depth):
        final = l == depth - 1
        weights = _LayerWeights(
            layer=l, g_mix=g_mix[l][None], w_in=w_in_b,
            gate_b=_gate_row(mlstm_gate_b[l]),
            conv_w=mlstm_conv_w[l], mlstm_g=mlstm_norm_g[l][None], lb=hgrn_lb[l][None], hgrn_g=hgrn_norm_g[l][None],
            w_out=w_out_b, g_ffn=g_ffn[l][None], w_up=w_up_b, w_down=w_down_b, g_final=g_final[None])
        hp, st = _trunk_layer(hp, bp, sp, sp, functools.partial(_attn_prompt, bias=p_tables),
                              zero_states, weights, final, MLSTM_CHUNK, HGRN_SUBCHUNK, n_keep, (l, depth, kv_t))
        kv_t = st[:2]
        p_states.append(st[2:])
        sample_attn = functools.partial(_attn_sample, cache_kt=cache_kt, cache_vt=cache_vt, layer=l, tables=s_tables)
        states = (state_mlstm_conv[l], state_mlstm_C[l], state_mlstm_n[l], state_mlstm_m[l], state_hgrn_S[l])
        hs, st = _trunk_layer(hs, bs, TPAD, ts, sample_attn, states, weights, final, TPAD, TPAD)
        s_states.append(st)
    y_prompt = hp.reshape(bp, sp, D_MODEL)
    y_sample = hs.reshape(bs, TPAD, D_MODEL)[:, :ts]
    to_rows = lambda t: jnp.transpose(t.reshape(depth, bp, N_ATTN_HEADS, HEAD_DIM, n_keep), (0, 1, 4, 2, 3))
    p_out = [to_rows(kv_t[0]), to_rows(kv_t[1])] + [jnp.stack(a) for a in zip(*p_states)]
    s_out = [jnp.stack(a) for a in zip(*s_states)]
    return (y_prompt, y_sample, *p_out, *s_out)
```

```python
import functools
from typing import Callable, NamedTuple

import jax
import jax.numpy as jnp
import numpy as np
from jax import lax
from jax.experimental import pallas as pl
from jax.experimental.pallas import tpu as pltpu

F32 = jnp.float32
BF16 = jnp.bfloat16

D_MODEL = 1024
HEAD_DIM = 64
N_ATTN_HEADS = 8
N_MLSTM_HEADS = 4
N_HGRN_HEADS = 4
ATTN_WIDTH = N_ATTN_HEADS * HEAD_DIM
MLSTM_WIDTH = N_MLSTM_HEADS * HEAD_DIM
HGRN_WIDTH = N_HGRN_HEADS * HEAD_DIM
DILATED_PATTERNS = ((128, 1), (512, 4), (2048, 16))
QBLOCK = 128
N_REL_BUCKETS = 32
REL_MAX_DISTANCE = 2048
CONV_WIDTH = 4
MLSTM_CHUNK = 64
D_FF = 4 * D_MODEL
EPS = 1e-6
NEG_BIG = -1e30
LOG2E = 1.4426950408889634

LANES = 128
SUBLANES = 8

COL_AQ = 0
COL_AK = COL_AQ + ATTN_WIDTH
COL_AV = COL_AK + ATTN_WIDTH
COL_MQK = COL_AV + ATTN_WIDTH
COL_MV = COL_MQK + 2 * MLSTM_WIDTH
COL_MO = COL_MV + MLSTM_WIDTH
COL_HQ = COL_MO + MLSTM_WIDTH
COL_HF = COL_HQ + HGRN_WIDTH
COL_HI = COL_HF + HGRN_WIDTH
COL_HG = COL_HI + HGRN_WIDTH
COL_GATE = COL_HG + HGRN_WIDTH
ZW = COL_GATE + LANES

VMEM_LIMIT = 56 * 1024 * 1024


def _cparams(sem, vmem=VMEM_LIMIT):
    return pltpu.CompilerParams(dimension_semantics=sem, vmem_limit_bytes=vmem)


def _const_spec(shape):
    nd = len(shape)
    return pl.BlockSpec(shape, lambda *_: (0,) * nd, pipeline_mode=pl.Buffered(1))


def _layer_spec(shape, layer):
    nd = len(shape)
    return pl.BlockSpec((None,) + tuple(shape), lambda *_: (layer,) + (0,) * nd, pipeline_mode=pl.Buffered(1))


class _CallSpec(NamedTuple):
    body: Callable
    grid: tuple
    args: list
    in_specs: list
    out_shape: list
    out_specs: list
    scratch: list


def _run(specs, sem, name):
    grid = specs[0].grid
    assert all(s.grid == grid for s in specs)
    n_in = [len(s.args) for s in specs]
    n_out = [len(s.out_shape) for s in specs]
    n_scr = [len(s.scratch) for s in specs]

    def body(*refs):
        ins, outs, scrs = refs[:sum(n_in)], refs[sum(n_in):sum(n_in) + sum(n_out)], refs[sum(n_in) + sum(n_out):]

        for k, s in enumerate(specs):
            a, b, c = sum(n_in[:k]), sum(n_out[:k]), sum(n_scr[:k])
            s.body(*ins[a:a + n_in[k]], *outs[b:b + n_out[k]], *scrs[c:c + n_scr[k]])

    flat = pl.pallas_call(
        body, grid=grid,
        in_specs=[x for s in specs for x in s.in_specs],
        out_shape=[x for s in specs for x in s.out_shape],
        out_specs=[x for s in specs for x in s.out_specs],
        scratch_shapes=[x for s in specs for x in s.scratch],
        compiler_params=_cparams(sem), name=name,
    )(*[x for s in specs for x in s.args])
    out, k = [], 0
    for n in n_out:
        out.append(list(flat[k:k + n]))
        k += n
    return out


def _inproj_kernel(x_ref, g_ref, wa_ref, wb_ref, *rest, tiles_per_seq, first_keep, n_prev=0):
    x = x_ref[...]
    ms = jnp.mean(x * x, axis=-1, keepdims=True)
    xn = ((x * lax.rsqrt(ms + EPS)) * g_ref[...]).astype(BF16)
    z_ref = rest[n_prev]
    z_ref[:, 0:COL_HQ] = jnp.dot(xn, wa_ref[...], preferred_element_type=F32)
    z_ref[:, COL_HQ:] = jnp.dot(xn, wb_ref[...], preferred_element_type=F32)
    if first_keep is None:
        return
    kt_ref, vt_ref = rest[n_prev + 1:]

    @pl.when(pl.program_id(0) % tiles_per_seq >= first_keep)
    def _():
        kt_ref[...] = z_ref[:, COL_AK:COL_AK + ATTN_WIDTH].T
        vt_ref[...] = z_ref[:, COL_AV:COL_AV + ATTN_WIDTH].T


def _inproj(x2d, g, w, layer, seq=None, n_keep=None, stacked=None):
    n = x2d.shape[0]
    tm = min(512, n)
    assert n % tm == 0
    x_spec = pl.BlockSpec((tm, D_MODEL), lambda i: (i, 0))
    z_spec = pl.BlockSpec((tm, ZW), lambda i: (i, 0))
    z_shape = jax.ShapeDtypeStruct((n, ZW), F32)
    w_specs = [_layer_spec((D_MODEL, COL_HQ), layer), _layer_spec((D_MODEL, ZW - COL_HQ), layer)]
    if stacked is None:
        return pl.pallas_call(
            functools.partial(_inproj_kernel, tiles_per_seq=None, first_keep=None),
            out_shape=z_shape, grid=(n // tm,),
            in_specs=[x_spec, _const_spec((1, D_MODEL))] + w_specs,
            out_specs=z_spec, compiler_params=_cparams(("parallel",)), name="inproj",
        )(x2d, g, *w)
    assert seq % tm == 0 and n_keep % tm == 0
    layer, depth, prev = stacked
    tps, first_keep = seq // tm, (seq - n_keep) // tm
    t_spec = pl.BlockSpec((None, None, ATTN_WIDTH, tm),
                          lambda i: (layer, i // tps, 0, jnp.maximum(i % tps - first_keep, 0)))
    t_shape = jax.ShapeDtypeStruct((depth, n // seq, ATTN_WIDTH, n_keep), F32)
    in_specs = [x_spec, _const_spec((1, D_MODEL))] + w_specs
    args = [x2d, g, *w]
    if prev is None:
        prev = (jnp.zeros(t_shape.shape, F32), jnp.zeros(t_shape.shape, F32))
    in_specs += [pl.BlockSpec(memory_space=pl.ANY)] * 2
    aliases = {len(args): 1, len(args) + 1: 2}
    args += list(prev)
    return pl.pallas_call(
        functools.partial(_inproj_kernel, tiles_per_seq=tps, first_keep=first_keep, n_prev=len(prev)),
        out_shape=[z_shape, t_shape, t_shape], grid=(n // tm,),
        in_specs=in_specs, out_specs=[z_spec, t_spec, t_spec], input_output_aliases=aliases,
        compiler_params=_cparams(("arbitrary",)), name="inproj_kt",
    )(*args)


FF_CHUNK = 1024
FFN_ROWS = 512


def _outffn_kernel(x_ref, a_ref, m_ref, h_ref, wo_ref, gf_ref, wu_ref, wd_ref, gl_ref, y_ref, xn_sc, *, final):
    x1 = x_ref[...]
    x1 = x1 + jnp.dot(a_ref[...], wo_ref[0:ATTN_WIDTH, :], preferred_element_type=F32)
    x1 = x1 + jnp.dot(m_ref[...], wo_ref[ATTN_WIDTH:ATTN_WIDTH + MLSTM_WIDTH, :], preferred_element_type=F32)
    x1 = x1 + jnp.dot(h_ref[...], wo_ref[ATTN_WIDTH + MLSTM_WIDTH:, :], preferred_element_type=F32)
    ms = jnp.mean(x1 * x1, axis=-1, keepdims=True)
    xn_sc[...] = ((x1 * lax.rsqrt(ms + EPS)) * gf_ref[...]).astype(BF16)
    y_ref[...] = x1
    nchunk = D_FF // FF_CHUNK
    up = lambda c: jnp.dot(xn_sc[...], wu_ref[:, c * FF_CHUNK:(c + 1) * FF_CHUNK], preferred_element_type=F32)
    u = up(0)
    for c in range(nchunk):
        u_next = up(c + 1) if c + 1 < nchunk else None
        hh = jnp.square(jnp.maximum(u, 0.0)).astype(BF16)
        y_ref[...] += jnp.dot(hh, wd_ref[c * FF_CHUNK:(c + 1) * FF_CHUNK, :], preferred_element_type=F32)
        u = u_next
    if final:
        x2 = y_ref[...]
        ms2 = jnp.mean(x2 * x2, axis=-1, keepdims=True)
        y_ref[...] = (x2 * lax.rsqrt(ms2 + EPS)) * gl_ref[...]


def _outffn(x2d, attn, ml, hg, w_out, g_ffn, w_up, w_down, g_final, final, layer):
    n = x2d.shape[0]
    tm = min(FFN_ROWS, n)
    assert n % tm == 0
    row = lambda w: pl.BlockSpec((tm, w), lambda i: (i, 0))
    return pl.pallas_call(
        functools.partial(_outffn_kernel, final=final),
        out_shape=jax.ShapeDtypeStruct((n, D_MODEL), F32),
        grid=(n // tm,),
        in_specs=[row(D_MODEL), row(ATTN_WIDTH), row(MLSTM_WIDTH), row(HGRN_WIDTH),
                  _layer_spec((D_MODEL, D_MODEL), layer), _const_spec((1, D_MODEL)),
                  _layer_spec((D_MODEL, D_FF), layer), _layer_spec((D_FF, D_MODEL), layer), _const_spec((1, D_MODEL))],
        out_specs=row(D_MODEL),
        scratch_shapes=[pltpu.VMEM((tm, D_MODEL), BF16)],
        compiler_params=_cparams(("parallel",)),
        name="outffn",
    )(x2d, attn, ml, hg, w_out, g_ffn, w_up, w_down, g_final)


def _t5_causal_bucket(dist):
    n = np.asarray(dist).astype(np.int32)
    max_exact = N_REL_BUCKETS // 2
    scaled = np.log(np.maximum(n, 1) / max_exact) / np.log(REL_MAX_DISTANCE / max_exact)
    large = np.minimum(max_exact + (scaled * (N_REL_BUCKETS - max_exact)).astype(np.int32), N_REL_BUCKETS - 1)
    return np.where(n < max_exact, n, large).astype(np.int32)


BIAS_DIST = 2304


def _bias_by_distance(rel_bias):
    assert BIAS_DIST > max(w for w, _ in DILATED_PATTERNS) + SUBLANES
    return rel_bias.astype(F32)[_t5_causal_bucket(np.arange(BIAS_DIST)[::-1])].T


def _dist_slice(bias_desc, lo, hi, step=1):
    last = BIAS_DIST - 1
    return bias_desc[:, last - hi:last - lo + 1:step]


def _prompt_tables(bias_dist):
    H = N_ATTN_HEADS
    a = np.arange(QBLOCK)[:, None]
    b = np.arange(2 * QBLOCK)[None, :]
    rel = QBLOCK + a - b
    band = (rel >= 0) & (rel <= QBLOCK)
    period = 3 * QBLOCK
    biases = []
    for window, dil in DILATED_PATTERNS:
        nsub = window // dil
        assert nsub == QBLOCK
        vd = _dist_slice(bias_dist, 0, nsub * dil, dil)
        rp = jnp.concatenate([vd, jnp.broadcast_to(vd[:, -1:], (H, QBLOCK)),
                              jnp.broadcast_to(vd[:, 0:1], (H, QBLOCK - 1))], axis=1)
        skew = jnp.tile(rp, (1, QBLOCK))[:, :QBLOCK * (period - 1)].reshape(H, QBLOCK, period - 1)
        biases.append(skew[:, :, :2 * QBLOCK].reshape(H // 2, 2 * QBLOCK, 2 * QBLOCK))
    mask = np.stack([np.tile(band, (2, 1)), np.tile(band & (b >= QBLOCK), (2, 1))])
    table = jnp.where(mask[None, None], jnp.stack(biases)[:, :, None] * LOG2E, -jnp.inf)
    return jnp.swapaxes(table, -1, -2)


ATTN_GROUP = 8


def _attn_prompt_kernel(q_ref, k_ref, v_ref, bias_ref, out_ref,
                        x4, qs, ks, vts, o1, o4, o16, l1, l4, l16, p_scr, st_scr, *, seq):
    S = seq
    nblk = S // QBLOCK
    scale = HEAD_DIM ** -0.5 * LOG2E
    zpad = jnp.zeros((QBLOCK, LANES), BF16)
    for di in range(3):
        ks[di, 0:QBLOCK, :] = zpad
        vts[di, 0] = zpad

    R = 512

    def put(src, di, row0, blk):
        if src is q_ref:
            qs[di, row0:row0 + blk.shape[0], :] = (blk * scale).astype(BF16)
        elif src is k_ref:
            ks[di, QBLOCK + row0:QBLOCK + row0 + blk.shape[0], :] = blk.astype(BF16)
        else:
            blk_t = blk.T.astype(BF16)
            for t in range(blk.shape[0] // QBLOCK):
                vts[di, 1 + row0 // QBLOCK + t] = blk_t[:, t * QBLOCK:(t + 1) * QBLOCK]

    for src in (q_ref, k_ref, v_ref):
        for i in range(S // R):
            put(src, 0, i * R, src[i * R:(i + 1) * R, :])
        for r in range(4):
            for i in range(S // 4 // R):
                blk = src[pl.ds(r + 4 * i * R, R, stride=4), :]
                row0 = r * (S // 4) + i * R
                x4[row0:row0 + R, :] = blk
                put(src, 1, row0, blk)
        n16 = S // 16
        for c4 in range(4):
            for a in range(4):
                put(src, 2, (c4 * 4 + a) * n16, x4[pl.ds(c4 * (S // 4) + a, n16, stride=4), :])

    lane = lax.broadcasted_iota(jnp.int32, (QBLOCK, LANES), 1)
    head_a = lane < HEAD_DIM
    row_a = lax.broadcasted_iota(jnp.int32, (QBLOCK, LANES), 0) < HEAD_DIM
    nt = (((1,), (1,)), ((), ()))

    def branch(di, dil, o_ref, l_ref):
        per_class = (S // dil) // QBLOCK

        R = range(ATTN_GROUP)
        zero = jnp.zeros((QBLOCK, LANES), BF16)

        def head_block(j):
            return per_class <= ATTN_GROUP and j % per_class == 0

        assert per_class % ATTN_GROUP == 0 or ATTN_GROUP % per_class == 0

        def scores(g, js):
            out = {}
            for j in js:
                i = g * ATTN_GROUP + j
                base = pl.multiple_of(i * QBLOCK, QBLOCK)
                q = qs[di, pl.ds(base, QBLOCK), :]
                q2 = jnp.concatenate([jnp.where(head_a, q, zero), jnp.where(head_a, zero, q)], axis=0)
                if head_block(j):
                    kk = ks[di, pl.ds(base + QBLOCK, QBLOCK), :]
                else:
                    kk = ks[di, pl.ds(base, 2 * QBLOCK), :]
                out[j] = lax.dot_general(kk, q2, nt, preferred_element_type=F32)
            return out

        def softmax(g, slot, st):
            for j in st:
                i = g * ATTN_GROUP + j
                if head_block(j):
                    s = st[j] + bias_ref[di, 1, QBLOCK:, :]
                else:
                    s = st[j] + bias_ref[di, jnp.where(i % per_class == 0, 1, 0)]
                m = jnp.max(s, axis=0, keepdims=True)
                p = jnp.exp2(s - m)
                if head_block(j):
                    p_scr[slot, j, QBLOCK:, :] = p.astype(BF16)
                else:
                    p_scr[slot, j] = p.astype(BF16)
                st_scr[slot, j, 0:1, :] = m
                st_scr[slot, j, 1:2, :] = jnp.sum(p, axis=0, keepdims=True)

        def weighted_values(g, slot, js):
            out = {}
            for j in js:
                i = g * ATTN_GROUP + j
                if head_block(j):
                    ot = jnp.dot(vts[di, i + 1], p_scr[slot, j, QBLOCK:, :], preferred_element_type=F32)
                else:
                    vt = jnp.concatenate([vts[di, i], vts[di, i + 1]], axis=1)
                    ot = jnp.dot(vt, p_scr[slot, j], preferred_element_type=F32)
                out[j] = (ot, st_scr[slot, j, 0:1, :], st_scr[slot, j, 1:2, :])
            return out

        def outputs(g, pv):
            for j in pv:
                i = g * ATTN_GROUP + j
                base = pl.multiple_of(i * QBLOCK, QBLOCK)
                ot, m, l = pv[j]
                otn = ot / l
                lse = m + jnp.log2(l)
                o = jnp.where(row_a, otn[:, 0:QBLOCK], otn[:, QBLOCK:]).T
                ls = jnp.where(row_a, jnp.broadcast_to(lse[:, 0:QBLOCK], (QBLOCK, LANES)),
                               jnp.broadcast_to(lse[:, QBLOCK:], (QBLOCK, LANES))).T
                if dil == 1:
                    o_ref[pl.ds(base, QBLOCK), :] = o
                    l_ref[pl.ds(base, QBLOCK), :] = ls
                else:
                    c, n = i // per_class, i % per_class
                    res = c if dil == 4 else 4 * (c % 4) + c // 4
                    start = dil * QBLOCK * n + res
                    o_ref[pl.ds(start, QBLOCK, stride=dil), :] = o
                    l_ref[pl.ds(start, QBLOCK, stride=dil), :] = ls

        ngroup = nblk // ATTN_GROUP
        softmax(0, 0, scores(0, R))

        def body(g, carry):
            slot = g % 2
            st = scores(g, R)
            pv = weighted_values(g - 1, 1 - slot, R)
            softmax(g, slot, st)
            outputs(g - 1, pv)
            return carry

        lax.fori_loop(1, ngroup, body, 0)
        outputs(ngroup - 1, weighted_values(ngroup - 1, (ngroup - 1) % 2, R))

    branch(0, 1, o1, l1)
    branch(1, 4, o4, l4)
    branch(2, 16, o16, l16)

    T = 256

    def mix(i, carry):
        r0 = pl.multiple_of(i * T, T)
        la, lb, lc = l1[pl.ds(r0, T), :], l4[pl.ds(r0, T), :], l16[pl.ds(r0, T), :]
        mx = jnp.maximum(jnp.maximum(la, lb), lc)
        ea, eb, ec = jnp.exp2(la - mx), jnp.exp2(lb - mx), jnp.exp2(lc - mx)
        num = ea * o1[pl.ds(r0, T), :] + eb * o4[pl.ds(r0, T), :] + ec * o16[pl.ds(r0, T), :]
        out_ref[pl.ds(r0, T), :] = (num / (ea + eb + ec)).astype(out_ref.dtype)
        return carry

    lax.fori_loop(0, S // T, mix, 0)


def _attn_prompt(z3, bias):
    B, S, _ = z3.shape
    assert S % (16 * QBLOCK) == 0
    npair = N_ATTN_HEADS // 2
    col = lambda c0: pl.BlockSpec((None, S, LANES), lambda b, p: (b, 0, c0 // LANES + p))
    f32s = lambda: pltpu.VMEM((S, LANES), F32)
    return pl.pallas_call(
        functools.partial(_attn_prompt_kernel, seq=S),
        out_shape=jax.ShapeDtypeStruct((B, S, ATTN_WIDTH), BF16),
        grid=(B, npair),
        in_specs=[col(COL_AQ), col(COL_AK), col(COL_AV),
                  pl.BlockSpec((3, None, 2, 2 * QBLOCK, 2 * QBLOCK), lambda b, p: (0, p, 0, 0, 0))],
        out_specs=pl.BlockSpec((None, S, LANES), lambda b, p: (b, 0, p)),
        scratch_shapes=[f32s(),
                        pltpu.VMEM((3, S, LANES), BF16),
                        pltpu.VMEM((3, S + QBLOCK, LANES), BF16),
                        pltpu.VMEM((3, S // QBLOCK + 1, LANES, QBLOCK), BF16),
                        f32s(), f32s(), f32s(), f32s(), f32s(), f32s(),
                        pltpu.VMEM((2, ATTN_GROUP, 2 * QBLOCK, 2 * QBLOCK), BF16),
                        pltpu.VMEM((2, ATTN_GROUP, SUBLANES, 2 * QBLOCK), F32)],
        compiler_params=_cparams(("parallel", "parallel")),
        name="attn_prompt",
    )(z3, z3, z3, bias)


TPAD = SUBLANES
TAIL = 512


def _sample_tables(bias_dist, n_past, n_tok):
    assert n_past >= max(w for w, _ in DILATED_PATTERNS) and n_tok <= TPAD and TAIL >= DILATED_PATTERNS[1][0]
    H = N_ATTN_HEADS
    t = np.arange(TPAD)[:, None]

    def by_row(width):
        rows = [_dist_slice(bias_dist, tt + 1, tt + width) for tt in range(TPAD)]
        return jnp.stack(rows, axis=1).reshape(H * TPAD, width)

    bias_tail, bias_full = by_row(TAIL), by_row(n_past)
    zero_dist = _dist_slice(bias_dist, 0, 0)
    new_rows = [jnp.concatenate([_dist_slice(bias_dist, 0, tt),
                                 jnp.broadcast_to(zero_dist, (H, TPAD - tt - 1))], axis=1) for tt in range(TPAD)]
    bias_new = jnp.stack(new_rows, axis=1).reshape(H * TPAD, TPAD)

    def valid(dj, dil, nsub, lo):
        ok = (dj % dil == 0) & (dj // dil >= lo) & (dj // dil <= nsub) & (t < n_tok)
        return np.tile(ok, (H, 1)).astype(np.float32)

    tn = np.arange(TPAD)[None, :]
    m_tail, m_new = [], []
    for window, dil in DILATED_PATTERNS:
        nsub = window // dil
        if window <= TAIL:
            m_tail.append(valid(TAIL + t - np.arange(TAIL)[None, :], dil, nsub, 1))
        else:
            m_full = valid(n_past + t - np.arange(n_past)[None, :], dil, nsub, 1)
        m_new.append(valid(t - tn, dil, nsub, 0) * (tn < n_tok))
    return (bias_tail, bias_full, bias_new, jnp.asarray(np.stack(m_tail)), jnp.asarray(m_full),
            jnp.asarray(np.stack(m_new).astype(np.float32)))


def _attn_sample_kernel(q_ref, kn_ref, vn_ref, kt_ref, vt_ref, bt_ref, bf_ref, bn_ref, mt_ref, mf_ref, mn_ref,
                        out_ref, *, n_past):
    H = N_ATTN_HEADS
    rows = H * TPAD
    same_head = (_iota2((rows, ATTN_WIDTH), 0) >> 3) == (_iota2((rows, ATTN_WIDTH), 1) >> HEAD_SHIFT)
    q = q_ref[...] * (HEAD_DIM ** -0.5)
    qm = jnp.where(same_head, jnp.concatenate([q] * H, axis=0), 0.0).astype(BF16)
    nt = (((1,), (1,)), ((), ()))
    kn = kn_ref[...].astype(BF16)
    vn = vn_ref[...].astype(BF16)
    kt = kt_ref[...].astype(BF16)
    vt = vt_ref[...].astype(BF16)
    s_new = lax.dot_general(qm, kn, nt, preferred_element_type=F32)
    s_full = jnp.dot(qm, kt, preferred_element_type=F32)
    s_tail = s_full[:, n_past - TAIL:]

    outs, lses = [], []
    for di, (window, _) in enumerate(DILATED_PATTERNS):
        if window <= TAIL:
            sm = jnp.where(mt_ref[di] > 0.5, s_tail + bt_ref[...], NEG_BIG)
            vmain = vt[:, n_past - TAIL:]
        else:
            sm = jnp.where(mf_ref[...] > 0.5, s_full + bf_ref[...], NEG_BIG)
            vmain = vt
        sn = jnp.where(mn_ref[di] > 0.5, s_new + bn_ref[...], NEG_BIG)
        m = jnp.maximum(jnp.max(sm, axis=-1, keepdims=True), jnp.max(sn, axis=-1, keepdims=True))
        pm = jnp.exp(sm - m)
        pn = jnp.exp(sn - m)
        l = jnp.sum(pm, axis=-1, keepdims=True) + jnp.sum(pn, axis=-1, keepdims=True)
        o = jnp.dot(pn.astype(BF16), vn, preferred_element_type=F32)
        o = o + lax.dot_general(pm.astype(BF16), vmain, nt, preferred_element_type=F32)
        outs.append(o / l)
        lses.append(m + jnp.log(l))
    mx = jnp.maximum(jnp.maximum(lses[0], lses[1]), lses[2])
    es = [jnp.exp(ls - mx) for ls in lses]
    mixed = (es[0] * outs[0] + es[1] * outs[1] + es[2] * outs[2]) / (es[0] + es[1] + es[2])
    mixed = jnp.where(same_head, mixed, 0.0)
    acc = mixed[0:TPAD]
    for h in range(1, H):
        acc = acc + mixed[h * TPAD:(h + 1) * TPAD]
    out_ref[...] = acc.astype(out_ref.dtype)


def _attn_sample(z3, cache_kt, cache_vt, layer, tables, host_grid=None):
    B, T, _ = z3.shape
    assert T == TPAD
    n_past = cache_kt.shape[-1]
    seq, _ = _host_maps(host_grid)
    new = lambda c0: pl.BlockSpec((None, TPAD, ATTN_WIDTH), lambda b, t: (seq(b, t), 0, c0 // ATTN_WIDTH))
    buf = pl.BlockSpec((None, None, ATTN_WIDTH, n_past), lambda b, t: (layer, seq(b, t), 0, 0))
    return _CallSpec(
        body=functools.partial(_attn_sample_kernel, n_past=n_past),
        grid=host_grid or (B, 1),
        args=[z3, z3, z3, cache_kt, cache_vt, *tables],
        in_specs=[new(COL_AQ), new(COL_AK), new(COL_AV), buf, buf] + [_const_spec(t.shape) for t in tables],
        out_shape=[jax.ShapeDtypeStruct((B, TPAD, ATTN_WIDTH), BF16)],
        out_specs=[pl.BlockSpec((None, TPAD, ATTN_WIDTH), lambda b, t: (seq(b, t), 0, 0))],
        scratch=[])


QUAD = N_MLSTM_HEADS * HEAD_DIM
CHUNK_UNROLL = 2
CHUNK_GROUP = 8
HEAD_SHIFT = 6


def _iota2(shape, axis):
    return lax.broadcasted_iota(jnp.int32, shape, axis)


def _log2(n):
    k = int(n).bit_length() - 1
    assert 1 << k == n
    return k


def _seg_mask(rows, row_shift, cols, col_shift):
    return (_iota2((rows, cols), 0) >> row_shift) == (_iota2((rows, cols), 1) >> col_shift)


def _cumulate_rows(x, length, op, fill):
    row = _iota2(x.shape, 0) & (length - 1)
    sh = 1
    while sh < length:
        x = op(x, jnp.where(row >= sh, pltpu.roll(x, sh, axis=0), fill))
        sh *= 2
    return x


def _split2(x):
    hi = x.astype(BF16)
    lo = (x - hi.astype(F32)).astype(BF16)
    return hi, lo


def _dot2(x, w):
    hi, lo = _split2(x)
    return jnp.dot(hi, w, preferred_element_type=F32) + jnp.dot(lo, w, preferred_element_type=F32)


def _dot2r(w, x):
    hi, lo = _split2(x)
    return jnp.dot(w, hi, preferred_element_type=F32) + jnp.dot(w, lo, preferred_element_type=F32)


def _tile_position(single_tile):
    if single_tile:
        return 0, True
    return pl.program_id(1), pl.program_id(1) == pl.num_programs(1) - 1


def _when(cond):
    if isinstance(cond, bool):
        return (lambda f: f()) if cond else (lambda f: None)
    return pl.when(cond)


def _host_maps(host_grid):
    if host_grid is None:
        return (lambda b, t: b), (lambda b, t: t)
    return (lambda b, t: b * host_grid[1] + t), (lambda b, t: 0)


def _load_block_diag(ref, seg64):
    flat = ref[...].reshape(QUAD, HEAD_DIM)
    return jnp.where(seg64, jnp.concatenate([flat] * (QUAD // HEAD_DIM), axis=1), 0.0)


def _store_block_diag(ref, mat):
    for h in range(QUAD // HEAD_DIM):
        ref[h] = mat[h * HEAD_DIM:(h + 1) * HEAD_DIM, h * HEAD_DIM:(h + 1) * HEAD_DIM]


def _head_rmsnorm_gate(h, seg64b, g_row, gate_pre):
    ms = _dot2(h * h, seg64b) * (1.0 / HEAD_DIM)
    return jax.nn.sigmoid(gate_pre) * (h * lax.rsqrt(ms + EPS) * g_row)


def _mlstm_kernel(qk_ref, v_ref, o_ref, gate_ref, conv0_ref, c0_ref, n0_ref, m0_ref, gb_ref, cw_ref, ng_ref,
                  out_ref, c_out, n_out, m_out, xp, cs, ns, ms, *, tile, chunk, n_valid, single_tile=False):
    TS, L = tile, chunk
    t, last = _tile_position(single_tile)
    PAD = SUBLANES

    @_when(t == 0)
    def _():
        cs[...] = _load_block_diag(c0_ref, _seg_mask(QUAD, HEAD_SHIFT, QUAD, HEAD_SHIFT))
        ns[...] = n0_ref[...]
        ms[...] = m0_ref[...]
        xp[0:PAD, :] = conv0_ref[...]

    @_when(t > 0)
    def _():
        xp[0:PAD, :] = xp[TS:TS + PAD, :]

    xp[PAD:PAD + TS, :] = qk_ref[...]

    LK = MLSTM_CHUNK
    seg64 = _seg_mask(QUAD, HEAD_SHIFT, QUAD, HEAD_SHIFT)
    seg64b = seg64.astype(BF16)
    dmask = (_iota2((LK, QUAD), 1) & (LK - 1)) == _iota2((LK, QUAD), 0)
    causal = (_iota2((L, QUAD), 1) & (LK - 1)) <= _iota2((L, QUAD), 0)
    tril = (_iota2((L, L), 1) <= _iota2((L, L), 0)).astype(BF16)
    ones_lk = jnp.ones((L, LK), BF16)
    row = _iota2((L, QUAD), 0)
    cw = cw_ref[...]
    gb = gb_ref[...]
    ng = ng_ref[...]

    first_half = _iota2((L, LANES), 1) < HEAD_DIM

    def per_head_lanes(g, lane0):
        col = [jnp.broadcast_to(g[:, lane0 + h:lane0 + h + 1], (L, LANES)) for h in range(N_MLSTM_HEADS)]
        return jnp.concatenate([jnp.where(first_half, col[0], col[1]), jnp.where(first_half, col[2], col[3])], axis=1)

    def key_rows(x, fill):
        if L == LK:
            return x
        return jnp.concatenate([x, jnp.full((LK - L, QUAD), fill, x.dtype)], axis=0)

    G = min(CHUNK_GROUP, TS // L)
    assert TS % (L * G) == 0
    nt = (((1,), (1,)), ((), ()))
    tn = (((0,), (0,)), ((), ()))
    zb = jnp.zeros((QUAD, QUAD), BF16)

    def group_body(gi, carry):
        R = range(G)
        r0 = [pl.multiple_of((gi * G + j) * L, L) for j in R]
        q, k, v, ig, lf = [], [], [], [], []
        for j in R:
            win = xp[pl.ds(r0[j], L + PAD), :]
            acc = win[PAD:PAD + L] * cw[CONV_WIDTH - 1:CONV_WIDTH, :]
            for s in range(1, CONV_WIDTH):
                acc = acc + win[PAD - s:PAD - s + L] * cw[CONV_WIDTH - 1 - s:CONV_WIDTH - s, :]
            qk = acc * jax.nn.sigmoid(acc)
            q.append(qk[:, 0:QUAD])
            k.append(qk[:, QUAD:] * (HEAD_DIM ** -0.5))
            v.append(v_ref[pl.ds(r0[j], L), :])
            gate = gate_ref[pl.ds(r0[j], L), :] + gb
            logsig = jnp.minimum(gate, 0.0) - jnp.log(1.0 + jnp.exp(-jnp.abs(gate)))
            ig_j = per_head_lanes(gate, 0)
            lf_j = per_head_lanes(logsig, N_MLSTM_HEADS)
            if n_valid < TS:
                ok = (row + r0[j]) < n_valid
                ig_j = jnp.where(ok, ig_j, NEG_BIG)
                lf_j = jnp.where(ok, lf_j, 0.0)
            ig.append(ig_j)
            lf.append(lf_j)
        b = [_dot2r(tril, lf[j]) for j in R]
        a = [ig[j] - b[j] for j in R]
        cm = [_cumulate_rows(a[j], L, jnp.maximum, -jnp.inf) for j in R]
        arow = [_dot2r(ones_lk, jnp.where(dmask, key_rows(a[j], NEG_BIG), 0.0)) for j in R]
        qb = [q[j].astype(BF16) for j in R]
        vb = [v[j].astype(BF16) for j in R]
        kbd = [jnp.where(seg64, jnp.concatenate([key_rows(k[j], 0.0).astype(BF16)] * N_MLSTM_HEADS, axis=0), zb) for j in R]
        vbd = [jnp.where(seg64, jnp.concatenate([key_rows(v[j], 0.0).astype(BF16)] * N_MLSTM_HEADS, axis=0), zb) for j in R]
        qkt = [lax.dot_general(qb[j], kbd[j], nt, preferred_element_type=F32) for j in R]

        mprev = ms[...]
        M, gg, emt = [], [], []
        for j in R:
            M.append(jnp.maximum(cm[j], mprev))
            mt = b[j] + M[j]
            gg.append(jnp.exp(mprev - M[j]))
            emt.append(jnp.exp(-mt))
            mprev = mt[L - 1:L, :]
        ms[...] = mprev

        wts = [jnp.where(causal, jnp.exp(arow[j] - M[j]), 0.0) * qkt[j] for j in R]
        kd = [k[j] * jnp.exp(a[j] - M[j][L - 1:L, :]) for j in R]
        num = [jnp.dot(wts[j].astype(BF16), vbd[j], preferred_element_type=F32) for j in R]
        u = [lax.dot_general(kd[j].astype(BF16), vb[j], tn, preferred_element_type=F32) for j in R]

        cmat, nvec = [cs[...]], [ns[...]]
        for j in R:
            gl = gg[j][L - 1:L, :]
            cmat.append(gl * cmat[j] + jnp.where(seg64, u[j], 0.0))
            nvec.append(gl * nvec[j] + jnp.sum(kd[j], axis=0, keepdims=True))
        cs[...] = cmat[G]
        ns[...] = nvec[G]

        inter = [jnp.dot(qb[j], cmat[j].astype(BF16), preferred_element_type=F32) for j in R]
        den = [_dot2(wts[j] + gg[j] * (q[j] * nvec[j]), seg64b) for j in R]
        h = [(num[j] + gg[j] * inter[j]) / jnp.maximum(jnp.abs(den[j]), emt[j]) for j in R]
        msq = [_dot2(h[j] * h[j], seg64b) * (1.0 / HEAD_DIM) for j in R]
        for j in R:
            y = h[j] * lax.rsqrt(msq[j] + EPS) * ng
            out_ref[pl.ds(r0[j], L), :] = (jax.nn.sigmoid(o_ref[pl.ds(r0[j], L), :]) * y).astype(out_ref.dtype)
        return carry

    lax.fori_loop(0, TS // (L * G), group_body, 0)

    @_when(last)
    def _():
        _store_block_diag(c_out, cs[...])
        n_out[...] = ns[...]
        m_out[...] = ms[...]


def _mlstm(z3, conv0, c0bd, n0, m0, gate_b, conv_w, norm_g, n_valid, chunk, host_grid=None):
    B, S, _ = z3.shape
    ts = min(512, S)
    assert S % ts == 0 and ts % chunk == 0 and (host_grid is None or S == ts)
    seq, til = _host_maps(host_grid)
    blk = lambda w, c0: pl.BlockSpec((None, ts, w), lambda b, t: (seq(b, t), til(b, t), c0 // w))
    per_b = lambda r, w: pl.BlockSpec((None, r, w), lambda b, t: (seq(b, t), 0, 0))
    per_head = pl.BlockSpec((None, N_MLSTM_HEADS, HEAD_DIM, HEAD_DIM), lambda b, t: (seq(b, t), 0, 0, 0))
    return _CallSpec(
        body=functools.partial(_mlstm_kernel, tile=ts, chunk=chunk, n_valid=n_valid, single_tile=host_grid is not None),
        grid=host_grid or (B, S // ts),
        args=[z3, z3, z3, z3, conv0, c0bd, n0, m0, gate_b, conv_w, norm_g],
        in_specs=[blk(2 * QUAD, COL_MQK), blk(QUAD, COL_MV), blk(QUAD, COL_MO), blk(LANES, COL_GATE),
                  per_b(SUBLANES, 2 * QUAD), per_head, per_b(1, QUAD), per_b(1, QUAD),
                  _const_spec((1, LANES)), _const_spec((CONV_WIDTH, 2 * QUAD)), _const_spec((1, QUAD))],
        out_shape=[jax.ShapeDtypeStruct((B, S, QUAD), BF16),
                   jax.ShapeDtypeStruct((B, N_MLSTM_HEADS, HEAD_DIM, HEAD_DIM), F32),
                   jax.ShapeDtypeStruct((B, 1, QUAD), F32),
                   jax.ShapeDtypeStruct((B, 1, QUAD), F32)],
        out_specs=[blk(QUAD, 0), per_head, per_b(1, QUAD), per_b(1, QUAD)],
        scratch=[pltpu.VMEM((ts + 2 * SUBLANES, 2 * QUAD), F32),
                 pltpu.VMEM((QUAD, QUAD), F32), pltpu.VMEM((1, QUAD), F32), pltpu.VMEM((1, QUAD), F32)])


HGRN_FAST_CHUNK = 64
HGRN_GROUP = 8
HGRN_SAFE_DECAY = 80.0


def _hgrn_kernel(q_ref, f_ref, i_ref, g_ref, s0_ref, lb_ref, ng_ref, out_ref, s_out, st, qs, fs, ks, *,
                 tile, sub, n_valid, fast, single_tile=False):
    TS = tile
    t, last = _tile_position(single_tile)

    seg64 = _seg_mask(QUAD, HEAD_SHIFT, QUAD, HEAD_SHIFT)
    seg64b = seg64.astype(BF16)

    @_when(t == 0)
    def _():
        st[...] = _load_block_diag(s0_ref, seg64)
    lb = lb_ref[...]
    ng = ng_ref[...]
    nt = (((1,), (1,)), ((), ()))
    tn = (((0,), (0,)), ((), ()))

    P = HGRN_FAST_CHUNK if fast else sub
    prow = _iota2((P, QUAD), 0)

    def prep(c, worst):
        r0 = pl.multiple_of(c * P, P)
        hq = q_ref[pl.ds(r0, P), :]
        f = lb + (1.0 - lb) * jax.nn.sigmoid(f_ref[pl.ds(r0, P), :])
        logf = jnp.log(f)
        kk = 1.0 - f
        if n_valid < TS:
            ok = (prow + r0) < n_valid
            logf = jnp.where(ok, logf, 0.0)
            kk = jnp.where(ok, kk, 0.0)
        qs[pl.ds(r0, P), :] = hq * jax.nn.sigmoid(hq)
        fs[pl.ds(r0, P), :] = logf
        ks[pl.ds(r0, P), :] = kk
        return jnp.minimum(worst, jnp.sum(logf, axis=0, keepdims=True))

    worst = lax.fori_loop(0, TS // P, prep, jnp.zeros((1, QUAD), F32))

    def finish(o, r0, rows, smat, b, kk, iv, qt):
        o = o + lax.dot_general(qt, smat.astype(BF16), nt, preferred_element_type=F32)
        out_ref[pl.ds(r0, rows), :] = _head_rmsnorm_gate(o, seg64b, ng, g_ref[pl.ds(r0, rows), :]).astype(out_ref.dtype)
        bl = b[rows - 1:rows, :]
        ktil = kk * jnp.exp(bl - b)
        u = lax.dot_general(iv.astype(BF16), ktil.astype(BF16), tn, preferred_element_type=F32)
        st[...] = smat * jnp.exp(bl) + jnp.where(seg64, u, 0.0)

    def fast_loop():
        L = HGRN_FAST_CHUNK
        tril = (_iota2((L, L), 1) <= _iota2((L, L), 0)).astype(BF16)
        causal = (_iota2((L, QUAD), 1) & (L - 1)) <= _iota2((L, QUAD), 0)
        zb = jnp.zeros((QUAD, QUAD), BF16)

        G = min(HGRN_GROUP, TS // L)
        assert TS % (L * G) == 0

        def body(gi, carry):
            R = range(G)
            r0 = [pl.multiple_of((gi * G + j) * L, L) for j in R]
            kk = [ks[pl.ds(r0[j], L), :] for j in R]
            ivb = [i_ref[pl.ds(r0[j], L), :].astype(BF16) for j in R]
            b = [_dot2r(tril, fs[pl.ds(r0[j], L), :]) for j in R]
            qt = [(qs[pl.ds(r0[j], L), :] * jnp.exp(b[j])).astype(BF16) for j in R]
            kbd = [jnp.where(seg64, jnp.concatenate([(kk[j] * jnp.exp(-b[j])).astype(BF16)] * N_HGRN_HEADS, axis=0), zb)
                   for j in R]
            ibd = [jnp.where(seg64, jnp.concatenate([ivb[j]] * N_HGRN_HEADS, axis=0), zb) for j in R]
            amat = [lax.dot_general(qt[j], kbd[j], nt, preferred_element_type=F32) for j in R]
            ktil = [(kk[j] * jnp.exp(b[j][L - 1:L, :] - b[j])).astype(BF16) for j in R]
            u = [lax.dot_general(ivb[j], ktil[j], tn, preferred_element_type=F32) for j in R]
            o = [jnp.dot(jnp.where(causal, amat[j], 0.0).astype(BF16), ibd[j], preferred_element_type=F32) for j in R]
            smat = [st[...]]
            for j in R:
                smat.append(smat[j] * jnp.exp(b[j][L - 1:L, :]) + jnp.where(seg64, u[j], 0.0))
            st[...] = smat[G]
            inter = [lax.dot_general(qt[j], smat[j].astype(BF16), nt, preferred_element_type=F32) for j in R]
            o = [o[j] + inter[j] for j in R]
            msq = [_dot2(o[j] * o[j], seg64b) * (1.0 / HEAD_DIM) for j in R]
            for j in R:
                y = o[j] * lax.rsqrt(msq[j] + EPS) * ng
                out_ref[pl.ds(r0[j], L), :] = (jax.nn.sigmoid(g_ref[pl.ds(r0[j], L), :]) * y).astype(out_ref.dtype)
            return carry

        lax.fori_loop(0, TS // (L * G), body, 0)

    def exact_loop():
        L = sub
        row = _iota2((L, QUAD), 0)

        def body(c, carry):
            r0 = pl.multiple_of(c * L, L)
            q = qs[pl.ds(r0, L), :]
            kk = ks[pl.ds(r0, L), :]
            iv = i_ref[pl.ds(r0, L), :]
            b = _cumulate_rows(fs[pl.ds(r0, L), :], L, jnp.add, 0.0)
            parts = []
            for j in range(L):
                dec = jnp.exp(jnp.where(row >= j, b - b[j:j + 1, :], NEG_BIG))
                parts.append(dec * q * kk[j:j + 1, :])
            tstack = jnp.concatenate(parts, axis=0).astype(BF16)
            y = jnp.dot(tstack, seg64b, preferred_element_type=F32)
            o = y[0:L] * iv[0:1, :]
            for j in range(1, L):
                o = o + y[j * L:(j + 1) * L] * iv[j:j + 1, :]
            finish(o, r0, L, st[...], b, kk, iv, (q * jnp.exp(b)).astype(BF16))
            return carry

        lax.fori_loop(0, TS // L, body, 0, unroll=min(CHUNK_UNROLL, TS // L))

    if fast:
        safe = jnp.min(worst) > -HGRN_SAFE_DECAY
        pl.when(safe)(fast_loop)
        pl.when(jnp.logical_not(safe))(exact_loop)
    else:
        exact_loop()

    @_when(last)
    def _():
        _store_block_diag(s_out, st[...])


def _hgrn(z3, s0t, lb, norm_g, n_valid, sub, host_grid=None):
    B, S, _ = z3.shape
    ts = min(512, S)
    fast = ts % HGRN_FAST_CHUNK == 0
    assert S % ts == 0 and ts % sub == 0 and (host_grid is None or S == ts)
    seq, til = _host_maps(host_grid)
    blk = lambda c0: pl.BlockSpec((None, ts, QUAD), lambda b, t: (seq(b, t), til(b, t), c0 // QUAD))
    per_b = pl.BlockSpec((None, N_HGRN_HEADS, HEAD_DIM, HEAD_DIM), lambda b, t: (seq(b, t), 0, 0, 0))
    stage = lambda: pltpu.VMEM((ts, QUAD), F32)
    return _CallSpec(
        body=functools.partial(_hgrn_kernel, tile=ts, sub=sub, n_valid=n_valid, fast=fast,
                               single_tile=host_grid is not None),
        grid=host_grid or (B, S // ts),
        args=[z3, z3, z3, z3, s0t, lb, norm_g],
        in_specs=[blk(COL_HQ), blk(COL_HF), blk(COL_HI), blk(COL_HG), per_b,
                  _const_spec((1, QUAD)), _const_spec((1, QUAD))],
        out_shape=[jax.ShapeDtypeStruct((B, S, QUAD), BF16),
                   jax.ShapeDtypeStruct((B, N_HGRN_HEADS, HEAD_DIM, HEAD_DIM), F32)],
        out_specs=[blk(0), per_b],
        scratch=[pltpu.VMEM((QUAD, QUAD), F32), stage(), stage(), stage()])


PROJ_SPLITS = (ATTN_WIDTH, ATTN_WIDTH, ATTN_WIDTH,
               MLSTM_WIDTH, MLSTM_WIDTH, MLSTM_WIDTH, MLSTM_WIDTH, N_MLSTM_HEADS, N_MLSTM_HEADS,
               HGRN_WIDTH, HGRN_WIDTH, HGRN_WIDTH, HGRN_WIDTH)


def _prep_w_in(w):
    g0 = int(np.sum(PROJ_SPLITS[:7]))
    g1 = g0 + 2 * N_MLSTM_HEADS
    assert (g0, w.shape[-1] - g1) == (COL_HQ, COL_GATE - COL_HQ)
    wb = w.astype(BF16)
    pad = jnp.zeros(w.shape[:-1] + (LANES - 2 * N_MLSTM_HEADS,), BF16)
    return wb, jnp.concatenate([wb[..., g1:], wb[..., g0:g1], pad], axis=-1)


HGRN_SUBCHUNK = 16


class _LayerWeights(NamedTuple):
    layer: int
    g_mix: jax.Array
    w_in: jax.Array
    gate_b: jax.Array
    conv_w: jax.Array
    mlstm_g: jax.Array
    lb: jax.Array
    hgrn_g: jax.Array
    w_out: jax.Array
    g_ffn: jax.Array
    w_up: jax.Array
    w_down: jax.Array
    g_final: jax.Array


def _gate_row(gate_b):
    flat = gate_b.astype(F32).reshape(1, 2 * N_MLSTM_HEADS)
    return jnp.concatenate([flat, jnp.zeros((1, LANES - 2 * N_MLSTM_HEADS), F32)], axis=1)


class _LayerFront(NamedTuple):
    z3: jax.Array
    kv_t: tuple
    mlstm: _CallSpec
    hgrn: _CallSpec


def _layer_front(x2d, B, S, n_valid, states, w, chunk_m, chunk_h, n_keep=None, stacked=None, host_grid=None):
    conv_buf, C0, n0, m0, S0 = states
    if n_keep is None:
        z, kv_t = _inproj(x2d, w.g_mix, w.w_in, w.layer), None
    else:
        z, kt, vt = _inproj(x2d, w.g_mix, w.w_in, w.layer, S, n_keep, stacked)
        kv_t = (kt, vt)
    z3 = z.reshape(B, S, ZW)
    conv0 = jnp.zeros((B, SUBLANES, 2 * MLSTM_WIDTH), F32).at[:, SUBLANES - (CONV_WIDTH - 1):].set(conv_buf.astype(F32))
    m0r = jnp.repeat(m0.astype(F32), HEAD_DIM, axis=1).reshape(B, 1, MLSTM_WIDTH)
    mlstm = _mlstm(z3, conv0, C0.astype(F32), n0.astype(F32).reshape(B, 1, MLSTM_WIDTH), m0r,
                   w.gate_b, w.conv_w, w.mlstm_g, n_valid, chunk_m, host_grid)
    hgrn = _hgrn(z3, jnp.swapaxes(S0.astype(F32), -1, -2), w.lb, w.hgrn_g, n_valid, chunk_h, host_grid)
    return _LayerFront(z3, kv_t, mlstm, hgrn)


def _layer_back(x2d, front, n_valid, w, final, attn, mlstm_out, hgrn_out):
    B, S, _ = front.z3.shape
    z3 = front.z3
    (ml, c_new, n, m), (hg, st) = mlstm_out, hgrn_out
    n2 = B * S
    y = _outffn(x2d, attn.reshape(n2, ATTN_WIDTH), ml.reshape(n2, MLSTM_WIDTH), hg.reshape(n2, HGRN_WIDTH),
                w.w_out, w.g_ffn, w.w_up, w.w_down, w.g_final, final, w.layer)
    k_rows = z3[:, :n_valid, COL_AK:COL_AK + ATTN_WIDTH].reshape(B, n_valid, N_ATTN_HEADS, HEAD_DIM)
    v_rows = z3[:, :n_valid, COL_AV:COL_AV + ATTN_WIDTH].reshape(B, n_valid, N_ATTN_HEADS, HEAD_DIM)
    conv_new = z3[:, n_valid - (CONV_WIDTH - 1):n_valid, COL_MQK:COL_MQK + 2 * MLSTM_WIDTH]
    s_new = jnp.swapaxes(st, -1, -2)
    return y, (k_rows, v_rows, conv_new, c_new, n.reshape(B, N_MLSTM_HEADS, HEAD_DIM), m[:, 0, ::HEAD_DIM], s_new)


def kernel(x_prompt, x_sample, cache_attn_k, cache_attn_v, state_mlstm_conv, state_mlstm_C, state_mlstm_n, state_mlstm_m, state_hgrn_S, rel_bias, g_mix, w_in, mlstm_gate_b, mlstm_conv_w, mlstm_norm_g, hgrn_lb_raw, hgrn_norm_g, w_out, g_ffn, w_up, w_down, g_final):
    depth = w_in.shape[0]
    bp, sp, _ = x_prompt.shape
    bs, ts, _ = x_sample.shape
    n_keep = min(max(w for w, _ in DILATED_PATTERNS), sp)
    n_past = cache_attn_k.shape[2]
    lb_p = jax.nn.softmax(hgrn_lb_raw.astype(F32), axis=0)
    hgrn_lb = jnp.cumsum(lb_p, axis=0) - lb_p[0]
    bias_dist = _bias_by_distance(rel_bias)
    p_tables = _prompt_tables(bias_dist)
    s_tables = _sample_tables(bias_dist, n_past, ts)
    cache_kt = jnp.transpose(cache_attn_k, (0, 1, 3, 4, 2)).reshape(depth, bs, ATTN_WIDTH, n_past)
    cache_vt = jnp.transpose(cache_attn_v, (0, 1, 3, 4, 2)).reshape(depth, bs, ATTN_WIDTH, n_past)

    hp = x_prompt.reshape(bp * sp, D_MODEL)
    hs = jnp.zeros((bs, TPAD, D_MODEL), F32).at[:, :ts].set(x_sample).reshape(bs * TPAD, D_MODEL)
    zero_states = (jnp.zeros((bp, CONV_WIDTH - 1, 2 * MLSTM_WIDTH), F32),
                   jnp.zeros((bp, N_MLSTM_HEADS, HEAD_DIM, HEAD_DIM), F32),
                   jnp.zeros((bp, N_MLSTM_HEADS, HEAD_DIM), F32),
                   jnp.zeros((bp, N_MLSTM_HEADS), F32),
                   jnp.zeros((bp, N_HGRN_HEADS, HEAD_DIM, HEAD_DIM), F32))
    p_states, s_states = [], []
    kv_t = None
    sem = ("parallel", "arbitrary")
    w_in_b, w_out_b, w_up_b, w_down_b = _prep_w_in(w_in), w_out.astype(BF16), w_up.astype(BF16), w_down.astype(BF16)
    for l in range(depth):
        final = l == depth - 1
        weights = _LayerWeights(
            layer=l, g_mix=g_mix[l][None], w_in=w_in_b,
            gate_b=_gate_row(mlstm_gate_b[l]),
            conv_w=mlstm_conv_w[l], mlstm_g=mlstm_norm_g[l][None], lb=hgrn_lb[l][None], hgrn_g=hgrn_norm_g[l][None],
            w_out=w_out_b, g_ffn=g_ffn[l][None], w_up=w_up_b, w_down=w_down_b, g_final=g_final[None])
        pf = _layer_front(hp, bp, sp, sp, zero_states, weights, MLSTM_CHUNK, HGRN_SUBCHUNK, n_keep, (l, depth, kv_t))
        kv_t = pf.kv_t
        grid = pf.mlstm.grid
        host = grid if bs == grid[0] * grid[1] else None
        states = (state_mlstm_conv[l], state_mlstm_C[l], state_mlstm_n[l], state_mlstm_m[l], state_hgrn_S[l])
        sf = _layer_front(hs, bs, TPAD, ts, states, weights, TPAD, TPAD, host_grid=host)
        s_attn = _attn_sample(sf.z3, cache_kt, cache_vt, l, s_tables, host)
        p_attn = _attn_prompt(pf.z3, p_tables)
        if host is None:
            p_ml, = _run([pf.mlstm], sem, "mlstm")
            (s_attn,), s_ml, s_hg = _run([s_attn, sf.mlstm, sf.hgrn], sem, "sample_mixers")
        else:
            p_ml, (s_attn,), s_ml, s_hg = _run([pf.mlstm, s_attn, sf.mlstm, sf.hgrn], sem, "mlstm_and_sample_mixers")
        p_hg, = _run([pf.hgrn], sem, "hgrn")
        hp, st = _layer_back(hp, pf, sp, weights, final, p_attn, p_ml, p_hg)
        p_states.append(st[2:])
        hs, st = _layer_back(hs, sf, ts, weights, final, s_attn, s_ml, s_hg)
        s_states.append(st)
    y_prompt = hp.reshape(bp, sp, D_MODEL)
    y_sample = hs.reshape(bs, TPAD, D_MODEL)[:, :ts]
    to_rows = lambda t: jnp.transpose(t.reshape(depth, bp, N_ATTN_HEADS, HEAD_DIM, n_keep), (0, 1, 4, 2, 3))
    p_out = [to_rows(kv_t[0]), to_rows(kv_t[1])] + [jnp.stack(a) for a in zip(*p_states)]
    s_out = [jnp.stack(a) for a in zip(*s_states)]
    return (y_prompt, y_sample, *p_out, *s_out)
```

```python
import functools
from typing import Callable, NamedTuple

import jax
import jax.numpy as jnp
import numpy as np
from jax import lax
from jax.experimental import pallas as pl
from jax.experimental.pallas import tpu as pltpu

F32 = jnp.float32
BF16 = jnp.bfloat16

D_MODEL = 1024
HEAD_DIM = 64
N_ATTN_HEADS = 8
N_MLSTM_HEADS = 4
N_HGRN_HEADS = 4
ATTN_WIDTH = N_ATTN_HEADS * HEAD_DIM
MLSTM_WIDTH = N_MLSTM_HEADS * HEAD_DIM
HGRN_WIDTH = N_HGRN_HEADS * HEAD_DIM
DILATED_PATTERNS = ((128, 1), (512, 4), (2048, 16))
QBLOCK = 128
N_REL_BUCKETS = 32
REL_MAX_DISTANCE = 2048
CONV_WIDTH = 4
MLSTM_CHUNK = 64
D_FF = 4 * D_MODEL
EPS = 1e-6
NEG_BIG = -1e30
LOG2E = 1.4426950408889634

LANES = 128
SUBLANES = 8

COL_AQ = 0
COL_AK = COL_AQ + ATTN_WIDTH
COL_AV = COL_AK + ATTN_WIDTH
COL_MQK = COL_AV + ATTN_WIDTH
COL_MV = COL_MQK + 2 * MLSTM_WIDTH
COL_MO = COL_MV + MLSTM_WIDTH
COL_HQ = COL_MO + MLSTM_WIDTH
COL_HF = COL_HQ + HGRN_WIDTH
COL_HI = COL_HF + HGRN_WIDTH
COL_HG = COL_HI + HGRN_WIDTH
COL_GATE = COL_HG + HGRN_WIDTH
ZW = COL_GATE + LANES

VMEM_LIMIT = 56 * 1024 * 1024
TOKEN_TILE = 512
ATTN_STAGE_ROWS = 512
ATTN_MIX_ROWS = 256


def _cparams(sem, vmem=VMEM_LIMIT):
    return pltpu.CompilerParams(dimension_semantics=sem, vmem_limit_bytes=vmem)


def _const_spec(shape):
    nd = len(shape)
    return pl.BlockSpec(shape, lambda *_: (0,) * nd, pipeline_mode=pl.Buffered(1))


def _layer_spec(shape, layer):
    nd = len(shape)
    return pl.BlockSpec((None,) + tuple(shape), lambda *_: (layer,) + (0,) * nd, pipeline_mode=pl.Buffered(1))


class _CallSpec(NamedTuple):
    body: Callable
    grid: tuple
    args: list
    in_specs: list
    out_shape: list
    out_specs: list
    scratch: list


def _run(specs, sem, name):
    grid = specs[0].grid
    assert all(s.grid == grid for s in specs)
    n_in = [len(s.args) for s in specs]
    n_out = [len(s.out_shape) for s in specs]
    n_scr = [len(s.scratch) for s in specs]

    def body(*refs):
        ins, outs, scrs = refs[:sum(n_in)], refs[sum(n_in):sum(n_in) + sum(n_out)], refs[sum(n_in) + sum(n_out):]

        for k, s in enumerate(specs):
            a, b, c = sum(n_in[:k]), sum(n_out[:k]), sum(n_scr[:k])
            s.body(*ins[a:a + n_in[k]], *outs[b:b + n_out[k]], *scrs[c:c + n_scr[k]])

    flat = pl.pallas_call(
        body, grid=grid,
        in_specs=[x for s in specs for x in s.in_specs],
        out_shape=[x for s in specs for x in s.out_shape],
        out_specs=[x for s in specs for x in s.out_specs],
        scratch_shapes=[x for s in specs for x in s.scratch],
        compiler_params=_cparams(sem), name=name,
    )(*[x for s in specs for x in s.args])
    out, k = [], 0
    for n in n_out:
        out.append(list(flat[k:k + n]))
        k += n
    return out


def _inproj_kernel(x_ref, g_ref, wa_ref, wb_ref, *rest, tiles_per_seq, first_keep, n_prev=0):
    x = x_ref[...]
    ms = jnp.mean(x * x, axis=-1, keepdims=True)
    xn = ((x * lax.rsqrt(ms + EPS)) * g_ref[...]).astype(BF16)
    z_ref = rest[n_prev]
    z_ref[:, 0:COL_HQ] = jnp.dot(xn, wa_ref[...], preferred_element_type=F32)
    z_ref[:, COL_HQ:] = jnp.dot(xn, wb_ref[...], preferred_element_type=F32)
    if first_keep is None:
        return
    kt_ref, vt_ref = rest[n_prev + 1:]

    @pl.when(pl.program_id(0) % tiles_per_seq >= first_keep)
    def _():
        kt_ref[...] = z_ref[:, COL_AK:COL_AK + ATTN_WIDTH].T
        vt_ref[...] = z_ref[:, COL_AV:COL_AV + ATTN_WIDTH].T


def _inproj(x2d, g, w, layer, seq=None, n_keep=None, stacked=None):
    n = x2d.shape[0]
    tm = min(TOKEN_TILE, n)
    assert n % tm == 0
    x_spec = pl.BlockSpec((tm, D_MODEL), lambda i: (i, 0))
    z_spec = pl.BlockSpec((tm, ZW), lambda i: (i, 0))
    z_shape = jax.ShapeDtypeStruct((n, ZW), F32)
    w_specs = [_layer_spec((D_MODEL, COL_HQ), layer), _layer_spec((D_MODEL, ZW - COL_HQ), layer)]
    if stacked is None:
        return pl.pallas_call(
            functools.partial(_inproj_kernel, tiles_per_seq=None, first_keep=None),
            out_shape=z_shape, grid=(n // tm,),
            in_specs=[x_spec, _const_spec((1, D_MODEL))] + w_specs,
            out_specs=z_spec, compiler_params=_cparams(("parallel",)), name="inproj",
        )(x2d, g, *w)
    assert seq % tm == 0 and n_keep % tm == 0
    layer, depth, prev = stacked
    tps, first_keep = seq // tm, (seq - n_keep) // tm
    t_spec = pl.BlockSpec((None, None, ATTN_WIDTH, tm),
                          lambda i: (layer, i // tps, 0, jnp.maximum(i % tps - first_keep, 0)))
    t_shape = jax.ShapeDtypeStruct((depth, n // seq, ATTN_WIDTH, n_keep), F32)
    in_specs = [x_spec, _const_spec((1, D_MODEL))] + w_specs
    args = [x2d, g, *w]
    if prev is None:
        prev = (jnp.zeros(t_shape.shape, F32), jnp.zeros(t_shape.shape, F32))
    in_specs += [pl.BlockSpec(memory_space=pl.ANY)] * 2
    aliases = {len(args): 1, len(args) + 1: 2}
    args += list(prev)
    return pl.pallas_call(
        functools.partial(_inproj_kernel, tiles_per_seq=tps, first_keep=first_keep, n_prev=len(prev)),
        out_shape=[z_shape, t_shape, t_shape], grid=(n // tm,),
        in_specs=in_specs, out_specs=[z_spec, t_spec, t_spec], input_output_aliases=aliases,
        compiler_params=_cparams(("arbitrary",)), name="inproj_kt",
    )(*args)


FF_CHUNK = 1024
FFN_ROWS = TOKEN_TILE


def _outffn_kernel(x_ref, a_ref, m_ref, h_ref, wo_ref, gf_ref, wu_ref, wd_ref, gl_ref, y_ref, xn_sc, *, final):
    x1 = x_ref[...]
    x1 = x1 + jnp.dot(a_ref[...], wo_ref[0:ATTN_WIDTH, :], preferred_element_type=F32)
    x1 = x1 + jnp.dot(m_ref[...], wo_ref[ATTN_WIDTH:ATTN_WIDTH + MLSTM_WIDTH, :], preferred_element_type=F32)
    x1 = x1 + jnp.dot(h_ref[...], wo_ref[ATTN_WIDTH + MLSTM_WIDTH:, :], preferred_element_type=F32)
    ms = jnp.mean(x1 * x1, axis=-1, keepdims=True)
    xn_sc[...] = ((x1 * lax.rsqrt(ms + EPS)) * gf_ref[...]).astype(BF16)
    y_ref[...] = x1
    nchunk = D_FF // FF_CHUNK
    up = lambda c: jnp.dot(xn_sc[...], wu_ref[:, c * FF_CHUNK:(c + 1) * FF_CHUNK], preferred_element_type=F32)
    u = up(0)
    for c in range(nchunk):
        u_next = up(c + 1) if c + 1 < nchunk else None
        hh = jnp.square(jnp.maximum(u, 0.0)).astype(BF16)
        y_ref[...] += jnp.dot(hh, wd_ref[c * FF_CHUNK:(c + 1) * FF_CHUNK, :], preferred_element_type=F32)
        u = u_next
    if final:
        x2 = y_ref[...]
        ms2 = jnp.mean(x2 * x2, axis=-1, keepdims=True)
        y_ref[...] = (x2 * lax.rsqrt(ms2 + EPS)) * gl_ref[...]


def _outffn(x2d, attn, ml, hg, w_out, g_ffn, w_up, w_down, g_final, final, layer):
    n = x2d.shape[0]
    tm = min(FFN_ROWS, n)
    assert n % tm == 0
    row = lambda w: pl.BlockSpec((tm, w), lambda i: (i, 0))
    return pl.pallas_call(
        functools.partial(_outffn_kernel, final=final),
        out_shape=jax.ShapeDtypeStruct((n, D_MODEL), F32),
        grid=(n // tm,),
        in_specs=[row(D_MODEL), row(ATTN_WIDTH), row(MLSTM_WIDTH), row(HGRN_WIDTH),
                  _layer_spec((D_MODEL, D_MODEL), layer), _const_spec((1, D_MODEL)),
                  _layer_spec((D_MODEL, D_FF), layer), _layer_spec((D_FF, D_MODEL), layer), _const_spec((1, D_MODEL))],
        out_specs=row(D_MODEL),
        scratch_shapes=[pltpu.VMEM((tm, D_MODEL), BF16)],
        compiler_params=_cparams(("parallel",)),
        name="outffn",
    )(x2d, attn, ml, hg, w_out, g_ffn, w_up, w_down, g_final)


def _t5_causal_bucket(dist):
    n = np.asarray(dist).astype(np.int32)
    max_exact = N_REL_BUCKETS // 2
    scaled = np.log(np.maximum(n, 1) / max_exact) / np.log(REL_MAX_DISTANCE / max_exact)
    large = np.minimum(max_exact + (scaled * (N_REL_BUCKETS - max_exact)).astype(np.int32), N_REL_BUCKETS - 1)
    return np.where(n < max_exact, n, large).astype(np.int32)


BIAS_DIST = 2304


def _bias_by_distance(rel_bias):
    assert BIAS_DIST > max(w for w, _ in DILATED_PATTERNS) + SUBLANES
    return rel_bias.astype(F32)[_t5_causal_bucket(np.arange(BIAS_DIST)[::-1])].T


def _dist_slice(bias_desc, lo, hi, step=1):
    last = BIAS_DIST - 1
    return bias_desc[:, last - hi:last - lo + 1:step]


def _prompt_tables(bias_dist):
    H = N_ATTN_HEADS
    a = np.arange(QBLOCK)[:, None]
    b = np.arange(2 * QBLOCK)[None, :]
    rel = QBLOCK + a - b
    band = (rel >= 0) & (rel <= QBLOCK)
    period = 3 * QBLOCK
    biases = []
    for window, dil in DILATED_PATTERNS:
        nsub = window // dil
        assert nsub == QBLOCK
        vd = _dist_slice(bias_dist, 0, nsub * dil, dil)
        rp = jnp.concatenate([vd, jnp.broadcast_to(vd[:, -1:], (H, QBLOCK)),
                              jnp.broadcast_to(vd[:, 0:1], (H, QBLOCK - 1))], axis=1)
        skew = jnp.tile(rp, (1, QBLOCK))[:, :QBLOCK * (period - 1)].reshape(H, QBLOCK, period - 1)
        biases.append(skew[:, :, :2 * QBLOCK].reshape(H // 2, 2 * QBLOCK, 2 * QBLOCK))
    mask = np.stack([np.tile(band, (2, 1)), np.tile(band & (b >= QBLOCK), (2, 1))])
    table = jnp.where(mask[None, None], jnp.stack(biases)[:, :, None] * LOG2E, -jnp.inf)
    return jnp.swapaxes(table, -1, -2)


ATTN_GROUP = 8


def _attn_prompt_kernel(q_ref, k_ref, v_ref, bias_ref, out_ref,
                        x4, qs, ks, vts, o1, o4, o16, l1, l4, l16, p_scr, st_scr, *, seq):
    S = seq
    nblk = S // QBLOCK
    scale = HEAD_DIM ** -0.5 * LOG2E
    zpad = jnp.zeros((QBLOCK, LANES), BF16)
    for di in range(3):
        ks[di, 0:QBLOCK, :] = zpad
        vts[di, 0] = zpad

    R = ATTN_STAGE_ROWS

    def put(src, di, row0, blk):
        if src is q_ref:
            qs[di, row0:row0 + blk.shape[0], :] = (blk * scale).astype(BF16)
        elif src is k_ref:
            ks[di, QBLOCK + row0:QBLOCK + row0 + blk.shape[0], :] = blk.astype(BF16)
        else:
            blk_t = blk.T.astype(BF16)
            for t in range(blk.shape[0] // QBLOCK):
                vts[di, 1 + row0 // QBLOCK + t] = blk_t[:, t * QBLOCK:(t + 1) * QBLOCK]

    for src in (q_ref, k_ref, v_ref):
        for i in range(S // R):
            put(src, 0, i * R, src[i * R:(i + 1) * R, :])
        for r in range(4):
            for i in range(S // 4 // R):
                blk = src[pl.ds(r + 4 * i * R, R, stride=4), :]
                row0 = r * (S // 4) + i * R
                x4[row0:row0 + R, :] = blk
                put(src, 1, row0, blk)
        n16 = S // 16
        for c4 in range(4):
            for a in range(4):
                put(src, 2, (c4 * 4 + a) * n16, x4[pl.ds(c4 * (S // 4) + a, n16, stride=4), :])

    lane = lax.broadcasted_iota(jnp.int32, (QBLOCK, LANES), 1)
    head_a = lane < HEAD_DIM
    row_a = lax.broadcasted_iota(jnp.int32, (QBLOCK, LANES), 0) < HEAD_DIM
    nt = (((1,), (1,)), ((), ()))

    def branch(di, dil, o_ref, l_ref):
        per_class = (S // dil) // QBLOCK

        R = range(ATTN_GROUP)
        zero = jnp.zeros((QBLOCK, LANES), BF16)

        def head_block(j):
            return per_class <= ATTN_GROUP and j % per_class == 0

        assert per_class % ATTN_GROUP == 0 or ATTN_GROUP % per_class == 0

        def scores(g, js):
            out = {}
            for j in js:
                i = g * ATTN_GROUP + j
                base = pl.multiple_of(i * QBLOCK, QBLOCK)
                q = qs[di, pl.ds(base, QBLOCK), :]
                q2 = jnp.concatenate([jnp.where(head_a, q, zero), jnp.where(head_a, zero, q)], axis=0)
                if head_block(j):
                    kk = ks[di, pl.ds(base + QBLOCK, QBLOCK), :]
                else:
                    kk = ks[di, pl.ds(base, 2 * QBLOCK), :]
                out[j] = lax.dot_general(kk, q2, nt, preferred_element_type=F32)
            return out

        def softmax(g, slot, st):
            for j in st:
                i = g * ATTN_GROUP + j
                if head_block(j):
                    s = st[j] + bias_ref[di, 1, QBLOCK:, :]
                else:
                    s = st[j] + bias_ref[di, jnp.where(i % per_class == 0, 1, 0)]
                m = jnp.max(s, axis=0, keepdims=True)
                p = jnp.exp2(s - m)
                if head_block(j):
                    p_scr[slot, j, QBLOCK:, :] = p.astype(BF16)
                else:
                    p_scr[slot, j] = p.astype(BF16)
                st_scr[slot, j, 0:1, :] = m
                st_scr[slot, j, 1:2, :] = jnp.sum(p, axis=0, keepdims=True)

        def weighted_values(g, slot, js):
            out = {}
            for j in js:
                i = g * ATTN_GROUP + j
                if head_block(j):
                    ot = jnp.dot(vts[di, i + 1], p_scr[slot, j, QBLOCK:, :], preferred_element_type=F32)
                else:
                    vt = jnp.concatenate([vts[di, i], vts[di, i + 1]], axis=1)
                    ot = jnp.dot(vt, p_scr[slot, j], preferred_element_type=F32)
                out[j] = (ot, st_scr[slot, j, 0:1, :], st_scr[slot, j, 1:2, :])
            return out

        def outputs(g, pv):
            for j in pv:
                i = g * ATTN_GROUP + j
                base = pl.multiple_of(i * QBLOCK, QBLOCK)
                ot, m, l = pv[j]
                otn = ot / l
                lse = m + jnp.log2(l)
                o = jnp.where(row_a, otn[:, 0:QBLOCK], otn[:, QBLOCK:]).T
                ls = jnp.where(row_a, jnp.broadcast_to(lse[:, 0:QBLOCK], (QBLOCK, LANES)),
                               jnp.broadcast_to(lse[:, QBLOCK:], (QBLOCK, LANES))).T
                if dil == 1:
                    o_ref[pl.ds(base, QBLOCK), :] = o
                    l_ref[pl.ds(base, QBLOCK), :] = ls
                else:
                    c, n = i // per_class, i % per_class
                    res = c if dil == 4 else 4 * (c % 4) + c // 4
                    start = dil * QBLOCK * n + res
                    o_ref[pl.ds(start, QBLOCK, stride=dil), :] = o
                    l_ref[pl.ds(start, QBLOCK, stride=dil), :] = ls

        ngroup = nblk // ATTN_GROUP
        softmax(0, 0, scores(0, R))

        def body(g, carry):
            slot = g % 2
            st = scores(g, R)
            pv = weighted_values(g - 1, 1 - slot, R)
            softmax(g, slot, st)
            outputs(g - 1, pv)
            return carry

        lax.fori_loop(1, ngroup, body, 0)
        outputs(ngroup - 1, weighted_values(ngroup - 1, (ngroup - 1) % 2, R))

    branch(0, 1, o1, l1)
    branch(1, 4, o4, l4)
    branch(2, 16, o16, l16)

    T = ATTN_MIX_ROWS

    def mix(i, carry):
        r0 = pl.multiple_of(i * T, T)
        la, lb, lc = l1[pl.ds(r0, T), :], l4[pl.ds(r0, T), :], l16[pl.ds(r0, T), :]
        mx = jnp.maximum(jnp.maximum(la, lb), lc)
        ea, eb, ec = jnp.exp2(la - mx), jnp.exp2(lb - mx), jnp.exp2(lc - mx)
        num = ea * o1[pl.ds(r0, T), :] + eb * o4[pl.ds(r0, T), :] + ec * o16[pl.ds(r0, T), :]
        out_ref[pl.ds(r0, T), :] = (num / (ea + eb + ec)).astype(out_ref.dtype)
        return carry

    lax.fori_loop(0, S // T, mix, 0)


def _attn_prompt(z3, bias):
    B, S, _ = z3.shape
    assert S % (16 * QBLOCK) == 0
    npair = N_ATTN_HEADS // 2
    col = lambda c0: pl.BlockSpec((None, S, LANES), lambda b, p: (b, 0, c0 // LANES + p))
    f32s = lambda: pltpu.VMEM((S, LANES), F32)
    return pl.pallas_call(
        functools.partial(_attn_prompt_kernel, seq=S),
        out_shape=jax.ShapeDtypeStruct((B, S, ATTN_WIDTH), BF16),
        grid=(B, npair),
        in_specs=[col(COL_AQ), col(COL_AK), col(COL_AV),
                  pl.BlockSpec((3, None, 2, 2 * QBLOCK, 2 * QBLOCK), lambda b, p: (0, p, 0, 0, 0))],
        out_specs=pl.BlockSpec((None, S, LANES), lambda b, p: (b, 0, p)),
        scratch_shapes=[f32s(),
                        pltpu.VMEM((3, S, LANES), BF16),
                        pltpu.VMEM((3, S + QBLOCK, LANES), BF16),
                        pltpu.VMEM((3, S // QBLOCK + 1, LANES, QBLOCK), BF16),
                        f32s(), f32s(), f32s(), f32s(), f32s(), f32s(),
                        pltpu.VMEM((2, ATTN_GROUP, 2 * QBLOCK, 2 * QBLOCK), BF16),
                        pltpu.VMEM((2, ATTN_GROUP, SUBLANES, 2 * QBLOCK), F32)],
        compiler_params=_cparams(("parallel", "parallel")),
        name="attn_prompt",
    )(z3, z3, z3, bias)


TPAD = SUBLANES
TAIL = 512


def _sample_tables(bias_dist, n_past, n_tok):
    assert n_past >= max(w for w, _ in DILATED_PATTERNS) and n_tok <= TPAD and TAIL >= DILATED_PATTERNS[1][0]
    H = N_ATTN_HEADS
    t = np.arange(TPAD)[:, None]

    def by_row(width):
        rows = [_dist_slice(bias_dist, tt + 1, tt + width) for tt in range(TPAD)]
        return jnp.stack(rows, axis=1).reshape(H * TPAD, width)

    bias_tail, bias_full = by_row(TAIL), by_row(n_past)
    zero_dist = _dist_slice(bias_dist, 0, 0)
    new_rows = [jnp.concatenate([_dist_slice(bias_dist, 0, tt),
                                 jnp.broadcast_to(zero_dist, (H, TPAD - tt - 1))], axis=1) for tt in range(TPAD)]
    bias_new = jnp.stack(new_rows, axis=1).reshape(H * TPAD, TPAD)

    def valid(dj, dil, nsub, lo):
        ok = (dj % dil == 0) & (dj // dil >= lo) & (dj // dil <= nsub) & (t < n_tok)
        return np.tile(ok, (H, 1)).astype(np.float32)

    tn = np.arange(TPAD)[None, :]
    m_tail, m_new = [], []
    for window, dil in DILATED_PATTERNS:
        nsub = window // dil
        if window <= TAIL:
            m_tail.append(valid(TAIL + t - np.arange(TAIL)[None, :], dil, nsub, 1))
        else:
            m_full = valid(n_past + t - np.arange(n_past)[None, :], dil, nsub, 1)
        m_new.append(valid(t - tn, dil, nsub, 0) * (tn < n_tok))
    return (bias_tail, bias_full, bias_new, jnp.asarray(np.stack(m_tail)), jnp.asarray(m_full),
            jnp.asarray(np.stack(m_new).astype(np.float32)))


def _attn_sample_kernel(q_ref, kn_ref, vn_ref, kt_ref, vt_ref, bt_ref, bf_ref, bn_ref, mt_ref, mf_ref, mn_ref,
                        out_ref, *, n_past):
    H = N_ATTN_HEADS
    rows = H * TPAD
    same_head = (_iota2((rows, ATTN_WIDTH), 0) >> _log2(TPAD)) == (_iota2((rows, ATTN_WIDTH), 1) >> HEAD_SHIFT)
    q = q_ref[...] * (HEAD_DIM ** -0.5)
    qm = jnp.where(same_head, jnp.concatenate([q] * H, axis=0), 0.0).astype(BF16)
    nt = (((1,), (1,)), ((), ()))
    kn = kn_ref[...].astype(BF16)
    vn = vn_ref[...].astype(BF16)
    kt = kt_ref[...].astype(BF16)
    vt = vt_ref[...].astype(BF16)
    s_new = lax.dot_general(qm, kn, nt, preferred_element_type=F32)
    s_full = jnp.dot(qm, kt, preferred_element_type=F32)
    s_tail = s_full[:, n_past - TAIL:]

    outs, lses = [], []
    for di, (window, _) in enumerate(DILATED_PATTERNS):
        if window <= TAIL:
            sm = jnp.where(mt_ref[di] > 0.5, s_tail + bt_ref[...], NEG_BIG)
            vmain = vt[:, n_past - TAIL:]
        else:
            sm = jnp.where(mf_ref[...] > 0.5, s_full + bf_ref[...], NEG_BIG)
            vmain = vt
        sn = jnp.where(mn_ref[di] > 0.5, s_new + bn_ref[...], NEG_BIG)
        m = jnp.maximum(jnp.max(sm, axis=-1, keepdims=True), jnp.max(sn, axis=-1, keepdims=True))
        pm = jnp.exp(sm - m)
        pn = jnp.exp(sn - m)
        l = jnp.sum(pm, axis=-1, keepdims=True) + jnp.sum(pn, axis=-1, keepdims=True)
        o = jnp.dot(pn.astype(BF16), vn, preferred_element_type=F32)
        o = o + lax.dot_general(pm.astype(BF16), vmain, nt, preferred_element_type=F32)
        outs.append(o / l)
        lses.append(m + jnp.log(l))
    mx = jnp.maximum(jnp.maximum(lses[0], lses[1]), lses[2])
    es = [jnp.exp(ls - mx) for ls in lses]
    mixed = (es[0] * outs[0] + es[1] * outs[1] + es[2] * outs[2]) / (es[0] + es[1] + es[2])
    mixed = jnp.where(same_head, mixed, 0.0)
    acc = mixed[0:TPAD]
    for h in range(1, H):
        acc = acc + mixed[h * TPAD:(h + 1) * TPAD]
    out_ref[...] = acc.astype(out_ref.dtype)


def _attn_sample(z3, cache_kt, cache_vt, layer, tables, host_grid=None):
    B, T, _ = z3.shape
    assert T == TPAD
    n_past = cache_kt.shape[-1]
    seq, _ = _host_maps(host_grid)
    new = lambda c0: pl.BlockSpec((None, TPAD, ATTN_WIDTH), lambda b, t: (seq(b, t), 0, c0 // ATTN_WIDTH))
    buf = pl.BlockSpec((None, None, ATTN_WIDTH, n_past), lambda b, t: (layer, seq(b, t), 0, 0))
    return _CallSpec(
        body=functools.partial(_attn_sample_kernel, n_past=n_past),
        grid=host_grid or (B, 1),
        args=[z3, z3, z3, cache_kt, cache_vt, *tables],
        in_specs=[new(COL_AQ), new(COL_AK), new(COL_AV), buf, buf] + [_const_spec(t.shape) for t in tables],
        out_shape=[jax.ShapeDtypeStruct((B, TPAD, ATTN_WIDTH), BF16)],
        out_specs=[pl.BlockSpec((None, TPAD, ATTN_WIDTH), lambda b, t: (seq(b, t), 0, 0))],
        scratch=[])


QUAD = N_MLSTM_HEADS * HEAD_DIM
CHUNK_UNROLL = 2
CHUNK_GROUP = 8
HEAD_SHIFT = 6


def _iota2(shape, axis):
    return lax.broadcasted_iota(jnp.int32, shape, axis)


def _log2(n):
    k = int(n).bit_length() - 1
    assert 1 << k == n
    return k


def _seg_mask(rows, row_shift, cols, col_shift):
    return (_iota2((rows, cols), 0) >> row_shift) == (_iota2((rows, cols), 1) >> col_shift)


def _cumulate_rows(x, length, op, fill):
    row = _iota2(x.shape, 0) & (length - 1)
    sh = 1
    while sh < length:
        x = op(x, jnp.where(row >= sh, pltpu.roll(x, sh, axis=0), fill))
        sh *= 2
    return x


def _split2(x):
    hi = x.astype(BF16)
    lo = (x - hi.astype(F32)).astype(BF16)
    return hi, lo


def _dot2(x, w):
    hi, lo = _split2(x)
    return jnp.dot(hi, w, preferred_element_type=F32) + jnp.dot(lo, w, preferred_element_type=F32)


def _dot2r(w, x):
    hi, lo = _split2(x)
    return jnp.dot(w, hi, preferred_element_type=F32) + jnp.dot(w, lo, preferred_element_type=F32)


def _tile_position(single_tile):
    if single_tile:
        return 0, True
    return pl.program_id(1), pl.program_id(1) == pl.num_programs(1) - 1


def _when(cond):
    if isinstance(cond, bool):
        return (lambda f: f()) if cond else (lambda f: None)
    return pl.when(cond)


def _host_maps(host_grid):
    if host_grid is None:
        return (lambda b, t: b), (lambda b, t: t)
    return (lambda b, t: b * host_grid[1] + t), (lambda b, t: 0)


def _load_block_diag(ref, seg64):
    flat = ref[...].reshape(QUAD, HEAD_DIM)
    return jnp.where(seg64, jnp.concatenate([flat] * (QUAD // HEAD_DIM), axis=1), 0.0)


def _store_block_diag(ref, mat):
    for h in range(QUAD // HEAD_DIM):
        ref[h] = mat[h * HEAD_DIM:(h + 1) * HEAD_DIM, h * HEAD_DIM:(h + 1) * HEAD_DIM]


def _head_rmsnorm_gate(h, seg64b, g_row, gate_pre):
    ms = _dot2(h * h, seg64b) * (1.0 / HEAD_DIM)
    return jax.nn.sigmoid(gate_pre) * (h * lax.rsqrt(ms + EPS) * g_row)


def _mlstm_kernel(qk_ref, v_ref, o_ref, gate_ref, conv0_ref, c0_ref, n0_ref, m0_ref, gb_ref, cw_ref, ng_ref,
                  out_ref, c_out, n_out, m_out, xp, cs, ns, ms, *, tile, chunk, n_valid, single_tile=False):
    TS, L = tile, chunk
    t, last = _tile_position(single_tile)
    PAD = SUBLANES

    @_when(t == 0)
    def _():
        cs[...] = _load_block_diag(c0_ref, _seg_mask(QUAD, HEAD_SHIFT, QUAD, HEAD_SHIFT))
        ns[...] = n0_ref[...]
        ms[...] = m0_ref[...]
        xp[0:PAD, :] = conv0_ref[...]

    @_when(t > 0)
    def _():
        xp[0:PAD, :] = xp[TS:TS + PAD, :]

    xp[PAD:PAD + TS, :] = qk_ref[...]

    LK = MLSTM_CHUNK
    seg64 = _seg_mask(QUAD, HEAD_SHIFT, QUAD, HEAD_SHIFT)
    seg64b = seg64.astype(BF16)
    dmask = (_iota2((LK, QUAD), 1) & (LK - 1)) == _iota2((LK, QUAD), 0)
    causal = (_iota2((L, QUAD), 1) & (LK - 1)) <= _iota2((L, QUAD), 0)
    tril = (_iota2((L, L), 1) <= _iota2((L, L), 0)).astype(BF16)
    ones_lk = jnp.ones((L, LK), BF16)
    row = _iota2((L, QUAD), 0)
    cw = cw_ref[...]
    gb = gb_ref[...]
    ng = ng_ref[...]

    first_half = _iota2((L, LANES), 1) < HEAD_DIM

    def per_head_lanes(g, lane0):
        col = [jnp.broadcast_to(g[:, lane0 + h:lane0 + h + 1], (L, LANES)) for h in range(N_MLSTM_HEADS)]
        return jnp.concatenate([jnp.where(first_half, col[0], col[1]), jnp.where(first_half, col[2], col[3])], axis=1)

    def key_rows(x, fill):
        if L == LK:
            return x
        return jnp.concatenate([x, jnp.full((LK - L, QUAD), fill, x.dtype)], axis=0)

    G = min(CHUNK_GROUP, TS // L)
    assert TS % (L * G) == 0
    nt = (((1,), (1,)), ((), ()))
    tn = (((0,), (0,)), ((), ()))
    zb = jnp.zeros((QUAD, QUAD), BF16)

    def group_body(gi, carry):
        R = range(G)
        r0 = [pl.multiple_of((gi * G + j) * L, L) for j in R]
        q, k, v, ig, lf = [], [], [], [], []
        for j in R:
            win = xp[pl.ds(r0[j], L + PAD), :]
            acc = win[PAD:PAD + L] * cw[CONV_WIDTH - 1:CONV_WIDTH, :]
            for s in range(1, CONV_WIDTH):
                acc = acc + win[PAD - s:PAD - s + L] * cw[CONV_WIDTH - 1 - s:CONV_WIDTH - s, :]
            qk = acc * jax.nn.sigmoid(acc)
            q.append(qk[:, 0:QUAD])
            k.append(qk[:, QUAD:] * (HEAD_DIM ** -0.5))
            v.append(v_ref[pl.ds(r0[j], L), :])
            gate = gate_ref[pl.ds(r0[j], L), :] + gb
            logsig = jnp.minimum(gate, 0.0) - jnp.log(1.0 + jnp.exp(-jnp.abs(gate)))
            ig_j = per_head_lanes(gate, 0)
            lf_j = per_head_lanes(logsig, N_MLSTM_HEADS)
            if n_valid < TS:
                ok = (row + r0[j]) < n_valid
                ig_j = jnp.where(ok, ig_j, NEG_BIG)
                lf_j = jnp.where(ok, lf_j, 0.0)
            ig.append(ig_j)
            lf.append(lf_j)
        b = [_dot2r(tril, lf[j]) for j in R]
        a = [ig[j] - b[j] for j in R]
        cm = [_cumulate_rows(a[j], L, jnp.maximum, -jnp.inf) for j in R]
        arow = [_dot2r(ones_lk, jnp.where(dmask, key_rows(a[j], NEG_BIG), 0.0)) for j in R]
        qb = [q[j].astype(BF16) for j in R]
        vb = [v[j].astype(BF16) for j in R]
        kbd = [jnp.where(seg64, jnp.concatenate([key_rows(k[j], 0.0).astype(BF16)] * N_MLSTM_HEADS, axis=0), zb) for j in R]
        vbd = [jnp.where(seg64, jnp.concatenate([key_rows(v[j], 0.0).astype(BF16)] * N_MLSTM_HEADS, axis=0), zb) for j in R]
        qkt = [lax.dot_general(qb[j], kbd[j], nt, preferred_element_type=F32) for j in R]

        mprev = ms[...]
        M, gg, emt = [], [], []
        for j in R:
            M.append(jnp.maximum(cm[j], mprev))
            mt = b[j] + M[j]
            gg.append(jnp.exp(mprev - M[j]))
            emt.append(jnp.exp(-mt))
            mprev = mt[L - 1:L, :]
        ms[...] = mprev

        wts = [jnp.where(causal, jnp.exp(arow[j] - M[j]), 0.0) * qkt[j] for j in R]
        kd = [k[j] * jnp.exp(a[j] - M[j][L - 1:L, :]) for j in R]
        num = [jnp.dot(wts[j].astype(BF16), vbd[j], preferred_element_type=F32) for j in R]
        u = [lax.dot_general(kd[j].astype(BF16), vb[j], tn, preferred_element_type=F32) for j in R]

        cmat, nvec = [cs[...]], [ns[...]]
        for j in R:
            gl = gg[j][L - 1:L, :]
            cmat.append(gl * cmat[j] + jnp.where(seg64, u[j], 0.0))
            nvec.append(gl * nvec[j] + jnp.sum(kd[j], axis=0, keepdims=True))
        cs[...] = cmat[G]
        ns[...] = nvec[G]

        inter = [jnp.dot(qb[j], cmat[j].astype(BF16), preferred_element_type=F32) for j in R]
        den = [_dot2(wts[j] + gg[j] * (q[j] * nvec[j]), seg64b) for j in R]
        h = [(num[j] + gg[j] * inter[j]) / jnp.maximum(jnp.abs(den[j]), emt[j]) for j in R]
        msq = [_dot2(h[j] * h[j], seg64b) * (1.0 / HEAD_DIM) for j in R]
        for j in R:
            y = h[j] * lax.rsqrt(msq[j] + EPS) * ng
            out_ref[pl.ds(r0[j], L), :] = (jax.nn.sigmoid(o_ref[pl.ds(r0[j], L), :]) * y).astype(out_ref.dtype)
        return carry

    lax.fori_loop(0, TS // (L * G), group_body, 0)

    @_when(last)
    def _():
        _store_block_diag(c_out, cs[...])
        n_out[...] = ns[...]
        m_out[...] = ms[...]


def _mlstm(z3, conv0, c0bd, n0, m0, gate_b, conv_w, norm_g, n_valid, chunk, host_grid=None):
    B, S, _ = z3.shape
    ts = min(TOKEN_TILE, S)
    assert S % ts == 0 and ts % chunk == 0 and (host_grid is None or S == ts)
    seq, til = _host_maps(host_grid)
    blk = lambda w, c0: pl.BlockSpec((None, ts, w), lambda b, t: (seq(b, t), til(b, t), c0 // w))
    per_b = lambda r, w: pl.BlockSpec((None, r, w), lambda b, t: (seq(b, t), 0, 0))
    per_head = pl.BlockSpec((None, N_MLSTM_HEADS, HEAD_DIM, HEAD_DIM), lambda b, t: (seq(b, t), 0, 0, 0))
    return _CallSpec(
        body=functools.partial(_mlstm_kernel, tile=ts, chunk=chunk, n_valid=n_valid, single_tile=host_grid is not None),
        grid=host_grid or (B, S // ts),
        args=[z3, z3, z3, z3, conv0, c0bd, n0, m0, gate_b, conv_w, norm_g],
        in_specs=[blk(2 * QUAD, COL_MQK), blk(QUAD, COL_MV), blk(QUAD, COL_MO), blk(LANES, COL_GATE),
                  per_b(SUBLANES, 2 * QUAD), per_head, per_b(1, QUAD), per_b(1, QUAD),
                  _const_spec((1, LANES)), _const_spec((CONV_WIDTH, 2 * QUAD)), _const_spec((1, QUAD))],
        out_shape=[jax.ShapeDtypeStruct((B, S, QUAD), BF16),
                   jax.ShapeDtypeStruct((B, N_MLSTM_HEADS, HEAD_DIM, HEAD_DIM), F32),
                   jax.ShapeDtypeStruct((B, 1, QUAD), F32),
                   jax.ShapeDtypeStruct((B, 1, QUAD), F32)],
        out_specs=[blk(QUAD, 0), per_head, per_b(1, QUAD), per_b(1, QUAD)],
        scratch=[pltpu.VMEM((ts + 2 * SUBLANES, 2 * QUAD), F32),
                 pltpu.VMEM((QUAD, QUAD), F32), pltpu.VMEM((1, QUAD), F32), pltpu.VMEM((1, QUAD), F32)])


HGRN_FAST_CHUNK = 64
HGRN_GROUP = 8
HGRN_SAFE_DECAY = 80.0


def _hgrn_kernel(q_ref, f_ref, i_ref, g_ref, s0_ref, lb_ref, ng_ref, out_ref, s_out, st, qs, fs, ks, *,
                 tile, sub, n_valid, fast, single_tile=False):
    TS = tile
    t, last = _tile_position(single_tile)

    seg64 = _seg_mask(QUAD, HEAD_SHIFT, QUAD, HEAD_SHIFT)
    seg64b = seg64.astype(BF16)

    @_when(t == 0)
    def _():
        st[...] = _load_block_diag(s0_ref, seg64)
    lb = lb_ref[...]
    ng = ng_ref[...]
    nt = (((1,), (1,)), ((), ()))
    tn = (((0,), (0,)), ((), ()))

    P = HGRN_FAST_CHUNK if fast else sub
    prow = _iota2((P, QUAD), 0)

    def prep(c, worst):
        r0 = pl.multiple_of(c * P, P)
        hq = q_ref[pl.ds(r0, P), :]
        f = lb + (1.0 - lb) * jax.nn.sigmoid(f_ref[pl.ds(r0, P), :])
        logf = jnp.log(f)
        kk = 1.0 - f
        if n_valid < TS:
            ok = (prow + r0) < n_valid
            logf = jnp.where(ok, logf, 0.0)
            kk = jnp.where(ok, kk, 0.0)
        qs[pl.ds(r0, P), :] = hq * jax.nn.sigmoid(hq)
        fs[pl.ds(r0, P), :] = logf
        ks[pl.ds(r0, P), :] = kk
        return jnp.minimum(worst, jnp.sum(logf, axis=0, keepdims=True))

    worst = lax.fori_loop(0, TS // P, prep, jnp.zeros((1, QUAD), F32))

    def finish(o, r0, rows, smat, b, kk, iv, qt):
        o = o + lax.dot_general(qt, smat.astype(BF16), nt, preferred_element_type=F32)
        out_ref[pl.ds(r0, rows), :] = _head_rmsnorm_gate(o, seg64b, ng, g_ref[pl.ds(r0, rows), :]).astype(out_ref.dtype)
        bl = b[rows - 1:rows, :]
        ktil = kk * jnp.exp(bl - b)
        u = lax.dot_general(iv.astype(BF16), ktil.astype(BF16), tn, preferred_element_type=F32)
        st[...] = smat * jnp.exp(bl) + jnp.where(seg64, u, 0.0)

    def fast_loop():
        L = HGRN_FAST_CHUNK
        tril = (_iota2((L, L), 1) <= _iota2((L, L), 0)).astype(BF16)
        causal = (_iota2((L, QUAD), 1) & (L - 1)) <= _iota2((L, QUAD), 0)
        zb = jnp.zeros((QUAD, QUAD), BF16)

        G = min(HGRN_GROUP, TS // L)
        assert TS % (L * G) == 0

        def body(gi, carry):
            R = range(G)
            r0 = [pl.multiple_of((gi * G + j) * L, L) for j in R]
            kk = [ks[pl.ds(r0[j], L), :] for j in R]
            ivb = [i_ref[pl.ds(r0[j], L), :].astype(BF16) for j in R]
            b = [_dot2r(tril, fs[pl.ds(r0[j], L), :]) for j in R]
            qt = [(qs[pl.ds(r0[j], L), :] * jnp.exp(b[j])).astype(BF16) for j in R]
            kbd = [jnp.where(seg64, jnp.concatenate([(kk[j] * jnp.exp(-b[j])).astype(BF16)] * N_HGRN_HEADS, axis=0), zb)
                   for j in R]
            ibd = [jnp.where(seg64, jnp.concatenate([ivb[j]] * N_HGRN_HEADS, axis=0), zb) for j in R]
            amat = [lax.dot_general(qt[j], kbd[j], nt, preferred_element_type=F32) for j in R]
            ktil = [(kk[j] * jnp.exp(b[j][L - 1:L, :] - b[j])).astype(BF16) for j in R]
            u = [lax.dot_general(ivb[j], ktil[j], tn, preferred_element_type=F32) for j in R]
            o = [jnp.dot(jnp.where(causal, amat[j], 0.0).astype(BF16), ibd[j], preferred_element_type=F32) for j in R]
            smat = [st[...]]
            for j in R:
                smat.append(smat[j] * jnp.exp(b[j][L - 1:L, :]) + jnp.where(seg64, u[j], 0.0))
            st[...] = smat[G]
            inter = [lax.dot_general(qt[j], smat[j].astype(BF16), nt, preferred_element_type=F32) for j in R]
            o = [o[j] + inter[j] for j in R]
            msq = [_dot2(o[j] * o[j], seg64b) * (1.0 / HEAD_DIM) for j in R]
            for j in R:
                y = o[j] * lax.rsqrt(msq[j] + EPS) * ng
                out_ref[pl.ds(r0[j], L), :] = (jax.nn.sigmoid(g_ref[pl.ds(r0[j], L), :]) * y).astype(out_ref.dtype)
            return carry

        lax.fori_loop(0, TS // (L * G), body, 0)

    def exact_loop():
        L = sub
        row = _iota2((L, QUAD), 0)

        def body(c, carry):
            r0 = pl.multiple_of(c * L, L)
            q = qs[pl.ds(r0, L), :]
            kk = ks[pl.ds(r0, L), :]
            iv = i_ref[pl.ds(r0, L), :]
            b = _cumulate_rows(fs[pl.ds(r0, L), :], L, jnp.add, 0.0)
            parts = []
            for j in range(L):
                dec = jnp.exp(jnp.where(row >= j, b - b[j:j + 1, :], NEG_BIG))
                parts.append(dec * q * kk[j:j + 1, :])
            tstack = jnp.concatenate(parts, axis=0).astype(BF16)
            y = jnp.dot(tstack, seg64b, preferred_element_type=F32)
            o = y[0:L] * iv[0:1, :]
            for j in range(1, L):
                o = o + y[j * L:(j + 1) * L] * iv[j:j + 1, :]
            finish(o, r0, L, st[...], b, kk, iv, (q * jnp.exp(b)).astype(BF16))
            return carry

        lax.fori_loop(0, TS // L, body, 0, unroll=min(CHUNK_UNROLL, TS // L))

    if fast:
        safe = jnp.min(worst) > -HGRN_SAFE_DECAY
        pl.when(safe)(fast_loop)
        pl.when(jnp.logical_not(safe))(exact_loop)
    else:
        exact_loop()

    @_when(last)
    def _():
        _store_block_diag(s_out, st[...])


def _hgrn(z3, s0t, lb, norm_g, n_valid, sub, host_grid=None):
    B, S, _ = z3.shape
    ts = min(TOKEN_TILE, S)
    fast = ts % HGRN_FAST_CHUNK == 0
    assert S % ts == 0 and ts % sub == 0 and (host_grid is None or S == ts)
    seq, til = _host_maps(host_grid)
    blk = lambda c0: pl.BlockSpec((None, ts, QUAD), lambda b, t: (seq(b, t), til(b, t), c0 // QUAD))
    per_b = pl.BlockSpec((None, N_HGRN_HEADS, HEAD_DIM, HEAD_DIM), lambda b, t: (seq(b, t), 0, 0, 0))
    stage = lambda: pltpu.VMEM((ts, QUAD), F32)
    return _CallSpec(
        body=functools.partial(_hgrn_kernel, tile=ts, sub=sub, n_valid=n_valid, fast=fast,
                               single_tile=host_grid is not None),
        grid=host_grid or (B, S // ts),
        args=[z3, z3, z3, z3, s0t, lb, norm_g],
        in_specs=[blk(COL_HQ), blk(COL_HF), blk(COL_HI), blk(COL_HG), per_b,
                  _const_spec((1, QUAD)), _const_spec((1, QUAD))],
        out_shape=[jax.ShapeDtypeStruct((B, S, QUAD), BF16),
                   jax.ShapeDtypeStruct((B, N_HGRN_HEADS, HEAD_DIM, HEAD_DIM), F32)],
        out_specs=[blk(0), per_b],
        scratch=[pltpu.VMEM((QUAD, QUAD), F32), stage(), stage(), stage()])


PROJ_SPLITS = (ATTN_WIDTH, ATTN_WIDTH, ATTN_WIDTH,
               MLSTM_WIDTH, MLSTM_WIDTH, MLSTM_WIDTH, MLSTM_WIDTH, N_MLSTM_HEADS, N_MLSTM_HEADS,
               HGRN_WIDTH, HGRN_WIDTH, HGRN_WIDTH, HGRN_WIDTH)


def _prep_w_in(w):
    g0 = int(np.sum(PROJ_SPLITS[:7]))
    g1 = g0 + 2 * N_MLSTM_HEADS
    assert (g0, w.shape[-1] - g1) == (COL_HQ, COL_GATE - COL_HQ)
    wb = w.astype(BF16)
    pad = jnp.zeros(w.shape[:-1] + (LANES - 2 * N_MLSTM_HEADS,), BF16)
    return wb, jnp.concatenate([wb[..., g1:], wb[..., g0:g1], pad], axis=-1)


HGRN_SUBCHUNK = 16


class _LayerWeights(NamedTuple):
    layer: int
    g_mix: jax.Array
    w_in: jax.Array
    gate_b: jax.Array
    conv_w: jax.Array
    mlstm_g: jax.Array
    lb: jax.Array
    hgrn_g: jax.Array
    w_out: jax.Array
    g_ffn: jax.Array
    w_up: jax.Array
    w_down: jax.Array
    g_final: jax.Array


def _gate_row(gate_b):
    flat = gate_b.astype(F32).reshape(1, 2 * N_MLSTM_HEADS)
    return jnp.concatenate([flat, jnp.zeros((1, LANES - 2 * N_MLSTM_HEADS), F32)], axis=1)


class _LayerFront(NamedTuple):
    z3: jax.Array
    kv_t: tuple
    mlstm: _CallSpec
    hgrn: _CallSpec


def _layer_front(x2d, B, S, n_valid, states, w, chunk_m, chunk_h, n_keep=None, stacked=None, host_grid=None):
    conv_buf, C0, n0, m0, S0 = states
    if n_keep is None:
        z, kv_t = _inproj(x2d, w.g_mix, w.w_in, w.layer), None
    else:
        z, kt, vt = _inproj(x2d, w.g_mix, w.w_in, w.layer, S, n_keep, stacked)
        kv_t = (kt, vt)
    z3 = z.reshape(B, S, ZW)
    conv0 = jnp.zeros((B, SUBLANES, 2 * MLSTM_WIDTH), F32).at[:, SUBLANES - (CONV_WIDTH - 1):].set(conv_buf.astype(F32))
    m0r = jnp.repeat(m0.astype(F32), HEAD_DIM, axis=1).reshape(B, 1, MLSTM_WIDTH)
    mlstm = _mlstm(z3, conv0, C0.astype(F32), n0.astype(F32).reshape(B, 1, MLSTM_WIDTH), m0r,
                   w.gate_b, w.conv_w, w.mlstm_g, n_valid, chunk_m, host_grid)
    hgrn = _hgrn(z3, jnp.swapaxes(S0.astype(F32), -1, -2), w.lb, w.hgrn_g, n_valid, chunk_h, host_grid)
    return _LayerFront(z3, kv_t, mlstm, hgrn)


def _layer_back(x2d, front, n_valid, w, final, attn, mlstm_out, hgrn_out):
    B, S, _ = front.z3.shape
    z3 = front.z3
    (ml, c_new, n, m), (hg, st) = mlstm_out, hgrn_out
    n2 = B * S
    y = _outffn(x2d, attn.reshape(n2, ATTN_WIDTH), ml.reshape(n2, MLSTM_WIDTH), hg.reshape(n2, HGRN_WIDTH),
                w.w_out, w.g_ffn, w.w_up, w.w_down, w.g_final, final, w.layer)
    k_rows = z3[:, :n_valid, COL_AK:COL_AK + ATTN_WIDTH].reshape(B, n_valid, N_ATTN_HEADS, HEAD_DIM)
    v_rows = z3[:, :n_valid, COL_AV:COL_AV + ATTN_WIDTH].reshape(B, n_valid, N_ATTN_HEADS, HEAD_DIM)
    conv_new = z3[:, n_valid - (CONV_WIDTH - 1):n_valid, COL_MQK:COL_MQK + 2 * MLSTM_WIDTH]
    s_new = jnp.swapaxes(st, -1, -2)
    return y, (k_rows, v_rows, conv_new, c_new, n.reshape(B, N_MLSTM_HEADS, HEAD_DIM), m[:, 0, ::HEAD_DIM], s_new)


def kernel(x_prompt, x_sample, cache_attn_k, cache_attn_v, state_mlstm_conv, state_mlstm_C, state_mlstm_n, state_mlstm_m, state_hgrn_S, rel_bias, g_mix, w_in, mlstm_gate_b, mlstm_conv_w, mlstm_norm_g, hgrn_lb_raw, hgrn_norm_g, w_out, g_ffn, w_up, w_down, g_final):
    depth = w_in.shape[0]
    bp, sp, _ = x_prompt.shape
    bs, ts, _ = x_sample.shape
    n_keep = min(max(w for w, _ in DILATED_PATTERNS), sp)
    n_past = cache_attn_k.shape[2]
    lb_p = jax.nn.softmax(hgrn_lb_raw.astype(F32), axis=0)
    hgrn_lb = jnp.cumsum(lb_p, axis=0) - lb_p[0]
    bias_dist = _bias_by_distance(rel_bias)
    p_tables = _prompt_tables(bias_dist)
    s_tables = _sample_tables(bias_dist, n_past, ts)
    cache_kt = jnp.transpose(cache_attn_k, (0, 1, 3, 4, 2)).reshape(depth, bs, ATTN_WIDTH, n_past)
    cache_vt = jnp.transpose(cache_attn_v, (0, 1, 3, 4, 2)).reshape(depth, bs, ATTN_WIDTH, n_past)

    hp = x_prompt.reshape(bp * sp, D_MODEL)
    hs = jnp.zeros((bs, TPAD, D_MODEL), F32).at[:, :ts].set(x_sample).reshape(bs * TPAD, D_MODEL)
    zero_states = (jnp.zeros((bp, CONV_WIDTH - 1, 2 * MLSTM_WIDTH), F32),
                   jnp.zeros((bp, N_MLSTM_HEADS, HEAD_DIM, HEAD_DIM), F32),
                   jnp.zeros((bp, N_MLSTM_HEADS, HEAD_DIM), F32),
                   jnp.zeros((bp, N_MLSTM_HEADS), F32),
                   jnp.zeros((bp, N_HGRN_HEADS, HEAD_DIM, HEAD_DIM), F32))
    p_states, s_states = [], []
    kv_t = None
    sem = ("parallel", "arbitrary")
    w_in_b, w_out_b, w_up_b, w_down_b = _prep_w_in(w_in), w_out.astype(BF16), w_up.astype(BF16), w_down.astype(BF16)
    for l in range(depth):
        final = l == depth - 1
        weights = _LayerWeights(
            layer=l, g_mix=g_mix[l][None], w_in=w_in_b,
            gate_b=_gate_row(mlstm_gate_b[l]),
            conv_w=mlstm_conv_w[l], mlstm_g=mlstm_norm_g[l][None], lb=hgrn_lb[l][None], hgrn_g=hgrn_norm_g[l][None],
            w_out=w_out_b, g_ffn=g_ffn[l][None], w_up=w_up_b, w_down=w_down_b, g_final=g_final[None])
        pf = _layer_front(hp, bp, sp, sp, zero_states, weights, MLSTM_CHUNK, HGRN_SUBCHUNK, n_keep, (l, depth, kv_t))
        kv_t = pf.kv_t
        grid = pf.mlstm.grid
        host = grid if bs == grid[0] * grid[1] else None
        states = (state_mlstm_conv[l], state_mlstm_C[l], state_mlstm_n[l], state_mlstm_m[l], state_hgrn_S[l])
        sf = _layer_front(hs, bs, TPAD, ts, states, weights, TPAD, TPAD, host_grid=host)
        s_attn = _attn_sample(sf.z3, cache_kt, cache_vt, l, s_tables, host)
        p_attn = _attn_prompt(pf.z3, p_tables)
        if host is None:
            p_ml, = _run([pf.mlstm], sem, "mlstm")
            (s_attn,), s_ml, s_hg = _run([s_attn, sf.mlstm, sf.hgrn], sem, "sample_mixers")
        else:
            p_ml, (s_attn,), s_ml, s_hg = _run([pf.mlstm, s_attn, sf.mlstm, sf.hgrn], sem, "mlstm_and_sample_mixers")
        p_hg, = _run([pf.hgrn], sem, "hgrn")
        hp, st = _layer_back(hp, pf, sp, weights, final, p_attn, p_ml, p_hg)
        p_states.append(st[2:])
        hs, st = _layer_back(hs, sf, ts, weights, final, s_attn, s_ml, s_hg)
        s_states.append(st)
    y_prompt = hp.reshape(bp, sp, D_MODEL)
    y_sample = hs.reshape(bs, TPAD, D_MODEL)[:, :ts]
    to_rows = lambda t: jnp.transpose(t.reshape(depth, bp, N_ATTN_HEADS, HEAD_DIM, n_keep), (0, 1, 4, 2, 3))
    p_out = [to_rows(kv_t[0]), to_rows(kv_t[1])] + [jnp.stack(a) for a in zip(*p_states)]
    s_out = [jnp.stack(a) for a in zip(*s_states)]
    return (y_prompt, y_sample, *p_out, *s_out)
```

```python
import functools
from typing import Callable, NamedTuple

import jax
import jax.numpy as jnp
import numpy as np
from jax import lax
from jax.experimental import pallas as pl
from jax.experimental.pallas import tpu as pltpu

F32 = jnp.float32
BF16 = jnp.bfloat16

D_MODEL = 1024
HEAD_DIM = 64
N_ATTN_HEADS = 8
N_MLSTM_HEADS = 4
N_HGRN_HEADS = 4
ATTN_WIDTH = N_ATTN_HEADS * HEAD_DIM
MLSTM_WIDTH = N_MLSTM_HEADS * HEAD_DIM
HGRN_WIDTH = N_HGRN_HEADS * HEAD_DIM
DILATED_PATTERNS = ((128, 1), (512, 4), (2048, 16))
QBLOCK = 128
N_REL_BUCKETS = 32
REL_MAX_DISTANCE = 2048
CONV_WIDTH = 4
MLSTM_CHUNK = 64
D_FF = 4 * D_MODEL
EPS = 1e-6
NEG_BIG = -1e30
LOG2E = 1.4426950408889634

LANES = 128
SUBLANES = 8

COL_AQ = 0
COL_AK = COL_AQ + ATTN_WIDTH
COL_AV = COL_AK + ATTN_WIDTH
COL_MQK = COL_AV + ATTN_WIDTH
COL_MV = COL_MQK + 2 * MLSTM_WIDTH
COL_MO = COL_MV + MLSTM_WIDTH
COL_HQ = COL_MO + MLSTM_WIDTH
COL_HF = COL_HQ + HGRN_WIDTH
COL_HI = COL_HF + HGRN_WIDTH
COL_HG = COL_HI + HGRN_WIDTH
COL_GATE = COL_HG + HGRN_WIDTH
ZW = COL_GATE + LANES

VMEM_LIMIT = 56 * 1024 * 1024
TOKEN_TILE = 512
ATTN_STAGE_ROWS = 512
ATTN_MIX_ROWS = 256


def _cparams(sem, vmem=VMEM_LIMIT):
    return pltpu.CompilerParams(dimension_semantics=sem, vmem_limit_bytes=vmem)


def _const_spec(shape):
    nd = len(shape)
    return pl.BlockSpec(shape, lambda *_: (0,) * nd, pipeline_mode=pl.Buffered(1))


def _layer_spec(shape, layer):
    nd = len(shape)
    return pl.BlockSpec((None,) + tuple(shape), lambda *_: (layer,) + (0,) * nd, pipeline_mode=pl.Buffered(1))


class _CallSpec(NamedTuple):
    body: Callable
    grid: tuple
    args: list
    in_specs: list
    out_shape: list
    out_specs: list
    scratch: list


def _run(specs, sem, name):
    grid = specs[0].grid
    assert all(s.grid == grid for s in specs)
    n_in = [len(s.args) for s in specs]
    n_out = [len(s.out_shape) for s in specs]
    n_scr = [len(s.scratch) for s in specs]

    def body(*refs):
        ins, outs, scrs = refs[:sum(n_in)], refs[sum(n_in):sum(n_in) + sum(n_out)], refs[sum(n_in) + sum(n_out):]

        for k, s in enumerate(specs):
            a, b, c = sum(n_in[:k]), sum(n_out[:k]), sum(n_scr[:k])
            s.body(*ins[a:a + n_in[k]], *outs[b:b + n_out[k]], *scrs[c:c + n_scr[k]])

    flat = pl.pallas_call(
        body, grid=grid,
        in_specs=[x for s in specs for x in s.in_specs],
        out_shape=[x for s in specs for x in s.out_shape],
        out_specs=[x for s in specs for x in s.out_specs],
        scratch_shapes=[x for s in specs for x in s.scratch],
        compiler_params=_cparams(sem), name=name,
    )(*[x for s in specs for x in s.args])
    out, k = [], 0
    for n in n_out:
        out.append(list(flat[k:k + n]))
        k += n
    return out


def _inproj_kernel(x_ref, g_ref, wa_ref, wb_ref, *rest, tiles_per_seq, first_keep, n_prev=0):
    x = x_ref[...]
    ms = jnp.mean(x * x, axis=-1, keepdims=True)
    xn = ((x * lax.rsqrt(ms + EPS)) * g_ref[...]).astype(BF16)
    z_ref = rest[n_prev]
    z_ref[:, 0:COL_HQ] = jnp.dot(xn, wa_ref[...], preferred_element_type=F32)
    z_ref[:, COL_HQ:] = jnp.dot(xn, wb_ref[...], preferred_element_type=F32)
    if first_keep is None:
        return
    kt_ref, vt_ref = rest[n_prev + 1:]

    kt_ref[...] = z_ref[:, COL_AK:COL_AK + ATTN_WIDTH].T
    vt_ref[...] = z_ref[:, COL_AV:COL_AV + ATTN_WIDTH].T


def _inproj(x2d, g, w, layer, seq=None, n_keep=None, stacked=None):
    n = x2d.shape[0]
    tm = min(TOKEN_TILE, n)
    assert n % tm == 0
    x_spec = pl.BlockSpec((tm, D_MODEL), lambda i: (i, 0))
    z_spec = pl.BlockSpec((tm, ZW), lambda i: (i, 0))
    z_shape = jax.ShapeDtypeStruct((n, ZW), F32)
    w_specs = [_layer_spec((D_MODEL, COL_HQ), layer), _layer_spec((D_MODEL, ZW - COL_HQ), layer)]
    if stacked is None:
        return pl.pallas_call(
            functools.partial(_inproj_kernel, tiles_per_seq=None, first_keep=None),
            out_shape=z_shape, grid=(n // tm,),
            in_specs=[x_spec, _const_spec((1, D_MODEL))] + w_specs,
            out_specs=z_spec, compiler_params=_cparams(("parallel",)), name="inproj",
        )(x2d, g, *w)
    assert seq % tm == 0 and n_keep % tm == 0
    layer, depth, prev = stacked
    tps, first_keep = seq // tm, (seq - n_keep) // tm
    t_spec = pl.BlockSpec((None, None, ATTN_WIDTH, tm),
                          lambda i: (layer, i // tps, 0, jnp.maximum(i % tps - first_keep, 0)))
    t_shape = jax.ShapeDtypeStruct((depth, n // seq, ATTN_WIDTH, n_keep), F32)
    in_specs = [x_spec, _const_spec((1, D_MODEL))] + w_specs
    args = [x2d, g, *w]
    if prev is None:
        prev = (jnp.zeros(t_shape.shape, F32), jnp.zeros(t_shape.shape, F32))
    in_specs += [pl.BlockSpec(memory_space=pl.ANY)] * 2
    aliases = {len(args): 1, len(args) + 1: 2}
    args += list(prev)
    return pl.pallas_call(
        functools.partial(_inproj_kernel, tiles_per_seq=tps, first_keep=first_keep, n_prev=len(prev)),
        out_shape=[z_shape, t_shape, t_shape], grid=(n // tm,),
        in_specs=in_specs, out_specs=[z_spec, t_spec, t_spec], input_output_aliases=aliases,
        compiler_params=_cparams(("arbitrary",)), name="inproj_kt",
    )(*args)


FF_CHUNK = 1024
FFN_ROWS = TOKEN_TILE


def _outffn_kernel(x_ref, a_ref, m_ref, h_ref, wo_ref, gf_ref, wu_ref, wd_ref, gl_ref, y_ref, xn_sc, *, final):
    x1 = x_ref[...]
    x1 = x1 + jnp.dot(a_ref[...], wo_ref[0:ATTN_WIDTH, :], preferred_element_type=F32)
    x1 = x1 + jnp.dot(m_ref[...], wo_ref[ATTN_WIDTH:ATTN_WIDTH + MLSTM_WIDTH, :], preferred_element_type=F32)
    x1 = x1 + jnp.dot(h_ref[...], wo_ref[ATTN_WIDTH + MLSTM_WIDTH:, :], preferred_element_type=F32)
    ms = jnp.mean(x1 * x1, axis=-1, keepdims=True)
    xn_sc[...] = ((x1 * lax.rsqrt(ms + EPS)) * gf_ref[...]).astype(BF16)
    y_ref[...] = x1
    nchunk = D_FF // FF_CHUNK
    up = lambda c: jnp.dot(xn_sc[...], wu_ref[:, c * FF_CHUNK:(c + 1) * FF_CHUNK], preferred_element_type=F32)
    u = up(0)
    for c in range(nchunk):
        u_next = up(c + 1) if c + 1 < nchunk else None
        hh = jnp.square(jnp.maximum(u, 0.0)).astype(BF16)
        y_ref[...] += jnp.dot(hh, wd_ref[c * FF_CHUNK:(c + 1) * FF_CHUNK, :], preferred_element_type=F32)
        u = u_next
    if final:
        x2 = y_ref[...]
        ms2 = jnp.mean(x2 * x2, axis=-1, keepdims=True)
        y_ref[...] = (x2 * lax.rsqrt(ms2 + EPS)) * gl_ref[...]


def _outffn(x2d, attn, ml, hg, w_out, g_ffn, w_up, w_down, g_final, final, layer):
    n = x2d.shape[0]
    tm = min(FFN_ROWS, n)
    assert n % tm == 0
    row = lambda w: pl.BlockSpec((tm, w), lambda i: (i, 0))
    return pl.pallas_call(
        functools.partial(_outffn_kernel, final=final),
        out_shape=jax.ShapeDtypeStruct((n, D_MODEL), F32),
        grid=(n // tm,),
        in_specs=[row(D_MODEL), row(ATTN_WIDTH), row(MLSTM_WIDTH), row(HGRN_WIDTH),
                  _layer_spec((D_MODEL, D_MODEL), layer), _const_spec((1, D_MODEL)),
                  _layer_spec((D_MODEL, D_FF), layer), _layer_spec((D_FF, D_MODEL), layer), _const_spec((1, D_MODEL))],
        out_specs=row(D_MODEL),
        scratch_shapes=[pltpu.VMEM((tm, D_MODEL), BF16)],
        compiler_params=_cparams(("parallel",)),
        name="outffn",
    )(x2d, attn, ml, hg, w_out, g_ffn, w_up, w_down, g_final)


def _t5_causal_bucket(dist):
    n = np.asarray(dist).astype(np.int32)
    max_exact = N_REL_BUCKETS // 2
    scaled = np.log(np.maximum(n, 1) / max_exact) / np.log(REL_MAX_DISTANCE / max_exact)
    large = np.minimum(max_exact + (scaled * (N_REL_BUCKETS - max_exact)).astype(np.int32), N_REL_BUCKETS - 1)
    return np.where(n < max_exact, n, large).astype(np.int32)


BIAS_DIST = 2304


def _bias_by_distance(rel_bias):
    assert BIAS_DIST > max(w for w, _ in DILATED_PATTERNS) + SUBLANES
    return rel_bias.astype(F32)[_t5_causal_bucket(np.arange(BIAS_DIST)[::-1])].T


def _dist_slice(bias_desc, lo, hi, step=1):
    last = BIAS_DIST - 1
    return bias_desc[:, last - hi:last - lo + 1:step]


def _prompt_tables(bias_dist):
    H = N_ATTN_HEADS
    a = np.arange(QBLOCK)[:, None]
    b = np.arange(2 * QBLOCK)[None, :]
    rel = QBLOCK + a - b
    band = (rel >= 0) & (rel <= QBLOCK)
    period = 3 * QBLOCK
    biases = []
    for window, dil in DILATED_PATTERNS:
        nsub = window // dil
        assert nsub == QBLOCK
        vd = _dist_slice(bias_dist, 0, nsub * dil, dil)
        rp = jnp.concatenate([vd, jnp.broadcast_to(vd[:, -1:], (H, QBLOCK)),
                              jnp.broadcast_to(vd[:, 0:1], (H, QBLOCK - 1))], axis=1)
        skew = jnp.tile(rp, (1, QBLOCK))[:, :QBLOCK * (period - 1)].reshape(H, QBLOCK, period - 1)
        biases.append(skew[:, :, :2 * QBLOCK].reshape(H // 2, 2 * QBLOCK, 2 * QBLOCK))
    mask = np.stack([np.tile(band, (2, 1)), np.tile(band & (b >= QBLOCK), (2, 1))])
    table = jnp.where(mask[None, None], jnp.stack(biases)[:, :, None] * LOG2E, -jnp.inf)
    return jnp.swapaxes(table, -1, -2)


ATTN_GROUP = 8


def _attn_prompt_kernel(q_ref, k_ref, v_ref, bias_ref, out_ref,
                        x4, qs, ks, vts, o1, o4, o16, l1, l4, l16, p_scr, st_scr, *, seq):
    S = seq
    nblk = S // QBLOCK
    scale = HEAD_DIM ** -0.5 * LOG2E
    zpad = jnp.zeros((QBLOCK, LANES), BF16)
    for di in range(3):
        ks[di, 0:QBLOCK, :] = zpad
        vts[di, 0] = zpad

    R = ATTN_STAGE_ROWS

    def put(src, di, row0, blk):
        if src is q_ref:
            qs[di, row0:row0 + blk.shape[0], :] = (blk * scale).astype(BF16)
        elif src is k_ref:
            ks[di, QBLOCK + row0:QBLOCK + row0 + blk.shape[0], :] = blk.astype(BF16)
        else:
            blk_t = blk.T.astype(BF16)
            for t in range(blk.shape[0] // QBLOCK):
                vts[di, 1 + row0 // QBLOCK + t] = blk_t[:, t * QBLOCK:(t + 1) * QBLOCK]

    for src in (q_ref, k_ref, v_ref):
        for i in range(S // R):
            put(src, 0, i * R, src[i * R:(i + 1) * R, :])
        for r in range(4):
            for i in range(S // 4 // R):
                blk = src[pl.ds(r + 4 * i * R, R, stride=4), :]
                row0 = r * (S // 4) + i * R
                x4[row0:row0 + R, :] = blk
                put(src, 1, row0, blk)
        n16 = S // 16
        for c4 in range(4):
            for a in range(4):
                put(src, 2, (c4 * 4 + a) * n16, x4[pl.ds(c4 * (S // 4) + a, n16, stride=4), :])

    lane = lax.broadcasted_iota(jnp.int32, (QBLOCK, LANES), 1)
    head_a = lane < HEAD_DIM
    row_a = lax.broadcasted_iota(jnp.int32, (QBLOCK, LANES), 0) < HEAD_DIM
    nt = (((1,), (1,)), ((), ()))

    def branch(di, dil, o_ref, l_ref):
        per_class = (S // dil) // QBLOCK

        R = range(ATTN_GROUP)
        zero = jnp.zeros((QBLOCK, LANES), BF16)

        def head_block(j):
            return per_class <= ATTN_GROUP and j % per_class == 0

        assert per_class % ATTN_GROUP == 0 or ATTN_GROUP % per_class == 0

        def scores(g, js):
            out = {}
            for j in js:
                i = g * ATTN_GROUP + j
                base = pl.multiple_of(i * QBLOCK, QBLOCK)
                q = qs[di, pl.ds(base, QBLOCK), :]
                q2 = jnp.concatenate([jnp.where(head_a, q, zero), jnp.where(head_a, zero, q)], axis=0)
                if head_block(j):
                    kk = ks[di, pl.ds(base + QBLOCK, QBLOCK), :]
                else:
                    kk = ks[di, pl.ds(base, 2 * QBLOCK), :]
                out[j] = lax.dot_general(kk, q2, nt, preferred_element_type=F32)
            return out

        def softmax(g, slot, st):
            for j in st:
                i = g * ATTN_GROUP + j
                if head_block(j):
                    s = st[j] + bias_ref[di, 1, QBLOCK:, :]
                else:
                    s = st[j] + bias_ref[di, jnp.where(i % per_class == 0, 1, 0)]
                m = jnp.max(s, axis=0, keepdims=True)
                p = jnp.exp2(s - m)
                if head_block(j):
                    p_scr[slot, j, QBLOCK:, :] = p.astype(BF16)
                else:
                    p_scr[slot, j] = p.astype(BF16)
                st_scr[slot, j, 0:1, :] = m
                st_scr[slot, j, 1:2, :] = jnp.sum(p, axis=0, keepdims=True)

        def weighted_values(g, slot, js):
            out = {}
            for j in js:
                i = g * ATTN_GROUP + j
                if head_block(j):
                    ot = jnp.dot(vts[di, i + 1], p_scr[slot, j, QBLOCK:, :], preferred_element_type=F32)
                else:
                    vt = jnp.concatenate([vts[di, i], vts[di, i + 1]], axis=1)
                    ot = jnp.dot(vt, p_scr[slot, j], preferred_element_type=F32)
                out[j] = (ot, st_scr[slot, j, 0:1, :], st_scr[slot, j, 1:2, :])
            return out

        def outputs(g, pv):
            for j in pv:
                i = g * ATTN_GROUP + j
                base = pl.multiple_of(i * QBLOCK, QBLOCK)
                ot, m, l = pv[j]
                otn = ot / l
                lse = m + jnp.log2(l)
                o = jnp.where(row_a, otn[:, 0:QBLOCK], otn[:, QBLOCK:]).T
                ls = jnp.where(row_a, jnp.broadcast_to(lse[:, 0:QBLOCK], (QBLOCK, LANES)),
                               jnp.broadcast_to(lse[:, QBLOCK:], (QBLOCK, LANES))).T
                if dil == 1:
                    o_ref[pl.ds(base, QBLOCK), :] = o
                    l_ref[pl.ds(base, QBLOCK), :] = ls
                else:
                    c, n = i // per_class, i % per_class
                    res = c if dil == 4 else 4 * (c % 4) + c // 4
                    start = dil * QBLOCK * n + res
                    o_ref[pl.ds(start, QBLOCK, stride=dil), :] = o
                    l_ref[pl.ds(start, QBLOCK, stride=dil), :] = ls

        ngroup = nblk // ATTN_GROUP
        softmax(0, 0, scores(0, R))

        def body(g, carry):
            slot = g % 2
            st = scores(g, R)
            pv = weighted_values(g - 1, 1 - slot, R)
            softmax(g, slot, st)
            outputs(g - 1, pv)
            return carry

        lax.fori_loop(1, ngroup, body, 0)
        outputs(ngroup - 1, weighted_values(ngroup - 1, (ngroup - 1) % 2, R))

    branch(0, 1, o1, l1)
    branch(1, 4, o4, l4)
    branch(2, 16, o16, l16)

    T = ATTN_MIX_ROWS

    def mix(i, carry):
        r0 = pl.multiple_of(i * T, T)
        la, lb, lc = l1[pl.ds(r0, T), :], l4[pl.ds(r0, T), :], l16[pl.ds(r0, T), :]
        mx = jnp.maximum(jnp.maximum(la, lb), lc)
        ea, eb, ec = jnp.exp2(la - mx), jnp.exp2(lb - mx), jnp.exp2(lc - mx)
        num = ea * o1[pl.ds(r0, T), :] + eb * o4[pl.ds(r0, T), :] + ec * o16[pl.ds(r0, T), :]
        out_ref[pl.ds(r0, T), :] = (num / (ea + eb + ec)).astype(out_ref.dtype)
        return carry

    lax.fori_loop(0, S // T, mix, 0)


def _attn_prompt(z3, bias):
    B, S, _ = z3.shape
    assert S % (16 * QBLOCK) == 0
    npair = N_ATTN_HEADS // 2
    col = lambda c0: pl.BlockSpec((None, S, LANES), lambda b, p: (b, 0, c0 // LANES + p))
    f32s = lambda: pltpu.VMEM((S, LANES), F32)
    return pl.pallas_call(
        functools.partial(_attn_prompt_kernel, seq=S),
        out_shape=jax.ShapeDtypeStruct((B, S, ATTN_WIDTH), BF16),
        grid=(B, npair),
        in_specs=[col(COL_AQ), col(COL_AK), col(COL_AV),
                  pl.BlockSpec((3, None, 2, 2 * QBLOCK, 2 * QBLOCK), lambda b, p: (0, p, 0, 0, 0))],
        out_specs=pl.BlockSpec((None, S, LANES), lambda b, p: (b, 0, p)),
        scratch_shapes=[f32s(),
                        pltpu.VMEM((3, S, LANES), BF16),
                        pltpu.VMEM((3, S + QBLOCK, LANES), BF16),
                        pltpu.VMEM((3, S // QBLOCK + 1, LANES, QBLOCK), BF16),
                        f32s(), f32s(), f32s(), f32s(), f32s(), f32s(),
                        pltpu.VMEM((2, ATTN_GROUP, 2 * QBLOCK, 2 * QBLOCK), BF16),
                        pltpu.VMEM((2, ATTN_GROUP, SUBLANES, 2 * QBLOCK), F32)],
        compiler_params=_cparams(("parallel", "parallel")),
        name="attn_prompt",
    )(z3, z3, z3, bias)


TPAD = SUBLANES
TAIL = 512


def _sample_tables(bias_dist, n_past, n_tok):
    assert n_past >= max(w for w, _ in DILATED_PATTERNS) and n_tok <= TPAD and TAIL >= DILATED_PATTERNS[1][0]
    H = N_ATTN_HEADS
    t = np.arange(TPAD)[:, None]

    def by_row(width):
        rows = [_dist_slice(bias_dist, tt + 1, tt + width) for tt in range(TPAD)]
        return jnp.stack(rows, axis=1).reshape(H * TPAD, width)

    bias_tail, bias_full = by_row(TAIL), by_row(n_past)
    zero_dist = _dist_slice(bias_dist, 0, 0)
    new_rows = [jnp.concatenate([_dist_slice(bias_dist, 0, tt),
                                 jnp.broadcast_to(zero_dist, (H, TPAD - tt - 1))], axis=1) for tt in range(TPAD)]
    bias_new = jnp.stack(new_rows, axis=1).reshape(H * TPAD, TPAD)

    def valid(dj, dil, nsub, lo):
        ok = (dj % dil == 0) & (dj // dil >= lo) & (dj // dil <= nsub) & (t < n_tok)
        return np.tile(ok, (H, 1)).astype(np.float32)

    tn = np.arange(TPAD)[None, :]
    m_tail, m_new = [], []
    for window, dil in DILATED_PATTERNS:
        nsub = window // dil
        if window <= TAIL:
            m_tail.append(valid(TAIL + t - np.arange(TAIL)[None, :], dil, nsub, 1))
        else:
            m_full = valid(n_past + t - np.arange(n_past)[None, :], dil, nsub, 1)
        m_new.append(valid(t - tn, dil, nsub, 0) * (tn < n_tok))
    return (bias_tail, bias_full, bias_new, jnp.asarray(np.stack(m_tail)), jnp.asarray(m_full),
            jnp.asarray(np.stack(m_new).astype(np.float32)))


def _attn_sample_kernel(q_ref, kn_ref, vn_ref, kt_ref, vt_ref, bt_ref, bf_ref, bn_ref, mt_ref, mf_ref, mn_ref,
                        out_ref, *, n_past):
    H = N_ATTN_HEADS
    rows = H * TPAD
    same_head = (_iota2((rows, ATTN_WIDTH), 0) >> _log2(TPAD)) == (_iota2((rows, ATTN_WIDTH), 1) >> HEAD_SHIFT)
    q = q_ref[...] * (HEAD_DIM ** -0.5)
    qm = jnp.where(same_head, jnp.concatenate([q] * H, axis=0), 0.0).astype(BF16)
    nt = (((1,), (1,)), ((), ()))
    kn = kn_ref[...].astype(BF16)
    vn = vn_ref[...].astype(BF16)
    kt = kt_ref[...].astype(BF16)
    vt = vt_ref[...].astype(BF16)
    s_new = lax.dot_general(qm, kn, nt, preferred_element_type=F32)
    s_full = jnp.dot(qm, kt, preferred_element_type=F32)
    s_tail = s_full[:, n_past - TAIL:]

    outs, lses = [], []
    for di, (window, _) in enumerate(DILATED_PATTERNS):
        if window <= TAIL:
            sm = jnp.where(mt_ref[di] > 0.5, s_tail + bt_ref[...], NEG_BIG)
            vmain = vt[:, n_past - TAIL:]
        else:
            sm = jnp.where(mf_ref[...] > 0.5, s_full + bf_ref[...], NEG_BIG)
            vmain = vt
        sn = jnp.where(mn_ref[di] > 0.5, s_new + bn_ref[...], NEG_BIG)
        m = jnp.maximum(jnp.max(sm, axis=-1, keepdims=True), jnp.max(sn, axis=-1, keepdims=True))
        pm = jnp.exp(sm - m)
        pn = jnp.exp(sn - m)
        l = jnp.sum(pm, axis=-1, keepdims=True) + jnp.sum(pn, axis=-1, keepdims=True)
        o = jnp.dot(pn.astype(BF16), vn, preferred_element_type=F32)
        o = o + lax.dot_general(pm.astype(BF16), vmain, nt, preferred_element_type=F32)
        outs.append(o / l)
        lses.append(m + jnp.log(l))
    mx = jnp.maximum(jnp.maximum(lses[0], lses[1]), lses[2])
    es = [jnp.exp(ls - mx) for ls in lses]
    mixed = (es[0] * outs[0] + es[1] * outs[1] + es[2] * outs[2]) / (es[0] + es[1] + es[2])
    mixed = jnp.where(same_head, mixed, 0.0)
    acc = mixed[0:TPAD]
    for h in range(1, H):
        acc = acc + mixed[h * TPAD:(h + 1) * TPAD]
    out_ref[...] = acc.astype(out_ref.dtype)


def _attn_sample(z3, cache_kt, cache_vt, layer, tables, host_grid=None):
    B, T, _ = z3.shape
    assert T == TPAD
    n_past = cache_kt.shape[-1]
    seq, _ = _host_maps(host_grid)
    new = lambda c0: pl.BlockSpec((None, TPAD, ATTN_WIDTH), lambda b, t: (seq(b, t), 0, c0 // ATTN_WIDTH))
    buf = pl.BlockSpec((None, None, ATTN_WIDTH, n_past), lambda b, t: (layer, seq(b, t), 0, 0))
    return _CallSpec(
        body=functools.partial(_attn_sample_kernel, n_past=n_past),
        grid=host_grid or (B, 1),
        args=[z3, z3, z3, cache_kt, cache_vt, *tables],
        in_specs=[new(COL_AQ), new(COL_AK), new(COL_AV), buf, buf] + [_const_spec(t.shape) for t in tables],
        out_shape=[jax.ShapeDtypeStruct((B, TPAD, ATTN_WIDTH), BF16)],
        out_specs=[pl.BlockSpec((None, TPAD, ATTN_WIDTH), lambda b, t: (seq(b, t), 0, 0))],
        scratch=[])


QUAD = N_MLSTM_HEADS * HEAD_DIM
CHUNK_UNROLL = 2
CHUNK_GROUP = 8
HEAD_SHIFT = 6


def _iota2(shape, axis):
    return lax.broadcasted_iota(jnp.int32, shape, axis)


def _log2(n):
    k = int(n).bit_length() - 1
    assert 1 << k == n
    return k


def _seg_mask(rows, row_shift, cols, col_shift):
    return (_iota2((rows, cols), 0) >> row_shift) == (_iota2((rows, cols), 1) >> col_shift)


def _cumulate_rows(x, length, op, fill):
    row = _iota2(x.shape, 0) & (length - 1)
    sh = 1
    while sh < length:
        x = op(x, jnp.where(row >= sh, pltpu.roll(x, sh, axis=0), fill))
        sh *= 2
    return x


def _split2(x):
    hi = x.astype(BF16)
    lo = (x - hi.astype(F32)).astype(BF16)
    return hi, lo


def _dot2(x, w):
    hi, lo = _split2(x)
    return jnp.dot(hi, w, preferred_element_type=F32) + jnp.dot(lo, w, preferred_element_type=F32)


def _dot2r(w, x):
    hi, lo = _split2(x)
    return jnp.dot(w, hi, preferred_element_type=F32) + jnp.dot(w, lo, preferred_element_type=F32)


def _tile_position(single_tile):
    if single_tile:
        return 0, True
    return pl.program_id(1), pl.program_id(1) == pl.num_programs(1) - 1


def _when(cond):
    if isinstance(cond, bool):
        return (lambda f: f()) if cond else (lambda f: None)
    return pl.when(cond)


def _host_maps(host_grid):
    if host_grid is None:
        return (lambda b, t: b), (lambda b, t: t)
    return (lambda b, t: b * host_grid[1] + t), (lambda b, t: 0)


def _load_block_diag(ref, seg64):
    flat = ref[...].reshape(QUAD, HEAD_DIM)
    return jnp.where(seg64, jnp.concatenate([flat] * (QUAD // HEAD_DIM), axis=1), 0.0)


def _store_block_diag(ref, mat):
    for h in range(QUAD // HEAD_DIM):
        ref[h] = mat[h * HEAD_DIM:(h + 1) * HEAD_DIM, h * HEAD_DIM:(h + 1) * HEAD_DIM]


def _head_rmsnorm_gate(h, seg64b, g_row, gate_pre):
    ms = _dot2(h * h, seg64b) * (1.0 / HEAD_DIM)
    return jax.nn.sigmoid(gate_pre) * (h * lax.rsqrt(ms + EPS) * g_row)


def _mlstm_kernel(qk_ref, v_ref, o_ref, gate_ref, conv0_ref, c0_ref, n0_ref, m0_ref, gb_ref, cw_ref, ng_ref,
                  out_ref, c_out, n_out, m_out, xp, cs, ns, ms, *, tile, chunk, n_valid, single_tile=False):
    TS, L = tile, chunk
    t, last = _tile_position(single_tile)
    PAD = SUBLANES

    @_when(t == 0)
    def _():
        cs[...] = _load_block_diag(c0_ref, _seg_mask(QUAD, HEAD_SHIFT, QUAD, HEAD_SHIFT))
        ns[...] = n0_ref[...]
        ms[...] = m0_ref[...]
        xp[0:PAD, :] = conv0_ref[...]

    @_when(t > 0)
    def _():
        xp[0:PAD, :] = xp[TS:TS + PAD, :]

    xp[PAD:PAD + TS, :] = qk_ref[...]

    LK = MLSTM_CHUNK
    seg64 = _seg_mask(QUAD, HEAD_SHIFT, QUAD, HEAD_SHIFT)
    seg64b = seg64.astype(BF16)
    dmask = (_iota2((LK, QUAD), 1) & (LK - 1)) == _iota2((LK, QUAD), 0)
    causal = (_iota2((L, QUAD), 1) & (LK - 1)) <= _iota2((L, QUAD), 0)
    tril = (_iota2((L, L), 1) <= _iota2((L, L), 0)).astype(BF16)
    ones_lk = jnp.ones((L, LK), BF16)
    row = _iota2((L, QUAD), 0)
    cw = cw_ref[...]
    gb = gb_ref[...]
    ng = ng_ref[...]

    first_half = _iota2((L, LANES), 1) < HEAD_DIM

    def per_head_lanes(g, lane0):
        col = [jnp.broadcast_to(g[:, lane0 + h:lane0 + h + 1], (L, LANES)) for h in range(N_MLSTM_HEADS)]
        return jnp.concatenate([jnp.where(first_half, col[0], col[1]), jnp.where(first_half, col[2], col[3])], axis=1)

    def key_rows(x, fill):
        if L == LK:
            return x
        return jnp.concatenate([x, jnp.full((LK - L, QUAD), fill, x.dtype)], axis=0)

    G = min(CHUNK_GROUP, TS // L)
    assert TS % (L * G) == 0
    nt = (((1,), (1,)), ((), ()))
    tn = (((0,), (0,)), ((), ()))
    zb = jnp.zeros((QUAD, QUAD), BF16)

    def group_body(gi, carry):
        R = range(G)
        r0 = [pl.multiple_of((gi * G + j) * L, L) for j in R]
        q, k, v, ig, lf = [], [], [], [], []
        for j in R:
            win = xp[pl.ds(r0[j], L + PAD), :]
            acc = win[PAD:PAD + L] * cw[CONV_WIDTH - 1:CONV_WIDTH, :]
            for s in range(1, CONV_WIDTH):
                acc = acc + win[PAD - s:PAD - s + L] * cw[CONV_WIDTH - 1 - s:CONV_WIDTH - s, :]
            qk = acc * jax.nn.sigmoid(acc)
            q.append(qk[:, 0:QUAD])
            k.append(qk[:, QUAD:] * (HEAD_DIM ** -0.5))
            v.append(v_ref[pl.ds(r0[j], L), :])
            gate = gate_ref[pl.ds(r0[j], L), :] + gb
            logsig = jnp.minimum(gate, 0.0) - jnp.log(1.0 + jnp.exp(-jnp.abs(gate)))
            ig_j = per_head_lanes(gate, 0)
            lf_j = per_head_lanes(logsig, N_MLSTM_HEADS)
            if n_valid < TS:
                ok = (row + r0[j]) < n_valid
                ig_j = jnp.where(ok, ig_j, NEG_BIG)
                lf_j = jnp.where(ok, lf_j, 0.0)
            ig.append(ig_j)
            lf.append(lf_j)
        b = [_dot2r(tril, lf[j]) for j in R]
        a = [ig[j] - b[j] for j in R]
        cm = [_cumulate_rows(a[j], L, jnp.maximum, -jnp.inf) for j in R]
        arow = [_dot2r(ones_lk, jnp.where(dmask, key_rows(a[j], NEG_BIG), 0.0)) for j in R]
        qb = [q[j].astype(BF16) for j in R]
        vb = [v[j].astype(BF16) for j in R]
        kbd = [jnp.where(seg64, jnp.concatenate([key_rows(k[j], 0.0).astype(BF16)] * N_MLSTM_HEADS, axis=0), zb) for j in R]
        vbd = [jnp.where(seg64, jnp.concatenate([key_rows(v[j], 0.0).astype(BF16)] * N_MLSTM_HEADS, axis=0), zb) for j in R]
        qkt = [lax.dot_general(qb[j], kbd[j], nt, preferred_element_type=F32) for j in R]

        mprev = ms[...]
        M, gg, emt = [], [], []
        for j in R:
            M.append(jnp.maximum(cm[j], mprev))
            mt = b[j] + M[j]
            gg.append(jnp.exp(mprev - M[j]))
            emt.append(jnp.exp(-mt))
            mprev = mt[L - 1:L, :]
        ms[...] = mprev

        wts = [jnp.where(causal, jnp.exp(arow[j] - M[j]), 0.0) * qkt[j] for j in R]
        kd = [k[j] * jnp.exp(a[j] - M[j][L - 1:L, :]) for j in R]
        num = [jnp.dot(wts[j].astype(BF16), vbd[j], preferred_element_type=F32) for j in R]
        u = [lax.dot_general(kd[j].astype(BF16), vb[j], tn, preferred_element_type=F32) for j in R]

        cmat, nvec = [cs[...]], [ns[...]]
        for j in R:
            gl = gg[j][L - 1:L, :]
            cmat.append(gl * cmat[j] + jnp.where(seg64, u[j], 0.0))
            nvec.append(gl * nvec[j] + jnp.sum(kd[j], axis=0, keepdims=True))
        cs[...] = cmat[G]
        ns[...] = nvec[G]

        inter = [jnp.dot(qb[j], cmat[j].astype(BF16), preferred_element_type=F32) for j in R]
        den = [_dot2(wts[j] + gg[j] * (q[j] * nvec[j]), seg64b) for j in R]
        h = [(num[j] + gg[j] * inter[j]) / jnp.maximum(jnp.abs(den[j]), emt[j]) for j in R]
        msq = [_dot2(h[j] * h[j], seg64b) * (1.0 / HEAD_DIM) for j in R]
        for j in R:
            y = h[j] * lax.rsqrt(msq[j] + EPS) * ng
            out_ref[pl.ds(r0[j], L), :] = (jax.nn.sigmoid(o_ref[pl.ds(r0[j], L), :]) * y).astype(out_ref.dtype)
        return carry

    lax.fori_loop(0, TS // (L * G), group_body, 0)

    @_when(last)
    def _():
        _store_block_diag(c_out, cs[...])
        n_out[...] = ns[...]
        m_out[...] = ms[...]


def _mlstm(z3, conv0, c0bd, n0, m0, gate_b, conv_w, norm_g, n_valid, chunk, host_grid=None):
    B, S, _ = z3.shape
    ts = min(TOKEN_TILE, S)
    assert S % ts == 0 and ts % chunk == 0 and (host_grid is None or S == ts)
    seq, til = _host_maps(host_grid)
    blk = lambda w, c0: pl.BlockSpec((None, ts, w), lambda b, t: (seq(b, t), til(b, t), c0 // w))
    per_b = lambda r, w: pl.BlockSpec((None, r, w), lambda b, t: (seq(b, t), 0, 0))
    per_head = pl.BlockSpec((None, N_MLSTM_HEADS, HEAD_DIM, HEAD_DIM), lambda b, t: (seq(b, t), 0, 0, 0))
    return _CallSpec(
        body=functools.partial(_mlstm_kernel, tile=ts, chunk=chunk, n_valid=n_valid, single_tile=host_grid is not None),
        grid=host_grid or (B, S // ts),
        args=[z3, z3, z3, z3, conv0, c0bd, n0, m0, gate_b, conv_w, norm_g],
        in_specs=[blk(2 * QUAD, COL_MQK), blk(QUAD, COL_MV), blk(QUAD, COL_MO), blk(LANES, COL_GATE),
                  per_b(SUBLANES, 2 * QUAD), per_head, per_b(1, QUAD), per_b(1, QUAD),
                  _const_spec((1, LANES)), _const_spec((CONV_WIDTH, 2 * QUAD)), _const_spec((1, QUAD))],
        out_shape=[jax.ShapeDtypeStruct((B, S, QUAD), BF16),
                   jax.ShapeDtypeStruct((B, N_MLSTM_HEADS, HEAD_DIM, HEAD_DIM), F32),
                   jax.ShapeDtypeStruct((B, 1, QUAD), F32),
                   jax.ShapeDtypeStruct((B, 1, QUAD), F32)],
        out_specs=[blk(QUAD, 0), per_head, per_b(1, QUAD), per_b(1, QUAD)],
        scratch=[pltpu.VMEM((ts + 2 * SUBLANES, 2 * QUAD), F32),
                 pltpu.VMEM((QUAD, QUAD), F32), pltpu.VMEM((1, QUAD), F32), pltpu.VMEM((1, QUAD), F32)])


HGRN_FAST_CHUNK = 64
HGRN_GROUP = 8
HGRN_SAFE_DECAY = 80.0


def _hgrn_kernel(q_ref, f_ref, i_ref, g_ref, s0_ref, lb_ref, ng_ref, out_ref, s_out, st, qs, fs, ks, *,
                 tile, sub, n_valid, fast, single_tile=False):
    TS = tile
    t, last = _tile_position(single_tile)

    seg64 = _seg_mask(QUAD, HEAD_SHIFT, QUAD, HEAD_SHIFT)
    seg64b = seg64.astype(BF16)

    @_when(t == 0)
    def _():
        st[...] = _load_block_diag(s0_ref, seg64)
    lb = lb_ref[...]
    ng = ng_ref[...]
    nt = (((1,), (1,)), ((), ()))
    tn = (((0,), (0,)), ((), ()))

    P = HGRN_FAST_CHUNK if fast else sub
    prow = _iota2((P, QUAD), 0)

    def prep(c, worst):
        r0 = pl.multiple_of(c * P, P)
        hq = q_ref[pl.ds(r0, P), :]
        f = lb + (1.0 - lb) * jax.nn.sigmoid(f_ref[pl.ds(r0, P), :])
        logf = jnp.log(f)
        kk = 1.0 - f
        if n_valid < TS:
            ok = (prow + r0) < n_valid
            logf = jnp.where(ok, logf, 0.0)
            kk = jnp.where(ok, kk, 0.0)
        qs[pl.ds(r0, P), :] = hq * jax.nn.sigmoid(hq)
        fs[pl.ds(r0, P), :] = logf
        ks[pl.ds(r0, P), :] = kk
        return jnp.minimum(worst, jnp.sum(logf, axis=0, keepdims=True))

    worst = lax.fori_loop(0, TS // P, prep, jnp.zeros((1, QUAD), F32))

    def finish(o, r0, rows, smat, b, kk, iv, qt):
        o = o + lax.dot_general(qt, smat.astype(BF16), nt, preferred_element_type=F32)
        out_ref[pl.ds(r0, rows), :] = _head_rmsnorm_gate(o, seg64b, ng, g_ref[pl.ds(r0, rows), :]).astype(out_ref.dtype)
        bl = b[rows - 1:rows, :]
        ktil = kk * jnp.exp(bl - b)
        u = lax.dot_general(iv.astype(BF16), ktil.astype(BF16), tn, preferred_element_type=F32)
        st[...] = smat * jnp.exp(bl) + jnp.where(seg64, u, 0.0)

    def fast_loop():
        L = HGRN_FAST_CHUNK
        tril = (_iota2((L, L), 1) <= _iota2((L, L), 0)).astype(BF16)
        causal = (_iota2((L, QUAD), 1) & (L - 1)) <= _iota2((L, QUAD), 0)
        zb = jnp.zeros((QUAD, QUAD), BF16)

        G = min(HGRN_GROUP, TS // L)
        assert TS % (L * G) == 0

        def body(gi, carry):
            R = range(G)
            r0 = [pl.multiple_of((gi * G + j) * L, L) for j in R]
            kk = [ks[pl.ds(r0[j], L), :] for j in R]
            ivb = [i_ref[pl.ds(r0[j], L), :].astype(BF16) for j in R]
            b = [_dot2r(tril, fs[pl.ds(r0[j], L), :]) for j in R]
            qt = [(qs[pl.ds(r0[j], L), :] * jnp.exp(b[j])).astype(BF16) for j in R]
            kbd = [jnp.where(seg64, jnp.concatenate([(kk[j] * jnp.exp(-b[j])).astype(BF16)] * N_HGRN_HEADS, axis=0), zb)
                   for j in R]
            ibd = [jnp.where(seg64, jnp.concatenate([ivb[j]] * N_HGRN_HEADS, axis=0), zb) for j in R]
            amat = [lax.dot_general(qt[j], kbd[j], nt, preferred_element_type=F32) for j in R]
            ktil = [(kk[j] * jnp.exp(b[j][L - 1:L, :] - b[j])).astype(BF16) for j in R]
            u = [lax.dot_general(ivb[j], ktil[j], tn, preferred_element_type=F32) for j in R]
            o = [jnp.dot(jnp.where(causal, amat[j], 0.0).astype(BF16), ibd[j], preferred_element_type=F32) for j in R]
            smat = [st[...]]
            for j in R:
                smat.append(smat[j] * jnp.exp(b[j][L - 1:L, :]) + jnp.where(seg64, u[j], 0.0))
            st[...] = smat[G]
            inter = [lax.dot_general(qt[j], smat[j].astype(BF16), nt, preferred_element_type=F32) for j in R]
            o = [o[j] + inter[j] for j in R]
            msq = [_dot2(o[j] * o[j], seg64b) * (1.0 / HEAD_DIM) for j in R]
            for j in R:
                y = o[j] * lax.rsqrt(msq[j] + EPS) * ng
                out_ref[pl.ds(r0[j], L), :] = (jax.nn.sigmoid(g_ref[pl.ds(r0[j], L), :]) * y).astype(out_ref.dtype)
            return carry

        lax.fori_loop(0, TS // (L * G), body, 0)

    def exact_loop():
        L = sub
        row = _iota2((L, QUAD), 0)

        def body(c, carry):
            r0 = pl.multiple_of(c * L, L)
            q = qs[pl.ds(r0, L), :]
            kk = ks[pl.ds(r0, L), :]
            iv = i_ref[pl.ds(r0, L), :]
            b = _cumulate_rows(fs[pl.ds(r0, L), :], L, jnp.add, 0.0)
            parts = []
            for j in range(L):
                dec = jnp.exp(jnp.where(row >= j, b - b[j:j + 1, :], NEG_BIG))
                parts.append(dec * q * kk[j:j + 1, :])
            tstack = jnp.concatenate(parts, axis=0).astype(BF16)
            y = jnp.dot(tstack, seg64b, preferred_element_type=F32)
            o = y[0:L] * iv[0:1, :]
            for j in range(1, L):
                o = o + y[j * L:(j + 1) * L] * iv[j:j + 1, :]
            finish(o, r0, L, st[...], b, kk, iv, (q * jnp.exp(b)).astype(BF16))
            return carry

        lax.fori_loop(0, TS // L, body, 0, unroll=min(CHUNK_UNROLL, TS // L))

    if fast:
        safe = jnp.min(worst) > -HGRN_SAFE_DECAY
        pl.when(safe)(fast_loop)
        pl.when(jnp.logical_not(safe))(exact_loop)
    else:
        exact_loop()

    @_when(last)
    def _():
        _store_block_diag(s_out, st[...])


def _hgrn(z3, s0t, lb, norm_g, n_valid, sub, host_grid=None):
    B, S, _ = z3.shape
    ts = min(TOKEN_TILE, S)
    fast = ts % HGRN_FAST_CHUNK == 0
    assert S % ts == 0 and ts % sub == 0 and (host_grid is None or S == ts)
    seq, til = _host_maps(host_grid)
    blk = lambda c0: pl.BlockSpec((None, ts, QUAD), lambda b, t: (seq(b, t), til(b, t), c0 // QUAD))
    per_b = pl.BlockSpec((None, N_HGRN_HEADS, HEAD_DIM, HEAD_DIM), lambda b, t: (seq(b, t), 0, 0, 0))
    stage = lambda: pltpu.VMEM((ts, QUAD), F32)
    return _CallSpec(
        body=functools.partial(_hgrn_kernel, tile=ts, sub=sub, n_valid=n_valid, fast=fast,
                               single_tile=host_grid is not None),
        grid=host_grid or (B, S // ts),
        args=[z3, z3, z3, z3, s0t, lb, norm_g],
        in_specs=[blk(COL_HQ), blk(COL_HF), blk(COL_HI), blk(COL_HG), per_b,
                  _const_spec((1, QUAD)), _const_spec((1, QUAD))],
        out_shape=[jax.ShapeDtypeStruct((B, S, QUAD), BF16),
                   jax.ShapeDtypeStruct((B, N_HGRN_HEADS, HEAD_DIM, HEAD_DIM), F32)],
        out_specs=[blk(0), per_b],
        scratch=[pltpu.VMEM((QUAD, QUAD), F32), stage(), stage(), stage()])


PROJ_SPLITS = (ATTN_WIDTH, ATTN_WIDTH, ATTN_WIDTH,
               MLSTM_WIDTH, MLSTM_WIDTH, MLSTM_WIDTH, MLSTM_WIDTH, N_MLSTM_HEADS, N_MLSTM_HEADS,
               HGRN_WIDTH, HGRN_WIDTH, HGRN_WIDTH, HGRN_WIDTH)


def _prep_w_in(w):
    g0 = int(np.sum(PROJ_SPLITS[:7]))
    g1 = g0 + 2 * N_MLSTM_HEADS
    assert (g0, w.shape[-1] - g1) == (COL_HQ, COL_GATE - COL_HQ)
    wb = w.astype(BF16)
    pad = jnp.zeros(w.shape[:-1] + (LANES - 2 * N_MLSTM_HEADS,), BF16)
    return wb, jnp.concatenate([wb[..., g1:], wb[..., g0:g1], pad], axis=-1)


HGRN_SUBCHUNK = 16


class _LayerWeights(NamedTuple):
    layer: int
    g_mix: jax.Array
    w_in: jax.Array
    gate_b: jax.Array
    conv_w: jax.Array
    mlstm_g: jax.Array
    lb: jax.Array
    hgrn_g: jax.Array
    w_out: jax.Array
    g_ffn: jax.Array
    w_up: jax.Array
    w_down: jax.Array
    g_final: jax.Array


def _gate_row(gate_b):
    flat = gate_b.astype(F32).reshape(1, 2 * N_MLSTM_HEADS)
    return jnp.concatenate([flat, jnp.zeros((1, LANES - 2 * N_MLSTM_HEADS), F32)], axis=1)


class _LayerFront(NamedTuple):
    z3: jax.Array
    kv_t: tuple
    mlstm: _CallSpec
    hgrn: _CallSpec


def _layer_front(x2d, B, S, n_valid, states, w, chunk_m, chunk_h, n_keep=None, stacked=None, host_grid=None):
    conv_buf, C0, n0, m0, S0 = states
    if n_keep is None:
        z, kv_t = _inproj(x2d, w.g_mix, w.w_in, w.layer), None
    else:
        z, kt, vt = _inproj(x2d, w.g_mix, w.w_in, w.layer, S, n_keep, stacked)
        kv_t = (kt, vt)
    z3 = z.reshape(B, S, ZW)
    conv0 = jnp.zeros((B, SUBLANES, 2 * MLSTM_WIDTH), F32).at[:, SUBLANES - (CONV_WIDTH - 1):].set(conv_buf.astype(F32))
    m0r = jnp.repeat(m0.astype(F32), HEAD_DIM, axis=1).reshape(B, 1, MLSTM_WIDTH)
    mlstm = _mlstm(z3, conv0, C0.astype(F32), n0.astype(F32).reshape(B, 1, MLSTM_WIDTH), m0r,
                   w.gate_b, w.conv_w, w.mlstm_g, n_valid, chunk_m, host_grid)
    hgrn = _hgrn(z3, jnp.swapaxes(S0.astype(F32), -1, -2), w.lb, w.hgrn_g, n_valid, chunk_h, host_grid)
    return _LayerFront(z3, kv_t, mlstm, hgrn)


def _layer_back(x2d, front, n_valid, w, final, attn, mlstm_out, hgrn_out):
    B, S, _ = front.z3.shape
    z3 = front.z3
    (ml, c_new, n, m), (hg, st) = mlstm_out, hgrn_out
    n2 = B * S
    y = _outffn(x2d, attn.reshape(n2, ATTN_WIDTH), ml.reshape(n2, MLSTM_WIDTH), hg.reshape(n2, HGRN_WIDTH),
                w.w_out, w.g_ffn, w.w_up, w.w_down, w.g_final, final, w.layer)
    k_rows = z3[:, :n_valid, COL_AK:COL_AK + ATTN_WIDTH].reshape(B, n_valid, N_ATTN_HEADS, HEAD_DIM)
    v_rows = z3[:, :n_valid, COL_AV:COL_AV + ATTN_WIDTH].reshape(B, n_valid, N_ATTN_HEADS, HEAD_DIM)
    conv_new = z3[:, n_valid - (CONV_WIDTH - 1):n_valid, COL_MQK:COL_MQK + 2 * MLSTM_WIDTH]
    s_new = jnp.swapaxes(st, -1, -2)
    return y, (k_rows, v_rows, conv_new, c_new, n.reshape(B, N_MLSTM_HEADS, HEAD_DIM), m[:, 0, ::HEAD_DIM], s_new)


def kernel(x_prompt, x_sample, cache_attn_k, cache_attn_v, state_mlstm_conv, state_mlstm_C, state_mlstm_n, state_mlstm_m, state_hgrn_S, rel_bias, g_mix, w_in, mlstm_gate_b, mlstm_conv_w, mlstm_norm_g, hgrn_lb_raw, hgrn_norm_g, w_out, g_ffn, w_up, w_down, g_final):
    depth = w_in.shape[0]
    bp, sp, _ = x_prompt.shape
    bs, ts, _ = x_sample.shape
    n_keep = min(max(w for w, _ in DILATED_PATTERNS), sp)
    n_past = cache_attn_k.shape[2]
    lb_p = jax.nn.softmax(hgrn_lb_raw.astype(F32), axis=0)
    hgrn_lb = jnp.cumsum(lb_p, axis=0) - lb_p[0]
    bias_dist = _bias_by_distance(rel_bias)
    p_tables = _prompt_tables(bias_dist)
    s_tables = _sample_tables(bias_dist, n_past, ts)
    cache_kt = jnp.transpose(cache_attn_k, (0, 1, 3, 4, 2)).reshape(depth, bs, ATTN_WIDTH, n_past)
    cache_vt = jnp.transpose(cache_attn_v, (0, 1, 3, 4, 2)).reshape(depth, bs, ATTN_WIDTH, n_past)

    hp = x_prompt.reshape(bp * sp, D_MODEL)
    hs = jnp.zeros((bs, TPAD, D_MODEL), F32).at[:, :ts].set(x_sample).reshape(bs * TPAD, D_MODEL)
    zero_states = (jnp.zeros((bp, CONV_WIDTH - 1, 2 * MLSTM_WIDTH), F32),
                   jnp.zeros((bp, N_MLSTM_HEADS, HEAD_DIM, HEAD_DIM), F32),
                   jnp.zeros((bp, N_MLSTM_HEADS, HEAD_DIM), F32),
                   jnp.zeros((bp, N_MLSTM_HEADS), F32),
                   jnp.zeros((bp, N_HGRN_HEADS, HEAD_DIM, HEAD_DIM), F32))
    p_states, s_states = [], []
    kv_t = None
    sem = ("parallel", "arbitrary")
    w_in_b, w_out_b, w_up_b, w_down_b = _prep_w_in(w_in), w_out.astype(BF16), w_up.astype(BF16), w_down.astype(BF16)
    for l in range(depth):
        final = l == depth - 1
        weights = _LayerWeights(
            layer=l, g_mix=g_mix[l][None], w_in=w_in_b,
            gate_b=_gate_row(mlstm_gate_b[l]),
            conv_w=mlstm_conv_w[l], mlstm_g=mlstm_norm_g[l][None], lb=hgrn_lb[l][None], hgrn_g=hgrn_norm_g[l][None],
            w_out=w_out_b, g_ffn=g_ffn[l][None], w_up=w_up_b, w_down=w_down_b, g_final=g_final[None])
        pf = _layer_front(hp, bp, sp, sp, zero_states, weights, MLSTM_CHUNK, HGRN_SUBCHUNK, n_keep, (l, depth, kv_t))
        kv_t = pf.kv_t
        grid = pf.mlstm.grid
        host = grid if bs == grid[0] * grid[1] else None
        states = (state_mlstm_conv[l], state_mlstm_C[l], state_mlstm_n[l], state_mlstm_m[l], state_hgrn_S[l])
        sf = _layer_front(hs, bs, TPAD, ts, states, weights, TPAD, TPAD, host_grid=host)
        s_attn = _attn_sample(sf.z3, cache_kt, cache_vt, l, s_tables, host)
        p_attn = _attn_prompt(pf.z3, p_tables)
        if host is None:
            p_ml, = _run([pf.mlstm], sem, "mlstm")
            (s_attn,), s_ml, s_hg = _run([s_attn, sf.mlstm, sf.hgrn], sem, "sample_mixers")
        else:
            p_ml, (s_attn,), s_ml, s_hg = _run([pf.mlstm, s_attn, sf.mlstm, sf.hgrn], sem, "mlstm_and_sample_mixers")
        p_hg, = _run([pf.hgrn], sem, "hgrn")
        hp, st = _layer_back(hp, pf, sp, weights, final, p_attn, p_ml, p_hg)
        p_states.append(st[2:])
        hs, st = _layer_back(hs, sf, ts, weights, final, s_attn, s_ml, s_hg)
        s_states.append(st)
    y_prompt = hp.reshape(bp, sp, D_MODEL)
    y_sample = hs.reshape(bs, TPAD, D_MODEL)[:, :ts]
    to_rows = lambda t: jnp.transpose(t.reshape(depth, bp, N_ATTN_HEADS, HEAD_DIM, n_keep), (0, 1, 4, 2, 3))
    p_out = [to_rows(kv_t[0]), to_rows(kv_t[1])] + [jnp.stack(a) for a in zip(*p_states)]
    s_out = [jnp.stack(a) for a in zip(*s_states)]
    return (y_prompt, y_sample, *p_out, *s_out)
```

```python
import functools
from typing import Callable, NamedTuple

import jax
import jax.numpy as jnp
import numpy as np
from jax import lax
from jax.experimental import pallas as pl
from jax.experimental.pallas import tpu as pltpu

F32 = jnp.float32
BF16 = jnp.bfloat16

D_MODEL = 1024
HEAD_DIM = 64
N_ATTN_HEADS = 8
N_MLSTM_HEADS = 4
N_HGRN_HEADS = 4
ATTN_WIDTH = N_ATTN_HEADS * HEAD_DIM
MLSTM_WIDTH = N_MLSTM_HEADS * HEAD_DIM
HGRN_WIDTH = N_HGRN_HEADS * HEAD_DIM
DILATED_PATTERNS = ((128, 1), (512, 4), (2048, 16))
QBLOCK = 128
N_REL_BUCKETS = 32
REL_MAX_DISTANCE = 2048
CONV_WIDTH = 4
MLSTM_CHUNK = 64
D_FF = 4 * D_MODEL
EPS = 1e-6
NEG_BIG = -1e30
LOG2E = 1.4426950408889634

LANES = 128
SUBLANES = 8

COL_AQ = 0
COL_AK = COL_AQ + ATTN_WIDTH
COL_AV = COL_AK + ATTN_WIDTH
COL_MQK = COL_AV + ATTN_WIDTH
COL_MV = COL_MQK + 2 * MLSTM_WIDTH
COL_MO = COL_MV + MLSTM_WIDTH
COL_HQ = COL_MO + MLSTM_WIDTH
COL_HF = COL_HQ + HGRN_WIDTH
COL_HI = COL_HF + HGRN_WIDTH
COL_HG = COL_HI + HGRN_WIDTH
COL_GATE = COL_HG + HGRN_WIDTH
ZW = COL_GATE + LANES

VMEM_LIMIT = 56 * 1024 * 1024
TOKEN_TILE = 512
ATTN_STAGE_ROWS = 512
ATTN_MIX_ROWS = 256


def _cparams(sem, vmem=VMEM_LIMIT):
    return pltpu.CompilerParams(dimension_semantics=sem, vmem_limit_bytes=vmem)


def _const_spec(shape):
    nd = len(shape)
    return pl.BlockSpec(shape, lambda *_: (0,) * nd, pipeline_mode=pl.Buffered(1))


def _layer_spec(shape, layer):
    nd = len(shape)
    return pl.BlockSpec((None,) + tuple(shape), lambda *_: (layer,) + (0,) * nd, pipeline_mode=pl.Buffered(1))


class _CallSpec(NamedTuple):
    body: Callable
    grid: tuple
    args: list
    in_specs: list
    out_shape: list
    out_specs: list
    scratch: list


def _run(specs, sem, name, host=None):
    grid = specs[0].grid
    assert all(s.grid == grid for s in specs)
    n_in = [len(s.args) for s in specs]
    n_out = [len(s.out_shape) for s in specs]
    n_scr = [len(s.scratch) for s in specs]

    def body(*refs):
        ins, outs, scrs = refs[:sum(n_in)], refs[sum(n_in):sum(n_in) + sum(n_out)], refs[sum(n_in) + sum(n_out):]

        def bound(k, **kw):
            a, b, c = sum(n_in[:k]), sum(n_out[:k]), sum(n_scr[:k])
            return functools.partial(specs[k].body, *ins[a:a + n_in[k]], *outs[b:b + n_out[k]], *scrs[c:c + n_scr[k]], **kw)

        others = [k for k in range(len(specs)) if k != host]
        if host is None:
            for k in others:
                bound(k)()
        else:
            bound(host, guests=lambda: [bound(k)() for k in others])()

    flat = pl.pallas_call(
        body, grid=grid,
        in_specs=[x for s in specs for x in s.in_specs],
        out_shape=[x for s in specs for x in s.out_shape],
        out_specs=[x for s in specs for x in s.out_specs],
        scratch_shapes=[x for s in specs for x in s.scratch],
        compiler_params=_cparams(sem), name=name,
    )(*[x for s in specs for x in s.args])
    out, k = [], 0
    for n in n_out:
        out.append(list(flat[k:k + n]))
        k += n
    return out


def _inproj_kernel(x_ref, g_ref, wa_ref, wb_ref, *rest, tiles_per_seq, first_keep, n_prev=0):
    x = x_ref[...]
    ms = jnp.mean(x * x, axis=-1, keepdims=True)
    xn = ((x * lax.rsqrt(ms + EPS)) * g_ref[...]).astype(BF16)
    z_ref = rest[n_prev]
    z_ref[:, 0:COL_HQ] = jnp.dot(xn, wa_ref[...], preferred_element_type=F32)
    z_ref[:, COL_HQ:] = jnp.dot(xn, wb_ref[...], preferred_element_type=F32)
    if first_keep is None:
        return
    kt_ref, vt_ref = rest[n_prev + 1:]

    kt_ref[...] = z_ref[:, COL_AK:COL_AK + ATTN_WIDTH].T
    vt_ref[...] = z_ref[:, COL_AV:COL_AV + ATTN_WIDTH].T


def _inproj(x2d, g, w, layer, seq=None, n_keep=None, stacked=None):
    n = x2d.shape[0]
    tm = min(TOKEN_TILE, n)
    assert n % tm == 0
    x_spec = pl.BlockSpec((tm, D_MODEL), lambda i: (i, 0))
    z_spec = pl.BlockSpec((tm, ZW), lambda i: (i, 0))
    z_shape = jax.ShapeDtypeStruct((n, ZW), F32)
    w_specs = [_layer_spec((D_MODEL, COL_HQ), layer), _layer_spec((D_MODEL, ZW - COL_HQ), layer)]
    if stacked is None:
        return pl.pallas_call(
            functools.partial(_inproj_kernel, tiles_per_seq=None, first_keep=None),
            out_shape=z_shape, grid=(n // tm,),
            in_specs=[x_spec, _const_spec((1, D_MODEL))] + w_specs,
            out_specs=z_spec, compiler_params=_cparams(("parallel",)), name="inproj",
        )(x2d, g, *w)
    assert seq % tm == 0 and n_keep % tm == 0
    layer, depth, prev = stacked
    tps, first_keep = seq // tm, (seq - n_keep) // tm
    t_spec = pl.BlockSpec((None, None, ATTN_WIDTH, tm),
                          lambda i: (layer, i // tps, 0, jnp.maximum(i % tps - first_keep, 0)))
    t_shape = jax.ShapeDtypeStruct((depth, n // seq, ATTN_WIDTH, n_keep), F32)
    in_specs = [x_spec, _const_spec((1, D_MODEL))] + w_specs
    args = [x2d, g, *w]
    if prev is None:
        prev = (jnp.zeros(t_shape.shape, F32), jnp.zeros(t_shape.shape, F32))
    in_specs += [pl.BlockSpec(memory_space=pl.ANY)] * 2
    aliases = {len(args): 1, len(args) + 1: 2}
    args += list(prev)
    return pl.pallas_call(
        functools.partial(_inproj_kernel, tiles_per_seq=tps, first_keep=first_keep, n_prev=len(prev)),
        out_shape=[z_shape, t_shape, t_shape], grid=(n // tm,),
        in_specs=in_specs, out_specs=[z_spec, t_spec, t_spec], input_output_aliases=aliases,
        compiler_params=_cparams(("arbitrary",)), name="inproj_kt",
    )(*args)


FF_CHUNK = 1024
FFN_ROWS = TOKEN_TILE


def _outffn_kernel(x_ref, a_ref, m_ref, h_ref, wo_ref, gf_ref, wu_ref, wd_ref, gl_ref, y_ref, xn_sc, *, final):
    x1 = x_ref[...]
    x1 = x1 + jnp.dot(a_ref[...], wo_ref[0:ATTN_WIDTH, :], preferred_element_type=F32)
    x1 = x1 + jnp.dot(m_ref[...], wo_ref[ATTN_WIDTH:ATTN_WIDTH + MLSTM_WIDTH, :], preferred_element_type=F32)
    x1 = x1 + jnp.dot(h_ref[...], wo_ref[ATTN_WIDTH + MLSTM_WIDTH:, :], preferred_element_type=F32)
    ms = jnp.mean(x1 * x1, axis=-1, keepdims=True)
    xn_sc[...] = ((x1 * lax.rsqrt(ms + EPS)) * gf_ref[...]).astype(BF16)
    y_ref[...] = x1
    nchunk = D_FF // FF_CHUNK
    up = lambda c: jnp.dot(xn_sc[...], wu_ref[:, c * FF_CHUNK:(c + 1) * FF_CHUNK], preferred_element_type=F32)
    u = up(0)
    for c in range(nchunk):
        u_next = up(c + 1) if c + 1 < nchunk else None
        hh = jnp.square(jnp.maximum(u, 0.0)).astype(BF16)
        y_ref[...] += jnp.dot(hh, wd_ref[c * FF_CHUNK:(c + 1) * FF_CHUNK, :], preferred_element_type=F32)
        u = u_next
    if final:
        x2 = y_ref[...]
        ms2 = jnp.mean(x2 * x2, axis=-1, keepdims=True)
        y_ref[...] = (x2 * lax.rsqrt(ms2 + EPS)) * gl_ref[...]


def _outffn(x2d, attn, ml, hg, w_out, g_ffn, w_up, w_down, g_final, final, layer):
    n = x2d.shape[0]
    tm = min(FFN_ROWS, n)
    assert n % tm == 0
    row = lambda w: pl.BlockSpec((tm, w), lambda i: (i, 0))
    return pl.pallas_call(
        functools.partial(_outffn_kernel, final=final),
        out_shape=jax.ShapeDtypeStruct((n, D_MODEL), F32),
        grid=(n // tm,),
        in_specs=[row(D_MODEL), row(ATTN_WIDTH), row(MLSTM_WIDTH), row(HGRN_WIDTH),
                  _layer_spec((D_MODEL, D_MODEL), layer), _const_spec((1, D_MODEL)),
                  _layer_spec((D_MODEL, D_FF), layer), _layer_spec((D_FF, D_MODEL), layer), _const_spec((1, D_MODEL))],
        out_specs=row(D_MODEL),
        scratch_shapes=[pltpu.VMEM((tm, D_MODEL), BF16)],
        compiler_params=_cparams(("parallel",)),
        name="outffn",
    )(x2d, attn, ml, hg, w_out, g_ffn, w_up, w_down, g_final)


def _t5_causal_bucket(dist):
    n = np.asarray(dist).astype(np.int32)
    max_exact = N_REL_BUCKETS // 2
    scaled = np.log(np.maximum(n, 1) / max_exact) / np.log(REL_MAX_DISTANCE / max_exact)
    large = np.minimum(max_exact + (scaled * (N_REL_BUCKETS - max_exact)).astype(np.int32), N_REL_BUCKETS - 1)
    return np.where(n < max_exact, n, large).astype(np.int32)


BIAS_DIST = 2304


def _bias_by_distance(rel_bias):
    assert BIAS_DIST > max(w for w, _ in DILATED_PATTERNS) + SUBLANES
    return rel_bias.astype(F32)[_t5_causal_bucket(np.arange(BIAS_DIST)[::-1])].T


def _dist_slice(bias_desc, lo, hi, step=1):
    last = BIAS_DIST - 1
    return bias_desc[:, last - hi:last - lo + 1:step]


def _prompt_tables(bias_dist):
    H = N_ATTN_HEADS
    a = np.arange(QBLOCK)[:, None]
    b = np.arange(2 * QBLOCK)[None, :]
    rel = QBLOCK + a - b
    band = (rel >= 0) & (rel <= QBLOCK)
    period = 3 * QBLOCK
    biases = []
    for window, dil in DILATED_PATTERNS:
        nsub = window // dil
        assert nsub == QBLOCK
        vd = _dist_slice(bias_dist, 0, nsub * dil, dil)
        rp = jnp.concatenate([vd, jnp.broadcast_to(vd[:, -1:], (H, QBLOCK)),
                              jnp.broadcast_to(vd[:, 0:1], (H, QBLOCK - 1))], axis=1)
        skew = jnp.tile(rp, (1, QBLOCK))[:, :QBLOCK * (period - 1)].reshape(H, QBLOCK, period - 1)
        biases.append(skew[:, :, :2 * QBLOCK].reshape(H // 2, 2 * QBLOCK, 2 * QBLOCK))
    mask = np.stack([np.tile(band, (2, 1)), np.tile(band & (b >= QBLOCK), (2, 1))])
    table = jnp.where(mask[None, None], jnp.stack(biases)[:, :, None] * LOG2E, -jnp.inf)
    return jnp.swapaxes(table, -1, -2)


ATTN_GROUP = 8


def _attn_prompt_kernel(q_ref, k_ref, v_ref, bias_ref, out_ref,
                        x4, qs, ks, vts, o1, o4, o16, l1, l4, l16, p_scr, st_scr, *, seq):
    S = seq
    nblk = S // QBLOCK
    scale = HEAD_DIM ** -0.5 * LOG2E
    zpad = jnp.zeros((QBLOCK, LANES), BF16)
    for di in range(3):
        ks[di, 0:QBLOCK, :] = zpad
        vts[di, 0] = zpad

    R = ATTN_STAGE_ROWS

    def put(src, di, row0, blk):
        if src is q_ref:
            qs[di, row0:row0 + blk.shape[0], :] = (blk * scale).astype(BF16)
        elif src is k_ref:
            ks[di, QBLOCK + row0:QBLOCK + row0 + blk.shape[0], :] = blk.astype(BF16)
        else:
            blk_t = blk.T.astype(BF16)
            for t in range(blk.shape[0] // QBLOCK):
                vts[di, 1 + row0 // QBLOCK + t] = blk_t[:, t * QBLOCK:(t + 1) * QBLOCK]

    for src in (q_ref, k_ref, v_ref):
        for i in range(S // R):
            put(src, 0, i * R, src[i * R:(i + 1) * R, :])
        for r in range(4):
            for i in range(S // 4 // R):
                blk = src[pl.ds(r + 4 * i * R, R, stride=4), :]
                row0 = r * (S // 4) + i * R
                x4[row0:row0 + R, :] = blk
                put(src, 1, row0, blk)
        n16 = S // 16
        for c4 in range(4):
            for a in range(4):
                put(src, 2, (c4 * 4 + a) * n16, x4[pl.ds(c4 * (S // 4) + a, n16, stride=4), :])

    lane = lax.broadcasted_iota(jnp.int32, (QBLOCK, LANES), 1)
    head_a = lane < HEAD_DIM
    row_a = lax.broadcasted_iota(jnp.int32, (QBLOCK, LANES), 0) < HEAD_DIM
    nt = (((1,), (1,)), ((), ()))

    def branch(di, dil, o_ref, l_ref):
        per_class = (S // dil) // QBLOCK

        R = range(ATTN_GROUP)
        zero = jnp.zeros((QBLOCK, LANES), BF16)

        def head_block(j):
            return per_class <= ATTN_GROUP and j % per_class == 0

        assert per_class % ATTN_GROUP == 0 or ATTN_GROUP % per_class == 0

        def scores(g, js):
            out = {}
            for j in js:
                i = g * ATTN_GROUP + j
                base = pl.multiple_of(i * QBLOCK, QBLOCK)
                q = qs[di, pl.ds(base, QBLOCK), :]
                q2 = jnp.concatenate([jnp.where(head_a, q, zero), jnp.where(head_a, zero, q)], axis=0)
                if head_block(j):
                    kk = ks[di, pl.ds(base + QBLOCK, QBLOCK), :]
                else:
                    kk = ks[di, pl.ds(base, 2 * QBLOCK), :]
                out[j] = lax.dot_general(kk, q2, nt, preferred_element_type=F32)
            return out

        def softmax(g, slot, st):
            for j in st:
                i = g * ATTN_GROUP + j
                if head_block(j):
                    s = st[j] + bias_ref[di, 1, QBLOCK:, :]
                else:
                    s = st[j] + bias_ref[di, jnp.where(i % per_class == 0, 1, 0)]
                m = jnp.max(s, axis=0, keepdims=True)
                p = jnp.exp2(s - m)
                if head_block(j):
                    p_scr[slot, j, QBLOCK:, :] = p.astype(BF16)
                else:
                    p_scr[slot, j] = p.astype(BF16)
                st_scr[slot, j, 0:1, :] = m
                st_scr[slot, j, 1:2, :] = jnp.sum(p, axis=0, keepdims=True)

        def weighted_values(g, slot, js):
            out = {}
            for j in js:
                i = g * ATTN_GROUP + j
                if head_block(j):
                    ot = jnp.dot(vts[di, i + 1], p_scr[slot, j, QBLOCK:, :], preferred_element_type=F32)
                else:
                    vt = jnp.concatenate([vts[di, i], vts[di, i + 1]], axis=1)
                    ot = jnp.dot(vt, p_scr[slot, j], preferred_element_type=F32)
                out[j] = (ot, st_scr[slot, j, 0:1, :], st_scr[slot, j, 1:2, :])
            return out

        def outputs(g, pv):
            for j in pv:
                i = g * ATTN_GROUP + j
                base = pl.multiple_of(i * QBLOCK, QBLOCK)
                ot, m, l = pv[j]
                otn = ot / l
                lse = m + jnp.log2(l)
                o = jnp.where(row_a, otn[:, 0:QBLOCK], otn[:, QBLOCK:]).T
                ls = jnp.where(row_a, jnp.broadcast_to(lse[:, 0:QBLOCK], (QBLOCK, LANES)),
                               jnp.broadcast_to(lse[:, QBLOCK:], (QBLOCK, LANES))).T
                if dil == 1:
                    o_ref[pl.ds(base, QBLOCK), :] = o
                    l_ref[pl.ds(base, QBLOCK), :] = ls
                else:
                    c, n = i // per_class, i % per_class
                    res = c if dil == 4 else 4 * (c % 4) + c // 4
                    start = dil * QBLOCK * n + res
                    o_ref[pl.ds(start, QBLOCK, stride=dil), :] = o
                    l_ref[pl.ds(start, QBLOCK, stride=dil), :] = ls

        ngroup = nblk // ATTN_GROUP
        softmax(0, 0, scores(0, R))

        def body(g, carry):
            slot = g % 2
            st = scores(g, R)
            pv = weighted_values(g - 1, 1 - slot, R)
            softmax(g, slot, st)
            outputs(g - 1, pv)
            return carry

        lax.fori_loop(1, ngroup, body, 0)
        outputs(ngroup - 1, weighted_values(ngroup - 1, (ngroup - 1) % 2, R))

    branch(0, 1, o1, l1)
    branch(1, 4, o4, l4)
    branch(2, 16, o16, l16)

    T = ATTN_MIX_ROWS

    def mix(i, carry):
        r0 = pl.multiple_of(i * T, T)
        la, lb, lc = l1[pl.ds(r0, T), :], l4[pl.ds(r0, T), :], l16[pl.ds(r0, T), :]
        mx = jnp.maximum(jnp.maximum(la, lb), lc)
        ea, eb, ec = jnp.exp2(la - mx), jnp.exp2(lb - mx), jnp.exp2(lc - mx)
        num = ea * o1[pl.ds(r0, T), :] + eb * o4[pl.ds(r0, T), :] + ec * o16[pl.ds(r0, T), :]
        out_ref[pl.ds(r0, T), :] = (num / (ea + eb + ec)).astype(out_ref.dtype)
        return carry

    lax.fori_loop(0, S // T, mix, 0)


def _attn_prompt(z3, bias):
    B, S, _ = z3.shape
    assert S % (16 * QBLOCK) == 0
    npair = N_ATTN_HEADS // 2
    col = lambda c0: pl.BlockSpec((None, S, LANES), lambda b, p: (b, 0, c0 // LANES + p))
    f32s = lambda: pltpu.VMEM((S, LANES), F32)
    return pl.pallas_call(
        functools.partial(_attn_prompt_kernel, seq=S),
        out_shape=jax.ShapeDtypeStruct((B, S, ATTN_WIDTH), BF16),
        grid=(B, npair),
        in_specs=[col(COL_AQ), col(COL_AK), col(COL_AV),
                  pl.BlockSpec((3, None, 2, 2 * QBLOCK, 2 * QBLOCK), lambda b, p: (0, p, 0, 0, 0))],
        out_specs=pl.BlockSpec((None, S, LANES), lambda b, p: (b, 0, p)),
        scratch_shapes=[f32s(),
                        pltpu.VMEM((3, S, LANES), BF16),
                        pltpu.VMEM((3, S + QBLOCK, LANES), BF16),
                        pltpu.VMEM((3, S // QBLOCK + 1, LANES, QBLOCK), BF16),
                        f32s(), f32s(), f32s(), f32s(), f32s(), f32s(),
                        pltpu.VMEM((2, ATTN_GROUP, 2 * QBLOCK, 2 * QBLOCK), BF16),
                        pltpu.VMEM((2, ATTN_GROUP, SUBLANES, 2 * QBLOCK), F32)],
        compiler_params=_cparams(("parallel", "parallel")),
        name="attn_prompt",
    )(z3, z3, z3, bias)


TPAD = SUBLANES
TAIL = 512


def _sample_tables(bias_dist, n_past, n_tok):
    assert n_past >= max(w for w, _ in DILATED_PATTERNS) and n_tok <= TPAD and TAIL >= DILATED_PATTERNS[1][0]
    H = N_ATTN_HEADS
    t = np.arange(TPAD)[:, None]

    def by_row(width):
        rows = [_dist_slice(bias_dist, tt + 1, tt + width) for tt in range(TPAD)]
        return jnp.stack(rows, axis=1).reshape(H * TPAD, width)

    bias_tail, bias_full = by_row(TAIL), by_row(n_past)
    zero_dist = _dist_slice(bias_dist, 0, 0)
    new_rows = [jnp.concatenate([_dist_slice(bias_dist, 0, tt),
                                 jnp.broadcast_to(zero_dist, (H, TPAD - tt - 1))], axis=1) for tt in range(TPAD)]
    bias_new = jnp.stack(new_rows, axis=1).reshape(H * TPAD, TPAD)

    def valid(dj, dil, nsub, lo):
        ok = (dj % dil == 0) & (dj // dil >= lo) & (dj // dil <= nsub) & (t < n_tok)
        return np.tile(ok, (H, 1)).astype(np.float32)

    tn = np.arange(TPAD)[None, :]
    m_tail, m_new = [], []
    for window, dil in DILATED_PATTERNS:
        nsub = window // dil
        if window <= TAIL:
            m_tail.append(valid(TAIL + t - np.arange(TAIL)[None, :], dil, nsub, 1))
        else:
            m_full = valid(n_past + t - np.arange(n_past)[None, :], dil, nsub, 1)
        m_new.append(valid(t - tn, dil, nsub, 0) * (tn < n_tok))
    return (bias_tail, bias_full, bias_new, jnp.asarray(np.stack(m_tail)), jnp.asarray(m_full),
            jnp.asarray(np.stack(m_new).astype(np.float32)))


def _attn_sample_kernel(q_ref, kn_ref, vn_ref, kt_ref, vt_ref, bt_ref, bf_ref, bn_ref, mt_ref, mf_ref, mn_ref,
                        out_ref, *, n_past):
    H = N_ATTN_HEADS
    rows = H * TPAD
    same_head = (_iota2((rows, ATTN_WIDTH), 0) >> _log2(TPAD)) == (_iota2((rows, ATTN_WIDTH), 1) >> HEAD_SHIFT)
    q = q_ref[...] * (HEAD_DIM ** -0.5)
    qm = jnp.where(same_head, jnp.concatenate([q] * H, axis=0), 0.0).astype(BF16)
    nt = (((1,), (1,)), ((), ()))
    kn = kn_ref[...].astype(BF16)
    vn = vn_ref[...].astype(BF16)
    kt = kt_ref[...].astype(BF16)
    vt = vt_ref[...].astype(BF16)
    s_new = lax.dot_general(qm, kn, nt, preferred_element_type=F32)
    s_full = jnp.dot(qm, kt, preferred_element_type=F32)
    s_tail = s_full[:, n_past - TAIL:]

    outs, lses = [], []
    for di, (window, _) in enumerate(DILATED_PATTERNS):
        if window <= TAIL:
            sm = jnp.where(mt_ref[di] > 0.5, s_tail + bt_ref[...], NEG_BIG)
            vmain = vt[:, n_past - TAIL:]
        else:
            sm = jnp.where(mf_ref[...] > 0.5, s_full + bf_ref[...], NEG_BIG)
            vmain = vt
        sn = jnp.where(mn_ref[di] > 0.5, s_new + bn_ref[...], NEG_BIG)
        m = jnp.maximum(jnp.max(sm, axis=-1, keepdims=True), jnp.max(sn, axis=-1, keepdims=True))
        pm = jnp.exp(sm - m)
        pn = jnp.exp(sn - m)
        l = jnp.sum(pm, axis=-1, keepdims=True) + jnp.sum(pn, axis=-1, keepdims=True)
        o = jnp.dot(pn.astype(BF16), vn, preferred_element_type=F32)
        o = o + lax.dot_general(pm.astype(BF16), vmain, nt, preferred_element_type=F32)
        outs.append(o / l)
        lses.append(m + jnp.log(l))
    mx = jnp.maximum(jnp.maximum(lses[0], lses[1]), lses[2])
    es = [jnp.exp(ls - mx) for ls in lses]
    mixed = (es[0] * outs[0] + es[1] * outs[1] + es[2] * outs[2]) / (es[0] + es[1] + es[2])
    mixed = jnp.where(same_head, mixed, 0.0)
    acc = mixed[0:TPAD]
    for h in range(1, H):
        acc = acc + mixed[h * TPAD:(h + 1) * TPAD]
    out_ref[...] = acc.astype(out_ref.dtype)


def _attn_sample(z3, cache_kt, cache_vt, layer, tables, host_grid=None):
    B, T, _ = z3.shape
    assert T == TPAD
    n_past = cache_kt.shape[-1]
    seq, _ = _host_maps(host_grid)
    new = lambda c0: pl.BlockSpec((None, TPAD, ATTN_WIDTH), lambda b, t: (seq(b, t), 0, c0 // ATTN_WIDTH))
    buf = pl.BlockSpec((None, None, ATTN_WIDTH, n_past), lambda b, t: (layer, seq(b, t), 0, 0))
    return _CallSpec(
        body=functools.partial(_attn_sample_kernel, n_past=n_past),
        grid=host_grid or (B, 1),
        args=[z3, z3, z3, cache_kt, cache_vt, *tables],
        in_specs=[new(COL_AQ), new(COL_AK), new(COL_AV), buf, buf] + [_const_spec(t.shape) for t in tables],
        out_shape=[jax.ShapeDtypeStruct((B, TPAD, ATTN_WIDTH), BF16)],
        out_specs=[pl.BlockSpec((None, TPAD, ATTN_WIDTH), lambda b, t: (seq(b, t), 0, 0))],
        scratch=[])


QUAD = N_MLSTM_HEADS * HEAD_DIM
CHUNK_UNROLL = 2
CHUNK_GROUP = 8
HEAD_SHIFT = 6


def _iota2(shape, axis):
    return lax.broadcasted_iota(jnp.int32, shape, axis)


def _log2(n):
    k = int(n).bit_length() - 1
    assert 1 << k == n
    return k


def _seg_mask(rows, row_shift, cols, col_shift):
    return (_iota2((rows, cols), 0) >> row_shift) == (_iota2((rows, cols), 1) >> col_shift)


def _cumulate_rows(x, length, op, fill):
    row = _iota2(x.shape, 0) & (length - 1)
    sh = 1
    while sh < length:
        x = op(x, jnp.where(row >= sh, pltpu.roll(x, sh, axis=0), fill))
        sh *= 2
    return x


def _split2(x):
    hi = x.astype(BF16)
    lo = (x - hi.astype(F32)).astype(BF16)
    return hi, lo


def _dot2(x, w):
    hi, lo = _split2(x)
    return jnp.dot(hi, w, preferred_element_type=F32) + jnp.dot(lo, w, preferred_element_type=F32)


def _dot2r(w, x):
    hi, lo = _split2(x)
    return jnp.dot(w, hi, preferred_element_type=F32) + jnp.dot(w, lo, preferred_element_type=F32)


def _tile_position(single_tile):
    if single_tile:
        return 0, True
    return pl.program_id(1), pl.program_id(1) == pl.num_programs(1) - 1


def _when(cond):
    if isinstance(cond, bool):
        return (lambda f: f()) if cond else (lambda f: None)
    return pl.when(cond)


def _host_maps(host_grid):
    if host_grid is None:
        return (lambda b, t: b), (lambda b, t: t)
    return (lambda b, t: b * host_grid[1] + t), (lambda b, t: 0)


def _load_block_diag(ref, seg64):
    flat = ref[...].reshape(QUAD, HEAD_DIM)
    return jnp.where(seg64, jnp.concatenate([flat] * (QUAD // HEAD_DIM), axis=1), 0.0)


def _store_block_diag(ref, mat):
    for h in range(QUAD // HEAD_DIM):
        ref[h] = mat[h * HEAD_DIM:(h + 1) * HEAD_DIM, h * HEAD_DIM:(h + 1) * HEAD_DIM]


def _head_rmsnorm_gate(h, seg64b, g_row, gate_pre):
    ms = _dot2(h * h, seg64b) * (1.0 / HEAD_DIM)
    return jax.nn.sigmoid(gate_pre) * (h * lax.rsqrt(ms + EPS) * g_row)


def _mlstm_kernel(qk_ref, v_ref, o_ref, gate_ref, conv0_ref, c0_ref, n0_ref, m0_ref, gb_ref, cw_ref, ng_ref,
                  out_ref, c_out, n_out, m_out, xp, cs, ns, ms, *, tile, chunk, n_valid, single_tile=False,
                  guests=None):
    TS, L = tile, chunk
    t, last = _tile_position(single_tile)
    PAD = SUBLANES

    @_when(t == 0)
    def _():
        cs[...] = _load_block_diag(c0_ref, _seg_mask(QUAD, HEAD_SHIFT, QUAD, HEAD_SHIFT))
        ns[...] = n0_ref[...]
        ms[...] = m0_ref[...]
        xp[0:PAD, :] = conv0_ref[...]

    @_when(t > 0)
    def _():
        xp[0:PAD, :] = xp[TS:TS + PAD, :]

    if guests is not None:
        guests()
    xp[PAD:PAD + TS, :] = qk_ref[...]

    LK = MLSTM_CHUNK
    seg64 = _seg_mask(QUAD, HEAD_SHIFT, QUAD, HEAD_SHIFT)
    seg64b = seg64.astype(BF16)
    dmask = (_iota2((LK, QUAD), 1) & (LK - 1)) == _iota2((LK, QUAD), 0)
    causal = (_iota2((L, QUAD), 1) & (LK - 1)) <= _iota2((L, QUAD), 0)
    tril = (_iota2((L, L), 1) <= _iota2((L, L), 0)).astype(BF16)
    ones_lk = jnp.ones((L, LK), BF16)
    row = _iota2((L, QUAD), 0)
    cw = cw_ref[...]
    gb = gb_ref[...]
    ng = ng_ref[...]

    first_half = _iota2((L, LANES), 1) < HEAD_DIM

    def per_head_lanes(g, lane0):
        col = [jnp.broadcast_to(g[:, lane0 + h:lane0 + h + 1], (L, LANES)) for h in range(N_MLSTM_HEADS)]
        return jnp.concatenate([jnp.where(first_half, col[0], col[1]), jnp.where(first_half, col[2], col[3])], axis=1)

    def key_rows(x, fill):
        if L == LK:
            return x
        return jnp.concatenate([x, jnp.full((LK - L, QUAD), fill, x.dtype)], axis=0)

    G = min(CHUNK_GROUP, TS // L)
    assert TS % (L * G) == 0
    nt = (((1,), (1,)), ((), ()))
    tn = (((0,), (0,)), ((), ()))
    zb = jnp.zeros((QUAD, QUAD), BF16)

    def group_body(gi, carry):
        R = range(G)
        r0 = [pl.multiple_of((gi * G + j) * L, L) for j in R]
        q, k, v, ig, lf = [], [], [], [], []
        for j in R:
            win = xp[pl.ds(r0[j], L + PAD), :]
            acc = win[PAD:PAD + L] * cw[CONV_WIDTH - 1:CONV_WIDTH, :]
            for s in range(1, CONV_WIDTH):
                acc = acc + win[PAD - s:PAD - s + L] * cw[CONV_WIDTH - 1 - s:CONV_WIDTH - s, :]
            qk = acc * jax.nn.sigmoid(acc)
            q.append(qk[:, 0:QUAD])
            k.append(qk[:, QUAD:] * (HEAD_DIM ** -0.5))
            v.append(v_ref[pl.ds(r0[j], L), :])
            gate = gate_ref[pl.ds(r0[j], L), :] + gb
            logsig = jnp.minimum(gate, 0.0) - jnp.log(1.0 + jnp.exp(-jnp.abs(gate)))
            ig_j = per_head_lanes(gate, 0)
            lf_j = per_head_lanes(logsig, N_MLSTM_HEADS)
            if n_valid < TS:
                ok = (row + r0[j]) < n_valid
                ig_j = jnp.where(ok, ig_j, NEG_BIG)
                lf_j = jnp.where(ok, lf_j, 0.0)
            ig.append(ig_j)
            lf.append(lf_j)
        b = [_dot2r(tril, lf[j]) for j in R]
        a = [ig[j] - b[j] for j in R]
        cm = [_cumulate_rows(a[j], L, jnp.maximum, -jnp.inf) for j in R]
        arow = [_dot2r(ones_lk, jnp.where(dmask, key_rows(a[j], NEG_BIG), 0.0)) for j in R]
        qb = [q[j].astype(BF16) for j in R]
        vb = [v[j].astype(BF16) for j in R]
        kbd = [jnp.where(seg64, jnp.concatenate([key_rows(k[j], 0.0).astype(BF16)] * N_MLSTM_HEADS, axis=0), zb) for j in R]
        vbd = [jnp.where(seg64, jnp.concatenate([key_rows(v[j], 0.0).astype(BF16)] * N_MLSTM_HEADS, axis=0), zb) for j in R]
        qkt = [lax.dot_general(qb[j], kbd[j], nt, preferred_element_type=F32) for j in R]

        mprev = ms[...]
        M, gg, emt = [], [], []
        for j in R:
            M.append(jnp.maximum(cm[j], mprev))
            mt = b[j] + M[j]
            gg.append(jnp.exp(mprev - M[j]))
            emt.append(jnp.exp(-mt))
            mprev = mt[L - 1:L, :]
        ms[...] = mprev

        wts = [jnp.where(causal, jnp.exp(arow[j] - M[j]), 0.0) * qkt[j] for j in R]
        kd = [k[j] * jnp.exp(a[j] - M[j][L - 1:L, :]) for j in R]
        num = [jnp.dot(wts[j].astype(BF16), vbd[j], preferred_element_type=F32) for j in R]
        u = [lax.dot_general(kd[j].astype(BF16), vb[j], tn, preferred_element_type=F32) for j in R]

        cmat, nvec = [cs[...]], [ns[...]]
        for j in R:
            gl = gg[j][L - 1:L, :]
            cmat.append(gl * cmat[j] + jnp.where(seg64, u[j], 0.0))
            nvec.append(gl * nvec[j] + jnp.sum(kd[j], axis=0, keepdims=True))
        cs[...] = cmat[G]
        ns[...] = nvec[G]

        inter = [jnp.dot(qb[j], cmat[j].astype(BF16), preferred_element_type=F32) for j in R]
        den = [_dot2(wts[j] + gg[j] * (q[j] * nvec[j]), seg64b) for j in R]
        h = [(num[j] + gg[j] * inter[j]) / jnp.maximum(jnp.abs(den[j]), emt[j]) for j in R]
        msq = [_dot2(h[j] * h[j], seg64b) * (1.0 / HEAD_DIM) for j in R]
        for j in R:
            y = h[j] * lax.rsqrt(msq[j] + EPS) * ng
            out_ref[pl.ds(r0[j], L), :] = (jax.nn.sigmoid(o_ref[pl.ds(r0[j], L), :]) * y).astype(out_ref.dtype)
        return carry

    lax.fori_loop(0, TS // (L * G), group_body, 0)

    @_when(last)
    def _():
        _store_block_diag(c_out, cs[...])
        n_out[...] = ns[...]
        m_out[...] = ms[...]


def _mlstm(z3, conv0, c0bd, n0, m0, gate_b, conv_w, norm_g, n_valid, chunk, host_grid=None):
    B, S, _ = z3.shape
    ts = min(TOKEN_TILE, S)
    assert S % ts == 0 and ts % chunk == 0 and (host_grid is None or S == ts)
    seq, til = _host_maps(host_grid)
    blk = lambda w, c0: pl.BlockSpec((None, ts, w), lambda b, t: (seq(b, t), til(b, t), c0 // w))
    per_b = lambda r, w: pl.BlockSpec((None, r, w), lambda b, t: (seq(b, t), 0, 0))
    per_head = pl.BlockSpec((None, N_MLSTM_HEADS, HEAD_DIM, HEAD_DIM), lambda b, t: (seq(b, t), 0, 0, 0))
    return _CallSpec(
        body=functools.partial(_mlstm_kernel, tile=ts, chunk=chunk, n_valid=n_valid, single_tile=host_grid is not None),
        grid=host_grid or (B, S // ts),
        args=[z3, z3, z3, z3, conv0, c0bd, n0, m0, gate_b, conv_w, norm_g],
        in_specs=[blk(2 * QUAD, COL_MQK), blk(QUAD, COL_MV), blk(QUAD, COL_MO), blk(LANES, COL_GATE),
                  per_b(SUBLANES, 2 * QUAD), per_head, per_b(1, QUAD), per_b(1, QUAD),
                  _const_spec((1, LANES)), _const_spec((CONV_WIDTH, 2 * QUAD)), _const_spec((1, QUAD))],
        out_shape=[jax.ShapeDtypeStruct((B, S, QUAD), BF16),
                   jax.ShapeDtypeStruct((B, N_MLSTM_HEADS, HEAD_DIM, HEAD_DIM), F32),
                   jax.ShapeDtypeStruct((B, 1, QUAD), F32),
                   jax.ShapeDtypeStruct((B, 1, QUAD), F32)],
        out_specs=[blk(QUAD, 0), per_head, per_b(1, QUAD), per_b(1, QUAD)],
        scratch=[pltpu.VMEM((ts + 2 * SUBLANES, 2 * QUAD), F32),
                 pltpu.VMEM((QUAD, QUAD), F32), pltpu.VMEM((1, QUAD), F32), pltpu.VMEM((1, QUAD), F32)])


HGRN_FAST_CHUNK = 64
HGRN_GROUP = 8
HGRN_SAFE_DECAY = 80.0


def _hgrn_kernel(q_ref, f_ref, i_ref, g_ref, s0_ref, lb_ref, ng_ref, out_ref, s_out, st, qs, fs, ks, *,
                 tile, sub, n_valid, fast, single_tile=False):
    TS = tile
    t, last = _tile_position(single_tile)

    seg64 = _seg_mask(QUAD, HEAD_SHIFT, QUAD, HEAD_SHIFT)
    seg64b = seg64.astype(BF16)

    @_when(t == 0)
    def _():
        st[...] = _load_block_diag(s0_ref, seg64)
    lb = lb_ref[...]
    ng = ng_ref[...]
    nt = (((1,), (1,)), ((), ()))
    tn = (((0,), (0,)), ((), ()))

    P = HGRN_FAST_CHUNK if fast else sub
    prow = _iota2((P, QUAD), 0)

    def prep(c, worst):
        r0 = pl.multiple_of(c * P, P)
        hq = q_ref[pl.ds(r0, P), :]
        f = lb + (1.0 - lb) * jax.nn.sigmoid(f_ref[pl.ds(r0, P), :])
        logf = jnp.log(f)
        kk = 1.0 - f
        if n_valid < TS:
            ok = (prow + r0) < n_valid
            logf = jnp.where(ok, logf, 0.0)
            kk = jnp.where(ok, kk, 0.0)
        qs[pl.ds(r0, P), :] = hq * jax.nn.sigmoid(hq)
        fs[pl.ds(r0, P), :] = logf
        ks[pl.ds(r0, P), :] = kk
        return jnp.minimum(worst, jnp.sum(logf, axis=0, keepdims=True))

    worst = lax.fori_loop(0, TS // P, prep, jnp.zeros((1, QUAD), F32))

    def finish(o, r0, rows, smat, b, kk, iv, qt):
        o = o + lax.dot_general(qt, smat.astype(BF16), nt, preferred_element_type=F32)
        out_ref[pl.ds(r0, rows), :] = _head_rmsnorm_gate(o, seg64b, ng, g_ref[pl.ds(r0, rows), :]).astype(out_ref.dtype)
        bl = b[rows - 1:rows, :]
        ktil = kk * jnp.exp(bl - b)
        u = lax.dot_general(iv.astype(BF16), ktil.astype(BF16), tn, preferred_element_type=F32)
        st[...] = smat * jnp.exp(bl) + jnp.where(seg64, u, 0.0)

    def fast_loop():
        L = HGRN_FAST_CHUNK
        tril = (_iota2((L, L), 1) <= _iota2((L, L), 0)).astype(BF16)
        causal = (_iota2((L, QUAD), 1) & (L - 1)) <= _iota2((L, QUAD), 0)
        zb = jnp.zeros((QUAD, QUAD), BF16)

        G = min(HGRN_GROUP, TS // L)
        assert TS % (L * G) == 0

        def body(gi, carry):
            R = range(G)
            r0 = [pl.multiple_of((gi * G + j) * L, L) for j in R]
            kk = [ks[pl.ds(r0[j], L), :] for j in R]
            ivb = [i_ref[pl.ds(r0[j], L), :].astype(BF16) for j in R]
            b = [_dot2r(tril, fs[pl.ds(r0[j], L), :]) for j in R]
            qt = [(qs[pl.ds(r0[j], L), :] * jnp.exp(b[j])).astype(BF16) for j in R]
            kbd = [jnp.where(seg64, jnp.concatenate([(kk[j] * jnp.exp(-b[j])).astype(BF16)] * N_HGRN_HEADS, axis=0), zb)
                   for j in R]
            ibd = [jnp.where(seg64, jnp.concatenate([ivb[j]] * N_HGRN_HEADS, axis=0), zb) for j in R]
            amat = [lax.dot_general(qt[j], kbd[j], nt, preferred_element_type=F32) for j in R]
            ktil = [(kk[j] * jnp.exp(b[j][L - 1:L, :] - b[j])).astype(BF16) for j in R]
            u = [lax.dot_general(ivb[j], ktil[j], tn, preferred_element_type=F32) for j in R]
            o = [jnp.dot(jnp.where(causal, amat[j], 0.0).astype(BF16), ibd[j], preferred_element_type=F32) for j in R]
            smat = [st[...]]
            for j in R:
                smat.append(smat[j] * jnp.exp(b[j][L - 1:L, :]) + jnp.where(seg64, u[j], 0.0))
            st[...] = smat[G]
            inter = [lax.dot_general(qt[j], smat[j].astype(BF16), nt, preferred_element_type=F32) for j in R]
            o = [o[j] + inter[j] for j in R]
            msq = [_dot2(o[j] * o[j], seg64b) * (1.0 / HEAD_DIM) for j in R]
            for j in R:
                y = o[j] * lax.rsqrt(msq[j] + EPS) * ng
                out_ref[pl.ds(r0[j], L), :] = (jax.nn.sigmoid(g_ref[pl.ds(r0[j], L), :]) * y).astype(out_ref.dtype)
            return carry

        lax.fori_loop(0, TS // (L * G), body, 0)

    def exact_loop():
        L = sub
        row = _iota2((L, QUAD), 0)

        def body(c, carry):
            r0 = pl.multiple_of(c * L, L)
            q = qs[pl.ds(r0, L), :]
            kk = ks[pl.ds(r0, L), :]
            iv = i_ref[pl.ds(r0, L), :]
            b = _cumulate_rows(fs[pl.ds(r0, L), :], L, jnp.add, 0.0)
            parts = []
            for j in range(L):
                dec = jnp.exp(jnp.where(row >= j, b - b[j:j + 1, :], NEG_BIG))
                parts.append(dec * q * kk[j:j + 1, :])
            tstack = jnp.concatenate(parts, axis=0).astype(BF16)
            y = jnp.dot(tstack, seg64b, preferred_element_type=F32)
            o = y[0:L] * iv[0:1, :]
            for j in range(1, L):
                o = o + y[j * L:(j + 1) * L] * iv[j:j + 1, :]
            finish(o, r0, L, st[...], b, kk, iv, (q * jnp.exp(b)).astype(BF16))
            return carry

        lax.fori_loop(0, TS // L, body, 0, unroll=min(CHUNK_UNROLL, TS // L))

    if fast:
        safe = jnp.min(worst) > -HGRN_SAFE_DECAY
        pl.when(safe)(fast_loop)
        pl.when(jnp.logical_not(safe))(exact_loop)
    else:
        exact_loop()

    @_when(last)
    def _():
        _store_block_diag(s_out, st[...])


def _hgrn(z3, s0t, lb, norm_g, n_valid, sub, host_grid=None):
    B, S, _ = z3.shape
    ts = min(TOKEN_TILE, S)
    fast = ts % HGRN_FAST_CHUNK == 0
    assert S % ts == 0 and ts % sub == 0 and (host_grid is None or S == ts)
    seq, til = _host_maps(host_grid)
    blk = lambda c0: pl.BlockSpec((None, ts, QUAD), lambda b, t: (seq(b, t), til(b, t), c0 // QUAD))
    per_b = pl.BlockSpec((None, N_HGRN_HEADS, HEAD_DIM, HEAD_DIM), lambda b, t: (seq(b, t), 0, 0, 0))
    stage = lambda: pltpu.VMEM((ts, QUAD), F32)
    return _CallSpec(
        body=functools.partial(_hgrn_kernel, tile=ts, sub=sub, n_valid=n_valid, fast=fast,
                               single_tile=host_grid is not None),
        grid=host_grid or (B, S // ts),
        args=[z3, z3, z3, z3, s0t, lb, norm_g],
        in_specs=[blk(COL_HQ), blk(COL_HF), blk(COL_HI), blk(COL_HG), per_b,
                  _const_spec((1, QUAD)), _const_spec((1, QUAD))],
        out_shape=[jax.ShapeDtypeStruct((B, S, QUAD), BF16),
                   jax.ShapeDtypeStruct((B, N_HGRN_HEADS, HEAD_DIM, HEAD_DIM), F32)],
        out_specs=[blk(0), per_b],
        scratch=[pltpu.VMEM((QUAD, QUAD), F32), stage(), stage(), stage()])


PROJ_SPLITS = (ATTN_WIDTH, ATTN_WIDTH, ATTN_WIDTH,
               MLSTM_WIDTH, MLSTM_WIDTH, MLSTM_WIDTH, MLSTM_WIDTH, N_MLSTM_HEADS, N_MLSTM_HEADS,
               HGRN_WIDTH, HGRN_WIDTH, HGRN_WIDTH, HGRN_WIDTH)


def _prep_w_in(w):
    g0 = int(np.sum(PROJ_SPLITS[:7]))
    g1 = g0 + 2 * N_MLSTM_HEADS
    assert (g0, w.shape[-1] - g1) == (COL_HQ, COL_GATE - COL_HQ)
    wb = w.astype(BF16)
    pad = jnp.zeros(w.shape[:-1] + (LANES - 2 * N_MLSTM_HEADS,), BF16)
    return wb, jnp.concatenate([wb[..., g1:], wb[..., g0:g1], pad], axis=-1)


HGRN_SUBCHUNK = 16


class _LayerWeights(NamedTuple):
    layer: int
    g_mix: jax.Array
    w_in: jax.Array
    gate_b: jax.Array
    conv_w: jax.Array
    mlstm_g: jax.Array
    lb: jax.Array
    hgrn_g: jax.Array
    w_out: jax.Array
    g_ffn: jax.Array
    w_up: jax.Array
    w_down: jax.Array
    g_final: jax.Array


def _gate_row(gate_b):
    flat = gate_b.astype(F32).reshape(1, 2 * N_MLSTM_HEADS)
    return jnp.concatenate([flat, jnp.zeros((1, LANES - 2 * N_MLSTM_HEADS), F32)], axis=1)


class _LayerFront(NamedTuple):
    z3: jax.Array
    kv_t: tuple
    mlstm: _CallSpec
    hgrn: _CallSpec


def _layer_front(x2d, B, S, n_valid, states, w, chunk_m, chunk_h, n_keep=None, stacked=None, host_grid=None):
    conv_buf, C0, n0, m0, S0 = states
    if n_keep is None:
        z, kv_t = _inproj(x2d, w.g_mix, w.w_in, w.layer), None
    else:
        z, kt, vt = _inproj(x2d, w.g_mix, w.w_in, w.layer, S, n_keep, stacked)
        kv_t = (kt, vt)
    z3 = z.reshape(B, S, ZW)
    conv0 = jnp.zeros((B, SUBLANES, 2 * MLSTM_WIDTH), F32).at[:, SUBLANES - (CONV_WIDTH - 1):].set(conv_buf.astype(F32))
    m0r = jnp.repeat(m0.astype(F32), HEAD_DIM, axis=1).reshape(B, 1, MLSTM_WIDTH)
    mlstm = _mlstm(z3, conv0, C0.astype(F32), n0.astype(F32).reshape(B, 1, MLSTM_WIDTH), m0r,
                   w.gate_b, w.conv_w, w.mlstm_g, n_valid, chunk_m, host_grid)
    hgrn = _hgrn(z3, jnp.swapaxes(S0.astype(F32), -1, -2), w.lb, w.hgrn_g, n_valid, chunk_h, host_grid)
    return _LayerFront(z3, kv_t, mlstm, hgrn)


def _layer_back(x2d, front, n_valid, w, final, attn, mlstm_out, hgrn_out):
    B, S, _ = front.z3.shape
    z3 = front.z3
    (ml, c_new, n, m), (hg, st) = mlstm_out, hgrn_out
    n2 = B * S
    y = _outffn(x2d, attn.reshape(n2, ATTN_WIDTH), ml.reshape(n2, MLSTM_WIDTH), hg.reshape(n2, HGRN_WIDTH),
                w.w_out, w.g_ffn, w.w_up, w.w_down, w.g_final, final, w.layer)
    k_rows = z3[:, :n_valid, COL_AK:COL_AK + ATTN_WIDTH].reshape(B, n_valid, N_ATTN_HEADS, HEAD_DIM)
    v_rows = z3[:, :n_valid, COL_AV:COL_AV + ATTN_WIDTH].reshape(B, n_valid, N_ATTN_HEADS, HEAD_DIM)
    conv_new = z3[:, n_valid - (CONV_WIDTH - 1):n_valid, COL_MQK:COL_MQK + 2 * MLSTM_WIDTH]
    s_new = jnp.swapaxes(st, -1, -2)
    return y, (k_rows, v_rows, conv_new, c_new, n.reshape(B, N_MLSTM_HEADS, HEAD_DIM), m[:, 0, ::HEAD_DIM], s_new)


def kernel(x_prompt, x_sample, cache_attn_k, cache_attn_v, state_mlstm_conv, state_mlstm_C, state_mlstm_n, state_mlstm_m, state_hgrn_S, rel_bias, g_mix, w_in, mlstm_gate_b, mlstm_conv_w, mlstm_norm_g, hgrn_lb_raw, hgrn_norm_g, w_out, g_ffn, w_up, w_down, g_final):
    depth = w_in.shape[0]
    bp, sp, _ = x_prompt.shape
    bs, ts, _ = x_sample.shape
    n_keep = min(max(w for w, _ in DILATED_PATTERNS), sp)
    n_past = cache_attn_k.shape[2]
    lb_p = jax.nn.softmax(hgrn_lb_raw.astype(F32), axis=0)
    hgrn_lb = jnp.cumsum(lb_p, axis=0) - lb_p[0]
    bias_dist = _bias_by_distance(rel_bias)
    p_tables = _prompt_tables(bias_dist)
    s_tables = _sample_tables(bias_dist, n_past, ts)
    cache_kt = jnp.transpose(cache_attn_k, (0, 1, 3, 4, 2)).reshape(depth, bs, ATTN_WIDTH, n_past)
    cache_vt = jnp.transpose(cache_attn_v, (0, 1, 3, 4, 2)).reshape(depth, bs, ATTN_WIDTH, n_past)

    hp = x_prompt.reshape(bp * sp, D_MODEL)
    hs = jnp.zeros((bs, TPAD, D_MODEL), F32).at[:, :ts].set(x_sample).reshape(bs * TPAD, D_MODEL)
    zero_states = (jnp.zeros((bp, CONV_WIDTH - 1, 2 * MLSTM_WIDTH), F32),
                   jnp.zeros((bp, N_MLSTM_HEADS, HEAD_DIM, HEAD_DIM), F32),
                   jnp.zeros((bp, N_MLSTM_HEADS, HEAD_DIM), F32),
                   jnp.zeros((bp, N_MLSTM_HEADS), F32),
                   jnp.zeros((bp, N_HGRN_HEADS, HEAD_DIM, HEAD_DIM), F32))
    p_states, s_states = [], []
    kv_t = None
    sem = ("parallel", "arbitrary")
    w_in_b, w_out_b, w_up_b, w_down_b = _prep_w_in(w_in), w_out.astype(BF16), w_up.astype(BF16), w_down.astype(BF16)
    for l in range(depth):
        final = l == depth - 1
        weights = _LayerWeights(
            layer=l, g_mix=g_mix[l][None], w_in=w_in_b,
            gate_b=_gate_row(mlstm_gate_b[l]),
            conv_w=mlstm_conv_w[l], mlstm_g=mlstm_norm_g[l][None], lb=hgrn_lb[l][None], hgrn_g=hgrn_norm_g[l][None],
            w_out=w_out_b, g_ffn=g_ffn[l][None], w_up=w_up_b, w_down=w_down_b, g_final=g_final[None])
        pf = _layer_front(hp, bp, sp, sp, zero_states, weights, MLSTM_CHUNK, HGRN_SUBCHUNK, n_keep, (l, depth, kv_t))
        kv_t = pf.kv_t
        grid = pf.mlstm.grid
        host = grid if bs == grid[0] * grid[1] else None
        states = (state_mlstm_conv[l], state_mlstm_C[l], state_mlstm_n[l], state_mlstm_m[l], state_hgrn_S[l])
        sf = _layer_front(hs, bs, TPAD, ts, states, weights, TPAD, TPAD, host_grid=host)
        s_attn = _attn_sample(sf.z3, cache_kt, cache_vt, l, s_tables, host)
        p_attn = _attn_prompt(pf.z3, p_tables)
        if host is None:
            p_ml, = _run([pf.mlstm], sem, "mlstm")
            (s_attn,), s_ml, s_hg = _run([s_attn, sf.mlstm, sf.hgrn], sem, "sample_mixers")
        else:
            p_ml, (s_attn,), s_ml, s_hg = _run([pf.mlstm, s_attn, sf.mlstm, sf.hgrn], sem, "mlstm_and_sample_mixers",
                                               host=0)
        p_hg, = _run([pf.hgrn], sem, "hgrn")
        hp, st = _layer_back(hp, pf, sp, weights, final, p_attn, p_ml, p_hg)
        p_states.append(st[2:])
        hs, st = _layer_back(hs, sf, ts, weights, final, s_attn, s_ml, s_hg)
        s_states.append(st)
    y_prompt = hp.reshape(bp, sp, D_MODEL)
    y_sample = hs.reshape(bs, TPAD, D_MODEL)[:, :ts]
    to_rows = lambda t: jnp.transpose(t.reshape(depth, bp, N_ATTN_HEADS, HEAD_DIM, n_keep), (0, 1, 4, 2, 3))
    p_out = [to_rows(kv_t[0]), to_rows(kv_t[1])] + [jnp.stack(a) for a in zip(*p_states)]
    s_out = [jnp.stack(a) for a in zip(*s_states)]
    return (y_prompt, y_sample, *p_out, *s_out)
```

```python
import functools
from typing import Callable, NamedTuple

import jax
import jax.numpy as jnp
import numpy as np
from jax import lax
from jax.experimental import pallas as pl
from jax.experimental.pallas import tpu as pltpu

F32 = jnp.float32
BF16 = jnp.bfloat16

D_MODEL = 1024
HEAD_DIM = 64
N_ATTN_HEADS = 8
N_MLSTM_HEADS = 4
N_HGRN_HEADS = 4
ATTN_WIDTH = N_ATTN_HEADS * HEAD_DIM
MLSTM_WIDTH = N_MLSTM_HEADS * HEAD_DIM
HGRN_WIDTH = N_HGRN_HEADS * HEAD_DIM
DILATED_PATTERNS = ((128, 1), (512, 4), (2048, 16))
QBLOCK = 128
N_REL_BUCKETS = 32
REL_MAX_DISTANCE = 2048
CONV_WIDTH = 4
MLSTM_CHUNK = 64
D_FF = 4 * D_MODEL
EPS = 1e-6
NEG_BIG = -1e30
LOG2E = 1.4426950408889634

LANES = 128
SUBLANES = 8

COL_AQ = 0
COL_AK = COL_AQ + ATTN_WIDTH
COL_AV = COL_AK + ATTN_WIDTH
COL_MQK = COL_AV + ATTN_WIDTH
COL_MV = COL_MQK + 2 * MLSTM_WIDTH
COL_MO = COL_MV + MLSTM_WIDTH
COL_HQ = COL_MO + MLSTM_WIDTH
COL_HF = COL_HQ + HGRN_WIDTH
COL_HI = COL_HF + HGRN_WIDTH
COL_HG = COL_HI + HGRN_WIDTH
COL_GATE = COL_HG + HGRN_WIDTH
ZW = COL_GATE + LANES

VMEM_LIMIT = 56 * 1024 * 1024
TOKEN_TILE = 512
ATTN_STAGE_ROWS = 512
ATTN_MIX_ROWS = 512


def _cparams(sem, vmem=VMEM_LIMIT):
    return pltpu.CompilerParams(dimension_semantics=sem, vmem_limit_bytes=vmem)


def _const_spec(shape):
    nd = len(shape)
    return pl.BlockSpec(shape, lambda *_: (0,) * nd, pipeline_mode=pl.Buffered(1))


def _layer_spec(shape, layer):
    nd = len(shape)
    return pl.BlockSpec((None,) + tuple(shape), lambda *_: (layer,) + (0,) * nd, pipeline_mode=pl.Buffered(1))


class _CallSpec(NamedTuple):
    body: Callable
    grid: tuple
    args: list
    in_specs: list
    out_shape: list
    out_specs: list
    scratch: list


def _run(specs, sem, name, host=None):
    grid = specs[0].grid
    assert all(s.grid == grid for s in specs)
    n_in = [len(s.args) for s in specs]
    n_out = [len(s.out_shape) for s in specs]
    n_scr = [len(s.scratch) for s in specs]

    def body(*refs):
        ins, outs, scrs = refs[:sum(n_in)], refs[sum(n_in):sum(n_in) + sum(n_out)], refs[sum(n_in) + sum(n_out):]

        def bound(k, **kw):
            a, b, c = sum(n_in[:k]), sum(n_out[:k]), sum(n_scr[:k])
            return functools.partial(specs[k].body, *ins[a:a + n_in[k]], *outs[b:b + n_out[k]], *scrs[c:c + n_scr[k]], **kw)

        others = [k for k in range(len(specs)) if k != host]
        if host is None:
            for k in others:
                bound(k)()
        else:
            bound(host, guests=lambda: [bound(k)() for k in others])()

    flat = pl.pallas_call(
        body, grid=grid,
        in_specs=[x for s in specs for x in s.in_specs],
        out_shape=[x for s in specs for x in s.out_shape],
        out_specs=[x for s in specs for x in s.out_specs],
        scratch_shapes=[x for s in specs for x in s.scratch],
        compiler_params=_cparams(sem), name=name,
    )(*[x for s in specs for x in s.args])
    out, k = [], 0
    for n in n_out:
        out.append(list(flat[k:k + n]))
        k += n
    return out


def _inproj_kernel(x_ref, g_ref, wa_ref, wb_ref, *rest, tiles_per_seq, first_keep, n_prev=0):
    x = x_ref[...]
    ms = jnp.mean(x * x, axis=-1, keepdims=True)
    xn = ((x * lax.rsqrt(ms + EPS)) * g_ref[...]).astype(BF16)
    z_ref = rest[n_prev]
    z_ref[:, 0:COL_HQ] = jnp.dot(xn, wa_ref[...], preferred_element_type=F32)
    z_ref[:, COL_HQ:] = jnp.dot(xn, wb_ref[...], preferred_element_type=F32)
    if first_keep is None:
        return
    kt_ref, vt_ref = rest[n_prev + 1:]

    kt_ref[...] = z_ref[:, COL_AK:COL_AK + ATTN_WIDTH].T
    vt_ref[...] = z_ref[:, COL_AV:COL_AV + ATTN_WIDTH].T


def _inproj(x2d, g, w, layer, seq=None, n_keep=None, stacked=None):
    n = x2d.shape[0]
    tm = min(TOKEN_TILE, n)
    assert n % tm == 0
    x_spec = pl.BlockSpec((tm, D_MODEL), lambda i: (i, 0))
    z_spec = pl.BlockSpec((tm, ZW), lambda i: (i, 0))
    z_shape = jax.ShapeDtypeStruct((n, ZW), F32)
    w_specs = [_layer_spec((D_MODEL, COL_HQ), layer), _layer_spec((D_MODEL, ZW - COL_HQ), layer)]
    if stacked is None:
        return pl.pallas_call(
            functools.partial(_inproj_kernel, tiles_per_seq=None, first_keep=None),
            out_shape=z_shape, grid=(n // tm,),
            in_specs=[x_spec, _const_spec((1, D_MODEL))] + w_specs,
            out_specs=z_spec, compiler_params=_cparams(("parallel",)), name="inproj",
        )(x2d, g, *w)
    assert seq % tm == 0 and n_keep % tm == 0
    layer, depth, prev = stacked
    tps, first_keep = seq // tm, (seq - n_keep) // tm
    t_spec = pl.BlockSpec((None, None, ATTN_WIDTH, tm),
                          lambda i: (layer, i // tps, 0, jnp.maximum(i % tps - first_keep, 0)))
    t_shape = jax.ShapeDtypeStruct((depth, n // seq, ATTN_WIDTH, n_keep), F32)
    in_specs = [x_spec, _const_spec((1, D_MODEL))] + w_specs
    args = [x2d, g, *w]
    if prev is None:
        prev = (jnp.zeros(t_shape.shape, F32), jnp.zeros(t_shape.shape, F32))
    in_specs += [pl.BlockSpec(memory_space=pl.ANY)] * 2
    aliases = {len(args): 1, len(args) + 1: 2}
    args += list(prev)
    return pl.pallas_call(
        functools.partial(_inproj_kernel, tiles_per_seq=tps, first_keep=first_keep, n_prev=len(prev)),
        out_shape=[z_shape, t_shape, t_shape], grid=(n // tm,),
        in_specs=in_specs, out_specs=[z_spec, t_spec, t_spec], input_output_aliases=aliases,
        compiler_params=_cparams(("arbitrary",)), name="inproj_kt",
    )(*args)


FF_CHUNK = 1024
FFN_ROWS = TOKEN_TILE


def _outffn_kernel(x_ref, a_ref, m_ref, h_ref, wo_ref, gf_ref, wu_ref, wd_ref, gl_ref, y_ref, xn_sc, *, final):
    x1 = x_ref[...]
    x1 = x1 + jnp.dot(a_ref[...], wo_ref[0:ATTN_WIDTH, :], preferred_element_type=F32)
    x1 = x1 + jnp.dot(m_ref[...], wo_ref[ATTN_WIDTH:ATTN_WIDTH + MLSTM_WIDTH, :], preferred_element_type=F32)
    x1 = x1 + jnp.dot(h_ref[...], wo_ref[ATTN_WIDTH + MLSTM_WIDTH:, :], preferred_element_type=F32)
    ms = jnp.mean(x1 * x1, axis=-1, keepdims=True)
    xn_sc[...] = ((x1 * lax.rsqrt(ms + EPS)) * gf_ref[...]).astype(BF16)
    y_ref[...] = x1
    nchunk = D_FF // FF_CHUNK
    up = lambda c: jnp.dot(xn_sc[...], wu_ref[:, c * FF_CHUNK:(c + 1) * FF_CHUNK], preferred_element_type=F32)
    u = up(0)
    for c in range(nchunk):
        u_next = up(c + 1) if c + 1 < nchunk else None
        hh = jnp.square(jnp.maximum(u, 0.0)).astype(BF16)
        y_ref[...] += jnp.dot(hh, wd_ref[c * FF_CHUNK:(c + 1) * FF_CHUNK, :], preferred_element_type=F32)
        u = u_next
    if final:
        x2 = y_ref[...]
        ms2 = jnp.mean(x2 * x2, axis=-1, keepdims=True)
        y_ref[...] = (x2 * lax.rsqrt(ms2 + EPS)) * gl_ref[...]


def _outffn(x2d, attn, ml, hg, w_out, g_ffn, w_up, w_down, g_final, final, layer):
    n = x2d.shape[0]
    tm = min(FFN_ROWS, n)
    assert n % tm == 0
    row = lambda w: pl.BlockSpec((tm, w), lambda i: (i, 0))
    return pl.pallas_call(
        functools.partial(_outffn_kernel, final=final),
        out_shape=jax.ShapeDtypeStruct((n, D_MODEL), F32),
        grid=(n // tm,),
        in_specs=[row(D_MODEL), row(ATTN_WIDTH), row(MLSTM_WIDTH), row(HGRN_WIDTH),
                  _layer_spec((D_MODEL, D_MODEL), layer), _const_spec((1, D_MODEL)),
                  _layer_spec((D_MODEL, D_FF), layer), _layer_spec((D_FF, D_MODEL), layer), _const_spec((1, D_MODEL))],
        out_specs=row(D_MODEL),
        scratch_shapes=[pltpu.VMEM((tm, D_MODEL), BF16)],
        compiler_params=_cparams(("parallel",)),
        name="outffn",
    )(x2d, attn, ml, hg, w_out, g_ffn, w_up, w_down, g_final)


def _t5_causal_bucket(dist):
    n = np.asarray(dist).astype(np.int32)
    max_exact = N_REL_BUCKETS // 2
    scaled = np.log(np.maximum(n, 1) / max_exact) / np.log(REL_MAX_DISTANCE / max_exact)
    large = np.minimum(max_exact + (scaled * (N_REL_BUCKETS - max_exact)).astype(np.int32), N_REL_BUCKETS - 1)
    return np.where(n < max_exact, n, large).astype(np.int32)


BIAS_DIST = 2304


def _bias_by_distance(rel_bias):
    assert BIAS_DIST > max(w for w, _ in DILATED_PATTERNS) + SUBLANES
    return rel_bias.astype(F32)[_t5_causal_bucket(np.arange(BIAS_DIST)[::-1])].T


def _dist_slice(bias_desc, lo, hi, step=1):
    last = BIAS_DIST - 1
    return bias_desc[:, last - hi:last - lo + 1:step]


def _prompt_tables(bias_dist):
    H = N_ATTN_HEADS
    a = np.arange(QBLOCK)[:, None]
    b = np.arange(2 * QBLOCK)[None, :]
    rel = QBLOCK + a - b
    band = (rel >= 0) & (rel <= QBLOCK)
    period = 3 * QBLOCK
    biases = []
    for window, dil in DILATED_PATTERNS:
        nsub = window // dil
        assert nsub == QBLOCK
        vd = _dist_slice(bias_dist, 0, nsub * dil, dil)
        rp = jnp.concatenate([vd, jnp.broadcast_to(vd[:, -1:], (H, QBLOCK)),
                              jnp.broadcast_to(vd[:, 0:1], (H, QBLOCK - 1))], axis=1)
        skew = jnp.tile(rp, (1, QBLOCK))[:, :QBLOCK * (period - 1)].reshape(H, QBLOCK, period - 1)
        biases.append(skew[:, :, :2 * QBLOCK].reshape(H // 2, 2 * QBLOCK, 2 * QBLOCK))
    mask = np.stack([np.tile(band, (2, 1)), np.tile(band & (b >= QBLOCK), (2, 1))])
    table = jnp.where(mask[None, None], jnp.stack(biases)[:, :, None] * LOG2E, -jnp.inf)
    return jnp.swapaxes(table, -1, -2)


ATTN_GROUP = 8


def _attn_prompt_kernel(q_ref, k_ref, v_ref, bias_ref, out_ref,
                        x4, qs, ks, vts, o1, o4, o16, l1, l4, l16, p_scr, st_scr, *, seq):
    S = seq
    nblk = S // QBLOCK
    scale = HEAD_DIM ** -0.5 * LOG2E
    zpad = jnp.zeros((QBLOCK, LANES), BF16)
    for di in range(3):
        ks[di, 0:QBLOCK, :] = zpad
        vts[di, 0] = zpad

    R = ATTN_STAGE_ROWS

    def put(src, di, row0, blk):
        if src is q_ref:
            qs[di, row0:row0 + blk.shape[0], :] = (blk * scale).astype(BF16)
        elif src is k_ref:
            ks[di, QBLOCK + row0:QBLOCK + row0 + blk.shape[0], :] = blk.astype(BF16)
        else:
            blk_t = blk.T.astype(BF16)
            for t in range(blk.shape[0] // QBLOCK):
                vts[di, 1 + row0 // QBLOCK + t] = blk_t[:, t * QBLOCK:(t + 1) * QBLOCK]

    for src in (q_ref, k_ref, v_ref):
        for i in range(S // R):
            put(src, 0, i * R, src[i * R:(i + 1) * R, :])
        for r in range(4):
            for i in range(S // 4 // R):
                blk = src[pl.ds(r + 4 * i * R, R, stride=4), :]
                row0 = r * (S // 4) + i * R
                x4[row0:row0 + R, :] = blk
                put(src, 1, row0, blk)
        n16 = S // 16
        for c4 in range(4):
            for a in range(4):
                put(src, 2, (c4 * 4 + a) * n16, x4[pl.ds(c4 * (S // 4) + a, n16, stride=4), :])

    lane = lax.broadcasted_iota(jnp.int32, (QBLOCK, LANES), 1)
    head_a = lane < HEAD_DIM
    row_a = lax.broadcasted_iota(jnp.int32, (QBLOCK, LANES), 0) < HEAD_DIM
    nt = (((1,), (1,)), ((), ()))

    def branch(di, dil, o_ref, l_ref):
        per_class = (S // dil) // QBLOCK

        R = range(ATTN_GROUP)
        zero = jnp.zeros((QBLOCK, LANES), BF16)

        def head_block(j):
            return per_class <= ATTN_GROUP and j % per_class == 0

        assert per_class % ATTN_GROUP == 0 or ATTN_GROUP % per_class == 0

        def scores(g, js):
            out = {}
            for j in js:
                i = g * ATTN_GROUP + j
                base = pl.multiple_of(i * QBLOCK, QBLOCK)
                q = qs[di, pl.ds(base, QBLOCK), :]
                q2 = jnp.concatenate([jnp.where(head_a, q, zero), jnp.where(head_a, zero, q)], axis=0)
                if head_block(j):
                    kk = ks[di, pl.ds(base + QBLOCK, QBLOCK), :]
                else:
                    kk = ks[di, pl.ds(base, 2 * QBLOCK), :]
                out[j] = lax.dot_general(kk, q2, nt, preferred_element_type=F32)
            return out

        def softmax(g, slot, st):
            for j in st:
                i = g * ATTN_GROUP + j
                if head_block(j):
                    s = st[j] + bias_ref[di, 1, QBLOCK:, :]
                else:
                    s = st[j] + bias_ref[di, jnp.where(i % per_class == 0, 1, 0)]
                m = jnp.max(s, axis=0, keepdims=True)
                p = jnp.exp2(s - m)
                if head_block(j):
                    p_scr[slot, j, QBLOCK:, :] = p.astype(BF16)
                else:
                    p_scr[slot, j] = p.astype(BF16)
                st_scr[slot, j, 0:1, :] = m
                st_scr[slot, j, 1:2, :] = jnp.sum(p, axis=0, keepdims=True)

        def weighted_values(g, slot, js):
            out = {}
            for j in js:
                i = g * ATTN_GROUP + j
                if head_block(j):
                    ot = jnp.dot(vts[di, i + 1], p_scr[slot, j, QBLOCK:, :], preferred_element_type=F32)
                else:
                    vt = jnp.concatenate([vts[di, i], vts[di, i + 1]], axis=1)
                    ot = jnp.dot(vt, p_scr[slot, j], preferred_element_type=F32)
                out[j] = (ot, st_scr[slot, j, 0:1, :], st_scr[slot, j, 1:2, :])
            return out

        def outputs(g, pv):
            for j in pv:
                i = g * ATTN_GROUP + j
                base = pl.multiple_of(i * QBLOCK, QBLOCK)
                ot, m, l = pv[j]
                otn = ot / l
                lse = m + jnp.log2(l)
                o = jnp.where(row_a, otn[:, 0:QBLOCK], otn[:, QBLOCK:]).T
                ls = jnp.where(row_a, jnp.broadcast_to(lse[:, 0:QBLOCK], (QBLOCK, LANES)),
                               jnp.broadcast_to(lse[:, QBLOCK:], (QBLOCK, LANES))).T
                if dil == 1:
                    o_ref[pl.ds(base, QBLOCK), :] = o
                    l_ref[pl.ds(base, QBLOCK), :] = ls
                else:
                    c, n = i // per_class, i % per_class
                    res = c if dil == 4 else 4 * (c % 4) + c // 4
                    start = dil * QBLOCK * n + res
                    o_ref[pl.ds(start, QBLOCK, stride=dil), :] = o
                    l_ref[pl.ds(start, QBLOCK, stride=dil), :] = ls

        ngroup = nblk // ATTN_GROUP
        softmax(0, 0, scores(0, R))

        def body(g, carry):
            slot = g % 2
            st = scores(g, R)
            pv = weighted_values(g - 1, 1 - slot, R)
            softmax(g, slot, st)
            outputs(g - 1, pv)
            return carry

        lax.fori_loop(1, ngroup, body, 0)
        outputs(ngroup - 1, weighted_values(ngroup - 1, (ngroup - 1) % 2, R))

    branch(0, 1, o1, l1)
    branch(1, 4, o4, l4)
    branch(2, 16, o16, l16)

    T = ATTN_MIX_ROWS

    def mix(i, carry):
        r0 = pl.multiple_of(i * T, T)
        la, lb, lc = l1[pl.ds(r0, T), :], l4[pl.ds(r0, T), :], l16[pl.ds(r0, T), :]
        mx = jnp.maximum(jnp.maximum(la, lb), lc)
        ea, eb, ec = jnp.exp2(la - mx), jnp.exp2(lb - mx), jnp.exp2(lc - mx)
        num = ea * o1[pl.ds(r0, T), :] + eb * o4[pl.ds(r0, T), :] + ec * o16[pl.ds(r0, T), :]
        out_ref[pl.ds(r0, T), :] = (num / (ea + eb + ec)).astype(out_ref.dtype)
        return carry

    lax.fori_loop(0, S // T, mix, 0)


def _attn_prompt(z3, bias):
    B, S, _ = z3.shape
    assert S % (16 * QBLOCK) == 0
    npair = N_ATTN_HEADS // 2
    col = lambda c0: pl.BlockSpec((None, S, LANES), lambda b, p: (b, 0, c0 // LANES + p))
    f32s = lambda: pltpu.VMEM((S, LANES), F32)
    return pl.pallas_call(
        functools.partial(_attn_prompt_kernel, seq=S),
        out_shape=jax.ShapeDtypeStruct((B, S, ATTN_WIDTH), BF16),
        grid=(B, npair),
        in_specs=[col(COL_AQ), col(COL_AK), col(COL_AV),
                  pl.BlockSpec((3, None, 2, 2 * QBLOCK, 2 * QBLOCK), lambda b, p: (0, p, 0, 0, 0))],
        out_specs=pl.BlockSpec((None, S, LANES), lambda b, p: (b, 0, p)),
        scratch_shapes=[f32s(),
                        pltpu.VMEM((3, S, LANES), BF16),
                        pltpu.VMEM((3, S + QBLOCK, LANES), BF16),
                        pltpu.VMEM((3, S // QBLOCK + 1, LANES, QBLOCK), BF16),
                        f32s(), f32s(), f32s(), f32s(), f32s(), f32s(),
                        pltpu.VMEM((2, ATTN_GROUP, 2 * QBLOCK, 2 * QBLOCK), BF16),
                        pltpu.VMEM((2, ATTN_GROUP, SUBLANES, 2 * QBLOCK), F32)],
        compiler_params=_cparams(("parallel", "parallel")),
        name="attn_prompt",
    )(z3, z3, z3, bias)


TPAD = SUBLANES
TAIL = 512


def _sample_tables(bias_dist, n_past, n_tok):
    assert n_past >= max(w for w, _ in DILATED_PATTERNS) and n_tok <= TPAD and TAIL >= DILATED_PATTERNS[1][0]
    H = N_ATTN_HEADS
    t = np.arange(TPAD)[:, None]

    def by_row(width):
        rows = [_dist_slice(bias_dist, tt + 1, tt + width) for tt in range(TPAD)]
        return jnp.stack(rows, axis=1).reshape(H * TPAD, width)

    bias_tail, bias_full = by_row(TAIL), by_row(n_past)
    zero_dist = _dist_slice(bias_dist, 0, 0)
    new_rows = [jnp.concatenate([_dist_slice(bias_dist, 0, tt),
                                 jnp.broadcast_to(zero_dist, (H, TPAD - tt - 1))], axis=1) for tt in range(TPAD)]
    bias_new = jnp.stack(new_rows, axis=1).reshape(H * TPAD, TPAD)

    def valid(dj, dil, nsub, lo):
        ok = (dj % dil == 0) & (dj // dil >= lo) & (dj // dil <= nsub) & (t < n_tok)
        return np.tile(ok, (H, 1)).astype(np.float32)

    tn = np.arange(TPAD)[None, :]
    m_tail, m_new = [], []
    for window, dil in DILATED_PATTERNS:
        nsub = window // dil
        if window <= TAIL:
            m_tail.append(valid(TAIL + t - np.arange(TAIL)[None, :], dil, nsub, 1))
        else:
            m_full = valid(n_past + t - np.arange(n_past)[None, :], dil, nsub, 1)
        m_new.append(valid(t - tn, dil, nsub, 0) * (tn < n_tok))
    return (bias_tail, bias_full, bias_new, jnp.asarray(np.stack(m_tail)), jnp.asarray(m_full),
            jnp.asarray(np.stack(m_new).astype(np.float32)))


def _attn_sample_kernel(q_ref, kn_ref, vn_ref, kt_ref, vt_ref, bt_ref, bf_ref, bn_ref, mt_ref, mf_ref, mn_ref,
                        out_ref, *, n_past):
    H = N_ATTN_HEADS
    rows = H * TPAD
    same_head = (_iota2((rows, ATTN_WIDTH), 0) >> _log2(TPAD)) == (_iota2((rows, ATTN_WIDTH), 1) >> HEAD_SHIFT)
    q = q_ref[...] * (HEAD_DIM ** -0.5)
    qm = jnp.where(same_head, jnp.concatenate([q] * H, axis=0), 0.0).astype(BF16)
    nt = (((1,), (1,)), ((), ()))
    kn = kn_ref[...].astype(BF16)
    vn = vn_ref[...].astype(BF16)
    kt = kt_ref[...].astype(BF16)
    vt = vt_ref[...].astype(BF16)
    s_new = lax.dot_general(qm, kn, nt, preferred_element_type=F32)
    s_full = jnp.dot(qm, kt, preferred_element_type=F32)
    s_tail = s_full[:, n_past - TAIL:]

    outs, lses = [], []
    for di, (window, _) in enumerate(DILATED_PATTERNS):
        if window <= TAIL:
            sm = jnp.where(mt_ref[di] > 0.5, s_tail + bt_ref[...], NEG_BIG)
            vmain = vt[:, n_past - TAIL:]
        else:
            sm = jnp.where(mf_ref[...] > 0.5, s_full + bf_ref[...], NEG_BIG)
            vmain = vt
        sn = jnp.where(mn_ref[di] > 0.5, s_new + bn_ref[...], NEG_BIG)
        m = jnp.maximum(jnp.max(sm, axis=-1, keepdims=True), jnp.max(sn, axis=-1, keepdims=True))
        pm = jnp.exp(sm - m)
        pn = jnp.exp(sn - m)
        l = jnp.sum(pm, axis=-1, keepdims=True) + jnp.sum(pn, axis=-1, keepdims=True)
        o = jnp.dot(pn.astype(BF16), vn, preferred_element_type=F32)
        o = o + lax.dot_general(pm.astype(BF16), vmain, nt, preferred_element_type=F32)
        outs.append(o / l)
        lses.append(m + jnp.log(l))
    mx = jnp.maximum(jnp.maximum(lses[0], lses[1]), lses[2])
    es = [jnp.exp(ls - mx) for ls in lses]
    mixed = (es[0] * outs[0] + es[1] * outs[1] + es[2] * outs[2]) / (es[0] + es[1] + es[2])
    mixed = jnp.where(same_head, mixed, 0.0)
    acc = mixed[0:TPAD]
    for h in range(1, H):
        acc = acc + mixed[h * TPAD:(h + 1) * TPAD]
    out_ref[...] = acc.astype(out_ref.dtype)


def _attn_sample(z3, cache_kt, cache_vt, layer, tables, host_grid=None):
    B, T, _ = z3.shape
    assert T == TPAD
    n_past = cache_kt.shape[-1]
    seq, _ = _host_maps(host_grid)
    new = lambda c0: pl.BlockSpec((None, TPAD, ATTN_WIDTH), lambda b, t: (seq(b, t), 0, c0 // ATTN_WIDTH))
    buf = pl.BlockSpec((None, None, ATTN_WIDTH, n_past), lambda b, t: (layer, seq(b, t), 0, 0))
    return _CallSpec(
        body=functools.partial(_attn_sample_kernel, n_past=n_past),
        grid=host_grid or (B, 1),
        args=[z3, z3, z3, cache_kt, cache_vt, *tables],
        in_specs=[new(COL_AQ), new(COL_AK), new(COL_AV), buf, buf] + [_const_spec(t.shape) for t in tables],
        out_shape=[jax.ShapeDtypeStruct((B, TPAD, ATTN_WIDTH), BF16)],
        out_specs=[pl.BlockSpec((None, TPAD, ATTN_WIDTH), lambda b, t: (seq(b, t), 0, 0))],
        scratch=[])


QUAD = N_MLSTM_HEADS * HEAD_DIM
CHUNK_UNROLL = 2
CHUNK_GROUP = 8
HEAD_SHIFT = 6


def _iota2(shape, axis):
    return lax.broadcasted_iota(jnp.int32, shape, axis)


def _log2(n):
    k = int(n).bit_length() - 1
    assert 1 << k == n
    return k


def _seg_mask(rows, row_shift, cols, col_shift):
    return (_iota2((rows, cols), 0) >> row_shift) == (_iota2((rows, cols), 1) >> col_shift)


def _cumulate_rows(x, length, op, fill):
    row = _iota2(x.shape, 0) & (length - 1)
    sh = 1
    while sh < length:
        x = op(x, jnp.where(row >= sh, pltpu.roll(x, sh, axis=0), fill))
        sh *= 2
    return x


def _split2(x):
    hi = x.astype(BF16)
    lo = (x - hi.astype(F32)).astype(BF16)
    return hi, lo


def _dot2(x, w):
    hi, lo = _split2(x)
    return jnp.dot(hi, w, preferred_element_type=F32) + jnp.dot(lo, w, preferred_element_type=F32)


def _dot2r(w, x):
    hi, lo = _split2(x)
    return jnp.dot(w, hi, preferred_element_type=F32) + jnp.dot(w, lo, preferred_element_type=F32)


def _tile_position(single_tile):
    if single_tile:
        return 0, True
    return pl.program_id(1), pl.program_id(1) == pl.num_programs(1) - 1


def _when(cond):
    if isinstance(cond, bool):
        return (lambda f: f()) if cond else (lambda f: None)
    return pl.when(cond)


def _host_maps(host_grid):
    if host_grid is None:
        return (lambda b, t: b), (lambda b, t: t)
    return (lambda b, t: b * host_grid[1] + t), (lambda b, t: 0)


def _load_block_diag(ref, seg64):
    flat = ref[...].reshape(QUAD, HEAD_DIM)
    return jnp.where(seg64, jnp.concatenate([flat] * (QUAD // HEAD_DIM), axis=1), 0.0)


def _store_block_diag(ref, mat):
    for h in range(QUAD // HEAD_DIM):
        ref[h] = mat[h * HEAD_DIM:(h + 1) * HEAD_DIM, h * HEAD_DIM:(h + 1) * HEAD_DIM]


def _head_rmsnorm_gate(h, seg64b, g_row, gate_pre):
    ms = _dot2(h * h, seg64b) * (1.0 / HEAD_DIM)
    return jax.nn.sigmoid(gate_pre) * (h * lax.rsqrt(ms + EPS) * g_row)


def _mlstm_kernel(qk_ref, v_ref, o_ref, gate_ref, conv0_ref, c0_ref, n0_ref, m0_ref, gb_ref, cw_ref, ng_ref,
                  out_ref, c_out, n_out, m_out, xp, cs, ns, ms, *, tile, chunk, n_valid, single_tile=False,
                  guests=None):
    TS, L = tile, chunk
    t, last = _tile_position(single_tile)
    PAD = SUBLANES

    @_when(t == 0)
    def _():
        cs[...] = _load_block_diag(c0_ref, _seg_mask(QUAD, HEAD_SHIFT, QUAD, HEAD_SHIFT))
        ns[...] = n0_ref[...]
        ms[...] = m0_ref[...]
        xp[0:PAD, :] = conv0_ref[...]

    @_when(t > 0)
    def _():
        xp[0:PAD, :] = xp[TS:TS + PAD, :]

    if guests is not None:
        guests()
    xp[PAD:PAD + TS, :] = qk_ref[...]

    LK = MLSTM_CHUNK
    seg64 = _seg_mask(QUAD, HEAD_SHIFT, QUAD, HEAD_SHIFT)
    seg64b = seg64.astype(BF16)
    dmask = (_iota2((LK, QUAD), 1) & (LK - 1)) == _iota2((LK, QUAD), 0)
    causal = (_iota2((L, QUAD), 1) & (LK - 1)) <= _iota2((L, QUAD), 0)
    tril = (_iota2((L, L), 1) <= _iota2((L, L), 0)).astype(BF16)
    ones_lk = jnp.ones((L, LK), BF16)
    row = _iota2((L, QUAD), 0)
    cw = cw_ref[...]
    gb = gb_ref[...]
    ng = ng_ref[...]

    first_half = _iota2((L, LANES), 1) < HEAD_DIM

    def per_head_lanes(g, lane0):
        col = [jnp.broadcast_to(g[:, lane0 + h:lane0 + h + 1], (L, LANES)) for h in range(N_MLSTM_HEADS)]
        return jnp.concatenate([jnp.where(first_half, col[0], col[1]), jnp.where(first_half, col[2], col[3])], axis=1)

    def key_rows(x, fill):
        if L == LK:
            return x
        return jnp.concatenate([x, jnp.full((LK - L, QUAD), fill, x.dtype)], axis=0)

    G = min(CHUNK_GROUP, TS // L)
    assert TS % (L * G) == 0
    nt = (((1,), (1,)), ((), ()))
    tn = (((0,), (0,)), ((), ()))
    zb = jnp.zeros((QUAD, QUAD), BF16)

    def group_body(gi, carry):
        R = range(G)
        r0 = [pl.multiple_of((gi * G + j) * L, L) for j in R]
        q, k, v, ig, lf = [], [], [], [], []
        for j in R:
            win = xp[pl.ds(r0[j], L + PAD), :]
            acc = win[PAD:PAD + L] * cw[CONV_WIDTH - 1:CONV_WIDTH, :]
            for s in range(1, CONV_WIDTH):
                acc = acc + win[PAD - s:PAD - s + L] * cw[CONV_WIDTH - 1 - s:CONV_WIDTH - s, :]
            qk = acc * jax.nn.sigmoid(acc)
            q.append(qk[:, 0:QUAD])
            k.append(qk[:, QUAD:] * (HEAD_DIM ** -0.5))
            v.append(v_ref[pl.ds(r0[j], L), :])
            gate = gate_ref[pl.ds(r0[j], L), :] + gb
            logsig = jnp.minimum(gate, 0.0) - jnp.log(1.0 + jnp.exp(-jnp.abs(gate)))
            ig_j = per_head_lanes(gate, 0)
            lf_j = per_head_lanes(logsig, N_MLSTM_HEADS)
            if n_valid < TS:
                ok = (row + r0[j]) < n_valid
                ig_j = jnp.where(ok, ig_j, NEG_BIG)
                lf_j = jnp.where(ok, lf_j, 0.0)
            ig.append(ig_j)
            lf.append(lf_j)
        b = [_dot2r(tril, lf[j]) for j in R]
        a = [ig[j] - b[j] for j in R]
        cm = [_cumulate_rows(a[j], L, jnp.maximum, -jnp.inf) for j in R]
        arow = [_dot2r(ones_lk, jnp.where(dmask, key_rows(a[j], NEG_BIG), 0.0)) for j in R]
        qb = [q[j].astype(BF16) for j in R]
        vb = [v[j].astype(BF16) for j in R]
        kbd = [jnp.where(seg64, jnp.concatenate([key_rows(k[j], 0.0).astype(BF16)] * N_MLSTM_HEADS, axis=0), zb) for j in R]
        vbd = [jnp.where(seg64, jnp.concatenate([key_rows(v[j], 0.0).astype(BF16)] * N_MLSTM_HEADS, axis=0), zb) for j in R]
        qkt = [lax.dot_general(qb[j], kbd[j], nt, preferred_element_type=F32) for j in R]

        mprev = ms[...]
        M, gg, emt = [], [], []
        for j in R:
            M.append(jnp.maximum(cm[j], mprev))
            mt = b[j] + M[j]
            gg.append(jnp.exp(mprev - M[j]))
            emt.append(jnp.exp(-mt))
            mprev = mt[L - 1:L, :]
        ms[...] = mprev

        wts = [jnp.where(causal, jnp.exp(arow[j] - M[j]), 0.0) * qkt[j] for j in R]
        kd = [k[j] * jnp.exp(a[j] - M[j][L - 1:L, :]) for j in R]
        num = [jnp.dot(wts[j].astype(BF16), vbd[j], preferred_element_type=F32) for j in R]
        u = [lax.dot_general(kd[j].astype(BF16), vb[j], tn, preferred_element_type=F32) for j in R]

        cmat, nvec = [cs[...]], [ns[...]]
        for j in R:
            gl = gg[j][L - 1:L, :]
            cmat.append(gl * cmat[j] + jnp.where(seg64, u[j], 0.0))
            nvec.append(gl * nvec[j] + jnp.sum(kd[j], axis=0, keepdims=True))
        cs[...] = cmat[G]
        ns[...] = nvec[G]

        inter = [jnp.dot(qb[j], cmat[j].astype(BF16), preferred_element_type=F32) for j in R]
        den = [_dot2(wts[j] + gg[j] * (q[j] * nvec[j]), seg64b) for j in R]
        h = [(num[j] + gg[j] * inter[j]) / jnp.maximum(jnp.abs(den[j]), emt[j]) for j in R]
        msq = [_dot2(h[j] * h[j], seg64b) * (1.0 / HEAD_DIM) for j in R]
        for j in R:
            y = h[j] * lax.rsqrt(msq[j] + EPS) * ng
            out_ref[pl.ds(r0[j], L), :] = (jax.nn.sigmoid(o_ref[pl.ds(r0[j], L), :]) * y).astype(out_ref.dtype)
        return carry

    lax.fori_loop(0, TS // (L * G), group_body, 0)

    @_when(last)
    def _():
        _store_block_diag(c_out, cs[...])
        n_out[...] = ns[...]
        m_out[...] = ms[...]


def _mlstm(z3, conv0, c0bd, n0, m0, gate_b, conv_w, norm_g, n_valid, chunk, host_grid=None):
    B, S, _ = z3.shape
    ts = min(TOKEN_TILE, S)
    assert S % ts == 0 and ts % chunk == 0 and (host_grid is None or S == ts)
    seq, til = _host_maps(host_grid)
    blk = lambda w, c0: pl.BlockSpec((None, ts, w), lambda b, t: (seq(b, t), til(b, t), c0 // w))
    per_b = lambda r, w: pl.BlockSpec((None, r, w), lambda b, t: (seq(b, t), 0, 0))
    per_head = pl.BlockSpec((None, N_MLSTM_HEADS, HEAD_DIM, HEAD_DIM), lambda b, t: (seq(b, t), 0, 0, 0))
    return _CallSpec(
        body=functools.partial(_mlstm_kernel, tile=ts, chunk=chunk, n_valid=n_valid, single_tile=host_grid is not None),
        grid=host_grid or (B, S // ts),
        args=[z3, z3, z3, z3, conv0, c0bd, n0, m0, gate_b, conv_w, norm_g],
        in_specs=[blk(2 * QUAD, COL_MQK), blk(QUAD, COL_MV), blk(QUAD, COL_MO), blk(LANES, COL_GATE),
                  per_b(SUBLANES, 2 * QUAD), per_head, per_b(1, QUAD), per_b(1, QUAD),
                  _const_spec((1, LANES)), _const_spec((CONV_WIDTH, 2 * QUAD)), _const_spec((1, QUAD))],
        out_shape=[jax.ShapeDtypeStruct((B, S, QUAD), BF16),
                   jax.ShapeDtypeStruct((B, N_MLSTM_HEADS, HEAD_DIM, HEAD_DIM), F32),
                   jax.ShapeDtypeStruct((B, 1, QUAD), F32),
                   jax.ShapeDtypeStruct((B, 1, QUAD), F32)],
        out_specs=[blk(QUAD, 0), per_head, per_b(1, QUAD), per_b(1, QUAD)],
        scratch=[pltpu.VMEM((ts + 2 * SUBLANES, 2 * QUAD), F32),
                 pltpu.VMEM((QUAD, QUAD), F32), pltpu.VMEM((1, QUAD), F32), pltpu.VMEM((1, QUAD), F32)])


HGRN_FAST_CHUNK = 64
HGRN_GROUP = 8
HGRN_SAFE_DECAY = 80.0


def _hgrn_kernel(q_ref, f_ref, i_ref, g_ref, s0_ref, lb_ref, ng_ref, out_ref, s_out, st, qs, fs, ks, *,
                 tile, sub, n_valid, fast, single_tile=False):
    TS = tile
    t, last = _tile_position(single_tile)

    seg64 = _seg_mask(QUAD, HEAD_SHIFT, QUAD, HEAD_SHIFT)
    seg64b = seg64.astype(BF16)

    @_when(t == 0)
    def _():
        st[...] = _load_block_diag(s0_ref, seg64)
    lb = lb_ref[...]
    ng = ng_ref[...]
    nt = (((1,), (1,)), ((), ()))
    tn = (((0,), (0,)), ((), ()))

    P = HGRN_FAST_CHUNK if fast else sub
    prow = _iota2((P, QUAD), 0)

    def prep(c, worst):
        r0 = pl.multiple_of(c * P, P)
        hq = q_ref[pl.ds(r0, P), :]
        f = lb + (1.0 - lb) * jax.nn.sigmoid(f_ref[pl.ds(r0, P), :])
        logf = jnp.log(f)
        kk = 1.0 - f
        if n_valid < TS:
            ok = (prow + r0) < n_valid
            logf = jnp.where(ok, logf, 0.0)
            kk = jnp.where(ok, kk, 0.0)
        qs[pl.ds(r0, P), :] = hq * jax.nn.sigmoid(hq)
        fs[pl.ds(r0, P), :] = logf
        ks[pl.ds(r0, P), :] = kk
        return jnp.minimum(worst, jnp.sum(logf, axis=0, keepdims=True))

    worst = lax.fori_loop(0, TS // P, prep, jnp.zeros((1, QUAD), F32))

    def finish(o, r0, rows, smat, b, kk, iv, qt):
        o = o + lax.dot_general(qt, smat.astype(BF16), nt, preferred_element_type=F32)
        out_ref[pl.ds(r0, rows), :] = _head_rmsnorm_gate(o, seg64b, ng, g_ref[pl.ds(r0, rows), :]).astype(out_ref.dtype)
        bl = b[rows - 1:rows, :]
        ktil = kk * jnp.exp(bl - b)
        u = lax.dot_general(iv.astype(BF16), ktil.astype(BF16), tn, preferred_element_type=F32)
        st[...] = smat * jnp.exp(bl) + jnp.where(seg64, u, 0.0)

    def fast_loop():
        L = HGRN_FAST_CHUNK
        tril = (_iota2((L, L), 1) <= _iota2((L, L), 0)).astype(BF16)
        causal = (_iota2((L, QUAD), 1) & (L - 1)) <= _iota2((L, QUAD), 0)
        zb = jnp.zeros((QUAD, QUAD), BF16)

        G = min(HGRN_GROUP, TS // L)
        assert TS % (L * G) == 0

        def body(gi, carry):
            R = range(G)
            r0 = [pl.multiple_of((gi * G + j) * L, L) for j in R]
            kk = [ks[pl.ds(r0[j], L), :] for j in R]
            ivb = [i_ref[pl.ds(r0[j], L), :].astype(BF16) for j in R]
            b = [_dot2r(tril, fs[pl.ds(r0[j], L), :]) for j in R]
            qt = [(qs[pl.ds(r0[j], L), :] * jnp.exp(b[j])).astype(BF16) for j in R]
            kbd = [jnp.where(seg64, jnp.concatenate([(kk[j] * jnp.exp(-b[j])).astype(BF16)] * N_HGRN_HEADS, axis=0), zb)
                   for j in R]
            ibd = [jnp.where(seg64, jnp.concatenate([ivb[j]] * N_HGRN_HEADS, axis=0), zb) for j in R]
            amat = [lax.dot_general(qt[j], kbd[j], nt, preferred_element_type=F32) for j in R]
            ktil = [(kk[j] * jnp.exp(b[j][L - 1:L, :] - b[j])).astype(BF16) for j in R]
            u = [lax.dot_general(ivb[j], ktil[j], tn, preferred_element_type=F32) for j in R]
            o = [jnp.dot(jnp.where(causal, amat[j], 0.0).astype(BF16), ibd[j], preferred_element_type=F32) for j in R]
            smat = [st[...]]
            for j in R:
                smat.append(smat[j] * jnp.exp(b[j][L - 1:L, :]) + jnp.where(seg64, u[j], 0.0))
            st[...] = smat[G]
            inter = [lax.dot_general(qt[j], smat[j].astype(BF16), nt, preferred_element_type=F32) for j in R]
            o = [o[j] + inter[j] for j in R]
            msq = [_dot2(o[j] * o[j], seg64b) * (1.0 / HEAD_DIM) for j in R]
            for j in R:
                y = o[j] * lax.rsqrt(msq[j] + EPS) * ng
                out_ref[pl.ds(r0[j], L), :] = (jax.nn.sigmoid(g_ref[pl.ds(r0[j], L), :]) * y).astype(out_ref.dtype)
            return carry

        lax.fori_loop(0, TS // (L * G), body, 0)

    def exact_loop():
        L = sub
        row = _iota2((L, QUAD), 0)

        def body(c, carry):
            r0 = pl.multiple_of(c * L, L)
            q = qs[pl.ds(r0, L), :]
            kk = ks[pl.ds(r0, L), :]
            iv = i_ref[pl.ds(r0, L), :]
            b = _cumulate_rows(fs[pl.ds(r0, L), :], L, jnp.add, 0.0)
            parts = []
            for j in range(L):
                dec = jnp.exp(jnp.where(row >= j, b - b[j:j + 1, :], NEG_BIG))
                parts.append(dec * q * kk[j:j + 1, :])
            tstack = jnp.concatenate(parts, axis=0).astype(BF16)
            y = jnp.dot(tstack, seg64b, preferred_element_type=F32)
            o = y[0:L] * iv[0:1, :]
            for j in range(1, L):
                o = o + y[j * L:(j + 1) * L] * iv[j:j + 1, :]
            finish(o, r0, L, st[...], b, kk, iv, (q * jnp.exp(b)).astype(BF16))
            return carry

        lax.fori_loop(0, TS // L, body, 0, unroll=min(CHUNK_UNROLL, TS // L))

    if fast:
        safe = jnp.min(worst) > -HGRN_SAFE_DECAY
        pl.when(safe)(fast_loop)
        pl.when(jnp.logical_not(safe))(exact_loop)
    else:
        exact_loop()

    @_when(last)
    def _():
        _store_block_diag(s_out, st[...])


def _hgrn(z3, s0t, lb, norm_g, n_valid, sub, host_grid=None):
    B, S, _ = z3.shape
    ts = min(TOKEN_TILE, S)
    fast = ts % HGRN_FAST_CHUNK == 0
    assert S % ts == 0 and ts % sub == 0 and (host_grid is None or S == ts)
    seq, til = _host_maps(host_grid)
    blk = lambda c0: pl.BlockSpec((None, ts, QUAD), lambda b, t: (seq(b, t), til(b, t), c0 // QUAD))
    per_b = pl.BlockSpec((None, N_HGRN_HEADS, HEAD_DIM, HEAD_DIM), lambda b, t: (seq(b, t), 0, 0, 0))
    stage = lambda: pltpu.VMEM((ts, QUAD), F32)
    return _CallSpec(
        body=functools.partial(_hgrn_kernel, tile=ts, sub=sub, n_valid=n_valid, fast=fast,
                               single_tile=host_grid is not None),
        grid=host_grid or (B, S // ts),
        args=[z3, z3, z3, z3, s0t, lb, norm_g],
        in_specs=[blk(COL_HQ), blk(COL_HF), blk(COL_HI), blk(COL_HG), per_b,
                  _const_spec((1, QUAD)), _const_spec((1, QUAD))],
        out_shape=[jax.ShapeDtypeStruct((B, S, QUAD), BF16),
                   jax.ShapeDtypeStruct((B, N_HGRN_HEADS, HEAD_DIM, HEAD_DIM), F32)],
        out_specs=[blk(0), per_b],
        scratch=[pltpu.VMEM((QUAD, QUAD), F32), stage(), stage(), stage()])


PROJ_SPLITS = (ATTN_WIDTH, ATTN_WIDTH, ATTN_WIDTH,
               MLSTM_WIDTH, MLSTM_WIDTH, MLSTM_WIDTH, MLSTM_WIDTH, N_MLSTM_HEADS, N_MLSTM_HEADS,
               HGRN_WIDTH, HGRN_WIDTH, HGRN_WIDTH, HGRN_WIDTH)


def _prep_w_in(w):
    g0 = int(np.sum(PROJ_SPLITS[:7]))
    g1 = g0 + 2 * N_MLSTM_HEADS
    assert (g0, w.shape[-1] - g1) == (COL_HQ, COL_GATE - COL_HQ)
    wb = w.astype(BF16)
    pad = jnp.zeros(w.shape[:-1] + (LANES - 2 * N_MLSTM_HEADS,), BF16)
    return wb, jnp.concatenate([wb[..., g1:], wb[..., g0:g1], pad], axis=-1)


HGRN_SUBCHUNK = 16


class _LayerWeights(NamedTuple):
    layer: int
    g_mix: jax.Array
    w_in: jax.Array
    gate_b: jax.Array
    conv_w: jax.Array
    mlstm_g: jax.Array
    lb: jax.Array
    hgrn_g: jax.Array
    w_out: jax.Array
    g_ffn: jax.Array
    w_up: jax.Array
    w_down: jax.Array
    g_final: jax.Array


def _gate_row(gate_b):
    flat = gate_b.astype(F32).reshape(1, 2 * N_MLSTM_HEADS)
    return jnp.concatenate([flat, jnp.zeros((1, LANES - 2 * N_MLSTM_HEADS), F32)], axis=1)


class _LayerFront(NamedTuple):
    z3: jax.Array
    kv_t: tuple
    mlstm: _CallSpec
    hgrn: _CallSpec


def _layer_front(x2d, B, S, n_valid, states, w, chunk_m, chunk_h, n_keep=None, stacked=None, host_grid=None):
    conv_buf, C0, n0, m0, S0 = states
    if n_keep is None:
        z, kv_t = _inproj(x2d, w.g_mix, w.w_in, w.layer), None
    else:
        z, kt, vt = _inproj(x2d, w.g_mix, w.w_in, w.layer, S, n_keep, stacked)
        kv_t = (kt, vt)
    z3 = z.reshape(B, S, ZW)
    conv0 = jnp.zeros((B, SUBLANES, 2 * MLSTM_WIDTH), F32).at[:, SUBLANES - (CONV_WIDTH - 1):].set(conv_buf.astype(F32))
    m0r = jnp.repeat(m0.astype(F32), HEAD_DIM, axis=1).reshape(B, 1, MLSTM_WIDTH)
    mlstm = _mlstm(z3, conv0, C0.astype(F32), n0.astype(F32).reshape(B, 1, MLSTM_WIDTH), m0r,
                   w.gate_b, w.conv_w, w.mlstm_g, n_valid, chunk_m, host_grid)
    hgrn = _hgrn(z3, jnp.swapaxes(S0.astype(F32), -1, -2), w.lb, w.hgrn_g, n_valid, chunk_h, host_grid)
    return _LayerFront(z3, kv_t, mlstm, hgrn)


def _layer_back(x2d, front, n_valid, w, final, attn, mlstm_out, hgrn_out):
    B, S, _ = front.z3.shape
    z3 = front.z3
    (ml, c_new, n, m), (hg, st) = mlstm_out, hgrn_out
    n2 = B * S
    y = _outffn(x2d, attn.reshape(n2, ATTN_WIDTH), ml.reshape(n2, MLSTM_WIDTH), hg.reshape(n2, HGRN_WIDTH),
                w.w_out, w.g_ffn, w.w_up, w.w_down, w.g_final, final, w.layer)
    k_rows = z3[:, :n_valid, COL_AK:COL_AK + ATTN_WIDTH].reshape(B, n_valid, N_ATTN_HEADS, HEAD_DIM)
    v_rows = z3[:, :n_valid, COL_AV:COL_AV + ATTN_WIDTH].reshape(B, n_valid, N_ATTN_HEADS, HEAD_DIM)
    conv_new = z3[:, n_valid - (CONV_WIDTH - 1):n_valid, COL_MQK:COL_MQK + 2 * MLSTM_WIDTH]
    s_new = jnp.swapaxes(st, -1, -2)
    return y, (k_rows, v_rows, conv_new, c_new, n.reshape(B, N_MLSTM_HEADS, HEAD_DIM), m[:, 0, ::HEAD_DIM], s_new)


def kernel(x_prompt, x_sample, cache_attn_k, cache_attn_v, state_mlstm_conv, state_mlstm_C, state_mlstm_n, state_mlstm_m, state_hgrn_S, rel_bias, g_mix, w_in, mlstm_gate_b, mlstm_conv_w, mlstm_norm_g, hgrn_lb_raw, hgrn_norm_g, w_out, g_ffn, w_up, w_down, g_final):
    depth = w_in.shape[0]
    bp, sp, _ = x_prompt.shape
    bs, ts, _ = x_sample.shape
    n_keep = min(max(w for w, _ in DILATED_PATTERNS), sp)
    n_past = cache_attn_k.shape[2]
    lb_p = jax.nn.softmax(hgrn_lb_raw.astype(F32), axis=0)
    hgrn_lb = jnp.cumsum(lb_p, axis=0) - lb_p[0]
    bias_dist = _bias_by_distance(rel_bias)
    p_tables = _prompt_tables(bias_dist)
    s_tables = _sample_tables(bias_dist, n_past, ts)
    cache_kt = jnp.transpose(cache_attn_k, (0, 1, 3, 4, 2)).reshape(depth, bs, ATTN_WIDTH, n_past)
    cache_vt = jnp.transpose(cache_attn_v, (0, 1, 3, 4, 2)).reshape(depth, bs, ATTN_WIDTH, n_past)

    hp = x_prompt.reshape(bp * sp, D_MODEL)
    hs = jnp.zeros((bs, TPAD, D_MODEL), F32).at[:, :ts].set(x_sample).reshape(bs * TPAD, D_MODEL)
    zero_states = (jnp.zeros((bp, CONV_WIDTH - 1, 2 * MLSTM_WIDTH), F32),
                   jnp.zeros((bp, N_MLSTM_HEADS, HEAD_DIM, HEAD_DIM), F32),
                   jnp.zeros((bp, N_MLSTM_HEADS, HEAD_DIM), F32),
                   jnp.zeros((bp, N_MLSTM_HEADS), F32),
                   jnp.zeros((bp, N_HGRN_HEADS, HEAD_DIM, HEAD_DIM), F32))
    p_states, s_states = [], []
    kv_t = None
    sem = ("parallel", "arbitrary")
    w_in_b, w_out_b, w_up_b, w_down_b = _prep_w_in(w_in), w_out.astype(BF16), w_up.astype(BF16), w_down.astype(BF16)
    for l in range(depth):
        final = l == depth - 1
        weights = _LayerWeights(
            layer=l, g_mix=g_mix[l][None], w_in=w_in_b,
            gate_b=_gate_row(mlstm_gate_b[l]),
            conv_w=mlstm_conv_w[l], mlstm_g=mlstm_norm_g[l][None], lb=hgrn_lb[l][None], hgrn_g=hgrn_norm_g[l][None],
            w_out=w_out_b, g_ffn=g_ffn[l][None], w_up=w_up_b, w_down=w_down_b, g_final=g_final[None])
        pf = _layer_front(hp, bp, sp, sp, zero_states, weights, MLSTM_CHUNK, HGRN_SUBCHUNK, n_keep, (l, depth, kv_t))
        kv_t = pf.kv_t
        grid = pf.mlstm.grid
        host = grid if bs == grid[0] * grid[1] else None
        states = (state_mlstm_conv[l], state_mlstm_C[l], state_mlstm_n[l], state_mlstm_m[l], state_hgrn_S[l])
        sf = _layer_front(hs, bs, TPAD, ts, states, weights, TPAD, TPAD, host_grid=host)
        s_attn = _attn_sample(sf.z3, cache_kt, cache_vt, l, s_tables, host)
        p_attn = _attn_prompt(pf.z3, p_tables)
        if host is None:
            p_ml, = _run([pf.mlstm], sem, "mlstm")
            (s_attn,), s_ml, s_hg = _run([s_attn, sf.mlstm, sf.hgrn], sem, "sample_mixers")
        else:
            p_ml, (s_attn,), s_ml, s_hg = _run([pf.mlstm, s_attn, sf.mlstm, sf.hgrn], sem, "mlstm_and_sample_mixers",
                                               host=0)
        p_hg, = _run([pf.hgrn], sem, "hgrn")
        hp, st = _layer_back(hp, pf, sp, weights, final, p_attn, p_ml, p_hg)
        p_states.append(st[2:])
        hs, st = _layer_back(hs, sf, ts, weights, final, s_attn, s_ml, s_hg)
        s_states.append(st)
    y_prompt = hp.reshape(bp, sp, D_MODEL)
    y_sample = hs.reshape(bs, TPAD, D_MODEL)[:, :ts]
    to_rows = lambda t: jnp.transpose(t.reshape(depth, bp, N_ATTN_HEADS, HEAD_DIM, n_keep), (0, 1, 4, 2, 3))
    p_out = [to_rows(kv_t[0]), to_rows(kv_t[1])] + [jnp.stack(a) for a in zip(*p_states)]
    s_out = [jnp.stack(a) for a in zip(*s_states)]
    return (y_prompt, y_sample, *p_out, *s_out)
```

```python
import functools
from typing import Callable, NamedTuple

import jax
import jax.numpy as jnp
import numpy as np
from jax import lax
from jax.experimental import pallas as pl
from jax.experimental.pallas import tpu as pltpu

F32 = jnp.float32
BF16 = jnp.bfloat16

D_MODEL = 1024
HEAD_DIM = 64
N_ATTN_HEADS = 8
N_MLSTM_HEADS = 4
N_HGRN_HEADS = 4
ATTN_WIDTH = N_ATTN_HEADS * HEAD_DIM
MLSTM_WIDTH = N_MLSTM_HEADS * HEAD_DIM
HGRN_WIDTH = N_HGRN_HEADS * HEAD_DIM
DILATED_PATTERNS = ((128, 1), (512, 4), (2048, 16))
QBLOCK = 128
N_REL_BUCKETS = 32
REL_MAX_DISTANCE = 2048
CONV_WIDTH = 4
MLSTM_CHUNK = 64
D_FF = 4 * D_MODEL
EPS = 1e-6
NEG_BIG = -1e30
LOG2E = 1.4426950408889634

LANES = 128
SUBLANES = 8

COL_AQ = 0
COL_AK = COL_AQ + ATTN_WIDTH
COL_AV = COL_AK + ATTN_WIDTH
COL_MQK = COL_AV + ATTN_WIDTH
COL_MV = COL_MQK + 2 * MLSTM_WIDTH
COL_MO = COL_MV + MLSTM_WIDTH
COL_HQ = COL_MO + MLSTM_WIDTH
COL_HF = COL_HQ + HGRN_WIDTH
COL_HI = COL_HF + HGRN_WIDTH
COL_HG = COL_HI + HGRN_WIDTH
COL_GATE = COL_HG + HGRN_WIDTH
ZW = COL_GATE + LANES

VMEM_LIMIT = 56 * 1024 * 1024
TOKEN_TILE = 512
ATTN_STAGE_ROWS = 512
ATTN_MIX_ROWS = 512


def _cparams(sem, vmem=VMEM_LIMIT):
    return pltpu.CompilerParams(dimension_semantics=sem, vmem_limit_bytes=vmem)


def _const_spec(shape):
    nd = len(shape)
    return pl.BlockSpec(shape, lambda *_: (0,) * nd, pipeline_mode=pl.Buffered(1))


def _layer_spec(shape, layer):
    nd = len(shape)
    return pl.BlockSpec((None,) + tuple(shape), lambda *_: (layer,) + (0,) * nd, pipeline_mode=pl.Buffered(1))


class _CallSpec(NamedTuple):
    body: Callable
    grid: tuple
    args: list
    in_specs: list
    out_shape: list
    out_specs: list
    scratch: list


def _run(specs, sem, name, host=None):
    grid = specs[0].grid
    assert all(s.grid == grid for s in specs)
    n_in = [len(s.args) for s in specs]
    n_out = [len(s.out_shape) for s in specs]
    n_scr = [len(s.scratch) for s in specs]

    def body(*refs):
        ins, outs, scrs = refs[:sum(n_in)], refs[sum(n_in):sum(n_in) + sum(n_out)], refs[sum(n_in) + sum(n_out):]

        def bound(k, **kw):
            a, b, c = sum(n_in[:k]), sum(n_out[:k]), sum(n_scr[:k])
            return functools.partial(specs[k].body, *ins[a:a + n_in[k]], *outs[b:b + n_out[k]], *scrs[c:c + n_scr[k]], **kw)

        others = [k for k in range(len(specs)) if k != host]
        if host is None:
            for k in others:
                bound(k)()
        else:
            bound(host, guests=lambda: [bound(k)() for k in others])()

    flat = pl.pallas_call(
        body, grid=grid,
        in_specs=[x for s in specs for x in s.in_specs],
        out_shape=[x for s in specs for x in s.out_shape],
        out_specs=[x for s in specs for x in s.out_specs],
        scratch_shapes=[x for s in specs for x in s.scratch],
        compiler_params=_cparams(sem), name=name,
    )(*[x for s in specs for x in s.args])
    out, k = [], 0
    for n in n_out:
        out.append(list(flat[k:k + n]))
        k += n
    return out


def _inproj_kernel(x_ref, g_ref, wa_ref, wb_ref, *rest, tiles_per_seq, first_keep, n_prev=0):
    x = x_ref[...]
    ms = jnp.mean(x * x, axis=-1, keepdims=True)
    xn = ((x * lax.rsqrt(ms + EPS)) * g_ref[...]).astype(BF16)
    z_ref = rest[n_prev]
    z_ref[:, 0:COL_HQ] = jnp.dot(xn, wa_ref[...], preferred_element_type=F32)
    z_ref[:, COL_HQ:] = jnp.dot(xn, wb_ref[...], preferred_element_type=F32)
    if first_keep is None:
        return
    kt_ref, vt_ref = rest[n_prev + 1:]

    kt_ref[...] = z_ref[:, COL_AK:COL_AK + ATTN_WIDTH].T
    vt_ref[...] = z_ref[:, COL_AV:COL_AV + ATTN_WIDTH].T


def _inproj(x2d, g, w, layer, seq=None, n_keep=None, stacked=None):
    n = x2d.shape[0]
    tm = min(TOKEN_TILE, n)
    assert n % tm == 0
    x_spec = pl.BlockSpec((tm, D_MODEL), lambda i: (i, 0))
    z_spec = pl.BlockSpec((tm, ZW), lambda i: (i, 0))
    z_shape = jax.ShapeDtypeStruct((n, ZW), F32)
    w_specs = [_layer_spec((D_MODEL, COL_HQ), layer), _layer_spec((D_MODEL, ZW - COL_HQ), layer)]
    if stacked is None:
        return pl.pallas_call(
            functools.partial(_inproj_kernel, tiles_per_seq=None, first_keep=None),
            out_shape=z_shape, grid=(n // tm,),
            in_specs=[x_spec, _const_spec((1, D_MODEL))] + w_specs,
            out_specs=z_spec, compiler_params=_cparams(("parallel",)), name="inproj",
        )(x2d, g, *w)
    assert seq % tm == 0 and n_keep % tm == 0
    layer, depth, prev = stacked
    tps, first_keep = seq // tm, (seq - n_keep) // tm
    t_spec = pl.BlockSpec((None, None, ATTN_WIDTH, tm),
                          lambda i: (layer, i // tps, 0, jnp.maximum(i % tps - first_keep, 0)))
    t_shape = jax.ShapeDtypeStruct((depth, n // seq, ATTN_WIDTH, n_keep), F32)
    in_specs = [x_spec, _const_spec((1, D_MODEL))] + w_specs
    args = [x2d, g, *w]
    if prev is None:
        prev = (jnp.zeros(t_shape.shape, F32), jnp.zeros(t_shape.shape, F32))
    in_specs += [pl.BlockSpec(memory_space=pl.ANY)] * 2
    aliases = {len(args): 1, len(args) + 1: 2}
    args += list(prev)
    return pl.pallas_call(
        functools.partial(_inproj_kernel, tiles_per_seq=tps, first_keep=first_keep, n_prev=len(prev)),
        out_shape=[z_shape, t_shape, t_shape], grid=(n // tm,),
        in_specs=in_specs, out_specs=[z_spec, t_spec, t_spec], input_output_aliases=aliases,
        compiler_params=_cparams(("arbitrary",)), name="inproj_kt",
    )(*args)


FF_CHUNK = 1024
FFN_ROWS = TOKEN_TILE


def _outffn_kernel(x_ref, a_ref, m_ref, h_ref, wo_ref, gf_ref, wu_ref, wd_ref, gl_ref, y_ref, xn_sc, *, final):
    x1 = x_ref[...]
    x1 = x1 + jnp.dot(a_ref[...], wo_ref[0:ATTN_WIDTH, :], preferred_element_type=F32)
    x1 = x1 + jnp.dot(m_ref[...], wo_ref[ATTN_WIDTH:ATTN_WIDTH + MLSTM_WIDTH, :], preferred_element_type=F32)
    x1 = x1 + jnp.dot(h_ref[...], wo_ref[ATTN_WIDTH + MLSTM_WIDTH:, :], preferred_element_type=F32)
    ms = jnp.mean(x1 * x1, axis=-1, keepdims=True)
    xn_sc[...] = ((x1 * lax.rsqrt(ms + EPS)) * gf_ref[...]).astype(BF16)
    y_ref[...] = x1
    nchunk = D_FF // FF_CHUNK
    up = lambda c: jnp.dot(xn_sc[...], wu_ref[:, c * FF_CHUNK:(c + 1) * FF_CHUNK], preferred_element_type=F32)
    u = up(0)
    for c in range(nchunk):
        u_next = up(c + 1) if c + 1 < nchunk else None
        hh = jnp.square(jnp.maximum(u, 0.0)).astype(BF16)
        y_ref[...] += jnp.dot(hh, wd_ref[c * FF_CHUNK:(c + 1) * FF_CHUNK, :], preferred_element_type=F32)
        u = u_next
    if final:
        x2 = y_ref[...]
        ms2 = jnp.mean(x2 * x2, axis=-1, keepdims=True)
        y_ref[...] = (x2 * lax.rsqrt(ms2 + EPS)) * gl_ref[...]


def _outffn(x2d, attn, ml, hg, w_out, g_ffn, w_up, w_down, g_final, final, layer):
    n = x2d.shape[0]
    tm = min(FFN_ROWS, n)
    assert n % tm == 0
    row = lambda w: pl.BlockSpec((tm, w), lambda i: (i, 0))
    return pl.pallas_call(
        functools.partial(_outffn_kernel, final=final),
        out_shape=jax.ShapeDtypeStruct((n, D_MODEL), F32),
        grid=(n // tm,),
        in_specs=[row(D_MODEL), row(ATTN_WIDTH), row(MLSTM_WIDTH), row(HGRN_WIDTH),
                  _layer_spec((D_MODEL, D_MODEL), layer), _const_spec((1, D_MODEL)),
                  _layer_spec((D_MODEL, D_FF), layer), _layer_spec((D_FF, D_MODEL), layer), _const_spec((1, D_MODEL))],
        out_specs=row(D_MODEL),
        scratch_shapes=[pltpu.VMEM((tm, D_MODEL), BF16)],
        compiler_params=_cparams(("parallel",)),
        name="outffn",
    )(x2d, attn, ml, hg, w_out, g_ffn, w_up, w_down, g_final)


def _t5_causal_bucket(dist):
    n = np.asarray(dist).astype(np.int32)
    max_exact = N_REL_BUCKETS // 2
    scaled = np.log(np.maximum(n, 1) / max_exact) / np.log(REL_MAX_DISTANCE / max_exact)
    large = np.minimum(max_exact + (scaled * (N_REL_BUCKETS - max_exact)).astype(np.int32), N_REL_BUCKETS - 1)
    return np.where(n < max_exact, n, large).astype(np.int32)


BIAS_DIST = 2304


def _bias_by_distance(rel_bias):
    assert BIAS_DIST > max(w for w, _ in DILATED_PATTERNS) + SUBLANES
    return rel_bias.astype(F32)[_t5_causal_bucket(np.arange(BIAS_DIST)[::-1])].T


def _dist_slice(bias_desc, lo, hi, step=1):
    last = BIAS_DIST - 1
    return bias_desc[:, last - hi:last - lo + 1:step]


def _prompt_tables(bias_dist):
    H = N_ATTN_HEADS
    a = np.arange(QBLOCK)[:, None]
    b = np.arange(2 * QBLOCK)[None, :]
    rel = QBLOCK + a - b
    band = (rel >= 0) & (rel <= QBLOCK)
    period = 3 * QBLOCK
    biases = []
    for window, dil in DILATED_PATTERNS:
        nsub = window // dil
        assert nsub == QBLOCK
        vd = _dist_slice(bias_dist, 0, nsub * dil, dil)
        rp = jnp.concatenate([vd, jnp.broadcast_to(vd[:, -1:], (H, QBLOCK)),
                              jnp.broadcast_to(vd[:, 0:1], (H, QBLOCK - 1))], axis=1)
        skew = jnp.tile(rp, (1, QBLOCK))[:, :QBLOCK * (period - 1)].reshape(H, QBLOCK, period - 1)
        biases.append(skew[:, :, :2 * QBLOCK].reshape(H // 2, 2 * QBLOCK, 2 * QBLOCK))
    mask = np.stack([np.tile(band, (2, 1)), np.tile(band & (b >= QBLOCK), (2, 1))])
    table = jnp.where(mask[None, None], jnp.stack(biases)[:, :, None] * LOG2E, -jnp.inf)
    return jnp.swapaxes(table, -1, -2)


ATTN_GROUP = 8


def _attn_prompt_kernel(q_ref, k_ref, v_ref, bias_ref, out_ref,
                        x4, qs, ks, vts, o1, o4, o16, l1, l4, l16, p_scr, st_scr, *, seq):
    S = seq
    nblk = S // QBLOCK
    scale = HEAD_DIM ** -0.5 * LOG2E
    zpad = jnp.zeros((QBLOCK, LANES), BF16)
    for di in range(3):
        ks[di, 0:QBLOCK, :] = zpad
        vts[di, 0] = zpad

    R = ATTN_STAGE_ROWS

    def put(src, di, row0, blk):
        if src is q_ref:
            qs[di, row0:row0 + blk.shape[0], :] = (blk * scale).astype(BF16)
        elif src is k_ref:
            ks[di, QBLOCK + row0:QBLOCK + row0 + blk.shape[0], :] = blk.astype(BF16)
        else:
            blk_t = blk.T.astype(BF16)
            for t in range(blk.shape[0] // QBLOCK):
                vts[di, 1 + row0 // QBLOCK + t] = blk_t[:, t * QBLOCK:(t + 1) * QBLOCK]

    for src in (q_ref, k_ref, v_ref):
        for i in range(S // R):
            put(src, 0, i * R, src[i * R:(i + 1) * R, :])
        for r in range(4):
            for i in range(S // 4 // R):
                blk = src[pl.ds(r + 4 * i * R, R, stride=4), :]
                row0 = r * (S // 4) + i * R
                x4[row0:row0 + R, :] = blk
                put(src, 1, row0, blk)
        n16 = S // 16
        for c4 in range(4):
            for a in range(4):
                put(src, 2, (c4 * 4 + a) * n16, x4[pl.ds(c4 * (S // 4) + a, n16, stride=4), :])

    lane = lax.broadcasted_iota(jnp.int32, (QBLOCK, LANES), 1)
    head_a = lane < HEAD_DIM
    row_a = lax.broadcasted_iota(jnp.int32, (QBLOCK, LANES), 0) < HEAD_DIM
    nt = (((1,), (1,)), ((), ()))

    def branch(di, dil, o_ref, l_ref):
        per_class = (S // dil) // QBLOCK

        R = range(ATTN_GROUP)
        zero = jnp.zeros((QBLOCK, LANES), BF16)

        def head_block(j):
            return per_class <= ATTN_GROUP and j % per_class == 0

        assert per_class % ATTN_GROUP == 0 or ATTN_GROUP % per_class == 0

        def scores(g, js):
            out = {}
            for j in js:
                i = g * ATTN_GROUP + j
                base = pl.multiple_of(i * QBLOCK, QBLOCK)
                q = qs[di, pl.ds(base, QBLOCK), :]
                q2 = jnp.concatenate([jnp.where(head_a, q, zero), jnp.where(head_a, zero, q)], axis=0)
                if head_block(j):
                    kk = ks[di, pl.ds(base + QBLOCK, QBLOCK), :]
                else:
                    kk = ks[di, pl.ds(base, 2 * QBLOCK), :]
                out[j] = lax.dot_general(kk, q2, nt, preferred_element_type=F32)
            return out

        def softmax(g, slot, st):
            for j in st:
                i = g * ATTN_GROUP + j
                if head_block(j):
                    s = st[j] + bias_ref[di, 1, QBLOCK:, :]
                else:
                    s = st[j] + bias_ref[di, jnp.where(i % per_class == 0, 1, 0)]
                m = jnp.max(s, axis=0, keepdims=True)
                p = jnp.exp2(s - m)
                if head_block(j):
                    p_scr[slot, j, QBLOCK:, :] = p.astype(BF16)
                else:
                    p_scr[slot, j] = p.astype(BF16)
                st_scr[slot, j, 0:1, :] = m
                st_scr[slot, j, 1:2, :] = jnp.sum(p, axis=0, keepdims=True)

        def weighted_values(g, slot, js):
            out = {}
            for j in js:
                i = g * ATTN_GROUP + j
                if head_block(j):
                    ot = jnp.dot(vts[di, i + 1], p_scr[slot, j, QBLOCK:, :], preferred_element_type=F32)
                else:
                    vt = jnp.concatenate([vts[di, i], vts[di, i + 1]], axis=1)
                    ot = jnp.dot(vt, p_scr[slot, j], preferred_element_type=F32)
                out[j] = (ot, st_scr[slot, j, 0:1, :], st_scr[slot, j, 1:2, :])
            return out

        def outputs(g, pv):
            for j in pv:
                i = g * ATTN_GROUP + j
                base = pl.multiple_of(i * QBLOCK, QBLOCK)
                ot, m, l = pv[j]
                otn = ot / l
                lse = m + jnp.log2(l)
                o = jnp.where(row_a, otn[:, 0:QBLOCK], otn[:, QBLOCK:]).T
                ls = jnp.where(row_a, jnp.broadcast_to(lse[:, 0:QBLOCK], (QBLOCK, LANES)),
                               jnp.broadcast_to(lse[:, QBLOCK:], (QBLOCK, LANES))).T
                if dil == 1:
                    o_ref[pl.ds(base, QBLOCK), :] = o
                    l_ref[pl.ds(base, QBLOCK), :] = ls
                else:
                    c, n = i // per_class, i % per_class
                    res = c if dil == 4 else 4 * (c % 4) + c // 4
                    start = dil * QBLOCK * n + res
                    o_ref[pl.ds(start, QBLOCK, stride=dil), :] = o
                    l_ref[pl.ds(start, QBLOCK, stride=dil), :] = ls

        ngroup = nblk // ATTN_GROUP
        softmax(0, 0, scores(0, R))

        def body(g, carry):
            slot = g % 2
            st = scores(g, R)
            pv = weighted_values(g - 1, 1 - slot, R)
            softmax(g, slot, st)
            outputs(g - 1, pv)
            return carry

        lax.fori_loop(1, ngroup, body, 0)
        outputs(ngroup - 1, weighted_values(ngroup - 1, (ngroup - 1) % 2, R))

    branch(0, 1, o1, l1)
    branch(1, 4, o4, l4)
    branch(2, 16, o16, l16)

    T = ATTN_MIX_ROWS

    def mix(i, carry):
        r0 = pl.multiple_of(i * T, T)
        la, lb, lc = l1[pl.ds(r0, T), :], l4[pl.ds(r0, T), :], l16[pl.ds(r0, T), :]
        mx = jnp.maximum(jnp.maximum(la, lb), lc)
        ea, eb, ec = jnp.exp2(la - mx), jnp.exp2(lb - mx), jnp.exp2(lc - mx)
        num = ea * o1[pl.ds(r0, T), :] + eb * o4[pl.ds(r0, T), :] + ec * o16[pl.ds(r0, T), :]
        out_ref[pl.ds(r0, T), :] = (num / (ea + eb + ec)).astype(out_ref.dtype)
        return carry

    lax.fori_loop(0, S // T, mix, 0)


def _attn_prompt(z3, bias):
    B, S, _ = z3.shape
    assert S % (16 * QBLOCK) == 0
    npair = N_ATTN_HEADS // 2
    col = lambda c0: pl.BlockSpec((None, S, LANES), lambda b, p: (b, 0, c0 // LANES + p))
    f32s = lambda: pltpu.VMEM((S, LANES), F32)
    return pl.pallas_call(
        functools.partial(_attn_prompt_kernel, seq=S),
        out_shape=jax.ShapeDtypeStruct((B, S, ATTN_WIDTH), BF16),
        grid=(B, npair),
        in_specs=[col(COL_AQ), col(COL_AK), col(COL_AV),
                  pl.BlockSpec((3, None, 2, 2 * QBLOCK, 2 * QBLOCK), lambda b, p: (0, p, 0, 0, 0))],
        out_specs=pl.BlockSpec((None, S, LANES), lambda b, p: (b, 0, p)),
        scratch_shapes=[f32s(),
                        pltpu.VMEM((3, S, LANES), BF16),
                        pltpu.VMEM((3, S + QBLOCK, LANES), BF16),
                        pltpu.VMEM((3, S // QBLOCK + 1, LANES, QBLOCK), BF16),
                        f32s(), f32s(), f32s(), f32s(), f32s(), f32s(),
                        pltpu.VMEM((2, ATTN_GROUP, 2 * QBLOCK, 2 * QBLOCK), BF16),
                        pltpu.VMEM((2, ATTN_GROUP, SUBLANES, 2 * QBLOCK), F32)],
        compiler_params=_cparams(("parallel", "parallel")),
        name="attn_prompt",
    )(z3, z3, z3, bias)


TPAD = SUBLANES
TAIL = 512


def _sample_tables(bias_dist, n_past, n_tok):
    assert n_past >= max(w for w, _ in DILATED_PATTERNS) and n_tok <= TPAD and TAIL >= DILATED_PATTERNS[1][0]
    H = N_ATTN_HEADS
    t = np.arange(TPAD)[:, None]

    def by_row(width):
        rows = [_dist_slice(bias_dist, tt + 1, tt + width) for tt in range(TPAD)]
        return jnp.stack(rows, axis=1).reshape(H * TPAD, width)

    bias_tail, bias_full = by_row(TAIL), by_row(n_past)
    zero_dist = _dist_slice(bias_dist, 0, 0)
    new_rows = [jnp.concatenate([_dist_slice(bias_dist, 0, tt),
                                 jnp.broadcast_to(zero_dist, (H, TPAD - tt - 1))], axis=1) for tt in range(TPAD)]
    bias_new = jnp.stack(new_rows, axis=1).reshape(H * TPAD, TPAD)

    def valid(dj, dil, nsub, lo):
        ok = (dj % dil == 0) & (dj // dil >= lo) & (dj // dil <= nsub) & (t < n_tok)
        return np.tile(ok, (H, 1)).astype(np.float32)

    tn = np.arange(TPAD)[None, :]
    m_tail, m_new = [], []
    for window, dil in DILATED_PATTERNS:
        nsub = window // dil
        if window <= TAIL:
            m_tail.append(valid(TAIL + t - np.arange(TAIL)[None, :], dil, nsub, 1))
        else:
            m_full = valid(n_past + t - np.arange(n_past)[None, :], dil, nsub, 1)
        m_new.append(valid(t - tn, dil, nsub, 0) * (tn < n_tok))
    return (bias_tail, bias_full, bias_new, jnp.asarray(np.stack(m_tail)), jnp.asarray(m_full),
            jnp.asarray(np.stack(m_new).astype(np.float32)))


def _attn_sample_kernel(q_ref, kn_ref, vn_ref, kt_ref, vt_ref, bt_ref, bf_ref, bn_ref, mt_ref, mf_ref, mn_ref,
                        out_ref, *, n_past):
    H = N_ATTN_HEADS
    rows = H * TPAD
    same_head = (_iota2((rows, ATTN_WIDTH), 0) >> _log2(TPAD)) == (_iota2((rows, ATTN_WIDTH), 1) >> HEAD_SHIFT)
    q = q_ref[...] * (HEAD_DIM ** -0.5)
    qm = jnp.where(same_head, jnp.concatenate([q] * H, axis=0), 0.0).astype(BF16)
    nt = (((1,), (1,)), ((), ()))
    kn = kn_ref[...].astype(BF16)
    vn = vn_ref[...].astype(BF16)
    kt = kt_ref[...].astype(BF16)
    vt = vt_ref[...].astype(BF16)
    s_new = lax.dot_general(qm, kn, nt, preferred_element_type=F32)
    s_full = jnp.dot(qm, kt, preferred_element_type=F32)
    s_tail = s_full[:, n_past - TAIL:]

    outs, lses = [], []
    for di, (window, _) in enumerate(DILATED_PATTERNS):
        if window <= TAIL:
            sm = jnp.where(mt_ref[di] > 0.5, s_tail + bt_ref[...], NEG_BIG)
            vmain = vt[:, n_past - TAIL:]
        else:
            sm = jnp.where(mf_ref[...] > 0.5, s_full + bf_ref[...], NEG_BIG)
            vmain = vt
        sn = jnp.where(mn_ref[di] > 0.5, s_new + bn_ref[...], NEG_BIG)
        m = jnp.maximum(jnp.max(sm, axis=-1, keepdims=True), jnp.max(sn, axis=-1, keepdims=True))
        pm = jnp.exp(sm - m)
        pn = jnp.exp(sn - m)
        l = jnp.sum(pm, axis=-1, keepdims=True) + jnp.sum(pn, axis=-1, keepdims=True)
        o = jnp.dot(pn.astype(BF16), vn, preferred_element_type=F32)
        o = o + lax.dot_general(pm.astype(BF16), vmain, nt, preferred_element_type=F32)
        outs.append(o / l)
        lses.append(m + jnp.log(l))
    mx = jnp.maximum(jnp.maximum(lses[0], lses[1]), lses[2])
    es = [jnp.exp(ls - mx) for ls in lses]
    mixed = (es[0] * outs[0] + es[1] * outs[1] + es[2] * outs[2]) / (es[0] + es[1] + es[2])
    mixed = jnp.where(same_head, mixed, 0.0)
    acc = mixed[0:TPAD]
    for h in range(1, H):
        acc = acc + mixed[h * TPAD:(h + 1) * TPAD]
    out_ref[...] = acc.astype(out_ref.dtype)


def _attn_sample(z3, cache_kt, cache_vt, layer, tables, host_grid=None):
    B, T, _ = z3.shape
    assert T == TPAD
    n_past = cache_kt.shape[-1]
    seq, _ = _host_maps(host_grid)
    new = lambda c0: pl.BlockSpec((None, TPAD, ATTN_WIDTH), lambda b, t: (seq(b, t), 0, c0 // ATTN_WIDTH))
    buf = pl.BlockSpec((None, None, ATTN_WIDTH, n_past), lambda b, t: (layer, seq(b, t), 0, 0))
    return _CallSpec(
        body=functools.partial(_attn_sample_kernel, n_past=n_past),
        grid=host_grid or (B, 1),
        args=[z3, z3, z3, cache_kt, cache_vt, *tables],
        in_specs=[new(COL_AQ), new(COL_AK), new(COL_AV), buf, buf] + [_const_spec(t.shape) for t in tables],
        out_shape=[jax.ShapeDtypeStruct((B, TPAD, ATTN_WIDTH), BF16)],
        out_specs=[pl.BlockSpec((None, TPAD, ATTN_WIDTH), lambda b, t: (seq(b, t), 0, 0))],
        scratch=[])


QUAD = N_MLSTM_HEADS * HEAD_DIM
CHUNK_UNROLL = 2
CHUNK_GROUP = 8
HEAD_SHIFT = 6


def _iota2(shape, axis):
    return lax.broadcasted_iota(jnp.int32, shape, axis)


def _log2(n):
    k = int(n).bit_length() - 1
    assert 1 << k == n
    return k


def _seg_mask(rows, row_shift, cols, col_shift):
    return (_iota2((rows, cols), 0) >> row_shift) == (_iota2((rows, cols), 1) >> col_shift)


def _cumulate_rows(x, length, op, fill):
    row = _iota2(x.shape, 0) & (length - 1)
    sh = 1
    while sh < length:
        x = op(x, jnp.where(row >= sh, pltpu.roll(x, sh, axis=0), fill))
        sh *= 2
    return x


def _split2(x):
    hi = x.astype(BF16)
    lo = (x - hi.astype(F32)).astype(BF16)
    return hi, lo


def _dot2(x, w):
    hi, lo = _split2(x)
    return jnp.dot(hi, w, preferred_element_type=F32) + jnp.dot(lo, w, preferred_element_type=F32)


def _dot2r(w, x):
    hi, lo = _split2(x)
    return jnp.dot(w, hi, preferred_element_type=F32) + jnp.dot(w, lo, preferred_element_type=F32)


def _tile_position(single_tile):
    if single_tile:
        return 0, True
    return pl.program_id(1), pl.program_id(1) == pl.num_programs(1) - 1


def _when(cond):
    if isinstance(cond, bool):
        return (lambda f: f()) if cond else (lambda f: None)
    return pl.when(cond)


def _host_maps(host_grid):
    if host_grid is None:
        return (lambda b, t: b), (lambda b, t: t)
    return (lambda b, t: b * host_grid[1] + t), (lambda b, t: 0)


def _load_block_diag(ref, seg64):
    flat = ref[...].reshape(QUAD, HEAD_DIM)
    return jnp.where(seg64, jnp.concatenate([flat] * (QUAD // HEAD_DIM), axis=1), 0.0)


def _store_block_diag(ref, mat):
    for h in range(QUAD // HEAD_DIM):
        ref[h] = mat[h * HEAD_DIM:(h + 1) * HEAD_DIM, h * HEAD_DIM:(h + 1) * HEAD_DIM]


def _head_rmsnorm_gate(h, seg64b, g_row, gate_pre):
    ms = _dot2(h * h, seg64b) * (1.0 / HEAD_DIM)
    return jax.nn.sigmoid(gate_pre) * (h * lax.rsqrt(ms + EPS) * g_row)


def _mlstm_kernel(qk_ref, v_ref, o_ref, gate_ref, conv0_ref, c0_ref, n0_ref, m0_ref, gb_ref, cw_ref, ng_ref,
                  out_ref, c_out, n_out, m_out, xp, cs, ns, ms, *, tile, chunk, n_valid, single_tile=False,
                  guests=None):
    TS, L = tile, chunk
    t, last = _tile_position(single_tile)
    PAD = SUBLANES

    @_when(t == 0)
    def _():
        cs[...] = _load_block_diag(c0_ref, _seg_mask(QUAD, HEAD_SHIFT, QUAD, HEAD_SHIFT))
        ns[...] = n0_ref[...]
        ms[...] = m0_ref[...]
        xp[0:PAD, :] = conv0_ref[...]

    @_when(t > 0)
    def _():
        xp[0:PAD, :] = xp[TS:TS + PAD, :]

    if guests is not None:
        guests()
    xp[PAD:PAD + TS, :] = qk_ref[...]

    LK = MLSTM_CHUNK
    seg64 = _seg_mask(QUAD, HEAD_SHIFT, QUAD, HEAD_SHIFT)
    seg64b = seg64.astype(BF16)
    dmask = (_iota2((LK, QUAD), 1) & (LK - 1)) == _iota2((LK, QUAD), 0)
    causal = (_iota2((L, QUAD), 1) & (LK - 1)) <= _iota2((L, QUAD), 0)
    tril = (_iota2((L, L), 1) <= _iota2((L, L), 0)).astype(BF16)
    ones_lk = jnp.ones((L, LK), BF16)
    row = _iota2((L, QUAD), 0)
    cw = cw_ref[...]
    gb = gb_ref[...]
    ng = ng_ref[...]

    first_half = _iota2((L, LANES), 1) < HEAD_DIM

    def per_head_lanes(g, lane0):
        col = [jnp.broadcast_to(g[:, lane0 + h:lane0 + h + 1], (L, LANES)) for h in range(N_MLSTM_HEADS)]
        return jnp.concatenate([jnp.where(first_half, col[0], col[1]), jnp.where(first_half, col[2], col[3])], axis=1)

    def key_rows(x, fill):
        if L == LK:
            return x
        return jnp.concatenate([x, jnp.full((LK - L, QUAD), fill, x.dtype)], axis=0)

    G = min(CHUNK_GROUP, TS // L)
    assert TS % (L * G) == 0
    nt = (((1,), (1,)), ((), ()))
    tn = (((0,), (0,)), ((), ()))
    zb = jnp.zeros((QUAD, QUAD), BF16)

    def group_body(gi, carry):
        R = range(G)
        r0 = [pl.multiple_of((gi * G + j) * L, L) for j in R]
        q, k, v, ig, lf = [], [], [], [], []
        for j in R:
            win = xp[pl.ds(r0[j], L + PAD), :]
            acc = win[PAD:PAD + L] * cw[CONV_WIDTH - 1:CONV_WIDTH, :]
            for s in range(1, CONV_WIDTH):
                acc = acc + win[PAD - s:PAD - s + L] * cw[CONV_WIDTH - 1 - s:CONV_WIDTH - s, :]
            qk = acc * jax.nn.sigmoid(acc)
            q.append(qk[:, 0:QUAD])
            k.append(qk[:, QUAD:] * (HEAD_DIM ** -0.5))
            v.append(v_ref[pl.ds(r0[j], L), :])
            gate = gate_ref[pl.ds(r0[j], L), :] + gb
            logsig = jnp.minimum(gate, 0.0) - jnp.log(1.0 + jnp.exp(-jnp.abs(gate)))
            ig_j = per_head_lanes(gate, 0)
            lf_j = per_head_lanes(logsig, N_MLSTM_HEADS)
            if n_valid < TS:
                ok = (row + r0[j]) < n_valid
                ig_j = jnp.where(ok, ig_j, NEG_BIG)
                lf_j = jnp.where(ok, lf_j, 0.0)
            ig.append(ig_j)
            lf.append(lf_j)
        b = [_dot2r(tril, lf[j]) for j in R]
        a = [ig[j] - b[j] for j in R]
        cm = [_cumulate_rows(a[j], L, jnp.maximum, -jnp.inf) for j in R]
        arow = [_dot2r(ones_lk, jnp.where(dmask, key_rows(a[j], NEG_BIG), 0.0)) for j in R]
        qb = [q[j].astype(BF16) for j in R]
        vb = [v[j].astype(BF16) for j in R]
        kbd = [jnp.where(seg64, jnp.concatenate([key_rows(k[j], 0.0).astype(BF16)] * N_MLSTM_HEADS, axis=0), zb) for j in R]
        vbd = [jnp.where(seg64, jnp.concatenate([key_rows(v[j], 0.0).astype(BF16)] * N_MLSTM_HEADS, axis=0), zb) for j in R]
        qkt = [lax.dot_general(qb[j], kbd[j], nt, preferred_element_type=F32) for j in R]

        mprev = ms[...]
        M, gg, emt = [], [], []
        for j in R:
            M.append(jnp.maximum(cm[j], mprev))
            mt = b[j] + M[j]
            gg.append(jnp.exp(mprev - M[j]))
            emt.append(jnp.exp(-mt))
            mprev = mt[L - 1:L, :]
        ms[...] = mprev

        wts = [jnp.where(causal, jnp.exp(arow[j] - M[j]), 0.0) * qkt[j] for j in R]
        kd = [k[j] * jnp.exp(a[j] - M[j][L - 1:L, :]) for j in R]
        num = [jnp.dot(wts[j].astype(BF16), vbd[j], preferred_element_type=F32) for j in R]
        u = [lax.dot_general(kd[j].astype(BF16), vb[j], tn, preferred_element_type=F32) for j in R]

        cmat, nvec = [cs[...]], [ns[...]]
        for j in R:
            gl = gg[j][L - 1:L, :]
            cmat.append(gl * cmat[j] + jnp.where(seg64, u[j], 0.0))
            nvec.append(gl * nvec[j] + jnp.sum(kd[j], axis=0, keepdims=True))
        cs[...] = cmat[G]
        ns[...] = nvec[G]

        inter = [jnp.dot(qb[j], cmat[j].astype(BF16), preferred_element_type=F32) for j in R]
        den = [_dot2(wts[j] + gg[j] * (q[j] * nvec[j]), seg64b) for j in R]
        h = [(num[j] + gg[j] * inter[j]) / jnp.maximum(jnp.abs(den[j]), emt[j]) for j in R]
        msq = [_dot2(h[j] * h[j], seg64b) * (1.0 / HEAD_DIM) for j in R]
        for j in R:
            y = h[j] * lax.rsqrt(msq[j] + EPS) * ng
            out_ref[pl.ds(r0[j], L), :] = (jax.nn.sigmoid(o_ref[pl.ds(r0[j], L), :]) * y).astype(out_ref.dtype)
        return carry

    lax.fori_loop(0, TS // (L * G), group_body, 0)

    @_when(last)
    def _():
        _store_block_diag(c_out, cs[...])
        n_out[...] = ns[...]
        m_out[...] = ms[...]


def _mlstm(z3, conv0, c0bd, n0, m0, gate_b, conv_w, norm_g, n_valid, chunk, host_grid=None):
    B, S, _ = z3.shape
    ts = min(TOKEN_TILE, S)
    assert S % ts == 0 and ts % chunk == 0 and (host_grid is None or S == ts)
    seq, til = _host_maps(host_grid)
    blk = lambda w, c0: pl.BlockSpec((None, ts, w), lambda b, t: (seq(b, t), til(b, t), c0 // w))
    per_b = lambda r, w: pl.BlockSpec((None, r, w), lambda b, t: (seq(b, t), 0, 0))
    per_head = pl.BlockSpec((None, N_MLSTM_HEADS, HEAD_DIM, HEAD_DIM), lambda b, t: (seq(b, t), 0, 0, 0))
    return _CallSpec(
        body=functools.partial(_mlstm_kernel, tile=ts, chunk=chunk, n_valid=n_valid, single_tile=host_grid is not None),
        grid=host_grid or (B, S // ts),
        args=[z3, z3, z3, z3, conv0, c0bd, n0, m0, gate_b, conv_w, norm_g],
        in_specs=[blk(2 * QUAD, COL_MQK), blk(QUAD, COL_MV), blk(QUAD, COL_MO), blk(LANES, COL_GATE),
                  per_b(SUBLANES, 2 * QUAD), per_head, per_b(1, QUAD), per_b(1, QUAD),
                  _const_spec((1, LANES)), _const_spec((CONV_WIDTH, 2 * QUAD)), _const_spec((1, QUAD))],
        out_shape=[jax.ShapeDtypeStruct((B, S, QUAD), BF16),
                   jax.ShapeDtypeStruct((B, N_MLSTM_HEADS, HEAD_DIM, HEAD_DIM), F32),
                   jax.ShapeDtypeStruct((B, 1, QUAD), F32),
                   jax.ShapeDtypeStruct((B, 1, QUAD), F32)],
        out_specs=[blk(QUAD, 0), per_head, per_b(1, QUAD), per_b(1, QUAD)],
        scratch=[pltpu.VMEM((ts + 2 * SUBLANES, 2 * QUAD), F32),
                 pltpu.VMEM((QUAD, QUAD), F32), pltpu.VMEM((1, QUAD), F32), pltpu.VMEM((1, QUAD), F32)])


HGRN_FAST_CHUNK = 64
HGRN_GROUP = 8
HGRN_SAFE_DECAY = 80.0


def _hgrn_kernel(q_ref, f_ref, i_ref, g_ref, s0_ref, lb_ref, ng_ref, out_ref, s_out, st, qs, fs, ks, *,
                 tile, sub, n_valid, fast, single_tile=False, guests=None):
    TS = tile
    t, last = _tile_position(single_tile)

    seg64 = _seg_mask(QUAD, HEAD_SHIFT, QUAD, HEAD_SHIFT)
    seg64b = seg64.astype(BF16)

    @_when(t == 0)
    def _():
        st[...] = _load_block_diag(s0_ref, seg64)
    lb = lb_ref[...]
    ng = ng_ref[...]
    nt = (((1,), (1,)), ((), ()))
    tn = (((0,), (0,)), ((), ()))

    P = HGRN_FAST_CHUNK if fast else sub
    prow = _iota2((P, QUAD), 0)

    def prep(c, worst):
        r0 = pl.multiple_of(c * P, P)
        hq = q_ref[pl.ds(r0, P), :]
        f = lb + (1.0 - lb) * jax.nn.sigmoid(f_ref[pl.ds(r0, P), :])
        logf = jnp.log(f)
        kk = 1.0 - f
        if n_valid < TS:
            ok = (prow + r0) < n_valid
            logf = jnp.where(ok, logf, 0.0)
            kk = jnp.where(ok, kk, 0.0)
        qs[pl.ds(r0, P), :] = hq * jax.nn.sigmoid(hq)
        fs[pl.ds(r0, P), :] = logf
        ks[pl.ds(r0, P), :] = kk
        return jnp.minimum(worst, jnp.sum(logf, axis=0, keepdims=True))

    worst = lax.fori_loop(0, TS // P, prep, jnp.zeros((1, QUAD), F32))

    def finish(o, r0, rows, smat, b, kk, iv, qt):
        o = o + lax.dot_general(qt, smat.astype(BF16), nt, preferred_element_type=F32)
        out_ref[pl.ds(r0, rows), :] = _head_rmsnorm_gate(o, seg64b, ng, g_ref[pl.ds(r0, rows), :]).astype(out_ref.dtype)
        bl = b[rows - 1:rows, :]
        ktil = kk * jnp.exp(bl - b)
        u = lax.dot_general(iv.astype(BF16), ktil.astype(BF16), tn, preferred_element_type=F32)
        st[...] = smat * jnp.exp(bl) + jnp.where(seg64, u, 0.0)

    def fast_loop():
        if guests is not None:
            guests()
        L = HGRN_FAST_CHUNK
        tril = (_iota2((L, L), 1) <= _iota2((L, L), 0)).astype(BF16)
        causal = (_iota2((L, QUAD), 1) & (L - 1)) <= _iota2((L, QUAD), 0)
        zb = jnp.zeros((QUAD, QUAD), BF16)

        G = min(HGRN_GROUP, TS // L)
        assert TS % (L * G) == 0

        def body(gi, carry):
            R = range(G)
            r0 = [pl.multiple_of((gi * G + j) * L, L) for j in R]
            kk = [ks[pl.ds(r0[j], L), :] for j in R]
            ivb = [i_ref[pl.ds(r0[j], L), :].astype(BF16) for j in R]
            b = [_dot2r(tril, fs[pl.ds(r0[j], L), :]) for j in R]
            qt = [(qs[pl.ds(r0[j], L), :] * jnp.exp(b[j])).astype(BF16) for j in R]
            kbd = [jnp.where(seg64, jnp.concatenate([(kk[j] * jnp.exp(-b[j])).astype(BF16)] * N_HGRN_HEADS, axis=0), zb)
                   for j in R]
            ibd = [jnp.where(seg64, jnp.concatenate([ivb[j]] * N_HGRN_HEADS, axis=0), zb) for j in R]
            amat = [lax.dot_general(qt[j], kbd[j], nt, preferred_element_type=F32) for j in R]
            ktil = [(kk[j] * jnp.exp(b[j][L - 1:L, :] - b[j])).astype(BF16) for j in R]
            u = [lax.dot_general(ivb[j], ktil[j], tn, preferred_element_type=F32) for j in R]
            o = [jnp.dot(jnp.where(causal, amat[j], 0.0).astype(BF16), ibd[j], preferred_element_type=F32) for j in R]
            smat = [st[...]]
            for j in R:
                smat.append(smat[j] * jnp.exp(b[j][L - 1:L, :]) + jnp.where(seg64, u[j], 0.0))
            st[...] = smat[G]
            inter = [lax.dot_general(qt[j], smat[j].astype(BF16), nt, preferred_element_type=F32) for j in R]
            o = [o[j] + inter[j] for j in R]
            msq = [_dot2(o[j] * o[j], seg64b) * (1.0 / HEAD_DIM) for j in R]
            for j in R:
                y = o[j] * lax.rsqrt(msq[j] + EPS) * ng
                out_ref[pl.ds(r0[j], L), :] = (jax.nn.sigmoid(g_ref[pl.ds(r0[j], L), :]) * y).astype(out_ref.dtype)
            return carry

        lax.fori_loop(0, TS // (L * G), body, 0)

    def exact_loop():
        if guests is not None:
            guests()
        L = sub
        row = _iota2((L, QUAD), 0)

        def body(c, carry):
            r0 = pl.multiple_of(c * L, L)
            q = qs[pl.ds(r0, L), :]
            kk = ks[pl.ds(r0, L), :]
            iv = i_ref[pl.ds(r0, L), :]
            b = _cumulate_rows(fs[pl.ds(r0, L), :], L, jnp.add, 0.0)
            parts = []
            for j in range(L):
                dec = jnp.exp(jnp.where(row >= j, b - b[j:j + 1, :], NEG_BIG))
                parts.append(dec * q * kk[j:j + 1, :])
            tstack = jnp.concatenate(parts, axis=0).astype(BF16)
            y = jnp.dot(tstack, seg64b, preferred_element_type=F32)
            o = y[0:L] * iv[0:1, :]
            for j in range(1, L):
                o = o + y[j * L:(j + 1) * L] * iv[j:j + 1, :]
            finish(o, r0, L, st[...], b, kk, iv, (q * jnp.exp(b)).astype(BF16))
            return carry

        lax.fori_loop(0, TS // L, body, 0, unroll=min(CHUNK_UNROLL, TS // L))

    if fast:
        safe = jnp.min(worst) > -HGRN_SAFE_DECAY
        pl.when(safe)(fast_loop)
        pl.when(jnp.logical_not(safe))(exact_loop)
    else:
        exact_loop()

    @_when(last)
    def _():
        _store_block_diag(s_out, st[...])


def _hgrn(z3, s0t, lb, norm_g, n_valid, sub, host_grid=None):
    B, S, _ = z3.shape
    ts = min(TOKEN_TILE, S)
    fast = ts % HGRN_FAST_CHUNK == 0
    assert S % ts == 0 and ts % sub == 0 and (host_grid is None or S == ts)
    seq, til = _host_maps(host_grid)
    blk = lambda c0: pl.BlockSpec((None, ts, QUAD), lambda b, t: (seq(b, t), til(b, t), c0 // QUAD))
    per_b = pl.BlockSpec((None, N_HGRN_HEADS, HEAD_DIM, HEAD_DIM), lambda b, t: (seq(b, t), 0, 0, 0))
    stage = lambda: pltpu.VMEM((ts, QUAD), F32)
    return _CallSpec(
        body=functools.partial(_hgrn_kernel, tile=ts, sub=sub, n_valid=n_valid, fast=fast,
                               single_tile=host_grid is not None),
        grid=host_grid or (B, S // ts),
        args=[z3, z3, z3, z3, s0t, lb, norm_g],
        in_specs=[blk(COL_HQ), blk(COL_HF), blk(COL_HI), blk(COL_HG), per_b,
                  _const_spec((1, QUAD)), _const_spec((1, QUAD))],
        out_shape=[jax.ShapeDtypeStruct((B, S, QUAD), BF16),
                   jax.ShapeDtypeStruct((B, N_HGRN_HEADS, HEAD_DIM, HEAD_DIM), F32)],
        out_specs=[blk(0), per_b],
        scratch=[pltpu.VMEM((QUAD, QUAD), F32), stage(), stage(), stage()])


PROJ_SPLITS = (ATTN_WIDTH, ATTN_WIDTH, ATTN_WIDTH,
               MLSTM_WIDTH, MLSTM_WIDTH, MLSTM_WIDTH, MLSTM_WIDTH, N_MLSTM_HEADS, N_MLSTM_HEADS,
               HGRN_WIDTH, HGRN_WIDTH, HGRN_WIDTH, HGRN_WIDTH)


def _prep_w_in(w):
    g0 = int(np.sum(PROJ_SPLITS[:7]))
    g1 = g0 + 2 * N_MLSTM_HEADS
    assert (g0, w.shape[-1] - g1) == (COL_HQ, COL_GATE - COL_HQ)
    wb = w.astype(BF16)
    pad = jnp.zeros(w.shape[:-1] + (LANES - 2 * N_MLSTM_HEADS,), BF16)
    return wb, jnp.concatenate([wb[..., g1:], wb[..., g0:g1], pad], axis=-1)


HGRN_SUBCHUNK = 16


class _LayerWeights(NamedTuple):
    layer: int
    g_mix: jax.Array
    w_in: jax.Array
    gate_b: jax.Array
    conv_w: jax.Array
    mlstm_g: jax.Array
    lb: jax.Array
    hgrn_g: jax.Array
    w_out: jax.Array
    g_ffn: jax.Array
    w_up: jax.Array
    w_down: jax.Array
    g_final: jax.Array


def _gate_row(gate_b):
    flat = gate_b.astype(F32).reshape(1, 2 * N_MLSTM_HEADS)
    return jnp.concatenate([flat, jnp.zeros((1, LANES - 2 * N_MLSTM_HEADS), F32)], axis=1)


class _LayerFront(NamedTuple):
    z3: jax.Array
    kv_t: tuple
    mlstm: _CallSpec
    hgrn: _CallSpec


def _layer_front(x2d, B, S, n_valid, states, w, chunk_m, chunk_h, n_keep=None, stacked=None, host_grid=None):
    conv_buf, C0, n0, m0, S0 = states
    if n_keep is None:
        z, kv_t = _inproj(x2d, w.g_mix, w.w_in, w.layer), None
    else:
        z, kt, vt = _inproj(x2d, w.g_mix, w.w_in, w.layer, S, n_keep, stacked)
        kv_t = (kt, vt)
    z3 = z.reshape(B, S, ZW)
    conv0 = jnp.zeros((B, SUBLANES, 2 * MLSTM_WIDTH), F32).at[:, SUBLANES - (CONV_WIDTH - 1):].set(conv_buf.astype(F32))
    m0r = jnp.repeat(m0.astype(F32), HEAD_DIM, axis=1).reshape(B, 1, MLSTM_WIDTH)
    mlstm = _mlstm(z3, conv0, C0.astype(F32), n0.astype(F32).reshape(B, 1, MLSTM_WIDTH), m0r,
                   w.gate_b, w.conv_w, w.mlstm_g, n_valid, chunk_m, host_grid)
    hgrn = _hgrn(z3, jnp.swapaxes(S0.astype(F32), -1, -2), w.lb, w.hgrn_g, n_valid, chunk_h, host_grid)
    return _LayerFront(z3, kv_t, mlstm, hgrn)


def _layer_back(x2d, front, n_valid, w, final, attn, mlstm_out, hgrn_out):
    B, S, _ = front.z3.shape
    z3 = front.z3
    (ml, c_new, n, m), (hg, st) = mlstm_out, hgrn_out
    n2 = B * S
    y = _outffn(x2d, attn.reshape(n2, ATTN_WIDTH), ml.reshape(n2, MLSTM_WIDTH), hg.reshape(n2, HGRN_WIDTH),
                w.w_out, w.g_ffn, w.w_up, w.w_down, w.g_final, final, w.layer)
    k_rows = z3[:, :n_valid, COL_AK:COL_AK + ATTN_WIDTH].reshape(B, n_valid, N_ATTN_HEADS, HEAD_DIM)
    v_rows = z3[:, :n_valid, COL_AV:COL_AV + ATTN_WIDTH].reshape(B, n_valid, N_ATTN_HEADS, HEAD_DIM)
    conv_new = z3[:, n_valid - (CONV_WIDTH - 1):n_valid, COL_MQK:COL_MQK + 2 * MLSTM_WIDTH]
    s_new = jnp.swapaxes(st, -1, -2)
    return y, (k_rows, v_rows, conv_new, c_new, n.reshape(B, N_MLSTM_HEADS, HEAD_DIM), m[:, 0, ::HEAD_DIM], s_new)


def kernel(x_prompt, x_sample, cache_attn_k, cache_attn_v, state_mlstm_conv, state_mlstm_C, state_mlstm_n, state_mlstm_m, state_hgrn_S, rel_bias, g_mix, w_in, mlstm_gate_b, mlstm_conv_w, mlstm_norm_g, hgrn_lb_raw, hgrn_norm_g, w_out, g_ffn, w_up, w_down, g_final):
    depth = w_in.shape[0]
    bp, sp, _ = x_prompt.shape
    bs, ts, _ = x_sample.shape
    n_keep = min(max(w for w, _ in DILATED_PATTERNS), sp)
    n_past = cache_attn_k.shape[2]
    lb_p = jax.nn.softmax(hgrn_lb_raw.astype(F32), axis=0)
    hgrn_lb = jnp.cumsum(lb_p, axis=0) - lb_p[0]
    bias_dist = _bias_by_distance(rel_bias)
    p_tables = _prompt_tables(bias_dist)
    s_tables = _sample_tables(bias_dist, n_past, ts)
    cache_kt = jnp.transpose(cache_attn_k, (0, 1, 3, 4, 2)).reshape(depth, bs, ATTN_WIDTH, n_past)
    cache_vt = jnp.transpose(cache_attn_v, (0, 1, 3, 4, 2)).reshape(depth, bs, ATTN_WIDTH, n_past)

    hp = x_prompt.reshape(bp * sp, D_MODEL)
    hs = jnp.zeros((bs, TPAD, D_MODEL), F32).at[:, :ts].set(x_sample).reshape(bs * TPAD, D_MODEL)
    zero_states = (jnp.zeros((bp, CONV_WIDTH - 1, 2 * MLSTM_WIDTH), F32),
                   jnp.zeros((bp, N_MLSTM_HEADS, HEAD_DIM, HEAD_DIM), F32),
                   jnp.zeros((bp, N_MLSTM_HEADS, HEAD_DIM), F32),
                   jnp.zeros((bp, N_MLSTM_HEADS), F32),
                   jnp.zeros((bp, N_HGRN_HEADS, HEAD_DIM, HEAD_DIM), F32))
    p_states, s_states = [], []
    kv_t = None
    sem = ("parallel", "arbitrary")
    w_in_b, w_out_b, w_up_b, w_down_b = _prep_w_in(w_in), w_out.astype(BF16), w_up.astype(BF16), w_down.astype(BF16)
    for l in range(depth):
        final = l == depth - 1
        weights = _LayerWeights(
            layer=l, g_mix=g_mix[l][None], w_in=w_in_b,
            gate_b=_gate_row(mlstm_gate_b[l]),
            conv_w=mlstm_conv_w[l], mlstm_g=mlstm_norm_g[l][None], lb=hgrn_lb[l][None], hgrn_g=hgrn_norm_g[l][None],
            w_out=w_out_b, g_ffn=g_ffn[l][None], w_up=w_up_b, w_down=w_down_b, g_final=g_final[None])
        pf = _layer_front(hp, bp, sp, sp, zero_states, weights, MLSTM_CHUNK, HGRN_SUBCHUNK, n_keep, (l, depth, kv_t))
        kv_t = pf.kv_t
        grid = pf.mlstm.grid
        host = grid if bs == grid[0] * grid[1] else None
        states = (state_mlstm_conv[l], state_mlstm_C[l], state_mlstm_n[l], state_mlstm_m[l], state_hgrn_S[l])
        sf = _layer_front(hs, bs, TPAD, ts, states, weights, TPAD, TPAD, host_grid=host)
        s_attn = _attn_sample(sf.z3, cache_kt, cache_vt, l, s_tables, host)
        p_attn = _attn_prompt(pf.z3, p_tables)
        if host is None:
            p_ml, = _run([pf.mlstm], sem, "mlstm")
            (s_attn,), s_ml, s_hg = _run([s_attn, sf.mlstm, sf.hgrn], sem, "sample_mixers")
        else:
            p_ml, (s_attn,) = _run([pf.mlstm, s_attn], sem, "mlstm_and_sample_attention", host=0)
            p_hg, s_ml, s_hg = _run([pf.hgrn, sf.mlstm, sf.hgrn], sem, "hgrn_and_sample_recurrences", host=0)
        if host is None:
            p_hg, = _run([pf.hgrn], sem, "hgrn")
        hp, st = _layer_back(hp, pf, sp, weights, final, p_attn, p_ml, p_hg)
        p_states.append(st[2:])
        hs, st = _layer_back(hs, sf, ts, weights, final, s_attn, s_ml, s_hg)
        s_states.append(st)
    y_prompt = hp.reshape(bp, sp, D_MODEL)
    y_sample = hs.reshape(bs, TPAD, D_MODEL)[:, :ts]
    to_rows = lambda t: jnp.transpose(t.reshape(depth, bp, N_ATTN_HEADS, HEAD_DIM, n_keep), (0, 1, 4, 2, 3))
    p_out = [to_rows(kv_t[0]), to_rows(kv_t[1])] + [jnp.stack(a) for a in zip(*p_states)]
    s_out = [jnp.stack(a) for a in zip(*s_states)]
    return (y_prompt, y_sample, *p_out, *s_out)
```
